```python
import math
import jax, jax.numpy as jnp
from jax import lax
import numpy as np

D_MODEL = 2048
BATCH = 4
SEQ = 4096
DEPTH = 4

GRID_W = 64
CTX_LEN = 256
N_MIXERS = 3
N_GA = len(range(0, DEPTH, N_MIXERS))
N_RT = len(range(1, DEPTH, N_MIXERS))
N_DF = len(range(2, DEPTH, N_MIXERS))

EPS = 1e-6
NEG_INF = -1e30
ROPE_THETA = 10000.0
BLOCK = 128

GA_HD = 128
GA_HEADS = D_MODEL // GA_HD
GA_KV = GA_HEADS // 4
GA_GROUP = GA_HEADS // GA_KV
WINDOW = 128

RT_HEADS = 8
RT_QK = D_MODEL // RT_HEADS
RT_V = 2 * RT_QK
CHUNK = 128

DF_HD = 128
DF_HEADS = D_MODEL // (2 * DF_HD)

D_FF = 256 * math.ceil(8 * D_MODEL / (3 * 256))
CONV_W = 3

kernel_name = "hybrid_interleaved_dit_prefix_block"


def rms_norm(x, g):
    xf = x.astype(jnp.float32)
    y = xf * lax.rsqrt(jnp.mean(xf * xf, -1, keepdims=True) + EPS)
    return (y * g.astype(jnp.float32)).astype(x.dtype)


def head_group_norm(o, g):
    mu = jnp.mean(o, -1, keepdims=True)
    d = o - mu
    var = jnp.mean(d * d, -1, keepdims=True)
    return d * lax.rsqrt(var + EPS) * g.astype(jnp.float32).reshape(o.shape[2:])


def modulate(h, shift, scale):
    return h * (1 + scale) + shift


def adaln(cond, w, b):
    m = jax.nn.silu(cond) @ w + b
    return jnp.split(m, 6, -1)


def rope(x, cos, sin):
    half = x.shape[-1] // 2
    xf = x.astype(jnp.float32)
    x1, x2 = xf[..., :half], xf[..., half:]
    return jnp.concatenate([x1 * cos - x2 * sin, x1 * sin + x2 * cos], -1).astype(x.dtype)


def axial_rope_tables(rows_count, head_dim):
    rows = jnp.repeat(jnp.arange(rows_count), GRID_W).astype(jnp.float32)
    cols = jnp.tile(jnp.arange(GRID_W), rows_count).astype(jnp.float32)
    n_freq = head_dim // 4
    inv = ROPE_THETA ** (-jnp.arange(n_freq, dtype=jnp.float32) / n_freq)
    ang = jnp.concatenate([rows[:, None] * inv, cols[:, None] * inv], -1)
    return jnp.cos(ang), jnp.sin(ang)


def linear_rope_tables(n_tokens, head_dim):
    half = head_dim // 2
    inv = ROPE_THETA ** (-jnp.arange(half, dtype=jnp.float32) / half)
    ang = jnp.arange(n_tokens, dtype=jnp.float32)[:, None] * inv
    return jnp.cos(ang), jnp.sin(ang)


def joint_softmax(parts, sink=None):
    m = parts[0].max(-1, keepdims=True)
    for p in parts[1:]:
        m = jnp.maximum(m, p.max(-1, keepdims=True))
    if sink is not None:
        m = jnp.maximum(m, sink)
    es = [jnp.exp(p - m) for p in parts]
    den = sum(e.sum(-1, keepdims=True) for e in es)
    if sink is not None:
        den = den + jnp.exp(sink - m)
    return [e / den for e in es]


def windowed_gqa_mixer(h_lat, h_ctx, wqkv, sink, qk_g, wo, cos, sin, need_ctx):
    B, S, _ = h_lat.shape
    G = GA_GROUP
    scale = GA_HD ** -0.5

    def project(h):
        n = h.shape[1]
        q, k, v = jnp.split(h @ wqkv, [GA_HEADS * GA_HD, (GA_HEADS + GA_KV) * GA_HD], -1)
        q = rms_norm(q.reshape(B, n, GA_KV, G, GA_HD), qk_g[0]) * scale
        k = rms_norm(k.reshape(B, n, GA_KV, GA_HD), qk_g[1])
        return q, k, v.reshape(B, n, GA_KV, GA_HD)

    qc, kc, vc = project(h_ctx)
    ql, kl, vl = project(h_lat)
    ql = rope(ql, cos[:, None, None], sin[:, None, None])
    kl = rope(kl, cos[:, None], sin[:, None])
    sink_b = sink.astype(jnp.float32).reshape(GA_KV, G)[:, :, None, None]

    nb = S // BLOCK
    qb = ql.reshape(B, nb, BLOCK, GA_KV, G, GA_HD)

    def band(t):
        tp = jnp.pad(t, ((0, 0), (BLOCK, BLOCK), (0, 0), (0, 0))).reshape(B, nb + 2, BLOCK, *t.shape[2:])
        return jnp.concatenate([tp[:, :-2], tp[:, 1:-1], tp[:, 2:]], axis=2)

    kw, vw = band(kl), band(vl)
    qpos = jnp.arange(S).reshape(nb, BLOCK)[:, :, None]
    kpos = (jnp.arange(nb)[:, None] * BLOCK - BLOCK + jnp.arange(3 * BLOCK)[None, :])[:, None, :]
    mask = (jnp.abs(kpos - qpos) <= WINDOW) & (kpos >= 0) & (kpos < S)
    s_loc = jnp.einsum("bcingd,bcjnd->bcngij", qb, kw).astype(jnp.float32)
    s_loc = jnp.where(mask[None, :, None, None], s_loc, NEG_INF)
    s_ctx = jnp.einsum("bcingd,bjnd->bcngij", qb, kc).astype(jnp.float32)
    p_loc, p_ctx = joint_softmax([s_loc, s_ctx], sink_b)
    o = (jnp.einsum("bcngij,bcjnd->bcingd", p_loc.astype(vw.dtype), vw)
         + jnp.einsum("bcngij,bjnd->bcingd", p_ctx.astype(vc.dtype), vc))
    y_lat = o.reshape(B, S, GA_HEADS * GA_HD) @ wo

    y_ctx = None
    if need_ctx:
        s = jnp.einsum("bingd,bjnd->bngij", qc, kc).astype(jnp.float32)
        (p,) = joint_softmax([s], sink_b)
        oc = jnp.einsum("bngij,bjnd->bingd", p.astype(vc.dtype), vc)
        y_ctx = oc.reshape(B, h_ctx.shape[1], GA_HEADS * GA_HD) @ wo
    return y_lat, y_ctx


def retention_scan(q, k, v, log_gamma, state0):
    B, N, H, dk = q.shape
    dv = v.shape[-1]
    nc = N // CHUNK
    pos = jnp.arange(CHUNK, dtype=jnp.float32)
    rel = pos[:, None] - pos[None, :]
    lg = log_gamma.astype(jnp.float32)
    decay_mask = jnp.where(rel >= 0, jnp.exp(lg[:, None, None] * jnp.maximum(rel, 0.0)), 0.0)
    q_decay = jnp.exp(lg[:, None] * (pos + 1))[:, :, None]
    k_decay = jnp.exp(lg[:, None] * (CHUNK - 1 - pos))[:, :, None]
    chunk_decay = jnp.exp(lg * CHUNK)[:, None, None]

    def to_chunks(t):
        return t.reshape(B, nc, CHUNK, H, t.shape[-1]).transpose(1, 0, 3, 2, 4)

    def step(state, inp):
        qi, ki, vi = inp
        inner = jnp.einsum("bhid,bhjd->bhij", qi, ki) * decay_mask
        o = (jnp.einsum("bhij,bhje->bhie", inner, vi)
             + jnp.einsum("bhid,bhde->bhie", qi, state) * q_decay)
        state = state * chunk_decay + jnp.einsum("bhjd,bhje->bhde", ki * k_decay, vi)
        return state, o

    state, o = lax.scan(step, state0, (to_chunks(q), to_chunks(k), to_chunks(v)))
    return o.transpose(1, 0, 3, 2, 4).reshape(B, N, H, dv), state


def retention_mixer(h_lat, h_ctx, w_in, decay_logit, gn_g, wo, cos, sin, need_ctx):
    B, S, _ = h_lat.shape
    QK, V = RT_HEADS * RT_QK, RT_HEADS * RT_V

    def project(h):
        n = h.shape[1]
        q, k, v, gf, gb = jnp.split(h @ w_in, [QK, 2 * QK, 2 * QK + V, 2 * QK + 2 * V], -1)
        q = q.reshape(B, n, RT_HEADS, RT_QK) * (RT_QK ** -0.5)
        return q, k.reshape(B, n, RT_HEADS, RT_QK), v.reshape(B, n, RT_HEADS, RT_V), gf, gb

    log_gamma = jax.nn.log_sigmoid(decay_logit.astype(jnp.float32))
    qc, kc, vc, gfc, gbc = project(h_ctx)
    ql, kl, vl, gfl, gbl = project(h_lat)
    ql = rope(ql, cos[:, None], sin[:, None])
    kl = rope(kl, cos[:, None], sin[:, None])
    flip = lambda t: jnp.flip(t, 1)
    zero = jnp.zeros((B, RT_HEADS, RT_QK, RT_V), jnp.float32)

    oc_f, st_f = retention_scan(qc, kc, vc, log_gamma[0], zero)
    oc_b, st_b = retention_scan(flip(qc), flip(kc), flip(vc), log_gamma[1], zero)
    ol_f, _ = retention_scan(ql, kl, vl, log_gamma[0], st_f)
    ol_b, _ = retention_scan(flip(ql), flip(kl), flip(vl), log_gamma[1], st_b)

    def combine(of, ob, gf, gb):
        n = of.shape[1]
        yf = head_group_norm(of, gn_g[0]).reshape(B, n, V).astype(gf.dtype)
        yb = head_group_norm(ob, gn_g[1]).reshape(B, n, V).astype(gb.dtype)
        return (jax.nn.silu(gf) * yf + jax.nn.silu(gb) * yb) @ wo

    y_lat = combine(ol_f, flip(ol_b), gfl, gbl)
    y_ctx = combine(oc_f, flip(oc_b), gfc, gbc) if need_ctx else None
    return y_lat, y_ctx


def diff_attention_mixer(h_lat, h_ctx, wqkv, lam, qk_g, subln_g, wo, cos, sin, lambda_init, need_ctx):
    B, S, _ = h_lat.shape
    scale = DF_HD ** -0.5

    def project(h):
        n = h.shape[1]
        q, k, v = jnp.split(h @ wqkv, [2 * DF_HEADS * DF_HD, 4 * DF_HEADS * DF_HD], -1)
        q = rms_norm(q.reshape(B, n, DF_HEADS, 2, DF_HD), qk_g[0]) * scale
        k = rms_norm(k.reshape(B, n, DF_HEADS, 2, DF_HD), qk_g[1])
        return q, k, v.reshape(B, n, DF_HEADS, 2 * DF_HD)

    lam_f = lam.astype(jnp.float32)
    lmbda = jnp.exp(jnp.sum(lam_f[0] * lam_f[1])) - jnp.exp(jnp.sum(lam_f[2] * lam_f[3])) + lambda_init

    def attend(q, k_parts, v_parts):
        s = [jnp.einsum("bihrd,bjhrd->bhrij", q, kp).astype(jnp.float32) for kp in k_parts]
        probs = joint_softmax(s)
        return sum(jnp.einsum("bhij,bjhe->bihe", (p[:, :, 0] - lmbda * p[:, :, 1]).astype(vp.dtype), vp)
                   for p, vp in zip(probs, v_parts))

    def finish(o):
        o = rms_norm(o, subln_g) * (1.0 - lambda_init)
        return o.reshape(B, o.shape[1], DF_HEADS * 2 * DF_HD) @ wo

    qc, kc, vc = project(h_ctx)
    ql, kl, vl = project(h_lat)
    ql = rope(ql, cos[:, None, None], sin[:, None, None])
    kl = rope(kl, cos[:, None, None], sin[:, None, None])

    nb = S // BLOCK
    qb = ql.reshape(B, nb, BLOCK, DF_HEADS, 2, DF_HD).transpose(1, 0, 2, 3, 4, 5)
    o = lax.map(lambda qblk: attend(qblk, [kl, kc], [vl, vc]), qb)
    y_lat = finish(o.transpose(1, 0, 2, 3, 4).reshape(B, S, DF_HEADS, 2 * DF_HD))
    y_ctx = finish(attend(qc, [kc], [vc])) if need_ctx else None
    return y_lat, y_ctx


def conv_ffn(h, w_in, conv_w, conv_b, w_out):
    u = h @ w_in
    n = u.shape[1]
    pad = CONV_W // 2
    up = jnp.pad(u, ((0, 0), (pad, pad), (0, 0)))
    u = conv_b + sum(up[:, t:t + n] * conv_w[t] for t in range(CONV_W))
    a, b = jnp.split(u, 2, -1)
    return (jax.nn.silu(a) * b) @ w_out


def setup_inputs(seed: int = 0) -> dict:
    key = jax.random.key(seed)
    ks = iter(jax.random.split(key, 32))
    f32 = jnp.float32
    D = D_MODEL

    def nrm(shape, s):
        return jax.random.normal(next(ks), shape, f32) * s

    gamma0 = 1.0 - 2.0 ** (-5.0 - np.arange(RT_HEADS))
    decay_init = jnp.asarray(np.log(gamma0 / (1.0 - gamma0)), f32)
    ga_q = GA_HEADS * GA_HD
    return {
        "x": nrm((BATCH, SEQ, D), 1.0),
        "c": nrm((BATCH, D), 1.0),
        "ctx": nrm((BATCH, CTX_LEN, D), 1.0),
        "c_ctx": nrm((D,), 1.0),
        "mod_w": nrm((DEPTH, D, 6 * D), 0.5 * D ** -0.5),
        "mod_b": nrm((DEPTH, 6 * D), 0.02),
        "norm_g": 1.0 + nrm((DEPTH, 2, D), 0.02),
        "ffn_w_in": nrm((DEPTH, D, 2 * D_FF), D ** -0.5),
        "ffn_conv_w": nrm((DEPTH, CONV_W, 2 * D_FF), CONV_W ** -0.5),
        "ffn_conv_b": nrm((DEPTH, 2 * D_FF), 0.02),
        "ffn_w_out": nrm((DEPTH, D_FF, D), D_FF ** -0.5),
        "ga_wqkv": nrm((N_GA, D, ga_q + 2 * GA_KV * GA_HD), D ** -0.5),
        "ga_sink": nrm((N_GA, GA_HEADS), 0.5),
        "ga_qk_norm": 1.0 + nrm((N_GA, 2, GA_HD), 0.02),
        "ga_wo": nrm((N_GA, ga_q, D), ga_q ** -0.5),
        "rt_w_in": nrm((N_RT, D, 2 * RT_HEADS * RT_QK + 3 * RT_HEADS * RT_V), D ** -0.5),
        "rt_decay": decay_init[None, None, :] + nrm((N_RT, 2, RT_HEADS), 0.1),
        "rt_gn": 1.0 + nrm((N_RT, 2, RT_HEADS * RT_V), 0.02),
        "rt_wo": nrm((N_RT, RT_HEADS * RT_V, D), (RT_HEADS * RT_V) ** -0.5),
        "df_wqkv": nrm((N_DF, D, 6 * DF_HEADS * DF_HD), D ** -0.5),
        "df_lambda": nrm((N_DF, 4, DF_HD), 0.1),
        "df_qk_norm": 1.0 + nrm((N_DF, 2, DF_HD), 0.02),
        "df_subln": 1.0 + nrm((N_DF, 2 * DF_HD), 0.02),
        "df_wo": nrm((N_DF, 2 * DF_HEADS * DF_HD, D), (2 * DF_HEADS * DF_HD) ** -0.5),
    }


def reference(x, c, ctx, c_ctx, mod_w, mod_b, norm_g, ffn_w_in, ffn_conv_w, ffn_conv_b, ffn_w_out,
              ga_wqkv, ga_sink, ga_qk_norm, ga_wo, rt_w_in, rt_decay, rt_gn, rt_wo,
              df_wqkv, df_lambda, df_qk_norm, df_subln, df_wo):
    n_lat = x.shape[1]
    ROWS = n_lat // GRID_W
    ga_cos, ga_sin = axial_rope_tables(ROWS, GA_HD)
    df_cos, df_sin = axial_rope_tables(ROWS, DF_HD)
    rt_cos, rt_sin = linear_rope_tables(n_lat, RT_QK)

    h_lat, h_ctx = x, ctx
    for i in range(DEPTH):
        need_ctx = i < DEPTH - 1
        kind, j = i % N_MIXERS, i // N_MIXERS
        m_lat = [m[:, None, :] for m in adaln(c, mod_w[i], mod_b[i])]
        m_ctx = adaln(c_ctx, mod_w[i], mod_b[i])
        a_lat = modulate(rms_norm(h_lat, norm_g[i, 0]), m_lat[0], m_lat[1])
        a_ctx = modulate(rms_norm(h_ctx, norm_g[i, 0]), m_ctx[0], m_ctx[1])
        if kind == 0:
            y_lat, y_ctx = windowed_gqa_mixer(a_lat, a_ctx, ga_wqkv[j], ga_sink[j], ga_qk_norm[j], ga_wo[j],
                                              ga_cos, ga_sin, need_ctx)
        elif kind == 1:
            y_lat, y_ctx = retention_mixer(a_lat, a_ctx, rt_w_in[j], rt_decay[j], rt_gn[j], rt_wo[j],
                                           rt_cos, rt_sin, need_ctx)
        else:
            y_lat, y_ctx = diff_attention_mixer(a_lat, a_ctx, df_wqkv[j], df_lambda[j], df_qk_norm[j],
                                                df_subln[j], df_wo[j], df_cos, df_sin,
                                                0.8 - 0.6 * math.exp(-0.3 * i), need_ctx)
        h_lat = h_lat + m_lat[2] * y_lat
        f_lat = modulate(rms_norm(h_lat, norm_g[i, 1]), m_lat[3], m_lat[4])
        h_lat = h_lat + m_lat[5] * conv_ffn(f_lat, ffn_w_in[i], ffn_conv_w[i], ffn_conv_b[i], ffn_w_out[i])
        if need_ctx:
            h_ctx = h_ctx + m_ctx[2] * y_ctx
            f_ctx = modulate(rms_norm(h_ctx, norm_g[i, 1]), m_ctx[3], m_ctx[4])
            h_ctx = h_ctx + m_ctx[5] * conv_ffn(f_ctx, ffn_w_in[i], ffn_conv_w[i], ffn_conv_b[i], ffn_w_out[i])
    return h_lat
```

```python
import functools
import math
from typing import Callable, NamedTuple

import jax
import jax.numpy as jnp
from jax import lax
from jax.experimental import pallas as pl
from jax.experimental.pallas import tpu as pltpu

F32 = jnp.float32
BF16 = jnp.bfloat16

EPS = 1e-6
NEG_INF = -1e30
ROPE_THETA = 10000.0
GRID_W = 64
WINDOW = 128
N_MIXERS = 3

GA_HD = 128
GA_GROUP = 4
RT_HEADS = 8
DF_HD = 128
CONV_W = 3

LANES = 128
BF16_ROWS = 16
MOD_ROWS = 8
VMEM_LIMIT = 52 * 1024 * 1024


class _Stream(NamedTuple):
    group: int
    mod_row: Callable
    tab_block: Callable
    rope: bool


def _pick(n, candidates):
    for c in candidates:
        if n % c == 0:
            return c
    raise ValueError(f"no tile size in {candidates} divides {n}")


def _params(*sem):
    return pltpu.CompilerParams(dimension_semantics=sem, vmem_limit_bytes=VMEM_LIMIT)


def _silu(x):
    return x * (1.0 / (1.0 + jnp.exp(-x)))


def _dot(a, b):
    return jnp.dot(a, b, preferred_element_type=F32)


def _dot_nt(a, b):
    return lax.dot_general(a, b, (((1,), (1,)), ((), ())), preferred_element_type=F32)


def _dot_tn(a, b):
    return lax.dot_general(a, b, (((0,), (0,)), ((), ())), preferred_element_type=F32)


def _adaln_kernel(c_ref, w_ref, b_ref, o_ref):
    s = _silu(c_ref[...]).astype(BF16)
    o_ref[...] = _dot(s, w_ref[...].astype(BF16)) + b_ref[...]


def _adaln_table(cond, mod_w, mod_b):
    depth, d, _ = mod_w.shape
    bn = _pick(d, (1024, 512, 256, 128))
    nj = d // bn
    return pl.pallas_call(
        _adaln_kernel,
        grid=(depth, 6, nj),
        in_specs=[
            pl.BlockSpec((MOD_ROWS, d), lambda l, k, j: (0, 0)),
            pl.BlockSpec((None, d, bn), lambda l, k, j: (l, 0, k * nj + j)),
            pl.BlockSpec((None, 1, bn), lambda l, k, j: (l, 0, k * nj + j)),
        ],
        out_specs=pl.BlockSpec((None, None, MOD_ROWS, bn), lambda l, k, j: (l, k, 0, j)),
        out_shape=jax.ShapeDtypeStruct((depth, 6, MOD_ROWS, d), F32),
        compiler_params=_params("parallel", "parallel", "parallel"),
        name="adaln_table",
    )(cond, mod_w, mod_b.reshape(depth, 1, 6 * d))


def _mod_spec(layer, slot, row_fn, d):
    return pl.BlockSpec((None, None, None, 1, d), lambda i, j: (layer, slot, row_fn(i), 0, 0))


def _mod_spec_cols(layer, slot, row_fn, bn):
    return pl.BlockSpec((None, None, None, 1, bn), lambda i, j: (layer, slot, row_fn(i), 0, j))


def _norm_mod(x, g, shift, scale):
    ms = jnp.mean(x * x, axis=-1, keepdims=True)
    y = x * lax.rsqrt(ms + EPS) * g
    return y * (1.0 + scale) + shift


def _proj_kernel(h_ref, g_ref, sh_ref, sc_ref, w_ref, cos_ref, sin_ref, gain_ref, o_ref, a_scr,
                 *, plan, rope):
    j = pl.program_id(1)

    @pl.when(j == 0)
    def _():
        a_scr[...] = _norm_mod(h_ref[...], g_ref[...], sh_ref[...], sc_ref[...]).astype(BF16)

    acc = _dot(a_scr[...], w_ref[...])
    bn = acc.shape[1]

    def norm_rope_128(gain_row, scale):
        gain = gain_ref[gain_row:gain_row + 1, :] * scale
        for s in range(0, bn, LANES):
            xs = acc[:, s:s + LANES]
            ms = jnp.mean(xs * xs, axis=-1, keepdims=True)
            y = xs * lax.rsqrt(ms + EPS) * gain
            if rope:
                y = y * cos_ref[...] + pltpu.roll(y, LANES // 2, 1) * sin_ref[...]
            o_ref[:, s:s + LANES] = y.astype(o_ref.dtype)

    def rope_256(scale):
        for s in range(0, bn, 2 * LANES):
            x1 = acc[:, s:s + LANES] * scale
            x2 = acc[:, s + LANES:s + 2 * LANES] * scale
            if rope:
                c, sn = cos_ref[...], sin_ref[...]
                x1, x2 = x1 * c - x2 * sn, x1 * sn + x2 * c
            o_ref[:, s:s + LANES] = x1.astype(o_ref.dtype)
            o_ref[:, s + LANES:s + 2 * LANES] = x2.astype(o_ref.dtype)

    for lo, hi, kind, arg, scale in plan:
        @pl.when((j >= lo) & (j < hi))
        def _(kind=kind, arg=arg, scale=scale):
            if kind == "plain":
                o_ref[...] = acc.astype(o_ref.dtype)
            elif kind == "norm_rope_128":
                norm_rope_128(arg, scale)
            else:
                rope_256(scale)


def _proj(h, norm_g, mods, layer, st, w, plan, bn, cos_t, sin_t, gains):
    m, d = h.shape
    n = w.shape[1]
    bm = _pick(st.group, (1024, 512, 256))
    row_fn, tab_fn = st.mod_row, st.tab_block
    kernel = functools.partial(_proj_kernel, plan=plan, rope=st.rope)
    return pl.pallas_call(
        kernel,
        grid=(m // bm, n // bn),
        in_specs=[
            pl.BlockSpec((bm, d), lambda i, j: (i, 0)),
            pl.BlockSpec((1, d), lambda i, j: (0, 0)),
            _mod_spec(layer, 0, lambda i: row_fn(i, bm), d),
            _mod_spec(layer, 1, lambda i: row_fn(i, bm), d),
            pl.BlockSpec((d, bn), lambda i, j: (0, j)),
            pl.BlockSpec((bm, LANES), lambda i, j: (tab_fn(i, bm), 0)),
            pl.BlockSpec((bm, LANES), lambda i, j: (tab_fn(i, bm), 0)),
            pl.BlockSpec(gains.shape, lambda i, j: (0, 0)),
        ],
        out_specs=pl.BlockSpec((bm, bn), lambda i, j: (i, j)),
        out_shape=jax.ShapeDtypeStruct((m, n), BF16),
        scratch_shapes=[pltpu.VMEM((bm, d), BF16)],
        compiler_params=_params("parallel", "arbitrary"),
        name="proj",
    )(h, norm_g, mods, mods, w, cos_t, sin_t, gains)


def _oproj_kernel(y_ref, w_ref, h_ref, gate_ref, o_ref):
    o_ref[...] = h_ref[...] + gate_ref[...] * _dot(y_ref[...], w_ref[...])


def _oproj_rt_kernel(yf_ref, yb_ref, gf_ref, gb_ref, w_ref, h_ref, gate_ref, o_ref):
    c = pl.program_id(1)
    zf = _silu(gf_ref[...].astype(F32)) * yf_ref[...].astype(F32)
    zb = _silu(gb_ref[...].astype(F32)) * yb_ref[...].astype(F32)
    part = _dot((zf + zb).astype(BF16), w_ref[...])

    @pl.when(c == 0)
    def _():
        o_ref[...] = part

    @pl.when(c > 0)
    def _():
        o_ref[...] += part

    @pl.when(c == pl.num_programs(1) - 1)
    def _():
        o_ref[...] = h_ref[...] + gate_ref[...] * o_ref[...]


def _oproj(y, w, h, mods, layer, st):
    m, k = y.shape
    d = w.shape[1]
    bm = _pick(st.group, (1024, 512, 256))
    row_fn = st.mod_row
    bn = _pick(d, (512, 256, 128))
    return pl.pallas_call(
        _oproj_kernel,
        grid=(m // bm, d // bn),
        in_specs=[
            pl.BlockSpec((bm, k), lambda i, j: (i, 0)),
            pl.BlockSpec((k, bn), lambda i, j: (0, j)),
            pl.BlockSpec((bm, bn), lambda i, j: (i, j)),
            _mod_spec_cols(layer, 2, lambda i: row_fn(i, bm), bn),
        ],
        out_specs=pl.BlockSpec((bm, bn), lambda i, j: (i, j)),
        out_shape=jax.ShapeDtypeStruct((m, d), F32),
        input_output_aliases={2: 0},
        compiler_params=_params("parallel", "arbitrary"),
        name="oproj",
    )(y, w, h, mods)


def _oproj_rt(yf, yb, proj, w, h, mods, layer, st):
    m, v = yf.shape
    d = w.shape[1]
    bm = _pick(st.group, (512, 256))
    row_fn = st.mod_row
    ck = _pick(v, (512, 256, 128))
    nc = v // ck
    gcol = (proj.shape[1] // v - 2) * nc
    return pl.pallas_call(
        _oproj_rt_kernel,
        grid=(m // bm, nc),
        in_specs=[
            pl.BlockSpec((bm, ck), lambda i, c: (i, c)),
            pl.BlockSpec((bm, ck), lambda i, c: (i, c)),
            pl.BlockSpec((bm, ck), lambda i, c: (i, gcol + c)),
            pl.BlockSpec((bm, ck), lambda i, c: (i, gcol + nc + c)),
            pl.BlockSpec((ck, d), lambda i, c: (c, 0)),
            pl.BlockSpec((bm, d), lambda i, c: (i, 0)),
            _mod_spec(layer, 2, lambda i: row_fn(i, bm), d),
        ],
        out_specs=pl.BlockSpec((bm, d), lambda i, c: (i, 0)),
        out_shape=jax.ShapeDtypeStruct((m, d), F32),
        input_output_aliases={5: 0},
        compiler_params=_params("parallel", "arbitrary"),
        name="oproj_rt",
    )(yf, yb, proj, proj, w, h, mods)


def _ffn_kernel(h_ref, hp_ref, hn_ref, g_ref, sh_ref, sc_ref, gate_ref, wa_ref, wb_ref,
                cwa_ref, cwb_ref, cba_ref, cbb_ref, wo_ref, o_ref, a_scr, *, bm, seq_len):
    i = pl.program_id(0)
    c = pl.program_id(1)
    halo = BF16_ROWS

    @pl.when(c == 0)
    def _():
        def nm(x):
            return _norm_mod(x, g_ref[...], sh_ref[...], sc_ref[...])

        keep_prev = jnp.where((i * bm) % seq_len != 0, 1.0, 0.0)
        keep_next = jnp.where(((i + 1) * bm) % seq_len != 0, 1.0, 0.0)
        a_scr[0:halo, :] = (nm(hp_ref[...]) * keep_prev).astype(BF16)
        a_scr[halo:halo + bm, :] = nm(h_ref[...]).astype(BF16)
        a_scr[halo + bm:, :] = (nm(hn_ref[...]) * keep_next).astype(BF16)

    a = a_scr[...]
    rows = bm + 2 * halo

    def conv(u, cw_ref, cb_ref):
        prev = pltpu.roll(u, 1, 0)[halo:halo + bm]
        nxt = pltpu.roll(u, rows - 1, 0)[halo:halo + bm]
        cur = u[halo:halo + bm]
        return cb_ref[...] + prev * cw_ref[0:1, :] + cur * cw_ref[1:2, :] + nxt * cw_ref[2:3, :]

    ua = conv(_dot(a, wa_ref[...]), cwa_ref, cba_ref)
    ub = conv(_dot(a, wb_ref[...]), cwb_ref, cbb_ref)
    part = _dot((_silu(ua) * ub).astype(BF16), wo_ref[...])

    @pl.when(c == 0)
    def _():
        o_ref[...] = part

    @pl.when(c > 0)
    def _():
        o_ref[...] += part

    @pl.when(c == pl.num_programs(1) - 1)
    def _():
        o_ref[...] = h_ref[...] + gate_ref[...] * o_ref[...]


def _ffn(h, norm_g, mods, layer, st, w_in, conv_w, conv_b, w_out, seq_len):
    m, d = h.shape
    dff = w_out.shape[0]
    bm = _pick(seq_len, (512, 256))
    ck = _pick(dff, (512, 256, 128))
    nc = dff // ck
    halo = BF16_ROWS
    hb = bm // halo
    last = m // halo - 1
    kernel = functools.partial(_ffn_kernel, bm=bm, seq_len=seq_len)
    mrow = lambda i: st.mod_row(i, bm)
    return pl.pallas_call(
        kernel,
        grid=(m // bm, nc),
        in_specs=[
            pl.BlockSpec((bm, d), lambda i, c: (i, 0)),
            pl.BlockSpec((halo, d), lambda i, c: (jnp.maximum(i * hb - 1, 0), 0)),
            pl.BlockSpec((halo, d), lambda i, c: (jnp.minimum((i + 1) * hb, last), 0)),
            pl.BlockSpec((1, d), lambda i, c: (0, 0)),
            _mod_spec(layer, 3, mrow, d),
            _mod_spec(layer, 4, mrow, d),
            _mod_spec(layer, 5, mrow, d),
            pl.BlockSpec((d, ck), lambda i, c: (0, c)),
            pl.BlockSpec((d, ck), lambda i, c: (0, nc + c)),
            pl.BlockSpec((CONV_W, ck), lambda i, c: (0, c)),
            pl.BlockSpec((CONV_W, ck), lambda i, c: (0, nc + c)),
            pl.BlockSpec((1, ck), lambda i, c: (0, c)),
            pl.BlockSpec((1, ck), lambda i, c: (0, nc + c)),
            pl.BlockSpec((ck, d), lambda i, c: (c, 0)),
        ],
        out_specs=pl.BlockSpec((bm, d), lambda i, c: (i, 0)),
        out_shape=jax.ShapeDtypeStruct((m, d), F32),
        scratch_shapes=[pltpu.VMEM((bm + 2 * halo, d), BF16)],
        compiler_params=_params("parallel", "arbitrary"),
        name="conv_ffn",
    )(h, h, h, norm_g, mods, mods, mods, w_in, w_in, conv_w, conv_w, conv_b, conv_b, w_out)


def _ga_kernel(sink_ref, q_ref, kvc_ref, *rest, bq, seq, n_kv, local):
    if local:
        kvo_ref, kvp_ref, kvn_ref, o_ref = rest
    else:
        (o_ref,) = rest
    hd = GA_HD
    kcols = n_kv * hd
    n_ctx = kvc_ref.shape[0]
    if local:
        i = pl.program_id(1)
        n_loc = bq + 2 * WINDOW
        ql = lax.broadcasted_iota(jnp.int32, (bq, n_loc), 0)
        kl = lax.broadcasted_iota(jnp.int32, (bq, n_loc), 1) - WINDOW
        pos = i * bq + kl
        ok = (jnp.abs(ql - kl) <= WINDOW) & (pos >= 0) & (pos < seq)
        bias = jnp.concatenate(
            [jnp.where(ok, 0.0, NEG_INF).astype(F32), jnp.zeros((bq, n_ctx), F32)], axis=1)
    for n in range(n_kv):
        ks, vs = slice(n * hd, (n + 1) * hd), slice(kcols + n * hd, kcols + (n + 1) * hd)
        if local:
            k_all = jnp.concatenate([kvp_ref[:, ks], kvo_ref[:, ks], kvn_ref[:, ks], kvc_ref[:, ks]], axis=0)
            v_all = jnp.concatenate([kvp_ref[:, vs], kvo_ref[:, vs], kvn_ref[:, vs], kvc_ref[:, vs]], axis=0)
        else:
            k_all, v_all = kvc_ref[:, ks], kvc_ref[:, vs]
        for g in range(GA_GROUP):
            head = n * GA_GROUP + g
            sink = sink_ref[head]
            s = _dot_nt(q_ref[:, head * hd:(head + 1) * hd], k_all)
            if local:
                s = s + bias
            m = jnp.maximum(jnp.max(s, axis=-1, keepdims=True), sink)
            p = jnp.exp(s - m)
            den = jnp.sum(p, axis=-1, keepdims=True) + jnp.exp(sink - m)
            o = _dot(p.astype(BF16), v_all) / den
            o_ref[:, head * hd:(head + 1) * hd] = o.astype(o_ref.dtype)


def _ga_attention(qkv_lat, qkv_ctx, sink, batch, seq, n_ctx, need_ctx):
    n_heads = sink.shape[0]
    qcols = n_heads * GA_HD
    n_kv = n_heads // GA_GROUP
    kvw = 2 * n_kv * GA_HD
    kvblk = qcols // kvw
    assert qcols % kvw == 0
    bq = _pick(seq, (512, 256, 128))
    nq = seq // bq
    wb = bq // WINDOW
    last_w = batch * seq // WINDOW - 1
    smem = pl.BlockSpec(memory_space=pltpu.SMEM)
    lat = pl.pallas_call(
        functools.partial(_ga_kernel, bq=bq, seq=seq, n_kv=n_kv, local=True),
        grid=(batch, nq),
        in_specs=[
            smem,
            pl.BlockSpec((bq, qcols), lambda b, i: (b * nq + i, 0)),
            pl.BlockSpec((n_ctx, kvw), lambda b, i: (b, kvblk)),
            pl.BlockSpec((bq, kvw), lambda b, i: (b * nq + i, kvblk)),
            pl.BlockSpec((WINDOW, kvw), lambda b, i: (jnp.maximum((b * nq + i) * wb - 1, 0), kvblk)),
            pl.BlockSpec((WINDOW, kvw), lambda b, i: (jnp.minimum((b * nq + i + 1) * wb, last_w), kvblk)),
        ],
        out_specs=pl.BlockSpec((bq, qcols), lambda b, i: (b * nq + i, 0)),
        out_shape=jax.ShapeDtypeStruct((batch * seq, qcols), BF16),
        compiler_params=_params("parallel", "parallel"),
        name="ga_attention_latent",
    )(sink, qkv_lat, qkv_ctx, qkv_lat, qkv_lat, qkv_lat)
    ctx = None
    if need_ctx:
        ctx = pl.pallas_call(
            functools.partial(_ga_kernel, bq=n_ctx, seq=n_ctx, n_kv=n_kv, local=False),
            grid=(batch,),
            in_specs=[
                smem,
                pl.BlockSpec((n_ctx, qcols), lambda b: (b, 0)),
                pl.BlockSpec((n_ctx, kvw), lambda b: (b, kvblk)),
            ],
            out_specs=pl.BlockSpec((n_ctx, qcols), lambda b: (b, 0)),
            out_shape=jax.ShapeDtypeStruct((batch * n_ctx, qcols), BF16),
            compiler_params=_params("parallel"),
            name="ga_attention_context",
        )(sink, qkv_ctx, qkv_ctx)
    return lat, ctx


def _df_kernel(lam_ref, g_ref, q_ref, *rest, lambda_init, chunks):
    o_ref = rest[-1]
    kv_refs = rest[:-1]
    hd = DF_HD
    bq = q_ref.shape[0]
    lam = lam_ref[...]
    lmbda = (jnp.exp(jnp.sum(lam[0:1] * lam[1:2], axis=-1, keepdims=True))
             - jnp.exp(jnp.sum(lam[2:3] * lam[3:4], axis=-1, keepdims=True)) + lambda_init)
    qs = (q_ref[:, 0:hd], q_ref[:, hd:2 * hd])

    def init():
        return (jnp.full((bq, 1), NEG_INF, F32), jnp.zeros((bq, 1), F32), jnp.zeros((bq, 2 * hd), F32))

    carry = (init(), init())
    for seg, ck in enumerate(chunks):
        k_ref, v_ref = kv_refs[2 * seg], kv_refs[2 * seg + 1]

        def body(c, carry, k_ref=k_ref, v_ref=v_ref, ck=ck):
            start = pl.multiple_of(c * ck, ck)
            kblk = k_ref[pl.ds(start, ck), :]
            vblk = v_ref[pl.ds(start, ck), :]
            new = []
            for r in range(2):
                m_old, l_old, acc = carry[r]
                s = _dot_nt(qs[r], kblk[:, r * hd:(r + 1) * hd])
                m_new = jnp.maximum(m_old, jnp.max(s, axis=-1, keepdims=True))
                alpha = jnp.exp(m_old - m_new)
                p = jnp.exp(s - m_new)
                l_new = alpha * l_old + jnp.sum(p, axis=-1, keepdims=True)
                acc = alpha * acc + _dot(p.astype(BF16), vblk)
                new.append((m_new, l_new, acc))
            return tuple(new)

        carry = lax.fori_loop(0, k_ref.shape[0] // ck, body, carry)
    (_, l0, acc0), (_, l1, acc1) = carry
    o = acc0 / l0 - lmbda * (acc1 / l1)
    ms = jnp.mean(o * o, axis=-1, keepdims=True)
    y = o * lax.rsqrt(ms + EPS) * g_ref[...] * (1.0 - lambda_init)
    o_ref[...] = y.astype(o_ref.dtype)


def _df_attention(qkv_lat, qkv_ctx, lam, subln_g, batch, seq, n_ctx, lambda_init, need_ctx):
    hw = 2 * DF_HD
    n_heads = qkv_lat.shape[1] // (3 * hw)
    bq = _pick(seq, (512, 256, 128))
    nq = seq // bq
    ck_lat = _pick(seq, (512, 256, 128))
    ck_ctx = _pick(n_ctx, (512, 256, 128))
    g2 = subln_g.reshape(1, hw)
    const = lambda shape: pl.BlockSpec(shape, lambda *_: (0,) * len(shape))
    lat = pl.pallas_call(
        functools.partial(_df_kernel, lambda_init=lambda_init, chunks=(ck_ctx, ck_lat)),
        grid=(batch, n_heads, nq),
        in_specs=[
            const(lam.shape),
            const((1, hw)),
            pl.BlockSpec((bq, hw), lambda b, h, i: (b * nq + i, h)),
            pl.BlockSpec((n_ctx, hw), lambda b, h, i: (b, n_heads + h)),
            pl.BlockSpec((n_ctx, hw), lambda b, h, i: (b, 2 * n_heads + h)),
            pl.BlockSpec((seq, hw), lambda b, h, i: (b, n_heads + h)),
            pl.BlockSpec((seq, hw), lambda b, h, i: (b, 2 * n_heads + h)),
        ],
        out_specs=pl.BlockSpec((bq, hw), lambda b, h, i: (b * nq + i, h)),
        out_shape=jax.ShapeDtypeStruct((batch * seq, n_heads * hw), BF16),
        compiler_params=_params("parallel", "parallel", "arbitrary"),
        name="df_attention_latent",
    )(lam, g2, qkv_lat, qkv_ctx, qkv_ctx, qkv_lat, qkv_lat)
    ctx = None
    if need_ctx:
        ctx = pl.pallas_call(
            functools.partial(_df_kernel, lambda_init=lambda_init, chunks=(ck_ctx,)),
            grid=(batch, n_heads),
            in_specs=[
                const(lam.shape),
                const((1, hw)),
                pl.BlockSpec((n_ctx, hw), lambda b, h: (b, h)),
                pl.BlockSpec((n_ctx, hw), lambda b, h: (b, n_heads + h)),
                pl.BlockSpec((n_ctx, hw), lambda b, h: (b, 2 * n_heads + h)),
            ],
            out_specs=pl.BlockSpec((n_ctx, hw), lambda b, h: (b, h)),
            out_shape=jax.ShapeDtypeStruct((batch * n_ctx, n_heads * hw), BF16),
            compiler_params=_params("parallel", "parallel"),
            name="df_attention_context",
        )(lam, g2, qkv_ctx, qkv_ctx, qkv_ctx)
    return lat, ctx


def _rt_kernel(lg_ref, gn_ref, *refs, n_heads, chunk, ctx_steps):
    ctx_in, lat_in = refs[0:6], refs[6:12]
    ctx_out, lat_out = refs[12:14], refs[14:16]
    st_scr = refs[16]
    head0 = pl.program_id(1) * n_heads
    s_idx = pl.program_id(2)
    dk = ctx_in[0].shape[1] // n_heads
    dv = ctx_in[2].shape[1] // n_heads

    @pl.when(s_idx == 0)
    def _():
        st_scr[...] = jnp.zeros_like(st_scr)

    def step(ins, outs):
        qf_ref, kf_ref, vf_ref, qb_ref, kb_ref, vb_ref = ins
        of_ref, ob_ref = outs
        row = lax.broadcasted_iota(jnp.int32, (chunk, chunk), 0).astype(F32)
        col = lax.broadcasted_iota(jnp.int32, (chunk, chunk), 1).astype(F32)
        pos = lax.broadcasted_iota(jnp.int32, (chunk, 1), 0).astype(F32)
        dirs = (
            (qf_ref, kf_ref, vf_ref, of_ref, row - col, pos + 1.0, chunk - 1.0 - pos),
            (qb_ref, kb_ref, vb_ref, ob_ref, col - row, chunk - pos, pos),
        )
        for d, (q_ref, k_ref, v_ref, o_ref, rel, q_pow, k_pow) in enumerate(dirs):
            for h in range(n_heads):
                lg = lg_ref[d, head0 + h]
                decay = jnp.where(rel >= 0, jnp.exp(lg * jnp.maximum(rel, 0.0)), 0.0)
                q_decay = jnp.exp(lg * q_pow)
                k_decay = jnp.exp(lg * k_pow)
                chunk_decay = jnp.exp(lg * chunk)
                q = q_ref[:, h * dk:(h + 1) * dk]
                k = k_ref[:, h * dk:(h + 1) * dk]
                v = v_ref[:, h * dv:(h + 1) * dv]
                state = st_scr[d, h]
                inner = _dot_nt(q, k) * decay
                o = _dot(inner.astype(BF16), v) + _dot(q, state.astype(BF16)) * q_decay
                kd = (k.astype(F32) * k_decay).astype(BF16)
                st_scr[d, h] = state * chunk_decay + _dot_tn(kd, v)
                mu = jnp.mean(o, axis=-1, keepdims=True)
                dev = o - mu
                var = jnp.mean(dev * dev, axis=-1, keepdims=True)
                y = dev * lax.rsqrt(var + EPS) * gn_ref[d:d + 1, h * dv:(h + 1) * dv]
                o_ref[:, h * dv:(h + 1) * dv] = y.astype(o_ref.dtype)

    @pl.when(s_idx < ctx_steps)
    def _():
        step(ctx_in, ctx_out)

    @pl.when(s_idx >= ctx_steps)
    def _():
        step(lat_in, lat_out)


def _retention(proj_lat, proj_ctx, log_gamma, gn_g, batch, seq, n_ctx):
    n_heads = log_gamma.shape[1]
    qk = proj_lat.shape[1] // 8
    v = 2 * qk
    chunk = _pick(math.gcd(seq, n_ctx), (256, 128))
    ncc, nlc = n_ctx // chunk, seq // chunk
    dk, dv = qk // n_heads, v // n_heads
    groups = 2
    hpg = n_heads // groups
    qkw, vw = hpg * dk, hpg * dv

    def ctx_f(b, s):
        return b * ncc + jnp.minimum(s, ncc - 1)

    def ctx_b(b, s):
        return b * ncc + (ncc - 1 - jnp.minimum(s, ncc - 1))

    def lat_f(b, s):
        return b * nlc + jnp.maximum(s - ncc, 0)

    def lat_b(b, s):
        return b * nlc + (nlc - 1 - jnp.maximum(s - ncc, 0))

    def in_specs(row):
        return [
            pl.BlockSpec((chunk, qkw), lambda b, g, s: (row(b, s), g)),
            pl.BlockSpec((chunk, qkw), lambda b, g, s: (row(b, s), groups + g)),
            pl.BlockSpec((chunk, vw), lambda b, g, s: (row(b, s), groups + g)),
        ]

    def out_spec(row):
        return pl.BlockSpec((chunk, vw), lambda b, g, s: (row(b, s), g))

    out_ctx = jax.ShapeDtypeStruct((batch * n_ctx, v), BF16)
    out_lat = jax.ShapeDtypeStruct((batch * seq, v), BF16)
    yf_ctx, yb_ctx, yf_lat, yb_lat = pl.pallas_call(
        functools.partial(_rt_kernel, n_heads=hpg, chunk=chunk, ctx_steps=ncc),
        grid=(batch, groups, ncc + nlc),
        in_specs=[pl.BlockSpec(memory_space=pltpu.SMEM), pl.BlockSpec((2, vw), lambda b, g, s: (0, g))]
        + in_specs(ctx_f) + in_specs(ctx_b) + in_specs(lat_f) + in_specs(lat_b),
        out_specs=[out_spec(ctx_f), out_spec(ctx_b), out_spec(lat_f), out_spec(lat_b)],
        out_shape=[out_ctx, out_ctx, out_lat, out_lat],
        scratch_shapes=[pltpu.VMEM((2, hpg, dk, dv), F32)],
        compiler_params=_params("parallel", "parallel", "arbitrary"),
        name="retention",
    )(log_gamma, gn_g, *([proj_ctx] * 6), *([proj_lat] * 6))
    return (yf_lat, yb_lat), (yf_ctx, yb_ctx)


def _axial_tables(rows_count, head_dim):
    rows = jnp.repeat(jnp.arange(rows_count), GRID_W).astype(F32)
    cols = jnp.tile(jnp.arange(GRID_W), rows_count).astype(F32)
    n_freq = head_dim // 4
    inv = ROPE_THETA ** (-jnp.arange(n_freq, dtype=F32) / n_freq)
    ang = jnp.concatenate([rows[:, None] * inv, cols[:, None] * inv], -1)
    cos, sin = jnp.cos(ang), jnp.sin(ang)
    return jnp.concatenate([cos, cos], -1), jnp.concatenate([-sin, sin], -1)


def _linear_tables(n_tokens, head_dim):
    half = head_dim // 2
    inv = ROPE_THETA ** (-jnp.arange(half, dtype=F32) / half)
    ang = jnp.arange(n_tokens, dtype=F32)[:, None] * inv
    return jnp.cos(ang), jnp.sin(ang)


def kernel(x, c, ctx, c_ctx, mod_w, mod_b, norm_g, ffn_w_in, ffn_conv_w, ffn_conv_b, ffn_w_out, ga_wqkv, ga_sink, ga_qk_norm, ga_wo, rt_w_in, rt_decay, rt_gn, rt_wo, df_wqkv, df_lambda, df_qk_norm, df_subln, df_wo):
    batch, seq, d = x.shape
    n_ctx = ctx.shape[1]
    depth = mod_w.shape[0]
    assert batch + 1 <= MOD_ROWS and seq % GRID_W == 0

    cond = jnp.zeros((MOD_ROWS, d), F32).at[0].set(c_ctx).at[1:batch + 1].set(c)
    mods = _adaln_table(cond, mod_w, mod_b).reshape(depth, 6, MOD_ROWS, 1, d)

    ax_cos, ax_sin = _axial_tables(seq // GRID_W, GA_HD)
    assert DF_HD == GA_HD
    rt_dk = d // RT_HEADS
    ln_cos, ln_sin = _linear_tables(seq, rt_dk)
    ones2 = jnp.ones((2, LANES), F32)

    h_lat = x.reshape(batch * seq, d)
    h_ctx = ctx.reshape(batch * n_ctx, d)

    lat = _Stream(seq, lambda i, bm: 1 + i // (seq // bm), lambda i, bm: i % (seq // bm), True)
    cst = _Stream(batch * n_ctx, lambda i, bm: 0, lambda i, bm: 0, False)

    def tiles(widths, bn):
        out, start = [], 0
        for w in widths:
            assert w % bn == 0
            out.append((start // bn, (start + w) // bn))
            start += w
        return out

    for i in range(depth):
        need_ctx = i < depth - 1
        kind, j = i % N_MIXERS, i // N_MIXERS
        g1 = norm_g[i, 0].reshape(1, d)
        g2 = norm_g[i, 1].reshape(1, d)

        if kind == 0:
            w = ga_wqkv[j].astype(BF16)
            qcols = ga_sink.shape[1] * GA_HD
            kcols = (w.shape[1] - qcols) // 2
            bn = 512
            (q0, q1), (k0, k1), (v0, v1) = tiles((qcols, kcols, kcols), bn)
            plan = ((q0, q1, "norm_rope_128", 0, GA_HD ** -0.5), (k0, k1, "norm_rope_128", 1, 1.0),
                    (v0, v1, "plain", 0, 1.0))
            cos_t, sin_t, gains = ax_cos, ax_sin, ga_qk_norm[j]
        elif kind == 1:
            w = rt_w_in[j].astype(BF16)
            qk = d
            bn = 512
            (q0, q1), (k0, k1), (v0, v1) = tiles((qk, qk, w.shape[1] - 2 * qk), bn)
            plan = ((q0, q1, "rope_256", 0, rt_dk ** -0.5), (k0, k1, "rope_256", 0, 1.0),
                    (v0, v1, "plain", 0, 1.0))
            cos_t, sin_t, gains = ln_cos, ln_sin, ones2
        else:
            w = df_wqkv[j].astype(BF16)
            qcols = w.shape[1] // 3
            bn = 512
            (q0, q1), (k0, k1), (v0, v1) = tiles((qcols, qcols, qcols), bn)
            plan = ((q0, q1, "norm_rope_128", 0, DF_HD ** -0.5), (k0, k1, "norm_rope_128", 1, 1.0),
                    (v0, v1, "plain", 0, 1.0))
            cos_t, sin_t, gains = ax_cos, ax_sin, df_qk_norm[j]

        proj = {"lat": _proj(h_lat, g1, mods, i, lat, w, plan, bn, cos_t, sin_t, gains),
                "ctx": _proj(h_ctx, g1, mods, i, cst, w, plan, bn, cos_t, sin_t, gains)}

        if kind == 0:
            y_lat, y_ctx = _ga_attention(proj["lat"], proj["ctx"], ga_sink[j], batch, seq, n_ctx, need_ctx)
            wo = ga_wo[j].astype(BF16)
            h_lat = _oproj(y_lat, wo, h_lat, mods, i, lat)
            if need_ctx:
                h_ctx = _oproj(y_ctx, wo, h_ctx, mods, i, cst)
        elif kind == 1:
            log_gamma = jax.nn.log_sigmoid(rt_decay[j].astype(F32))
            y_lat, y_ctx = _retention(proj["lat"], proj["ctx"], log_gamma, rt_gn[j], batch, seq, n_ctx)
            wo = rt_wo[j].astype(BF16)
            h_lat = _oproj_rt(y_lat[0], y_lat[1], proj["lat"], wo, h_lat, mods, i, lat)
            if need_ctx:
                h_ctx = _oproj_rt(y_ctx[0], y_ctx[1], proj["ctx"], wo, h_ctx, mods, i, cst)
        else:
            lambda_init = 0.8 - 0.6 * math.exp(-0.3 * i)
            y_lat, y_ctx = _df_attention(proj["lat"], proj["ctx"], df_lambda[j], df_subln[j], batch, seq, n_ctx,
                                         lambda_init, need_ctx)
            wo = df_wo[j].astype(BF16)
            h_lat = _oproj(y_lat, wo, h_lat, mods, i, lat)
            if need_ctx:
                h_ctx = _oproj(y_ctx, wo, h_ctx, mods, i, cst)

        w_in = ffn_w_in[i].astype(BF16)
        w_out = ffn_w_out[i].astype(BF16)
        cb = ffn_conv_b[i].reshape(1, -1)
        h_lat = _ffn(h_lat, g2, mods, i, lat, w_in, ffn_conv_w[i], cb, w_out, seq)
        if need_ctx:
            h_ctx = _ffn(h_ctx, g2, mods, i, cst, w_in, ffn_conv_w[i], cb, w_out, n_ctx)

    return h_lat.reshape(batch, seq, d)
```

```python
import functools
import math
from typing import Callable, NamedTuple

import jax
import jax.numpy as jnp
from jax import lax
from jax.experimental import pallas as pl
from jax.experimental.pallas import tpu as pltpu

F32 = jnp.float32
BF16 = jnp.bfloat16

EPS = 1e-6
NEG_INF = -1e30
LOG2E = math.log2(math.e)
ROPE_THETA = 10000.0
GRID_W = 64
WINDOW = 128
N_MIXERS = 3

GA_HD = 128
GA_GROUP = 4
RT_HEADS = 8
DF_HD = 128
CONV_W = 3

LANES = 128
MXU_COLS = 256
BF16_ROWS = 16
MOD_ROWS = 8
VMEM_LIMIT = 52 * 1024 * 1024


class _Stream(NamedTuple):
    group: int
    mod_row: Callable
    tab_block: Callable
    rope: bool


def _pick(n, candidates):
    for c in candidates:
        if n % c == 0:
            return c
    raise ValueError(f"no tile size in {candidates} divides {n}")


def _params(*sem):
    return pltpu.CompilerParams(dimension_semantics=sem, vmem_limit_bytes=VMEM_LIMIT)


def _silu(x):
    return x * (1.0 / (1.0 + jnp.exp(-x)))


def _dot(a, b):
    return jnp.dot(a, b, preferred_element_type=F32)


def _dot_nt(a, b):
    return lax.dot_general(a, b, (((1,), (1,)), ((), ())), preferred_element_type=F32)


def _dot_tn(a, b):
    return lax.dot_general(a, b, (((0,), (0,)), ((), ())), preferred_element_type=F32)


def _adaln_kernel(c_ref, w_ref, b_ref, o_ref):
    s = _silu(c_ref[...]).astype(BF16)
    o_ref[...] = _dot(s, w_ref[...].astype(BF16)) + b_ref[...]


def _adaln_table(cond, mod_w, mod_b):
    depth, d, _ = mod_w.shape
    bn = _pick(d, (1024, 512, 256, 128))
    nj = d // bn
    return pl.pallas_call(
        _adaln_kernel,
        grid=(depth, 6, nj),
        in_specs=[
            pl.BlockSpec((MOD_ROWS, d), lambda l, k, j: (0, 0)),
            pl.BlockSpec((None, d, bn), lambda l, k, j: (l, 0, k * nj + j)),
            pl.BlockSpec((None, 1, bn), lambda l, k, j: (l, 0, k * nj + j)),
        ],
        out_specs=pl.BlockSpec((None, None, MOD_ROWS, bn), lambda l, k, j: (l, k, 0, j)),
        out_shape=jax.ShapeDtypeStruct((depth, 6, MOD_ROWS, d), F32),
        compiler_params=_params("parallel", "parallel", "parallel"),
        name="adaln_table",
    )(cond, mod_w, mod_b.reshape(depth, 1, 6 * d))


def _mod_spec(layer, slot, row_fn, d):
    return pl.BlockSpec((None, None, None, 1, d), lambda i, j: (layer, slot, row_fn(i), 0, 0))


def _mod_spec_cols(layer, slot, row_fn, bn):
    return pl.BlockSpec((None, None, None, 1, bn), lambda i, j: (layer, slot, row_fn(i), 0, j))


def _norm_mod(x, g, shift, scale):
    ms = jnp.mean(x * x, axis=-1, keepdims=True)
    y = x * lax.rsqrt(ms + EPS) * g
    return y * (1.0 + scale) + shift


def _proj_kernel(h_ref, g_ref, sh_ref, sc_ref, w_ref, cos_ref, sin_ref, gain_ref, o_ref, a_scr,
                 *, plan, rope):
    j = pl.program_id(1)

    @pl.when(j == 0)
    def _():
        a_scr[...] = _norm_mod(h_ref[...], g_ref[...], sh_ref[...], sc_ref[...]).astype(BF16)

    acc = _dot(a_scr[...], w_ref[...])
    bn = acc.shape[1]

    def norm_rope_128(gain_row, scale):
        gain = gain_ref[gain_row:gain_row + 1, :] * scale
        for s in range(0, bn, LANES):
            xs = acc[:, s:s + LANES]
            ms = jnp.mean(xs * xs, axis=-1, keepdims=True)
            y = xs * lax.rsqrt(ms + EPS) * gain
            if rope:
                y = y * cos_ref[...] + pltpu.roll(y, LANES // 2, 1) * sin_ref[...]
            o_ref[:, s:s + LANES] = y.astype(o_ref.dtype)

    def rope_256(scale):
        for s in range(0, bn, 2 * LANES):
            x1 = acc[:, s:s + LANES] * scale
            x2 = acc[:, s + LANES:s + 2 * LANES] * scale
            if rope:
                c, sn = cos_ref[...], sin_ref[...]
                x1, x2 = x1 * c - x2 * sn, x1 * sn + x2 * c
            o_ref[:, s:s + LANES] = x1.astype(o_ref.dtype)
            o_ref[:, s + LANES:s + 2 * LANES] = x2.astype(o_ref.dtype)

    for lo, hi, kind, arg, scale in plan:
        @pl.when((j >= lo) & (j < hi))
        def _(kind=kind, arg=arg, scale=scale):
            if kind == "plain":
                o_ref[...] = acc.astype(o_ref.dtype)
            elif kind == "norm_rope_128":
                norm_rope_128(arg, scale)
            else:
                rope_256(scale)


def _proj(h, norm_g, mods, layer, st, w, plan, bn, cos_t, sin_t, gains):
    m, d = h.shape
    n = w.shape[1]
    bm = _pick(st.group, (1024, 512, 256))
    row_fn, tab_fn = st.mod_row, st.tab_block
    kernel = functools.partial(_proj_kernel, plan=plan, rope=st.rope)
    return pl.pallas_call(
        kernel,
        grid=(m // bm, n // bn),
        in_specs=[
            pl.BlockSpec((bm, d), lambda i, j: (i, 0)),
            pl.BlockSpec((1, d), lambda i, j: (0, 0)),
            _mod_spec(layer, 0, lambda i: row_fn(i, bm), d),
            _mod_spec(layer, 1, lambda i: row_fn(i, bm), d),
            pl.BlockSpec((d, bn), lambda i, j: (0, j)),
            pl.BlockSpec((bm, LANES), lambda i, j: (tab_fn(i, bm), 0)),
            pl.BlockSpec((bm, LANES), lambda i, j: (tab_fn(i, bm), 0)),
            pl.BlockSpec(gains.shape, lambda i, j: (0, 0)),
        ],
        out_specs=pl.BlockSpec((bm, bn), lambda i, j: (i, j)),
        out_shape=jax.ShapeDtypeStruct((m, n), BF16),
        scratch_shapes=[pltpu.VMEM((bm, d), BF16)],
        compiler_params=_params("parallel", "arbitrary"),
        name="proj",
    )(h, norm_g, mods, mods, w, cos_t, sin_t, gains)


def _oproj_kernel(y_ref, w_ref, h_ref, gate_ref, o_ref):
    o_ref[...] = h_ref[...] + gate_ref[...] * _dot(y_ref[...], w_ref[...])


def _oproj_rt_kernel(yf_ref, yb_ref, gf_ref, gb_ref, w_ref, h_ref, gate_ref, o_ref):
    c = pl.program_id(1)
    zf = _silu(gf_ref[...].astype(F32)) * yf_ref[...].astype(F32)
    zb = _silu(gb_ref[...].astype(F32)) * yb_ref[...].astype(F32)
    part = _dot((zf + zb).astype(BF16), w_ref[...])

    @pl.when(c == 0)
    def _():
        o_ref[...] = part

    @pl.when(c > 0)
    def _():
        o_ref[...] += part

    @pl.when(c == pl.num_programs(1) - 1)
    def _():
        o_ref[...] = h_ref[...] + gate_ref[...] * o_ref[...]


def _oproj(y, w, h, mods, layer, st):
    m, k = y.shape
    d = w.shape[1]
    bm = _pick(st.group, (1024, 512, 256))
    row_fn = st.mod_row
    bn = _pick(d, (512, 256, 128))
    return pl.pallas_call(
        _oproj_kernel,
        grid=(m // bm, d // bn),
        in_specs=[
            pl.BlockSpec((bm, k), lambda i, j: (i, 0)),
            pl.BlockSpec((k, bn), lambda i, j: (0, j)),
            pl.BlockSpec((bm, bn), lambda i, j: (i, j)),
            _mod_spec_cols(layer, 2, lambda i: row_fn(i, bm), bn),
        ],
        out_specs=pl.BlockSpec((bm, bn), lambda i, j: (i, j)),
        out_shape=jax.ShapeDtypeStruct((m, d), F32),
        input_output_aliases={2: 0},
        compiler_params=_params("parallel", "arbitrary"),
        name="oproj",
    )(y, w, h, mods)


def _oproj_rt(yf, yb, proj, w, h, mods, layer, st):
    m, v = yf.shape
    d = w.shape[1]
    bm = _pick(st.group, (512, 256))
    row_fn = st.mod_row
    ck = _pick(v, (512, 256, 128))
    nc = v // ck
    gcol = (proj.shape[1] // v - 2) * nc
    return pl.pallas_call(
        _oproj_rt_kernel,
        grid=(m // bm, nc),
        in_specs=[
            pl.BlockSpec((bm, ck), lambda i, c: (i, c)),
            pl.BlockSpec((bm, ck), lambda i, c: (i, c)),
            pl.BlockSpec((bm, ck), lambda i, c: (i, gcol + c)),
            pl.BlockSpec((bm, ck), lambda i, c: (i, gcol + nc + c)),
            pl.BlockSpec((ck, d), lambda i, c: (c, 0)),
            pl.BlockSpec((bm, d), lambda i, c: (i, 0)),
            _mod_spec(layer, 2, lambda i: row_fn(i, bm), d),
        ],
        out_specs=pl.BlockSpec((bm, d), lambda i, c: (i, 0)),
        out_shape=jax.ShapeDtypeStruct((m, d), F32),
        input_output_aliases={5: 0},
        compiler_params=_params("parallel", "arbitrary"),
        name="oproj_rt",
    )(yf, yb, proj, proj, w, h, mods)


def _ffn_kernel(h_ref, hp_ref, hn_ref, g_ref, sh_ref, sc_ref, gate_ref, wa_ref, wb_ref,
                cwa_ref, cwb_ref, cba_ref, cbb_ref, wo_ref, o_ref, a_scr, u_scr, *, bm, seq_len):
    i = pl.program_id(0)
    c = pl.program_id(1)
    halo = BF16_ROWS

    @pl.when(c == 0)
    def _():
        def nm(x):
            return _norm_mod(x, g_ref[...], sh_ref[...], sc_ref[...])

        keep_prev = jnp.where((i * bm) % seq_len != 0, 1.0, 0.0)
        keep_next = jnp.where(((i + 1) * bm) % seq_len != 0, 1.0, 0.0)
        a_scr[0:halo, :] = (nm(hp_ref[...]) * keep_prev).astype(BF16)
        a_scr[halo:halo + bm, :] = nm(h_ref[...]).astype(BF16)
        a_scr[halo + bm:, :] = (nm(hn_ref[...]) * keep_next).astype(BF16)
        o_ref[...] = jnp.zeros_like(o_ref)

    a = a_scr[...]

    ck = wa_ref.shape[1]
    sub = u_scr.shape[2]
    n_sub = ck // sub
    for s in range(n_sub):
        cols = slice(s * sub, (s + 1) * sub)
        u_scr[2 * s] = _dot(a, wa_ref[:, cols])
        u_scr[2 * s + 1] = _dot(a, wb_ref[:, cols])

    def conv(idx, cw_ref, cb_ref, cols):
        prev = u_scr[idx, halo - 1:halo - 1 + bm, :]
        cur = u_scr[idx, halo:halo + bm, :]
        nxt = u_scr[idx, halo + 1:halo + 1 + bm, :]
        return (cb_ref[:, cols] + prev * cw_ref[0:1, cols] + cur * cw_ref[1:2, cols]
                + nxt * cw_ref[2:3, cols])

    part = None
    for s in range(n_sub):
        cols = slice(s * sub, (s + 1) * sub)
        ua = conv(2 * s, cwa_ref, cba_ref, cols)
        ub = conv(2 * s + 1, cwb_ref, cbb_ref, cols)
        d = _dot((_silu(ua) * ub).astype(BF16), wo_ref[cols, :])
        part = d if part is None else part + d
    o_ref[...] += part

    @pl.when(c == pl.num_programs(1) - 1)
    def _():
        o_ref[...] = h_ref[...] + gate_ref[...] * o_ref[...]


def _ffn(h, norm_g, mods, layer, st, w_in, conv_w, conv_b, w_out, seq_len):
    m, d = h.shape
    dff = w_out.shape[0]
    bm = _pick(seq_len, (512, 256))
    ck = _pick(dff, (512, 256, 128))
    nc = dff // ck
    halo = BF16_ROWS
    hb = bm // halo
    last = m // halo - 1
    kernel = functools.partial(_ffn_kernel, bm=bm, seq_len=seq_len)
    sub = min(ck, MXU_COLS)
    mrow = lambda i: st.mod_row(i, bm)
    return pl.pallas_call(
        kernel,
        grid=(m // bm, nc),
        in_specs=[
            pl.BlockSpec((bm, d), lambda i, c: (i, 0)),
            pl.BlockSpec((halo, d), lambda i, c: (jnp.maximum(i * hb - 1, 0), 0)),
            pl.BlockSpec((halo, d), lambda i, c: (jnp.minimum((i + 1) * hb, last), 0)),
            pl.BlockSpec((1, d), lambda i, c: (0, 0)),
            _mod_spec(layer, 3, mrow, d),
            _mod_spec(layer, 4, mrow, d),
            _mod_spec(layer, 5, mrow, d),
            pl.BlockSpec((d, ck), lambda i, c: (0, c)),
            pl.BlockSpec((d, ck), lambda i, c: (0, nc + c)),
            pl.BlockSpec((CONV_W, ck), lambda i, c: (0, c)),
            pl.BlockSpec((CONV_W, ck), lambda i, c: (0, nc + c)),
            pl.BlockSpec((1, ck), lambda i, c: (0, c)),
            pl.BlockSpec((1, ck), lambda i, c: (0, nc + c)),
            pl.BlockSpec((ck, d), lambda i, c: (c, 0)),
        ],
        out_specs=pl.BlockSpec((bm, d), lambda i, c: (i, 0)),
        out_shape=jax.ShapeDtypeStruct((m, d), F32),
        scratch_shapes=[pltpu.VMEM((bm + 2 * halo, d), BF16),
                        pltpu.VMEM((2 * (ck // sub), bm + 2 * halo, sub), F32)],
        compiler_params=_params("parallel", "arbitrary"),
        name="conv_ffn",
    )(h, h, h, norm_g, mods, mods, mods, w_in, w_in, conv_w, conv_w, conv_b, conv_b, w_out)


def _ga_kernel(sink_ref, q_ref, kvc_ref, *rest, bq, seq, n_kv, local):
    if local:
        kvo_ref, kvp_ref, kvn_ref, o_ref = rest
    else:
        (o_ref,) = rest
    hd = GA_HD
    kcols = n_kv * hd
    n_ctx = kvc_ref.shape[0]
    if local:
        i = pl.program_id(1)
        n_loc = bq + 2 * WINDOW
        ql = lax.broadcasted_iota(jnp.int32, (bq, n_loc), 0)
        kl = lax.broadcasted_iota(jnp.int32, (bq, n_loc), 1) - WINDOW
        pos = i * bq + kl
        ok = (jnp.abs(ql - kl) <= WINDOW) & (pos >= 0) & (pos < seq)
        bias = jnp.concatenate(
            [jnp.where(ok, 0.0, NEG_INF).astype(F32), jnp.zeros((bq, n_ctx), F32)], axis=1)
    for n in range(n_kv):
        ks, vs = slice(n * hd, (n + 1) * hd), slice(kcols + n * hd, kcols + (n + 1) * hd)
        if local:
            k_all = jnp.concatenate([kvp_ref[:, ks], kvo_ref[:, ks], kvn_ref[:, ks], kvc_ref[:, ks]], axis=0)
            v_all = jnp.concatenate([kvp_ref[:, vs], kvo_ref[:, vs], kvn_ref[:, vs], kvc_ref[:, vs]], axis=0)
        else:
            k_all, v_all = kvc_ref[:, ks], kvc_ref[:, vs]
        for g in range(GA_GROUP):
            head = n * GA_GROUP + g
            sink = sink_ref[head]
            s = _dot_nt(q_ref[:, head * hd:(head + 1) * hd], k_all)
            if local:
                s = s + bias
            m = jnp.maximum(jnp.max(s, axis=-1, keepdims=True), sink)
            p = jnp.exp(s - m)
            den = jnp.sum(p, axis=-1, keepdims=True) + jnp.exp(sink - m)
            o = _dot(p.astype(BF16), v_all) / den
            o_ref[:, head * hd:(head + 1) * hd] = o.astype(o_ref.dtype)


def _ga_attention(qkv_lat, qkv_ctx, sink, batch, seq, n_ctx, need_ctx):
    n_heads = sink.shape[0]
    qcols = n_heads * GA_HD
    n_kv = n_heads // GA_GROUP
    kvw = 2 * n_kv * GA_HD
    kvblk = qcols // kvw
    assert qcols % kvw == 0
    bq = _pick(seq, (512, 256, 128))
    nq = seq // bq
    wb = bq // WINDOW
    last_w = batch * seq // WINDOW - 1
    smem = pl.BlockSpec(memory_space=pltpu.SMEM)
    lat = pl.pallas_call(
        functools.partial(_ga_kernel, bq=bq, seq=seq, n_kv=n_kv, local=True),
        grid=(batch, nq),
        in_specs=[
            smem,
            pl.BlockSpec((bq, qcols), lambda b, i: (b * nq + i, 0)),
            pl.BlockSpec((n_ctx, kvw), lambda b, i: (b, kvblk)),
            pl.BlockSpec((bq, kvw), lambda b, i: (b * nq + i, kvblk)),
            pl.BlockSpec((WINDOW, kvw), lambda b, i: (jnp.maximum((b * nq + i) * wb - 1, 0), kvblk)),
            pl.BlockSpec((WINDOW, kvw), lambda b, i: (jnp.minimum((b * nq + i + 1) * wb, last_w), kvblk)),
        ],
        out_specs=pl.BlockSpec((bq, qcols), lambda b, i: (b * nq + i, 0)),
        out_shape=jax.ShapeDtypeStruct((batch * seq, qcols), BF16),
        compiler_params=_params("parallel", "parallel"),
        name="ga_attention_latent",
    )(sink, qkv_lat, qkv_ctx, qkv_lat, qkv_lat, qkv_lat)
    ctx = None
    if need_ctx:
        ctx = pl.pallas_call(
            functools.partial(_ga_kernel, bq=n_ctx, seq=n_ctx, n_kv=n_kv, local=False),
            grid=(batch,),
            in_specs=[
                smem,
                pl.BlockSpec((n_ctx, qcols), lambda b: (b, 0)),
                pl.BlockSpec((n_ctx, kvw), lambda b: (b, kvblk)),
            ],
            out_specs=pl.BlockSpec((n_ctx, qcols), lambda b: (b, 0)),
            out_shape=jax.ShapeDtypeStruct((batch * n_ctx, qcols), BF16),
            compiler_params=_params("parallel"),
            name="ga_attention_context",
        )(sink, qkv_ctx, qkv_ctx)
    return lat, ctx


def _df_kernel(lam_ref, g_ref, q_ref, *rest, lambda_init, chunks, first_q_axis):
    n_seg = len(chunks)
    kv_refs = rest[:2 * n_seg]
    o_ref = rest[2 * n_seg]
    vt_scrs = rest[2 * n_seg + 1:2 * n_seg + 1 + n_seg]
    acc_scr, s_a, s_b = rest[-3:]
    hd = DF_HD
    bq = q_ref.shape[0]
    lam = lam_ref[...]
    lmbda = (jnp.exp(jnp.sum(lam[0:1] * lam[1:2], axis=-1, keepdims=True))
             - jnp.exp(jnp.sum(lam[2:3] * lam[3:4], axis=-1, keepdims=True)) + lambda_init)

    def transpose_values():
        for seg, ck in enumerate(chunks):
            v_ref, vt = kv_refs[2 * seg + 1], vt_scrs[seg]
            for c in range(v_ref.shape[0] // ck):
                vt[c] = v_ref[c * ck:(c + 1) * ck, :].astype(F32).T.astype(BF16)

    if first_q_axis is None:
        transpose_values()
    else:
        pl.when(pl.program_id(first_q_axis) == 0)(transpose_values)

    qs = (q_ref[:, 0:hd], q_ref[:, hd:2 * hd])
    acc_scr[...] = jnp.zeros_like(acc_scr)
    stat0 = (jnp.full((1, bq), NEG_INF, F32), jnp.zeros((1, bq), F32))
    carry = (stat0, stat0)

    def scores(k_ref, c, ck, dst):
        start = c * ck if isinstance(c, int) else pl.multiple_of(c * ck, ck)
        kblk = k_ref[pl.ds(start, ck), :]
        for r in range(2):
            dst[r] = _dot_nt(kblk[:, r * hd:(r + 1) * hd], qs[r])

    def update(src, v_t, carry):
        new = []
        for r in range(2):
            m_old, l_old = carry[r]
            m_new = jnp.maximum(m_old, jnp.max(src[r], axis=0, keepdims=True))
            alpha = jnp.exp2(m_old - m_new)
            p = jnp.exp2(src[r] - m_new)
            l_new = alpha * l_old + jnp.sum(p, axis=0, keepdims=True)
            acc_scr[r] = alpha * acc_scr[r] + _dot(v_t, p.astype(BF16))
            new.append((m_new, l_new))
        return tuple(new)

    for seg in range(n_seg - 1):
        k_ref, vt, ck = kv_refs[2 * seg], vt_scrs[seg], chunks[seg]
        for c in range(k_ref.shape[0] // ck):
            scores(k_ref, c, ck, s_a.at[:, 0:ck, :])
            carry = update(s_a.at[:, 0:ck, :], vt[c], carry)

    k_ref, vt, ck = kv_refs[2 * n_seg - 2], vt_scrs[n_seg - 1], chunks[n_seg - 1]
    n = k_ref.shape[0] // ck
    pairs = (n - 1) // 2
    scores(k_ref, 0, ck, s_a)

    def body(j, carry):
        scores(k_ref, 2 * j + 1, ck, s_b)
        carry = update(s_a, vt[2 * j], carry)
        scores(k_ref, 2 * j + 2, ck, s_a)
        return update(s_b, vt[2 * j + 1], carry)

    carry = lax.fori_loop(0, pairs, body, carry)
    if n - 2 * pairs == 2:
        scores(k_ref, n - 1, ck, s_b)
        carry = update(s_a, vt[n - 2], carry)
        carry = update(s_b, vt[n - 1], carry)
    else:
        carry = update(s_a, vt[n - 1], carry)
    (_, l0), (_, l1) = carry
    o = (acc_scr[0] / l0 - lmbda * (acc_scr[1] / l1)).T
    ms = jnp.mean(o * o, axis=-1, keepdims=True)
    y = o * lax.rsqrt(ms + EPS) * g_ref[...] * (1.0 - lambda_init)
    o_ref[...] = y.astype(o_ref.dtype)


def _df_attention(qkv_lat, qkv_ctx, lam, subln_g, batch, seq, n_ctx, lambda_init, need_ctx):
    hw = 2 * DF_HD
    n_heads = qkv_lat.shape[1] // (3 * hw)
    bq = _pick(seq, (512, 256, 128))
    nq = seq // bq
    ck_lat = _pick(seq, (512, 256, 128))
    ck_ctx = _pick(n_ctx, (512, 256, 128))
    g2 = subln_g.reshape(1, hw)
    const = lambda shape: pl.BlockSpec(shape, lambda *_: (0,) * len(shape))
    vt_ctx = pltpu.VMEM((n_ctx // ck_ctx, hw, ck_ctx), BF16)
    vt_lat = pltpu.VMEM((seq // ck_lat, hw, ck_lat), BF16)
    lat = pl.pallas_call(
        functools.partial(_df_kernel, lambda_init=lambda_init, chunks=(ck_ctx, ck_lat), first_q_axis=2),
        grid=(batch, n_heads, nq),
        in_specs=[
            const(lam.shape),
            const((1, hw)),
            pl.BlockSpec((bq, hw), lambda b, h, i: (b * nq + i, h)),
            pl.BlockSpec((n_ctx, hw), lambda b, h, i: (b, n_heads + h)),
            pl.BlockSpec((n_ctx, hw), lambda b, h, i: (b, 2 * n_heads + h)),
            pl.BlockSpec((seq, hw), lambda b, h, i: (b, n_heads + h)),
            pl.BlockSpec((seq, hw), lambda b, h, i: (b, 2 * n_heads + h)),
        ],
        out_specs=pl.BlockSpec((bq, hw), lambda b, h, i: (b * nq + i, h)),
        out_shape=jax.ShapeDtypeStruct((batch * seq, n_heads * hw), BF16),
        scratch_shapes=[vt_ctx, vt_lat, pltpu.VMEM((2, hw, bq), F32)]
        + [pltpu.VMEM((2, max(ck_lat, ck_ctx), bq), F32)] * 2,
        compiler_params=_params("parallel", "parallel", "arbitrary"),
        name="df_attention_latent",
    )(lam, g2, qkv_lat, qkv_ctx, qkv_ctx, qkv_lat, qkv_lat)
    ctx = None
    if need_ctx:
        ctx = pl.pallas_call(
            functools.partial(_df_kernel, lambda_init=lambda_init, chunks=(ck_ctx,), first_q_axis=None),
            grid=(batch, n_heads),
            in_specs=[
                const(lam.shape),
                const((1, hw)),
                pl.BlockSpec((n_ctx, hw), lambda b, h: (b, h)),
                pl.BlockSpec((n_ctx, hw), lambda b, h: (b, n_heads + h)),
                pl.BlockSpec((n_ctx, hw), lambda b, h: (b, 2 * n_heads + h)),
            ],
            out_specs=pl.BlockSpec((n_ctx, hw), lambda b, h: (b, h)),
            out_shape=jax.ShapeDtypeStruct((batch * n_ctx, n_heads * hw), BF16),
            scratch_shapes=[vt_ctx, pltpu.VMEM((2, hw, n_ctx), F32)]
            + [pltpu.VMEM((2, ck_ctx, n_ctx), F32)] * 2,
            compiler_params=_params("parallel", "parallel"),
            name="df_attention_context",
        )(lam, g2, qkv_ctx, qkv_ctx, qkv_ctx)
    return lat, ctx


def _rt_kernel(lg_ref, gn_ref, *refs, n_heads, chunk, ctx_steps):
    ctx_in, lat_in = refs[0:6], refs[6:12]
    ctx_out, lat_out = refs[12:14], refs[14:16]
    st_scr = refs[16]
    head0 = pl.program_id(1) * n_heads
    s_idx = pl.program_id(2)
    dk = ctx_in[0].shape[1] // n_heads
    dv = ctx_in[2].shape[1] // n_heads

    @pl.when(s_idx == 0)
    def _():
        st_scr[...] = jnp.zeros_like(st_scr)

    def step(ins, outs):
        qf_ref, kf_ref, vf_ref, qb_ref, kb_ref, vb_ref = ins
        of_ref, ob_ref = outs
        row = lax.broadcasted_iota(jnp.int32, (chunk, chunk), 0).astype(F32)
        col = lax.broadcasted_iota(jnp.int32, (chunk, chunk), 1).astype(F32)
        pos = lax.broadcasted_iota(jnp.int32, (chunk, 1), 0).astype(F32)
        dirs = (
            (qf_ref, kf_ref, vf_ref, of_ref, row - col, pos + 1.0, chunk - 1.0 - pos),
            (qb_ref, kb_ref, vb_ref, ob_ref, col - row, chunk - pos, pos),
        )
        for d, (q_ref, k_ref, v_ref, o_ref, rel, q_pow, k_pow) in enumerate(dirs):
            for h in range(n_heads):
                lg = lg_ref[d, head0 + h]
                decay = jnp.where(rel >= 0, jnp.exp(lg * jnp.maximum(rel, 0.0)), 0.0)
                q_decay = jnp.exp(lg * q_pow)
                k_decay = jnp.exp(lg * k_pow)
                chunk_decay = jnp.exp(lg * chunk)
                q = q_ref[:, h * dk:(h + 1) * dk]
                k = k_ref[:, h * dk:(h + 1) * dk]
                v = v_ref[:, h * dv:(h + 1) * dv]
                state = st_scr[d, h]
                inner = _dot_nt(q, k) * decay
                o = _dot(inner.astype(BF16), v) + _dot(q, state.astype(BF16)) * q_decay
                kd = (k.astype(F32) * k_decay).astype(BF16)
                st_scr[d, h] = state * chunk_decay + _dot_tn(kd, v)
                mu = jnp.mean(o, axis=-1, keepdims=True)
                dev = o - mu
                var = jnp.mean(dev * dev, axis=-1, keepdims=True)
                y = dev * lax.rsqrt(var + EPS) * gn_ref[d:d + 1, h * dv:(h + 1) * dv]
                o_ref[:, h * dv:(h + 1) * dv] = y.astype(o_ref.dtype)

    @pl.when(s_idx < ctx_steps)
    def _():
        step(ctx_in, ctx_out)

    @pl.when(s_idx >= ctx_steps)
    def _():
        step(lat_in, lat_out)


def _retention(proj_lat, proj_ctx, log_gamma, gn_g, batch, seq, n_ctx):
    n_heads = log_gamma.shape[1]
    qk = proj_lat.shape[1] // 8
    v = 2 * qk
    chunk = _pick(math.gcd(seq, n_ctx), (256, 128))
    ncc, nlc = n_ctx // chunk, seq // chunk
    dk, dv = qk // n_heads, v // n_heads
    groups = 2
    hpg = n_heads // groups
    qkw, vw = hpg * dk, hpg * dv

    def ctx_f(b, s):
        return b * ncc + jnp.minimum(s, ncc - 1)

    def ctx_b(b, s):
        return b * ncc + (ncc - 1 - jnp.minimum(s, ncc - 1))

    def lat_f(b, s):
        return b * nlc + jnp.maximum(s - ncc, 0)

    def lat_b(b, s):
        return b * nlc + (nlc - 1 - jnp.maximum(s - ncc, 0))

    def in_specs(row):
        return [
            pl.BlockSpec((chunk, qkw), lambda b, g, s: (row(b, s), g)),
            pl.BlockSpec((chunk, qkw), lambda b, g, s: (row(b, s), groups + g)),
            pl.BlockSpec((chunk, vw), lambda b, g, s: (row(b, s), groups + g)),
        ]

    def out_spec(row):
        return pl.BlockSpec((chunk, vw), lambda b, g, s: (row(b, s), g))

    out_ctx = jax.ShapeDtypeStruct((batch * n_ctx, v), BF16)
    out_lat = jax.ShapeDtypeStruct((batch * seq, v), BF16)
    yf_ctx, yb_ctx, yf_lat, yb_lat = pl.pallas_call(
        functools.partial(_rt_kernel, n_heads=hpg, chunk=chunk, ctx_steps=ncc),
        grid=(batch, groups, ncc + nlc),
        in_specs=[pl.BlockSpec(memory_space=pltpu.SMEM), pl.BlockSpec((2, vw), lambda b, g, s: (0, g))]
        + in_specs(ctx_f) + in_specs(ctx_b) + in_specs(lat_f) + in_specs(lat_b),
        out_specs=[out_spec(ctx_f), out_spec(ctx_b), out_spec(lat_f), out_spec(lat_b)],
        out_shape=[out_ctx, out_ctx, out_lat, out_lat],
        scratch_shapes=[pltpu.VMEM((2, hpg, dk, dv), F32)],
        compiler_params=_params("parallel", "parallel", "arbitrary"),
        name="retention",
    )(log_gamma, gn_g, *([proj_ctx] * 6), *([proj_lat] * 6))
    return (yf_lat, yb_lat), (yf_ctx, yb_ctx)


def _axial_tables(rows_count, head_dim):
    rows = jnp.repeat(jnp.arange(rows_count), GRID_W).astype(F32)
    cols = jnp.tile(jnp.arange(GRID_W), rows_count).astype(F32)
    n_freq = head_dim // 4
    inv = ROPE_THETA ** (-jnp.arange(n_freq, dtype=F32) / n_freq)
    ang = jnp.concatenate([rows[:, None] * inv, cols[:, None] * inv], -1)
    cos, sin = jnp.cos(ang), jnp.sin(ang)
    return jnp.concatenate([cos, cos], -1), jnp.concatenate([-sin, sin], -1)


def _linear_tables(n_tokens, head_dim):
    half = head_dim // 2
    inv = ROPE_THETA ** (-jnp.arange(half, dtype=F32) / half)
    ang = jnp.arange(n_tokens, dtype=F32)[:, None] * inv
    return jnp.cos(ang), jnp.sin(ang)


def kernel(x, c, ctx, c_ctx, mod_w, mod_b, norm_g, ffn_w_in, ffn_conv_w, ffn_conv_b, ffn_w_out, ga_wqkv, ga_sink, ga_qk_norm, ga_wo, rt_w_in, rt_decay, rt_gn, rt_wo, df_wqkv, df_lambda, df_qk_norm, df_subln, df_wo):
    batch, seq, d = x.shape
    n_ctx = ctx.shape[1]
    depth = mod_w.shape[0]
    assert batch + 1 <= MOD_ROWS and seq % GRID_W == 0

    cond = jnp.zeros((MOD_ROWS, d), F32).at[0].set(c_ctx).at[1:batch + 1].set(c)
    mods = _adaln_table(cond, mod_w, mod_b).reshape(depth, 6, MOD_ROWS, 1, d)

    ax_cos, ax_sin = _axial_tables(seq // GRID_W, GA_HD)
    assert DF_HD == GA_HD
    rt_dk = d // RT_HEADS
    ln_cos, ln_sin = _linear_tables(seq, rt_dk)
    ones2 = jnp.ones((2, LANES), F32)

    h_lat = x.reshape(batch * seq, d)
    h_ctx = ctx.reshape(batch * n_ctx, d)

    lat = _Stream(seq, lambda i, bm: 1 + i // (seq // bm), lambda i, bm: i % (seq // bm), True)
    cst = _Stream(batch * n_ctx, lambda i, bm: 0, lambda i, bm: 0, False)

    def tiles(widths, bn):
        out, start = [], 0
        for w in widths:
            assert w % bn == 0
            out.append((start // bn, (start + w) // bn))
            start += w
        return out

    for i in range(depth):
        need_ctx = i < depth - 1
        kind, j = i % N_MIXERS, i // N_MIXERS
        g1 = norm_g[i, 0].reshape(1, d)
        g2 = norm_g[i, 1].reshape(1, d)

        if kind == 0:
            w = ga_wqkv[j].astype(BF16)
            qcols = ga_sink.shape[1] * GA_HD
            kcols = (w.shape[1] - qcols) // 2
            bn = 512
            (q0, q1), (k0, k1), (v0, v1) = tiles((qcols, kcols, kcols), bn)
            plan = ((q0, q1, "norm_rope_128", 0, GA_HD ** -0.5), (k0, k1, "norm_rope_128", 1, 1.0),
                    (v0, v1, "plain", 0, 1.0))
            cos_t, sin_t, gains = ax_cos, ax_sin, ga_qk_norm[j]
        elif kind == 1:
            w = rt_w_in[j].astype(BF16)
            qk = d
            bn = 512
            (q0, q1), (k0, k1), (v0, v1) = tiles((qk, qk, w.shape[1] - 2 * qk), bn)
            plan = ((q0, q1, "rope_256", 0, rt_dk ** -0.5), (k0, k1, "rope_256", 0, 1.0),
                    (v0, v1, "plain", 0, 1.0))
            cos_t, sin_t, gains = ln_cos, ln_sin, ones2
        else:
            w = df_wqkv[j].astype(BF16)
            qcols = w.shape[1] // 3
            bn = 512
            (q0, q1), (k0, k1), (v0, v1) = tiles((qcols, qcols, qcols), bn)
            plan = ((q0, q1, "norm_rope_128", 0, DF_HD ** -0.5 * LOG2E), (k0, k1, "norm_rope_128", 1, 1.0),
                    (v0, v1, "plain", 0, 1.0))
            cos_t, sin_t, gains = ax_cos, ax_sin, df_qk_norm[j]

        proj = {"lat": _proj(h_lat, g1, mods, i, lat, w, plan, bn, cos_t, sin_t, gains),
                "ctx": _proj(h_ctx, g1, mods, i, cst, w, plan, bn, cos_t, sin_t, gains)}

        if kind == 0:
            y_lat, y_ctx = _ga_attention(proj["lat"], proj["ctx"], ga_sink[j], batch, seq, n_ctx, need_ctx)
            wo = ga_wo[j].astype(BF16)
            h_lat = _oproj(y_lat, wo, h_lat, mods, i, lat)
            if need_ctx:
                h_ctx = _oproj(y_ctx, wo, h_ctx, mods, i, cst)
        elif kind == 1:
            log_gamma = jax.nn.log_sigmoid(rt_decay[j].astype(F32))
            y_lat, y_ctx = _retention(proj["lat"], proj["ctx"], log_gamma, rt_gn[j], batch, seq, n_ctx)
            wo = rt_wo[j].astype(BF16)
            h_lat = _oproj_rt(y_lat[0], y_lat[1], proj["lat"], wo, h_lat, mods, i, lat)
            if need_ctx:
                h_ctx = _oproj_rt(y_ctx[0], y_ctx[1], proj["ctx"], wo, h_ctx, mods, i, cst)
        else:
            lambda_init = 0.8 - 0.6 * math.exp(-0.3 * i)
            y_lat, y_ctx = _df_attention(proj["lat"], proj["ctx"], df_lambda[j], df_subln[j], batch, seq, n_ctx,
                                         lambda_init, need_ctx)
            wo = df_wo[j].astype(BF16)
            h_lat = _oproj(y_lat, wo, h_lat, mods, i, lat)
            if need_ctx:
                h_ctx = _oproj(y_ctx, wo, h_ctx, mods, i, cst)

        w_in = ffn_w_in[i].astype(BF16)
        w_out = ffn_w_out[i].astype(BF16)
        cb = ffn_conv_b[i].reshape(1, -1)
        h_lat = _ffn(h_lat, g2, mods, i, lat, w_in, ffn_conv_w[i], cb, w_out, seq)
        if need_ctx:
            h_ctx = _ffn(h_ctx, g2, mods, i, cst, w_in, ffn_conv_w[i], cb, w_out, n_ctx)

    return h_lat.reshape(batch, seq, d)
```

```python
import functools
import math
from typing import Callable, NamedTuple

import jax
import jax.numpy as jnp
from jax import lax
from jax.experimental import pallas as pl
from jax.experimental.pallas import tpu as pltpu

F32 = jnp.float32
BF16 = jnp.bfloat16

EPS = 1e-6
NEG_INF = -1e30
LOG2E = math.log2(math.e)
ROPE_THETA = 10000.0
GRID_W = 64
WINDOW = 128
N_MIXERS = 3

GA_HD = 128
GA_GROUP = 4
RT_HEADS = 8
DF_HD = 128
CONV_W = 3

LANES = 128
MXU_COLS = 256
BF16_ROWS = 16
MOD_ROWS = 8
VMEM_LIMIT = 52 * 1024 * 1024


class _Stream(NamedTuple):
    group: int
    mod_row: Callable
    tab_block: Callable
    rope: bool


def _pick(n, candidates):
    for c in candidates:
        if n % c == 0:
            return c
    raise ValueError(f"no tile size in {candidates} divides {n}")


def _params(*sem):
    return pltpu.CompilerParams(dimension_semantics=sem, vmem_limit_bytes=VMEM_LIMIT)


def _silu(x):
    return x * (1.0 / (1.0 + jnp.exp(-x)))


def _dot(a, b):
    return jnp.dot(a, b, preferred_element_type=F32)


def _dot_nt(a, b):
    return lax.dot_general(a, b, (((1,), (1,)), ((), ())), preferred_element_type=F32)


def _dot_tn(a, b):
    return lax.dot_general(a, b, (((0,), (0,)), ((), ())), preferred_element_type=F32)


def _adaln_kernel(c_ref, w_ref, b_ref, o_ref):
    s = _silu(c_ref[...]).astype(BF16)
    o_ref[...] = _dot(s, w_ref[...].astype(BF16)) + b_ref[...]


def _adaln_table(cond, mod_w, mod_b):
    depth, d, _ = mod_w.shape
    bn = _pick(d, (1024, 512, 256, 128))
    nj = d // bn
    return pl.pallas_call(
        _adaln_kernel,
        grid=(depth, 6, nj),
        in_specs=[
            pl.BlockSpec((MOD_ROWS, d), lambda l, k, j: (0, 0)),
            pl.BlockSpec((None, d, bn), lambda l, k, j: (l, 0, k * nj + j)),
            pl.BlockSpec((None, 1, bn), lambda l, k, j: (l, 0, k * nj + j)),
        ],
        out_specs=pl.BlockSpec((None, None, MOD_ROWS, bn), lambda l, k, j: (l, k, 0, j)),
        out_shape=jax.ShapeDtypeStruct((depth, 6, MOD_ROWS, d), F32),
        compiler_params=_params("parallel", "parallel", "parallel"),
        name="adaln_table",
    )(cond, mod_w, mod_b.reshape(depth, 1, 6 * d))


def _mod_spec(layer, slot, row_fn, d):
    return pl.BlockSpec((None, None, None, 1, d), lambda i, j: (layer, slot, row_fn(i), 0, 0))


def _mod_spec_cols(layer, slot, row_fn, bn):
    return pl.BlockSpec((None, None, None, 1, bn), lambda i, j: (layer, slot, row_fn(i), 0, j))


def _norm_mod(x, g, shift, scale):
    ms = jnp.mean(x * x, axis=-1, keepdims=True)
    y = x * lax.rsqrt(ms + EPS) * g
    return y * (1.0 + scale) + shift


def _proj_kernel(h_ref, g_ref, sh_ref, sc_ref, w_ref, cos_ref, sin_ref, gain_ref, seg_ref, o_ref, a_scr, acc_scr,
                 *, plan, rope):
    j = pl.program_id(1)
    sub = acc_scr.shape[2]
    n_sub = o_ref.shape[1] // sub

    @pl.when(j == 0)
    def _():
        a_scr[...] = _norm_mod(h_ref[...], g_ref[...], sh_ref[...], sc_ref[...]).astype(BF16)

    def epilogue(s, kind, arg, scale):
        base = s * sub
        if kind == "plain":
            o_ref[:, base:base + sub] = acc_scr[s].astype(o_ref.dtype)
        elif kind == "silu":
            o_ref[:, base:base + sub] = _silu(acc_scr[s]).astype(o_ref.dtype)
        elif kind == "norm_rope_128":
            x = acc_scr[s]
            ms = _dot((x * x).astype(BF16), seg_ref[...])
            y = x * lax.rsqrt(ms + EPS) * (gain_ref[arg:arg + 1, :] * scale)
            for t in range(0, sub, LANES):
                yt = y[:, t:t + LANES]
                if rope:
                    yt = yt * cos_ref[...] + pltpu.roll(yt, LANES // 2, 1) * sin_ref[...]
                o_ref[:, base + t:base + t + LANES] = yt.astype(o_ref.dtype)
        else:
            assert kind == "rope_256" and sub == 2 * LANES
            x1 = acc_scr[s, :, 0:LANES] * scale
            x2 = acc_scr[s, :, LANES:sub] * scale
            if rope:
                c, sn = cos_ref[...], sin_ref[...]
                x1, x2 = x1 * c - x2 * sn, x1 * sn + x2 * c
            o_ref[:, base:base + LANES] = x1.astype(o_ref.dtype)
            o_ref[:, base + LANES:base + sub] = x2.astype(o_ref.dtype)

    for lo, hi, kinds in plan:
        @pl.when((j >= lo) & (j < hi))
        def _(kinds=kinds):
            a = a_scr[...]
            for s in range(n_sub):
                acc_scr[s] = _dot(a, w_ref[:, s * sub:(s + 1) * sub])
            for s, (kind, arg, scale) in enumerate(kinds):
                epilogue(s, kind, arg, scale)


def _proj(h, norm_g, mods, layer, st, w_all, w_idx, segments, cos_t, sin_t, gains):
    m, d = h.shape
    n = w_all.shape[2]
    bm = _pick(st.group, (1024, 512, 256))
    bn = _pick(n, (1024, 512, 256))
    sub = MXU_COLS
    n_sub = bn // sub
    per_sub = []
    for width, kind, arg, scale in segments:
        assert width % sub == 0
        per_sub += [(kind, arg, scale)] * (width // sub)
    assert len(per_sub) * sub == n
    tiles = [tuple(per_sub[t * n_sub:(t + 1) * n_sub]) for t in range(n // bn)]
    plan, lo = [], 0
    for t in range(1, len(tiles) + 1):
        if t == len(tiles) or tiles[t] != tiles[lo]:
            plan.append((lo, t, tiles[lo]))
            lo = t
    row_fn, tab_fn = st.mod_row, st.tab_block
    kernel = functools.partial(_proj_kernel, plan=tuple(plan), rope=st.rope)
    head = jnp.arange(sub) // LANES
    seg_mean = jnp.where(head[:, None] == head[None, :], 1.0 / LANES, 0.0).astype(BF16)
    gains = jnp.tile(gains, (1, sub // LANES))
    return pl.pallas_call(
        kernel,
        grid=(m // bm, n // bn),
        in_specs=[
            pl.BlockSpec((bm, d), lambda i, j: (i, 0)),
            pl.BlockSpec((1, d), lambda i, j: (0, 0)),
            _mod_spec(layer, 0, lambda i: row_fn(i, bm), d),
            _mod_spec(layer, 1, lambda i: row_fn(i, bm), d),
            pl.BlockSpec((None, d, bn), lambda i, j: (w_idx, 0, j)),
            pl.BlockSpec((bm, LANES), lambda i, j: (tab_fn(i, bm), 0)),
            pl.BlockSpec((bm, LANES), lambda i, j: (tab_fn(i, bm), 0)),
            pl.BlockSpec(gains.shape, lambda i, j: (0, 0)),
            pl.BlockSpec((sub, sub), lambda i, j: (0, 0)),
        ],
        out_specs=pl.BlockSpec((bm, bn), lambda i, j: (i, j)),
        out_shape=jax.ShapeDtypeStruct((m, n), BF16),
        scratch_shapes=[pltpu.VMEM((bm, d), BF16), pltpu.VMEM((n_sub, bm, sub), F32)],
        compiler_params=_params("parallel", "arbitrary"),
        name="proj",
    )(h, norm_g, mods, mods, w_all, cos_t, sin_t, gains, seg_mean)


def _oproj_kernel(y_ref, w_ref, h_ref, gate_ref, o_ref):
    o_ref[...] = h_ref[...] + gate_ref[...] * _dot(y_ref[...], w_ref[...])


def _oproj_rt_kernel(yf_ref, yb_ref, sf_ref, sb_ref, w_ref, h_ref, gate_ref, o_ref):
    c = pl.program_id(1)

    @pl.when(c == 0)
    def _():
        o_ref[...] = jnp.zeros_like(o_ref)

    ck = w_ref.shape[0]
    sub = min(ck, MXU_COLS)
    part = None
    for s in range(0, ck, sub):
        cols = slice(s, s + sub)
        z = sf_ref[:, cols] * yf_ref[:, cols] + sb_ref[:, cols] * yb_ref[:, cols]
        d = _dot(z, w_ref[cols, :])
        part = d if part is None else part + d
    o_ref[...] += part

    @pl.when(c == pl.num_programs(1) - 1)
    def _():
        o_ref[...] = h_ref[...] + gate_ref[...] * o_ref[...]


def _oproj(y, w_all, w_idx, h, mods, layer, st):
    m, k = y.shape
    d = w_all.shape[2]
    bm = _pick(st.group, (1024, 512, 256))
    row_fn = st.mod_row
    bn = _pick(d, (512, 256, 128))
    return pl.pallas_call(
        _oproj_kernel,
        grid=(m // bm, d // bn),
        in_specs=[
            pl.BlockSpec((bm, k), lambda i, j: (i, 0)),
            pl.BlockSpec((None, k, bn), lambda i, j: (w_idx, 0, j)),
            pl.BlockSpec((bm, bn), lambda i, j: (i, j)),
            _mod_spec_cols(layer, 2, lambda i: row_fn(i, bm), bn),
        ],
        out_specs=pl.BlockSpec((bm, bn), lambda i, j: (i, j)),
        out_shape=jax.ShapeDtypeStruct((m, d), F32),
        input_output_aliases={2: 0},
        compiler_params=_params("parallel", "arbitrary"),
        name="oproj",
    )(y, w_all, h, mods)


def _oproj_rt(yf, yb, proj, w_all, w_idx, h, mods, layer, st):
    m, v = yf.shape
    d = w_all.shape[2]
    bm = _pick(st.group, (512, 256))
    row_fn = st.mod_row
    ck = _pick(v, (1024, 512, 256, 128))
    nc = v // ck
    gcol = (proj.shape[1] // v - 2) * nc
    return pl.pallas_call(
        _oproj_rt_kernel,
        grid=(m // bm, nc),
        in_specs=[
            pl.BlockSpec((bm, ck), lambda i, c: (i, c)),
            pl.BlockSpec((bm, ck), lambda i, c: (i, c)),
            pl.BlockSpec((bm, ck), lambda i, c: (i, gcol + c)),
            pl.BlockSpec((bm, ck), lambda i, c: (i, gcol + nc + c)),
            pl.BlockSpec((None, ck, d), lambda i, c: (w_idx, c, 0)),
            pl.BlockSpec((bm, d), lambda i, c: (i, 0)),
            _mod_spec(layer, 2, lambda i: row_fn(i, bm), d),
        ],
        out_specs=pl.BlockSpec((bm, d), lambda i, c: (i, 0)),
        out_shape=jax.ShapeDtypeStruct((m, d), F32),
        input_output_aliases={5: 0},
        compiler_params=_params("parallel", "arbitrary"),
        name="oproj_rt",
    )(yf, yb, proj, proj, w_all, h, mods)


def _ffn_kernel(h_ref, hp_ref, hn_ref, g_ref, sh_ref, sc_ref, gate_ref, wa_ref, wb_ref,
                cwa_ref, cwb_ref, cba_ref, cbb_ref, wo_ref, o_ref, a_scr, u_scr, *, bm, seq_len):
    i = pl.program_id(0)
    c = pl.program_id(1)
    halo = BF16_ROWS

    @pl.when(c == 0)
    def _():
        def nm(x):
            return _norm_mod(x, g_ref[...], sh_ref[...], sc_ref[...])

        keep_prev = jnp.where((i * bm) % seq_len != 0, 1.0, 0.0)
        keep_next = jnp.where(((i + 1) * bm) % seq_len != 0, 1.0, 0.0)
        a_scr[0:halo, :] = (nm(hp_ref[...]) * keep_prev).astype(BF16)
        a_scr[halo:halo + bm, :] = nm(h_ref[...]).astype(BF16)
        a_scr[halo + bm:, :] = (nm(hn_ref[...]) * keep_next).astype(BF16)
        o_ref[...] = jnp.zeros_like(o_ref)

    a = a_scr[...]

    ck = wa_ref.shape[1]
    sub = u_scr.shape[2]
    n_sub = ck // sub
    for s in range(n_sub):
        cols = slice(s * sub, (s + 1) * sub)
        u_scr[2 * s] = _dot(a, wa_ref[:, cols])
        u_scr[2 * s + 1] = _dot(a, wb_ref[:, cols])

    def conv(idx, cw_ref, cb_ref, cols):
        prev = u_scr[idx, halo - 1:halo - 1 + bm, :]
        cur = u_scr[idx, halo:halo + bm, :]
        nxt = u_scr[idx, halo + 1:halo + 1 + bm, :]
        return (cb_ref[:, cols] + prev * cw_ref[0:1, cols] + cur * cw_ref[1:2, cols]
                + nxt * cw_ref[2:3, cols])

    part = None
    for s in range(n_sub):
        cols = slice(s * sub, (s + 1) * sub)
        ua = conv(2 * s, cwa_ref, cba_ref, cols)
        ub = conv(2 * s + 1, cwb_ref, cbb_ref, cols)
        d = _dot((_silu(ua) * ub).astype(BF16), wo_ref[cols, :])
        part = d if part is None else part + d
    o_ref[...] += part

    @pl.when(c == pl.num_programs(1) - 1)
    def _():
        o_ref[...] = h_ref[...] + gate_ref[...] * o_ref[...]


def _ffn(h, norm_g, mods, layer, st, w_in, conv_w, conv_b, w_out, seq_len):
    m, d = h.shape
    dff = w_out.shape[1]
    bm = _pick(seq_len, (512, 256))
    ck = _pick(dff, (512, 256, 128))
    nc = dff // ck
    halo = BF16_ROWS
    hb = bm // halo
    last = m // halo - 1
    kernel = functools.partial(_ffn_kernel, bm=bm, seq_len=seq_len)
    sub = min(ck, MXU_COLS)
    mrow = lambda i: st.mod_row(i, bm)
    return pl.pallas_call(
        kernel,
        grid=(m // bm, nc),
        in_specs=[
            pl.BlockSpec((bm, d), lambda i, c: (i, 0)),
            pl.BlockSpec((halo, d), lambda i, c: (jnp.maximum(i * hb - 1, 0), 0)),
            pl.BlockSpec((halo, d), lambda i, c: (jnp.minimum((i + 1) * hb, last), 0)),
            pl.BlockSpec((1, d), lambda i, c: (0, 0)),
            _mod_spec(layer, 3, mrow, d),
            _mod_spec(layer, 4, mrow, d),
            _mod_spec(layer, 5, mrow, d),
            pl.BlockSpec((None, d, ck), lambda i, c: (layer, 0, c)),
            pl.BlockSpec((None, d, ck), lambda i, c: (layer, 0, nc + c)),
            pl.BlockSpec((None, CONV_W, ck), lambda i, c: (layer, 0, c)),
            pl.BlockSpec((None, CONV_W, ck), lambda i, c: (layer, 0, nc + c)),
            pl.BlockSpec((None, 1, ck), lambda i, c: (layer, 0, c)),
            pl.BlockSpec((None, 1, ck), lambda i, c: (layer, 0, nc + c)),
            pl.BlockSpec((None, ck, d), lambda i, c: (layer, c, 0)),
        ],
        out_specs=pl.BlockSpec((bm, d), lambda i, c: (i, 0)),
        out_shape=jax.ShapeDtypeStruct((m, d), F32),
        scratch_shapes=[pltpu.VMEM((bm + 2 * halo, d), BF16),
                        pltpu.VMEM((2 * (ck // sub), bm + 2 * halo, sub), F32)],
        compiler_params=_params("parallel", "arbitrary"),
        name="conv_ffn",
    )(h, h, h, norm_g, mods, mods, mods, w_in, w_in, conv_w, conv_w, conv_b, conv_b, w_out)


def _ga_kernel(sink_ref, q_ref, kvc_ref, *rest, seq, n_kv, local):
    if local:
        kvo_ref, kvp_ref, kvn_ref, o_ref, s_a, s_b = rest
    else:
        o_ref, s_a, s_b = rest
    hd = GA_HD
    kcols = n_kv * hd
    bq = q_ref.shape[0]
    sq = s_a.shape[1]
    n_ctx = kvc_ref.shape[0]
    bufs = (s_a, s_b)

    def band(u):
        if not local:
            return [], 0
        lo, hi = u * sq - WINDOW, (u + 1) * sq + WINDOW
        pieces = []
        if lo < 0:
            pieces.append((kvp_ref, WINDOW + lo, -lo))
        pieces.append((kvo_ref, max(lo, 0), min(hi, bq) - max(lo, 0)))
        if hi > bq:
            pieces.append((kvn_ref, 0, hi - bq))
        return pieces, lo

    def bias(u):
        pieces, lo = band(u)
        n_loc = sum(p[2] for p in pieces)
        kl = lax.broadcasted_iota(jnp.int32, (n_loc, sq), 0) + lo
        ql = lax.broadcasted_iota(jnp.int32, (n_loc, sq), 1) + u * sq
        pos = pl.program_id(1) * bq + kl
        ok = (jnp.abs(ql - kl) <= WINDOW) & (pos >= 0) & (pos < seq)
        return jnp.concatenate([jnp.where(ok, 0.0, NEG_INF).astype(F32), jnp.zeros((n_ctx, sq), F32)], axis=0)

    def keys_values(u, n):
        pieces, _ = band(u)
        ks, vs = slice(n * hd, (n + 1) * hd), slice(kcols + n * hd, kcols + (n + 1) * hd)
        k_all = jnp.concatenate([r[a:a + c, ks] for r, a, c in pieces] + [kvc_ref[:, ks]], axis=0)
        v_all = jnp.concatenate([r[a:a + c, vs] for r, a, c in pieces] + [kvc_ref[:, vs]], axis=0)
        return k_all, v_all.astype(F32).T.astype(BF16)

    chains = [(u, n, g) for u in range(bq // sq) for n in range(n_kv) for g in range(GA_GROUP)]
    cache = {}

    def operands(u, n):
        if (u, n) not in cache:
            cache.clear()
            cache[(u, n)] = keys_values(u, n)
        return cache[(u, n)]

    biases = {}

    def scores(idx):
        u, n, g = chains[idx]
        head = n * GA_GROUP + g
        k_all, _ = operands(u, n)
        s = _dot_nt(k_all, q_ref[u * sq:(u + 1) * sq, head * hd:(head + 1) * hd])
        if local:
            if u not in biases:
                biases[u] = bias(u)
            s = s + biases[u]
        bufs[idx % 2][...] = s

    scores(0)
    for idx, (u, n, g) in enumerate(chains):
        _, v_t = operands(u, n)
        if idx + 1 < len(chains):
            scores(idx + 1)
        buf = bufs[idx % 2]
        head = n * GA_GROUP + g
        sink = sink_ref[head] * LOG2E
        m = jnp.maximum(jnp.max(buf[...], axis=0, keepdims=True), sink)
        p = jnp.exp2(buf[...] - m)
        den = jnp.sum(p, axis=0, keepdims=True) + jnp.exp2(sink - m)
        o_t = _dot(v_t, p.astype(BF16)) / den
        o_ref[u * sq:(u + 1) * sq, head * hd:(head + 1) * hd] = o_t.T.astype(o_ref.dtype)


def _ga_attention(qkv_lat, qkv_ctx, sink, batch, seq, n_ctx, need_ctx):
    n_heads = sink.shape[0]
    qcols = n_heads * GA_HD
    n_kv = n_heads // GA_GROUP
    kvw = 2 * n_kv * GA_HD
    kvblk = qcols // kvw
    assert qcols % kvw == 0
    bq = _pick(seq, (512, 256, 128))
    sq = min(bq, MXU_COLS)
    nq = seq // bq
    wb = bq // WINDOW
    last_w = batch * seq // WINDOW - 1
    smem = pl.BlockSpec(memory_space=pltpu.SMEM)
    lat = pl.pallas_call(
        functools.partial(_ga_kernel, seq=seq, n_kv=n_kv, local=True),
        grid=(batch, nq),
        in_specs=[
            smem,
            pl.BlockSpec((bq, qcols), lambda b, i: (b * nq + i, 0)),
            pl.BlockSpec((n_ctx, kvw), lambda b, i: (b, kvblk)),
            pl.BlockSpec((bq, kvw), lambda b, i: (b * nq + i, kvblk)),
            pl.BlockSpec((WINDOW, kvw), lambda b, i: (jnp.maximum((b * nq + i) * wb - 1, 0), kvblk)),
            pl.BlockSpec((WINDOW, kvw), lambda b, i: (jnp.minimum((b * nq + i + 1) * wb, last_w), kvblk)),
        ],
        out_specs=pl.BlockSpec((bq, qcols), lambda b, i: (b * nq + i, 0)),
        out_shape=jax.ShapeDtypeStruct((batch * seq, qcols), BF16),
        scratch_shapes=[pltpu.VMEM((sq + 2 * WINDOW + n_ctx, sq), F32)] * 2,
        compiler_params=_params("parallel", "parallel"),
        name="ga_attention_latent",
    )(sink, qkv_lat, qkv_ctx, qkv_lat, qkv_lat, qkv_lat)
    ctx = None
    if need_ctx:
        cq = min(n_ctx, MXU_COLS)
        ctx = pl.pallas_call(
            functools.partial(_ga_kernel, seq=n_ctx, n_kv=n_kv, local=False),
            grid=(batch,),
            in_specs=[
                smem,
                pl.BlockSpec((n_ctx, qcols), lambda b: (b, 0)),
                pl.BlockSpec((n_ctx, kvw), lambda b: (b, kvblk)),
            ],
            out_specs=pl.BlockSpec((n_ctx, qcols), lambda b: (b, 0)),
            out_shape=jax.ShapeDtypeStruct((batch * n_ctx, qcols), BF16),
            scratch_shapes=[pltpu.VMEM((n_ctx, cq), F32)] * 2,
            compiler_params=_params("parallel"),
            name="ga_attention_context",
        )(sink, qkv_ctx, qkv_ctx)
    return lat, ctx


def _df_kernel(lam_ref, g_ref, q_ref, *rest, lambda_init, chunks, first_q_axis):
    n_seg = len(chunks)
    kv_refs = rest[:2 * n_seg]
    o_ref = rest[2 * n_seg]
    vt_scrs = rest[2 * n_seg + 1:2 * n_seg + 1 + n_seg]
    acc_scr, s_a, s_b = rest[-3:]
    hd = DF_HD
    bq = q_ref.shape[0]
    lam = lam_ref[...]
    lmbda = (jnp.exp(jnp.sum(lam[0:1] * lam[1:2], axis=-1, keepdims=True))
             - jnp.exp(jnp.sum(lam[2:3] * lam[3:4], axis=-1, keepdims=True)) + lambda_init)

    def transpose_values():
        for seg, ck in enumerate(chunks):
            v_ref, vt = kv_refs[2 * seg + 1], vt_scrs[seg]
            for c in range(v_ref.shape[0] // ck):
                vt[c] = v_ref[c * ck:(c + 1) * ck, :].astype(F32).T.astype(BF16)

    if first_q_axis is None:
        transpose_values()
    else:
        pl.when(pl.program_id(first_q_axis) == 0)(transpose_values)

    qs = (q_ref[:, 0:hd], q_ref[:, hd:2 * hd])
    acc_scr[...] = jnp.zeros_like(acc_scr)
    stat0 = (jnp.full((1, bq), NEG_INF, F32), jnp.zeros((1, bq), F32))
    carry = (stat0, stat0)

    def scores(k_ref, c, ck, dst):
        start = c * ck if isinstance(c, int) else pl.multiple_of(c * ck, ck)
        kblk = k_ref[pl.ds(start, ck), :]
        for r in range(2):
            dst[r] = _dot_nt(kblk[:, r * hd:(r + 1) * hd], qs[r])

    def update(src, v_t, carry):
        new = []
        for r in range(2):
            m_old, l_old = carry[r]
            m_new = jnp.maximum(m_old, jnp.max(src[r], axis=0, keepdims=True))
            alpha = jnp.exp2(m_old - m_new)
            p = jnp.exp2(src[r] - m_new)
            l_new = alpha * l_old + jnp.sum(p, axis=0, keepdims=True)
            acc_scr[r] = alpha * acc_scr[r] + _dot(v_t, p.astype(BF16))
            new.append((m_new, l_new))
        return tuple(new)

    for seg in range(n_seg - 1):
        k_ref, vt, ck = kv_refs[2 * seg], vt_scrs[seg], chunks[seg]
        for c in range(k_ref.shape[0] // ck):
            scores(k_ref, c, ck, s_a.at[:, 0:ck, :])
            carry = update(s_a.at[:, 0:ck, :], vt[c], carry)

    k_ref, vt, ck = kv_refs[2 * n_seg - 2], vt_scrs[n_seg - 1], chunks[n_seg - 1]
    n = k_ref.shape[0] // ck
    pairs = (n - 1) // 2
    scores(k_ref, 0, ck, s_a)

    def body(j, carry):
        scores(k_ref, 2 * j + 1, ck, s_b)
        carry = update(s_a, vt[2 * j], carry)
        scores(k_ref, 2 * j + 2, ck, s_a)
        return update(s_b, vt[2 * j + 1], carry)

    carry = lax.fori_loop(0, pairs, body, carry)
    if n - 2 * pairs == 2:
        scores(k_ref, n - 1, ck, s_b)
        carry = update(s_a, vt[n - 2], carry)
        carry = update(s_b, vt[n - 1], carry)
    else:
        carry = update(s_a, vt[n - 1], carry)
    (_, l0), (_, l1) = carry
    o = (acc_scr[0] / l0 - lmbda * (acc_scr[1] / l1)).T
    ms = jnp.mean(o * o, axis=-1, keepdims=True)
    y = o * lax.rsqrt(ms + EPS) * g_ref[...] * (1.0 - lambda_init)
    o_ref[...] = y.astype(o_ref.dtype)


def _df_attention(qkv_lat, qkv_ctx, lam, subln_g, batch, seq, n_ctx, lambda_init, need_ctx):
    hw = 2 * DF_HD
    n_heads = qkv_lat.shape[1] // (3 * hw)
    bq = _pick(seq, (512, 256, 128))
    nq = seq // bq
    ck_lat = _pick(seq, (512, 256, 128))
    ck_ctx = _pick(n_ctx, (512, 256, 128))
    g2 = subln_g.reshape(1, hw)
    const = lambda shape: pl.BlockSpec(shape, lambda *_: (0,) * len(shape))
    vt_ctx = pltpu.VMEM((n_ctx // ck_ctx, hw, ck_ctx), BF16)
    vt_lat = pltpu.VMEM((seq // ck_lat, hw, ck_lat), BF16)
    lat = pl.pallas_call(
        functools.partial(_df_kernel, lambda_init=lambda_init, chunks=(ck_ctx, ck_lat), first_q_axis=2),
        grid=(batch, n_heads, nq),
        in_specs=[
            const(lam.shape),
            const((1, hw)),
            pl.BlockSpec((bq, hw), lambda b, h, i: (b * nq + i, h)),
            pl.BlockSpec((n_ctx, hw), lambda b, h, i: (b, n_heads + h)),
            pl.BlockSpec((n_ctx, hw), lambda b, h, i: (b, 2 * n_heads + h)),
            pl.BlockSpec((seq, hw), lambda b, h, i: (b, n_heads + h)),
            pl.BlockSpec((seq, hw), lambda b, h, i: (b, 2 * n_heads + h)),
        ],
        out_specs=pl.BlockSpec((bq, hw), lambda b, h, i: (b * nq + i, h)),
        out_shape=jax.ShapeDtypeStruct((batch * seq, n_heads * hw), BF16),
        scratch_shapes=[vt_ctx, vt_lat, pltpu.VMEM((2, hw, bq), F32)]
        + [pltpu.VMEM((2, max(ck_lat, ck_ctx), bq), F32)] * 2,
        compiler_params=_params("parallel", "parallel", "arbitrary"),
        name="df_attention_latent",
    )(lam, g2, qkv_lat, qkv_ctx, qkv_ctx, qkv_lat, qkv_lat)
    ctx = None
    if need_ctx:
        ctx = pl.pallas_call(
            functools.partial(_df_kernel, lambda_init=lambda_init, chunks=(ck_ctx,), first_q_axis=None),
            grid=(batch, n_heads),
            in_specs=[
                const(lam.shape),
                const((1, hw)),
                pl.BlockSpec((n_ctx, hw), lambda b, h: (b, h)),
                pl.BlockSpec((n_ctx, hw), lambda b, h: (b, n_heads + h)),
                pl.BlockSpec((n_ctx, hw), lambda b, h: (b, 2 * n_heads + h)),
            ],
            out_specs=pl.BlockSpec((n_ctx, hw), lambda b, h: (b, h)),
            out_shape=jax.ShapeDtypeStruct((batch * n_ctx, n_heads * hw), BF16),
            scratch_shapes=[vt_ctx, pltpu.VMEM((2, hw, n_ctx), F32)]
            + [pltpu.VMEM((2, ck_ctx, n_ctx), F32)] * 2,
            compiler_params=_params("parallel", "parallel"),
            name="df_attention_context",
        )(lam, g2, qkv_ctx, qkv_ctx, qkv_ctx)
    return lat, ctx


def _rt_kernel(lg_ref, gn_ref, *refs, n_heads, chunk, ctx_steps):
    ctx_in, lat_in = refs[0:6], refs[6:12]
    ctx_out, lat_out = refs[12:14], refs[14:16]
    st_scr = refs[16]
    head0 = pl.program_id(1) * n_heads
    s_idx = pl.program_id(2)
    dk = ctx_in[0].shape[1] // n_heads
    dv = ctx_in[2].shape[1] // n_heads

    @pl.when(s_idx == 0)
    def _():
        st_scr[...] = jnp.zeros_like(st_scr)

    def step(ins, outs):
        qf_ref, kf_ref, vf_ref, qb_ref, kb_ref, vb_ref = ins
        of_ref, ob_ref = outs
        row = lax.broadcasted_iota(jnp.int32, (chunk, chunk), 0).astype(F32)
        col = lax.broadcasted_iota(jnp.int32, (chunk, chunk), 1).astype(F32)
        pos = lax.broadcasted_iota(jnp.int32, (chunk, 1), 0).astype(F32)
        dirs = (
            (qf_ref, kf_ref, vf_ref, of_ref, row - col, pos + 1.0, chunk - 1.0 - pos),
            (qb_ref, kb_ref, vb_ref, ob_ref, col - row, chunk - pos, pos),
        )
        for d, (q_ref, k_ref, v_ref, o_ref, rel, q_pow, k_pow) in enumerate(dirs):
            for h in range(n_heads):
                lg = lg_ref[d, head0 + h]
                decay = jnp.where(rel >= 0, jnp.exp(lg * jnp.maximum(rel, 0.0)), 0.0)
                q_decay = jnp.exp(lg * q_pow)
                k_decay = jnp.exp(lg * k_pow)
                chunk_decay = jnp.exp(lg * chunk)
                q = q_ref[:, h * dk:(h + 1) * dk]
                k = k_ref[:, h * dk:(h + 1) * dk]
                v = v_ref[:, h * dv:(h + 1) * dv]
                state = st_scr[d, h]
                inner = _dot_nt(q, k) * decay
                o = _dot(inner.astype(BF16), v) + _dot(q, state.astype(BF16)) * q_decay
                kd = (k.astype(F32) * k_decay).astype(BF16)
                st_scr[d, h] = state * chunk_decay + _dot_tn(kd, v)
                mu = jnp.mean(o, axis=-1, keepdims=True)
                dev = o - mu
                var = jnp.mean(dev * dev, axis=-1, keepdims=True)
                y = dev * lax.rsqrt(var + EPS) * gn_ref[d:d + 1, h * dv:(h + 1) * dv]
                o_ref[:, h * dv:(h + 1) * dv] = y.astype(o_ref.dtype)

    @pl.when(s_idx < ctx_steps)
    def _():
        step(ctx_in, ctx_out)

    @pl.when(s_idx >= ctx_steps)
    def _():
        step(lat_in, lat_out)


def _retention(proj_lat, proj_ctx, log_gamma, gn_g, batch, seq, n_ctx):
    n_heads = log_gamma.shape[1]
    qk = proj_lat.shape[1] // 8
    v = 2 * qk
    chunk = _pick(math.gcd(seq, n_ctx), (256, 128))
    ncc, nlc = n_ctx // chunk, seq // chunk
    dk, dv = qk // n_heads, v // n_heads
    groups = 2
    hpg = n_heads // groups
    qkw, vw = hpg * dk, hpg * dv

    def ctx_f(b, s):
        return b * ncc + jnp.minimum(s, ncc - 1)

    def ctx_b(b, s):
        return b * ncc + (ncc - 1 - jnp.minimum(s, ncc - 1))

    def lat_f(b, s):
        return b * nlc + jnp.maximum(s - ncc, 0)

    def lat_b(b, s):
        return b * nlc + (nlc - 1 - jnp.maximum(s - ncc, 0))

    def in_specs(row):
        return [
            pl.BlockSpec((chunk, qkw), lambda b, g, s: (row(b, s), g)),
            pl.BlockSpec((chunk, qkw), lambda b, g, s: (row(b, s), groups + g)),
            pl.BlockSpec((chunk, vw), lambda b, g, s: (row(b, s), groups + g)),
        ]

    def out_spec(row):
        return pl.BlockSpec((chunk, vw), lambda b, g, s: (row(b, s), g))

    out_ctx = jax.ShapeDtypeStruct((batch * n_ctx, v), BF16)
    out_lat = jax.ShapeDtypeStruct((batch * seq, v), BF16)
    yf_ctx, yb_ctx, yf_lat, yb_lat = pl.pallas_call(
        functools.partial(_rt_kernel, n_heads=hpg, chunk=chunk, ctx_steps=ncc),
        grid=(batch, groups, ncc + nlc),
        in_specs=[pl.BlockSpec(memory_space=pltpu.SMEM), pl.BlockSpec((2, vw), lambda b, g, s: (0, g))]
        + in_specs(ctx_f) + in_specs(ctx_b) + in_specs(lat_f) + in_specs(lat_b),
        out_specs=[out_spec(ctx_f), out_spec(ctx_b), out_spec(lat_f), out_spec(lat_b)],
        out_shape=[out_ctx, out_ctx, out_lat, out_lat],
        scratch_shapes=[pltpu.VMEM((2, hpg, dk, dv), F32)],
        compiler_params=_params("parallel", "parallel", "arbitrary"),
        name="retention",
    )(log_gamma, gn_g, *([proj_ctx] * 6), *([proj_lat] * 6))
    return (yf_lat, yb_lat), (yf_ctx, yb_ctx)


def _axial_tables(rows_count, head_dim):
    rows = jnp.repeat(jnp.arange(rows_count), GRID_W).astype(F32)
    cols = jnp.tile(jnp.arange(GRID_W), rows_count).astype(F32)
    n_freq = head_dim // 4
    inv = ROPE_THETA ** (-jnp.arange(n_freq, dtype=F32) / n_freq)
    ang = jnp.concatenate([rows[:, None] * inv, cols[:, None] * inv], -1)
    cos, sin = jnp.cos(ang), jnp.sin(ang)
    return jnp.concatenate([cos, cos], -1), jnp.concatenate([-sin, sin], -1)


def _linear_tables(n_tokens, head_dim):
    half = head_dim // 2
    inv = ROPE_THETA ** (-jnp.arange(half, dtype=F32) / half)
    ang = jnp.arange(n_tokens, dtype=F32)[:, None] * inv
    return jnp.cos(ang), jnp.sin(ang)


def kernel(x, c, ctx, c_ctx, mod_w, mod_b, norm_g, ffn_w_in, ffn_conv_w, ffn_conv_b, ffn_w_out, ga_wqkv, ga_sink, ga_qk_norm, ga_wo, rt_w_in, rt_decay, rt_gn, rt_wo, df_wqkv, df_lambda, df_qk_norm, df_subln, df_wo):
    batch, seq, d = x.shape
    n_ctx = ctx.shape[1]
    depth = mod_w.shape[0]
    assert batch + 1 <= MOD_ROWS and seq % GRID_W == 0

    cond = jnp.zeros((MOD_ROWS, d), F32).at[0].set(c_ctx).at[1:batch + 1].set(c)
    mods = _adaln_table(cond, mod_w, mod_b).reshape(depth, 6, MOD_ROWS, 1, d)

    ax_cos, ax_sin = _axial_tables(seq // GRID_W, GA_HD)
    assert DF_HD == GA_HD
    rt_dk = d // RT_HEADS
    ln_cos, ln_sin = _linear_tables(seq, rt_dk)
    ones2 = jnp.ones((2, LANES), F32)

    h_lat = x.reshape(batch * seq, d)
    h_ctx = ctx.reshape(batch * n_ctx, d)

    lat = _Stream(seq, lambda i, bm: 1 + i // (seq // bm), lambda i, bm: i % (seq // bm), True)
    cst = _Stream(batch * n_ctx, lambda i, bm: 0, lambda i, bm: 0, False)

    ga_w, ga_o = ga_wqkv.astype(BF16), ga_wo.astype(BF16)
    rt_w, rt_o = rt_w_in.astype(BF16), rt_wo.astype(BF16)
    df_w, df_o = df_wqkv.astype(BF16), df_wo.astype(BF16)
    f_in, f_out = ffn_w_in.astype(BF16), ffn_w_out.astype(BF16)
    f_cb = ffn_conv_b.reshape(depth, 1, -1)

    for i in range(depth):
        need_ctx = i < depth - 1
        kind, j = i % N_MIXERS, i // N_MIXERS
        g1 = norm_g[i, 0].reshape(1, d)
        g2 = norm_g[i, 1].reshape(1, d)

        if kind == 0:
            w_all = ga_w
            qcols = ga_sink.shape[1] * GA_HD
            kcols = (w_all.shape[2] - qcols) // 2
            segments = ((qcols, "norm_rope_128", 0, GA_HD ** -0.5 * LOG2E), (kcols, "norm_rope_128", 1, 1.0),
                        (kcols, "plain", 0, 1.0))
            cos_t, sin_t, gains = ax_cos, ax_sin, ga_qk_norm[j]
        elif kind == 1:
            w_all = rt_w
            vcols = (w_all.shape[2] - 2 * d) // 3
            segments = ((d, "rope_256", 0, rt_dk ** -0.5), (d, "rope_256", 0, 1.0),
                        (vcols, "plain", 0, 1.0), (2 * vcols, "silu", 0, 1.0))
            cos_t, sin_t, gains = ln_cos, ln_sin, ones2
        else:
            w_all = df_w
            qcols = w_all.shape[2] // 3
            segments = ((qcols, "norm_rope_128", 0, DF_HD ** -0.5 * LOG2E), (qcols, "norm_rope_128", 1, 1.0),
                        (qcols, "plain", 0, 1.0))
            cos_t, sin_t, gains = ax_cos, ax_sin, df_qk_norm[j]

        proj = {"lat": _proj(h_lat, g1, mods, i, lat, w_all, j, segments, cos_t, sin_t, gains),
                "ctx": _proj(h_ctx, g1, mods, i, cst, w_all, j, segments, cos_t, sin_t, gains)}

        if kind == 0:
            y_lat, y_ctx = _ga_attention(proj["lat"], proj["ctx"], ga_sink[j], batch, seq, n_ctx, need_ctx)
            h_lat = _oproj(y_lat, ga_o, j, h_lat, mods, i, lat)
            if need_ctx:
                h_ctx = _oproj(y_ctx, ga_o, j, h_ctx, mods, i, cst)
        elif kind == 1:
            log_gamma = jax.nn.log_sigmoid(rt_decay[j].astype(F32))
            y_lat, y_ctx = _retention(proj["lat"], proj["ctx"], log_gamma, rt_gn[j], batch, seq, n_ctx)
            h_lat = _oproj_rt(y_lat[0], y_lat[1], proj["lat"], rt_o, j, h_lat, mods, i, lat)
            if need_ctx:
                h_ctx = _oproj_rt(y_ctx[0], y_ctx[1], proj["ctx"], rt_o, j, h_ctx, mods, i, cst)
        else:
            lambda_init = 0.8 - 0.6 * math.exp(-0.3 * i)
            y_lat, y_ctx = _df_attention(proj["lat"], proj["ctx"], df_lambda[j], df_subln[j], batch, seq, n_ctx,
                                         lambda_init, need_ctx)
            h_lat = _oproj(y_lat, df_o, j, h_lat, mods, i, lat)
            if need_ctx:
                h_ctx = _oproj(y_ctx, df_o, j, h_ctx, mods, i, cst)

        h_lat = _ffn(h_lat, g2, mods, i, lat, f_in, ffn_conv_w, f_cb, f_out, seq)
        if need_ctx:
            h_ctx = _ffn(h_ctx, g2, mods, i, cst, f_in, ffn_conv_w, f_cb, f_out, n_ctx)

    return h_lat.reshape(batch, seq, d)
```

```python
import functools
import math
from typing import Callable, NamedTuple

import jax
import jax.numpy as jnp
from jax import lax
from jax.experimental import pallas as pl
from jax.experimental.pallas import tpu as pltpu

F32 = jnp.float32
BF16 = jnp.bfloat16

EPS = 1e-6
NEG_INF = -1e30
LOG2E = math.log2(math.e)
ROPE_THETA = 10000.0
GRID_W = 64
WINDOW = 128
N_MIXERS = 3

GA_HD = 128
GA_GROUP = 4
RT_HEADS = 8
DF_HD = 128
CONV_W = 3

LANES = 128
MXU_COLS = 256
BF16_ROWS = 16
MOD_ROWS = 8
VMEM_LIMIT = 52 * 1024 * 1024


class _Stream(NamedTuple):
    group: int
    mod_row: Callable
    tab_block: Callable
    rope: bool


def _pick(n, candidates):
    for c in candidates:
        if n % c == 0:
            return c
    raise ValueError(f"no tile size in {candidates} divides {n}")


def _params(*sem):
    return pltpu.CompilerParams(dimension_semantics=sem, vmem_limit_bytes=VMEM_LIMIT)


def _silu(x):
    return x * (1.0 / (1.0 + jnp.exp(-x)))


def _dot(a, b):
    return jnp.dot(a, b, preferred_element_type=F32)


def _dot_nt(a, b):
    return lax.dot_general(a, b, (((1,), (1,)), ((), ())), preferred_element_type=F32)


def _dot_tn(a, b):
    return lax.dot_general(a, b, (((0,), (0,)), ((), ())), preferred_element_type=F32)


def _adaln_kernel(c_ref, w_ref, b_ref, o_ref):
    s = _silu(c_ref[...]).astype(BF16)
    o_ref[...] = _dot(s, w_ref[...].astype(BF16)) + b_ref[...]


def _adaln_table(cond, mod_w, mod_b):
    depth, d, _ = mod_w.shape
    bn = _pick(d, (1024, 512, 256, 128))
    nj = d // bn
    return pl.pallas_call(
        _adaln_kernel,
        grid=(depth, 6, nj),
        in_specs=[
            pl.BlockSpec((MOD_ROWS, d), lambda l, k, j: (0, 0)),
            pl.BlockSpec((None, d, bn), lambda l, k, j: (l, 0, k * nj + j)),
            pl.BlockSpec((None, 1, bn), lambda l, k, j: (l, 0, k * nj + j)),
        ],
        out_specs=pl.BlockSpec((None, None, MOD_ROWS, bn), lambda l, k, j: (l, k, 0, j)),
        out_shape=jax.ShapeDtypeStruct((depth, 6, MOD_ROWS, d), F32),
        compiler_params=_params("parallel", "parallel", "parallel"),
        name="adaln_table",
    )(cond, mod_w, mod_b.reshape(depth, 1, 6 * d))


def _mod_spec(layer, slot, row_fn, d):
    return pl.BlockSpec((None, None, None, 1, d), lambda i, j: (layer, slot, row_fn(i), 0, 0))


def _mod_spec_cols(layer, slot, row_fn, bn):
    return pl.BlockSpec((None, None, None, 1, bn), lambda i, j: (layer, slot, row_fn(i), 0, j))


def _norm_mod(x, g, shift, scale):
    ms = jnp.mean(x * x, axis=-1, keepdims=True)
    y = x * lax.rsqrt(ms + EPS) * g
    return y * (1.0 + scale) + shift


def _proj_kernel(h_ref, g_ref, sh_ref, sc_ref, w_ref, cos_ref, sin_ref, gain_ref, seg_ref, o_ref, a_scr, acc_scr,
                 *, plan, rope):
    j = pl.program_id(1)
    sub = acc_scr.shape[2]
    n_sub = o_ref.shape[1] // sub

    @pl.when(j == 0)
    def _():
        a_scr[...] = _norm_mod(h_ref[...], g_ref[...], sh_ref[...], sc_ref[...]).astype(BF16)

    def epilogue(s, kind, arg, scale):
        base = s * sub
        if kind == "plain":
            o_ref[:, base:base + sub] = acc_scr[s].astype(o_ref.dtype)
        elif kind == "silu":
            o_ref[:, base:base + sub] = _silu(acc_scr[s]).astype(o_ref.dtype)
        elif kind == "norm_rope_128":
            x = acc_scr[s]
            ms = _dot((x * x).astype(BF16), seg_ref[...])
            y = x * lax.rsqrt(ms + EPS) * (gain_ref[arg:arg + 1, :] * scale)
            for t in range(0, sub, LANES):
                yt = y[:, t:t + LANES]
                if rope:
                    yt = yt * cos_ref[...] + pltpu.roll(yt, LANES // 2, 1) * sin_ref[...]
                o_ref[:, base + t:base + t + LANES] = yt.astype(o_ref.dtype)
        else:
            assert kind == "rope_256" and sub == 2 * LANES
            x1 = acc_scr[s, :, 0:LANES] * scale
            x2 = acc_scr[s, :, LANES:sub] * scale
            if rope:
                c, sn = cos_ref[...], sin_ref[...]
                x1, x2 = x1 * c - x2 * sn, x1 * sn + x2 * c
            o_ref[:, base:base + LANES] = x1.astype(o_ref.dtype)
            o_ref[:, base + LANES:base + sub] = x2.astype(o_ref.dtype)

    for lo, hi, kinds in plan:
        @pl.when((j >= lo) & (j < hi))
        def _(kinds=kinds):
            a = a_scr[...]
            for s in range(n_sub):
                acc_scr[s] = _dot(a, w_ref[:, s * sub:(s + 1) * sub])
            for s, (kind, arg, scale) in enumerate(kinds):
                epilogue(s, kind, arg, scale)


def _proj(h, norm_g, mods, layer, st, w_all, w_idx, segments, cos_t, sin_t, gains):
    m, d = h.shape
    n = w_all.shape[2]
    bm = _pick(st.group, (1024, 512, 256))
    bn = _pick(n, (1024, 512, 256))
    sub = MXU_COLS
    n_sub = bn // sub
    per_sub = []
    for width, kind, arg, scale in segments:
        assert width % sub == 0
        per_sub += [(kind, arg, scale)] * (width // sub)
    assert len(per_sub) * sub == n
    tiles = [tuple(per_sub[t * n_sub:(t + 1) * n_sub]) for t in range(n // bn)]
    plan, lo = [], 0
    for t in range(1, len(tiles) + 1):
        if t == len(tiles) or tiles[t] != tiles[lo]:
            plan.append((lo, t, tiles[lo]))
            lo = t
    row_fn, tab_fn = st.mod_row, st.tab_block
    kernel = functools.partial(_proj_kernel, plan=tuple(plan), rope=st.rope)
    head = jnp.arange(sub) // LANES
    seg_mean = jnp.where(head[:, None] == head[None, :], 1.0 / LANES, 0.0).astype(BF16)
    gains = jnp.tile(gains, (1, sub // LANES))
    return pl.pallas_call(
        kernel,
        grid=(m // bm, n // bn),
        in_specs=[
            pl.BlockSpec((bm, d), lambda i, j: (i, 0)),
            pl.BlockSpec((1, d), lambda i, j: (0, 0)),
            _mod_spec(layer, 0, lambda i: row_fn(i, bm), d),
            _mod_spec(layer, 1, lambda i: row_fn(i, bm), d),
            pl.BlockSpec((None, d, bn), lambda i, j: (w_idx, 0, j)),
            pl.BlockSpec((bm, LANES), lambda i, j: (tab_fn(i, bm), 0)),
            pl.BlockSpec((bm, LANES), lambda i, j: (tab_fn(i, bm), 0)),
            pl.BlockSpec(gains.shape, lambda i, j: (0, 0)),
            pl.BlockSpec((sub, sub), lambda i, j: (0, 0)),
        ],
        out_specs=pl.BlockSpec((bm, bn), lambda i, j: (i, j)),
        out_shape=jax.ShapeDtypeStruct((m, n), BF16),
        scratch_shapes=[pltpu.VMEM((bm, d), BF16), pltpu.VMEM((n_sub, bm, sub), F32)],
        compiler_params=_params("parallel", "arbitrary"),
        name="proj",
    )(h, norm_g, mods, mods, w_all, cos_t, sin_t, gains, seg_mean)


def _oproj_kernel(y_ref, w_ref, h_ref, gate_ref, o_ref):
    o_ref[...] = h_ref[...] + gate_ref[...] * _dot(y_ref[...], w_ref[...])


def _oproj_rt_kernel(yf_ref, yb_ref, sf_ref, sb_ref, w_ref, h_ref, gate_ref, o_ref):
    c = pl.program_id(1)

    @pl.when(c == 0)
    def _():
        o_ref[...] = jnp.zeros_like(o_ref)

    ck = w_ref.shape[0]
    sub = min(ck, MXU_COLS)
    part = None
    for s in range(0, ck, sub):
        cols = slice(s, s + sub)
        z = sf_ref[:, cols] * yf_ref[:, cols] + sb_ref[:, cols] * yb_ref[:, cols]
        d = _dot(z, w_ref[cols, :])
        part = d if part is None else part + d
    o_ref[...] += part

    @pl.when(c == pl.num_programs(1) - 1)
    def _():
        o_ref[...] = h_ref[...] + gate_ref[...] * o_ref[...]


def _oproj(y, w_all, w_idx, h, mods, layer, st):
    m, k = y.shape
    d = w_all.shape[2]
    bm = _pick(st.group, (1024, 512, 256))
    row_fn = st.mod_row
    bn = _pick(d, (512, 256, 128))
    return pl.pallas_call(
        _oproj_kernel,
        grid=(m // bm, d // bn),
        in_specs=[
            pl.BlockSpec((bm, k), lambda i, j: (i, 0)),
            pl.BlockSpec((None, k, bn), lambda i, j: (w_idx, 0, j)),
            pl.BlockSpec((bm, bn), lambda i, j: (i, j)),
            _mod_spec_cols(layer, 2, lambda i: row_fn(i, bm), bn),
        ],
        out_specs=pl.BlockSpec((bm, bn), lambda i, j: (i, j)),
        out_shape=jax.ShapeDtypeStruct((m, d), F32),
        input_output_aliases={2: 0},
        compiler_params=_params("parallel", "arbitrary"),
        name="oproj",
    )(y, w_all, h, mods)


def _oproj_rt(yf, yb, proj, w_all, w_idx, h, mods, layer, st):
    m, v = yf.shape
    d = w_all.shape[2]
    bm = _pick(st.group, (512, 256))
    row_fn = st.mod_row
    ck = _pick(v, (1024, 512, 256, 128))
    nc = v // ck
    gcol = (proj.shape[1] // v - 2) * nc
    return pl.pallas_call(
        _oproj_rt_kernel,
        grid=(m // bm, nc),
        in_specs=[
            pl.BlockSpec((bm, ck), lambda i, c: (i, c)),
            pl.BlockSpec((bm, ck), lambda i, c: (i, c)),
            pl.BlockSpec((bm, ck), lambda i, c: (i, gcol + c)),
            pl.BlockSpec((bm, ck), lambda i, c: (i, gcol + nc + c)),
            pl.BlockSpec((None, ck, d), lambda i, c: (w_idx, c, 0)),
            pl.BlockSpec((bm, d), lambda i, c: (i, 0)),
            _mod_spec(layer, 2, lambda i: row_fn(i, bm), d),
        ],
        out_specs=pl.BlockSpec((bm, d), lambda i, c: (i, 0)),
        out_shape=jax.ShapeDtypeStruct((m, d), F32),
        input_output_aliases={5: 0},
        compiler_params=_params("parallel", "arbitrary"),
        name="oproj_rt",
    )(yf, yb, proj, proj, w_all, h, mods)


def _ffn_kernel(h_ref, hp_ref, hn_ref, g_ref, sh_ref, sc_ref, gate_ref, wa_ref, wb_ref,
                cwa_lo, cwb_lo, cba_lo, cbb_lo, wo_lo, cwa_hi, cwb_hi, cba_hi, cbb_hi, wo_hi,
                o_ref, a_scr, u_even, u_odd_a, u_odd_b, *, bm, seq_len, steps):
    i = pl.program_id(0)
    j = pl.program_id(1)
    halo = BF16_ROWS
    sub = u_even.shape[2]
    lo = (cwa_lo, cwb_lo, cba_lo, cbb_lo, wo_lo)
    hi = (cwa_hi, cwb_hi, cba_hi, cbb_hi, wo_hi)

    def up(half, u):
        cols = slice(half * sub, (half + 1) * sub)
        a = a_scr[...]
        u[0] = _dot(a, wa_ref[:, cols])
        u[1] = _dot(a, wb_ref[:, cols])

    def gate(u, cwa, cwb, cba, cbb, wo):
        def conv(idx, cw_ref, cb_ref):
            prev = u[idx, halo - 1:halo - 1 + bm, :]
            cur = u[idx, halo:halo + bm, :]
            nxt = u[idx, halo + 1:halo + 1 + bm, :]
            return cb_ref[...] + prev * cw_ref[0:1, :] + cur * cw_ref[1:2, :] + nxt * cw_ref[2:3, :]

        return (_silu(conv(0, cwa, cba)) * conv(1, cwb, cbb)).astype(BF16)

    def down(u, *params):
        o_ref[...] += _dot(gate(u, *params), params[-1][...])

    @pl.when(j == 0)
    def _():
        def nm(x):
            return _norm_mod(x, g_ref[...], sh_ref[...], sc_ref[...])

        keep_prev = jnp.where((i * bm) % seq_len != 0, 1.0, 0.0)
        keep_next = jnp.where(((i + 1) * bm) % seq_len != 0, 1.0, 0.0)
        a_scr[0:halo, :] = (nm(hp_ref[...]) * keep_prev).astype(BF16)
        a_scr[halo:halo + bm, :] = nm(h_ref[...]).astype(BF16)
        a_scr[halo + bm:, :] = (nm(hn_ref[...]) * keep_next).astype(BF16)
        o_ref[...] = jnp.zeros_like(o_ref)
        up(0, u_even)
        up(1, u_odd_a)
        down(u_even, *hi)

    def middle(u_new, u_old):
        up(0, u_even)
        up(1, u_new)
        g = jnp.concatenate([gate(u_old, *lo), gate(u_even, *hi)], axis=1)
        w = jnp.concatenate([wo_lo[...], wo_hi[...]], axis=0)
        o_ref[...] += _dot(g, w)

    inner = (j > 0) & (j < steps - 1)
    pl.when(inner & (j % 2 == 1))(lambda: middle(u_odd_b, u_odd_a))
    pl.when(inner & (j % 2 == 0))(lambda: middle(u_odd_a, u_odd_b))

    @pl.when(j == steps - 1)
    def _():
        down(u_odd_a if (steps - 1) % 2 == 1 else u_odd_b, *lo)
        o_ref[...] = h_ref[...] + gate_ref[...] * o_ref[...]


def _ffn(h, norm_g, mods, layer, st, w_in, conv_w, conv_b, w_out, seq_len):
    m, d = h.shape
    dff = w_out.shape[1]
    bm = _pick(seq_len, (512, 256))
    sub = MXU_COLS
    nsc = dff // sub
    assert dff % (2 * sub) == 0
    steps = nsc // 2 + 1
    halo = BF16_ROWS
    hb = bm // halo
    last = m // halo - 1
    kernel = functools.partial(_ffn_kernel, bm=bm, seq_len=seq_len, steps=steps)
    mrow = lambda i: st.mod_row(i, bm)
    pair = lambda j: jnp.minimum(j, steps - 2)
    chunk = (lambda j: jnp.maximum(2 * j - 1, 0), lambda j: jnp.minimum(2 * j, nsc - 1))

    def down_specs(k):
        return [
            pl.BlockSpec((None, CONV_W, sub), lambda i, j: (layer, 0, k(j))),
            pl.BlockSpec((None, CONV_W, sub), lambda i, j: (layer, 0, nsc + k(j))),
            pl.BlockSpec((None, 1, sub), lambda i, j: (layer, 0, k(j))),
            pl.BlockSpec((None, 1, sub), lambda i, j: (layer, 0, nsc + k(j))),
            pl.BlockSpec((None, sub, d), lambda i, j: (layer, k(j), 0)),
        ]

    down_args = [conv_w, conv_w, conv_b, conv_b, w_out]
    return pl.pallas_call(
        kernel,
        grid=(m // bm, steps),
        in_specs=[
            pl.BlockSpec((bm, d), lambda i, j: (i, 0)),
            pl.BlockSpec((halo, d), lambda i, j: (jnp.maximum(i * hb - 1, 0), 0)),
            pl.BlockSpec((halo, d), lambda i, j: (jnp.minimum((i + 1) * hb, last), 0)),
            pl.BlockSpec((1, d), lambda i, j: (0, 0)),
            _mod_spec(layer, 3, mrow, d),
            _mod_spec(layer, 4, mrow, d),
            _mod_spec(layer, 5, mrow, d),
            pl.BlockSpec((None, d, 2 * sub), lambda i, j: (layer, 0, pair(j))),
            pl.BlockSpec((None, d, 2 * sub), lambda i, j: (layer, 0, nsc // 2 + pair(j))),
        ] + down_specs(chunk[0]) + down_specs(chunk[1]),
        out_specs=pl.BlockSpec((bm, d), lambda i, j: (i, 0)),
        out_shape=jax.ShapeDtypeStruct((m, d), F32),
        scratch_shapes=[pltpu.VMEM((bm + 2 * halo, d), BF16),
                        pltpu.VMEM((2, bm + 2 * halo, sub), F32),
                        pltpu.VMEM((2, bm + 2 * halo, sub), F32),
                        pltpu.VMEM((2, bm + 2 * halo, sub), F32)],
        compiler_params=_params("parallel", "arbitrary"),
        name="conv_ffn",
    )(h, h, h, norm_g, mods, mods, mods, w_in, w_in, *down_args, *down_args)


def _ga_kernel(sink_ref, q_ref, kvc_ref, *rest, seq, n_kv, local):
    if local:
        kvo_ref, kvp_ref, kvn_ref, o_ref, s_a, s_b = rest
    else:
        o_ref, s_a, s_b = rest
    hd = GA_HD
    kcols = n_kv * hd
    bq = q_ref.shape[0]
    sq = s_a.shape[1]
    n_ctx = kvc_ref.shape[0]
    bufs = (s_a, s_b)

    def band(u):
        if not local:
            return [], 0
        lo, hi = u * sq - WINDOW, (u + 1) * sq + WINDOW
        pieces = []
        if lo < 0:
            pieces.append((kvp_ref, WINDOW + lo, -lo))
        pieces.append((kvo_ref, max(lo, 0), min(hi, bq) - max(lo, 0)))
        if hi > bq:
            pieces.append((kvn_ref, 0, hi - bq))
        return pieces, lo

    def bias(u):
        pieces, lo = band(u)
        n_loc = sum(p[2] for p in pieces)
        kl = lax.broadcasted_iota(jnp.int32, (n_loc, sq), 0) + lo
        ql = lax.broadcasted_iota(jnp.int32, (n_loc, sq), 1) + u * sq
        pos = pl.program_id(1) * bq + kl
        ok = (jnp.abs(ql - kl) <= WINDOW) & (pos >= 0) & (pos < seq)
        return jnp.concatenate([jnp.where(ok, 0.0, NEG_INF).astype(F32), jnp.zeros((n_ctx, sq), F32)], axis=0)

    def keys_values(u, n):
        pieces, _ = band(u)
        ks, vs = slice(n * hd, (n + 1) * hd), slice(kcols + n * hd, kcols + (n + 1) * hd)
        k_all = jnp.concatenate([r[a:a + c, ks] for r, a, c in pieces] + [kvc_ref[:, ks]], axis=0)
        v_all = jnp.concatenate([r[a:a + c, vs] for r, a, c in pieces] + [kvc_ref[:, vs]], axis=0)
        return k_all, v_all.astype(F32).T.astype(BF16)

    chains = [(u, n, g) for u in range(bq // sq) for n in range(n_kv) for g in range(GA_GROUP)]
    cache = {}

    def operands(u, n):
        if (u, n) not in cache:
            cache.clear()
            cache[(u, n)] = keys_values(u, n)
        return cache[(u, n)]

    biases = {}

    def scores(idx):
        u, n, g = chains[idx]
        head = n * GA_GROUP + g
        k_all, _ = operands(u, n)
        s = _dot_nt(k_all, q_ref[u * sq:(u + 1) * sq, head * hd:(head + 1) * hd])
        if local:
            if u not in biases:
                biases[u] = bias(u)
            s = s + biases[u]
        bufs[idx % 2][...] = s

    scores(0)
    for idx, (u, n, g) in enumerate(chains):
        _, v_t = operands(u, n)
        if idx + 1 < len(chains):
            scores(idx + 1)
        buf = bufs[idx % 2]
        head = n * GA_GROUP + g
        sink = sink_ref[head] * LOG2E
        m = jnp.maximum(jnp.max(buf[...], axis=0, keepdims=True), sink)
        p = jnp.exp2(buf[...] - m)
        den = jnp.sum(p, axis=0, keepdims=True) + jnp.exp2(sink - m)
        o_t = _dot(v_t, p.astype(BF16)) / den
        o_ref[u * sq:(u + 1) * sq, head * hd:(head + 1) * hd] = o_t.T.astype(o_ref.dtype)


def _ga_attention(qkv_lat, qkv_ctx, sink, batch, seq, n_ctx, need_ctx):
    n_heads = sink.shape[0]
    qcols = n_heads * GA_HD
    n_kv = n_heads // GA_GROUP
    kvw = 2 * n_kv * GA_HD
    kvblk = qcols // kvw
    assert qcols % kvw == 0
    bq = _pick(seq, (512, 256, 128))
    sq = min(bq, MXU_COLS)
    nq = seq // bq
    wb = bq // WINDOW
    last_w = batch * seq // WINDOW - 1
    smem = pl.BlockSpec(memory_space=pltpu.SMEM)
    lat = pl.pallas_call(
        functools.partial(_ga_kernel, seq=seq, n_kv=n_kv, local=True),
        grid=(batch, nq),
        in_specs=[
            smem,
            pl.BlockSpec((bq, qcols), lambda b, i: (b * nq + i, 0)),
            pl.BlockSpec((n_ctx, kvw), lambda b, i: (b, kvblk)),
            pl.BlockSpec((bq, kvw), lambda b, i: (b * nq + i, kvblk)),
            pl.BlockSpec((WINDOW, kvw), lambda b, i: (jnp.maximum((b * nq + i) * wb - 1, 0), kvblk)),
            pl.BlockSpec((WINDOW, kvw), lambda b, i: (jnp.minimum((b * nq + i + 1) * wb, last_w), kvblk)),
        ],
        out_specs=pl.BlockSpec((bq, qcols), lambda b, i: (b * nq + i, 0)),
        out_shape=jax.ShapeDtypeStruct((batch * seq, qcols), BF16),
        scratch_shapes=[pltpu.VMEM((sq + 2 * WINDOW + n_ctx, sq), F32)] * 2,
        compiler_params=_params("parallel", "parallel"),
        name="ga_attention_latent",
    )(sink, qkv_lat, qkv_ctx, qkv_lat, qkv_lat, qkv_lat)
    ctx = None
    if need_ctx:
        cq = min(n_ctx, MXU_COLS)
        ctx = pl.pallas_call(
            functools.partial(_ga_kernel, seq=n_ctx, n_kv=n_kv, local=False),
            grid=(batch,),
            in_specs=[
                smem,
                pl.BlockSpec((n_ctx, qcols), lambda b: (b, 0)),
                pl.BlockSpec((n_ctx, kvw), lambda b: (b, kvblk)),
            ],
            out_specs=pl.BlockSpec((n_ctx, qcols), lambda b: (b, 0)),
            out_shape=jax.ShapeDtypeStruct((batch * n_ctx, qcols), BF16),
            scratch_shapes=[pltpu.VMEM((n_ctx, cq), F32)] * 2,
            compiler_params=_params("parallel"),
            name="ga_attention_context",
        )(sink, qkv_ctx, qkv_ctx)
    return lat, ctx


def _df_kernel(lam_ref, g_ref, q_ref, *rest, lambda_init, chunks, first_q_axis):
    n_seg = len(chunks)
    kv_refs = rest[:2 * n_seg]
    o_ref = rest[2 * n_seg]
    vt_scrs = rest[2 * n_seg + 1:2 * n_seg + 1 + n_seg]
    acc_scr, s_a, s_b = rest[-3:]
    hd = DF_HD
    bq = q_ref.shape[0]
    lam = lam_ref[...]
    lmbda = (jnp.exp(jnp.sum(lam[0:1] * lam[1:2], axis=-1, keepdims=True))
             - jnp.exp(jnp.sum(lam[2:3] * lam[3:4], axis=-1, keepdims=True)) + lambda_init)

    def transpose_values():
        for seg, ck in enumerate(chunks):
            v_ref, vt = kv_refs[2 * seg + 1], vt_scrs[seg]
            for c in range(v_ref.shape[0] // ck):
                vt[c] = v_ref[c * ck:(c + 1) * ck, :].astype(F32).T.astype(BF16)

    if first_q_axis is None:
        transpose_values()
    else:
        pl.when(pl.program_id(first_q_axis) == 0)(transpose_values)

    qs = (q_ref[:, 0:hd], q_ref[:, hd:2 * hd])
    acc_scr[...] = jnp.zeros_like(acc_scr)
    stat0 = (jnp.full((1, bq), NEG_INF, F32), jnp.zeros((1, bq), F32))
    carry = (stat0, stat0)

    def scores(k_ref, c, ck, dst):
        start = c * ck if isinstance(c, int) else pl.multiple_of(c * ck, ck)
        kblk = k_ref[pl.ds(start, ck), :]
        for r in range(2):
            dst[r] = _dot_nt(kblk[:, r * hd:(r + 1) * hd], qs[r])

    def update(src, v_t, carry):
        new = []
        for r in range(2):
            m_old, l_old = carry[r]
            m_new = jnp.maximum(m_old, jnp.max(src[r], axis=0, keepdims=True))
            alpha = jnp.exp2(m_old - m_new)
            p = jnp.exp2(src[r] - m_new)
            l_new = alpha * l_old + jnp.sum(p, axis=0, keepdims=True)
            acc_scr[r] = alpha * acc_scr[r] + _dot(v_t, p.astype(BF16))
            new.append((m_new, l_new))
        return tuple(new)

    for seg in range(n_seg - 1):
        k_ref, vt, ck = kv_refs[2 * seg], vt_scrs[seg], chunks[seg]
        for c in range(k_ref.shape[0] // ck):
            scores(k_ref, c, ck, s_a.at[:, 0:ck, :])
            carry = update(s_a.at[:, 0:ck, :], vt[c], carry)

    k_ref, vt, ck = kv_refs[2 * n_seg - 2], vt_scrs[n_seg - 1], chunks[n_seg - 1]
    n = k_ref.shape[0] // ck
    pairs = (n - 1) // 2
    scores(k_ref, 0, ck, s_a)

    def body(j, carry):
        scores(k_ref, 2 * j + 1, ck, s_b)
        carry = update(s_a, vt[2 * j], carry)
        scores(k_ref, 2 * j + 2, ck, s_a)
        return update(s_b, vt[2 * j + 1], carry)

    carry = lax.fori_loop(0, pairs, body, carry)
    if n - 2 * pairs == 2:
        scores(k_ref, n - 1, ck, s_b)
        carry = update(s_a, vt[n - 2], carry)
        carry = update(s_b, vt[n - 1], carry)
    else:
        carry = update(s_a, vt[n - 1], carry)
    (_, l0), (_, l1) = carry
    o = (acc_scr[0] / l0 - lmbda * (acc_scr[1] / l1)).T
    ms = jnp.mean(o * o, axis=-1, keepdims=True)
    y = o * lax.rsqrt(ms + EPS) * g_ref[...] * (1.0 - lambda_init)
    o_ref[...] = y.astype(o_ref.dtype)


def _df_attention(qkv_lat, qkv_ctx, lam, subln_g, batch, seq, n_ctx, lambda_init, need_ctx):
    hw = 2 * DF_HD
    n_heads = qkv_lat.shape[1] // (3 * hw)
    bq = _pick(seq, (512, 256, 128))
    nq = seq // bq
    ck_lat = _pick(seq, (512, 256, 128))
    ck_ctx = _pick(n_ctx, (512, 256, 128))
    g2 = subln_g.reshape(1, hw)
    const = lambda shape: pl.BlockSpec(shape, lambda *_: (0,) * len(shape))
    vt_ctx = pltpu.VMEM((n_ctx // ck_ctx, hw, ck_ctx), BF16)
    vt_lat = pltpu.VMEM((seq // ck_lat, hw, ck_lat), BF16)
    lat = pl.pallas_call(
        functools.partial(_df_kernel, lambda_init=lambda_init, chunks=(ck_ctx, ck_lat), first_q_axis=2),
        grid=(batch, n_heads, nq),
        in_specs=[
            const(lam.shape),
            const((1, hw)),
            pl.BlockSpec((bq, hw), lambda b, h, i: (b * nq + i, h)),
            pl.BlockSpec((n_ctx, hw), lambda b, h, i: (b, n_heads + h)),
            pl.BlockSpec((n_ctx, hw), lambda b, h, i: (b, 2 * n_heads + h)),
            pl.BlockSpec((seq, hw), lambda b, h, i: (b, n_heads + h)),
            pl.BlockSpec((seq, hw), lambda b, h, i: (b, 2 * n_heads + h)),
        ],
        out_specs=pl.BlockSpec((bq, hw), lambda b, h, i: (b * nq + i, h)),
        out_shape=jax.ShapeDtypeStruct((batch * seq, n_heads * hw), BF16),
        scratch_shapes=[vt_ctx, vt_lat, pltpu.VMEM((2, hw, bq), F32)]
        + [pltpu.VMEM((2, max(ck_lat, ck_ctx), bq), F32)] * 2,
        compiler_params=_params("parallel", "parallel", "arbitrary"),
        name="df_attention_latent",
    )(lam, g2, qkv_lat, qkv_ctx, qkv_ctx, qkv_lat, qkv_lat)
    ctx = None
    if need_ctx:
        ctx = pl.pallas_call(
            functools.partial(_df_kernel, lambda_init=lambda_init, chunks=(ck_ctx,), first_q_axis=None),
            grid=(batch, n_heads),
            in_specs=[
                const(lam.shape),
                const((1, hw)),
                pl.BlockSpec((n_ctx, hw), lambda b, h: (b, h)),
                pl.BlockSpec((n_ctx, hw), lambda b, h: (b, n_heads + h)),
                pl.BlockSpec((n_ctx, hw), lambda b, h: (b, 2 * n_heads + h)),
            ],
            out_specs=pl.BlockSpec((n_ctx, hw), lambda b, h: (b, h)),
            out_shape=jax.ShapeDtypeStruct((batch * n_ctx, n_heads * hw), BF16),
            scratch_shapes=[vt_ctx, pltpu.VMEM((2, hw, n_ctx), F32)]
            + [pltpu.VMEM((2, ck_ctx, n_ctx), F32)] * 2,
            compiler_params=_params("parallel", "parallel"),
            name="df_attention_context",
        )(lam, g2, qkv_ctx, qkv_ctx, qkv_ctx)
    return lat, ctx


def _rt_kernel(lg_ref, gn_ref, *refs, n_heads, chunk, ctx_steps):
    ctx_in, lat_in = refs[0:6], refs[6:12]
    ctx_out, lat_out = refs[12:14], refs[14:16]
    st_scr = refs[16]
    head0 = pl.program_id(1) * n_heads
    s_idx = pl.program_id(2)
    dk = ctx_in[0].shape[1] // n_heads
    dv = ctx_in[2].shape[1] // n_heads

    @pl.when(s_idx == 0)
    def _():
        st_scr[...] = jnp.zeros_like(st_scr)

    def step(ins, outs):
        qf_ref, kf_ref, vf_ref, qb_ref, kb_ref, vb_ref = ins
        of_ref, ob_ref = outs
        row = lax.broadcasted_iota(jnp.int32, (chunk, chunk), 0).astype(F32)
        col = lax.broadcasted_iota(jnp.int32, (chunk, chunk), 1).astype(F32)
        pos = lax.broadcasted_iota(jnp.int32, (chunk, 1), 0).astype(F32)
        dirs = (
            (qf_ref, kf_ref, vf_ref, of_ref, row - col, pos + 1.0, chunk - 1.0 - pos),
            (qb_ref, kb_ref, vb_ref, ob_ref, col - row, chunk - pos, pos),
        )
        for d, (q_ref, k_ref, v_ref, o_ref, rel, q_pow, k_pow) in enumerate(dirs):
            for h in range(n_heads):
                lg = lg_ref[d, head0 + h]
                decay = jnp.where(rel >= 0, jnp.exp(lg * jnp.maximum(rel, 0.0)), 0.0)
                q_decay = jnp.exp(lg * q_pow)
                k_decay = jnp.exp(lg * k_pow)
                chunk_decay = jnp.exp(lg * chunk)
                q = q_ref[:, h * dk:(h + 1) * dk]
                k = k_ref[:, h * dk:(h + 1) * dk]
                v = v_ref[:, h * dv:(h + 1) * dv]
                state = st_scr[d, h]
                inner = _dot_nt(q, k) * decay
                o = _dot(inner.astype(BF16), v) + _dot(q, state.astype(BF16)) * q_decay
                kd = (k.astype(F32) * k_decay).astype(BF16)
                st_scr[d, h] = state * chunk_decay + _dot_tn(kd, v)
                mu = jnp.mean(o, axis=-1, keepdims=True)
                dev = o - mu
                var = jnp.mean(dev * dev, axis=-1, keepdims=True)
                y = dev * lax.rsqrt(var + EPS) * gn_ref[d:d + 1, h * dv:(h + 1) * dv]
                o_ref[:, h * dv:(h + 1) * dv] = y.astype(o_ref.dtype)

    @pl.when(s_idx < ctx_steps)
    def _():
        step(ctx_in, ctx_out)

    @pl.when(s_idx >= ctx_steps)
    def _():
        step(lat_in, lat_out)


def _retention(proj_lat, proj_ctx, log_gamma, gn_g, batch, seq, n_ctx):
    n_heads = log_gamma.shape[1]
    qk = proj_lat.shape[1] // 8
    v = 2 * qk
    chunk = _pick(math.gcd(seq, n_ctx), (256, 128))
    ncc, nlc = n_ctx // chunk, seq // chunk
    dk, dv = qk // n_heads, v // n_heads
    groups = 2
    hpg = n_heads // groups
    qkw, vw = hpg * dk, hpg * dv

    def ctx_f(b, s):
        return b * ncc + jnp.minimum(s, ncc - 1)

    def ctx_b(b, s):
        return b * ncc + (ncc - 1 - jnp.minimum(s, ncc - 1))

    def lat_f(b, s):
        return b * nlc + jnp.maximum(s - ncc, 0)

    def lat_b(b, s):
        return b * nlc + (nlc - 1 - jnp.maximum(s - ncc, 0))

    def in_specs(row):
        return [
            pl.BlockSpec((chunk, qkw), lambda b, g, s: (row(b, s), g)),
            pl.BlockSpec((chunk, qkw), lambda b, g, s: (row(b, s), groups + g)),
            pl.BlockSpec((chunk, vw), lambda b, g, s: (row(b, s), groups + g)),
        ]

    def out_spec(row):
        return pl.BlockSpec((chunk, vw), lambda b, g, s: (row(b, s), g))

    out_ctx = jax.ShapeDtypeStruct((batch * n_ctx, v), BF16)
    out_lat = jax.ShapeDtypeStruct((batch * seq, v), BF16)
    yf_ctx, yb_ctx, yf_lat, yb_lat = pl.pallas_call(
        functools.partial(_rt_kernel, n_heads=hpg, chunk=chunk, ctx_steps=ncc),
        grid=(batch, groups, ncc + nlc),
        in_specs=[pl.BlockSpec(memory_space=pltpu.SMEM), pl.BlockSpec((2, vw), lambda b, g, s: (0, g))]
        + in_specs(ctx_f) + in_specs(ctx_b) + in_specs(lat_f) + in_specs(lat_b),
        out_specs=[out_spec(ctx_f), out_spec(ctx_b), out_spec(lat_f), out_spec(lat_b)],
        out_shape=[out_ctx, out_ctx, out_lat, out_lat],
        scratch_shapes=[pltpu.VMEM((2, hpg, dk, dv), F32)],
        compiler_params=_params("parallel", "parallel", "arbitrary"),
        name="retention",
    )(log_gamma, gn_g, *([proj_ctx] * 6), *([proj_lat] * 6))
    return (yf_lat, yb_lat), (yf_ctx, yb_ctx)


def _axial_tables(rows_count, head_dim):
    rows = jnp.repeat(jnp.arange(rows_count), GRID_W).astype(F32)
    cols = jnp.tile(jnp.arange(GRID_W), rows_count).astype(F32)
    n_freq = head_dim // 4
    inv = ROPE_THETA ** (-jnp.arange(n_freq, dtype=F32) / n_freq)
    ang = jnp.concatenate([rows[:, None] * inv, cols[:, None] * inv], -1)
    cos, sin = jnp.cos(ang), jnp.sin(ang)
    return jnp.concatenate([cos, cos], -1), jnp.concatenate([-sin, sin], -1)


def _linear_tables(n_tokens, head_dim):
    half = head_dim // 2
    inv = ROPE_THETA ** (-jnp.arange(half, dtype=F32) / half)
    ang = jnp.arange(n_tokens, dtype=F32)[:, None] * inv
    return jnp.cos(ang), jnp.sin(ang)


def kernel(x, c, ctx, c_ctx, mod_w, mod_b, norm_g, ffn_w_in, ffn_conv_w, ffn_conv_b, ffn_w_out, ga_wqkv, ga_sink, ga_qk_norm, ga_wo, rt_w_in, rt_decay, rt_gn, rt_wo, df_wqkv, df_lambda, df_qk_norm, df_subln, df_wo):
    batch, seq, d = x.shape
    n_ctx = ctx.shape[1]
    depth = mod_w.shape[0]
    assert batch + 1 <= MOD_ROWS and seq % GRID_W == 0

    cond = jnp.zeros((MOD_ROWS, d), F32).at[0].set(c_ctx).at[1:batch + 1].set(c)
    mods = _adaln_table(cond, mod_w, mod_b).reshape(depth, 6, MOD_ROWS, 1, d)

    ax_cos, ax_sin = _axial_tables(seq // GRID_W, GA_HD)
    assert DF_HD == GA_HD
    rt_dk = d // RT_HEADS
    ln_cos, ln_sin = _linear_tables(seq, rt_dk)
    ones2 = jnp.ones((2, LANES), F32)

    h_lat = x.reshape(batch * seq, d)
    h_ctx = ctx.reshape(batch * n_ctx, d)

    lat = _Stream(seq, lambda i, bm: 1 + i // (seq // bm), lambda i, bm: i % (seq // bm), True)
    cst = _Stream(batch * n_ctx, lambda i, bm: 0, lambda i, bm: 0, False)

    ga_w, ga_o = ga_wqkv.astype(BF16), ga_wo.astype(BF16)
    rt_w, rt_o = rt_w_in.astype(BF16), rt_wo.astype(BF16)
    df_w, df_o = df_wqkv.astype(BF16), df_wo.astype(BF16)
    f_in, f_out = ffn_w_in.astype(BF16), ffn_w_out.astype(BF16)
    f_cb = ffn_conv_b.reshape(depth, 1, -1)

    for i in range(depth):
        need_ctx = i < depth - 1
        kind, j = i % N_MIXERS, i // N_MIXERS
        g1 = norm_g[i, 0].reshape(1, d)
        g2 = norm_g[i, 1].reshape(1, d)

        if kind == 0:
            w_all = ga_w
            qcols = ga_sink.shape[1] * GA_HD
            kcols = (w_all.shape[2] - qcols) // 2
            segments = ((qcols, "norm_rope_128", 0, GA_HD ** -0.5 * LOG2E), (kcols, "norm_rope_128", 1, 1.0),
                        (kcols, "plain", 0, 1.0))
            cos_t, sin_t, gains = ax_cos, ax_sin, ga_qk_norm[j]
        elif kind == 1:
            w_all = rt_w
            vcols = (w_all.shape[2] - 2 * d) // 3
            segments = ((d, "rope_256", 0, rt_dk ** -0.5), (d, "rope_256", 0, 1.0),
                        (vcols, "plain", 0, 1.0), (2 * vcols, "silu", 0, 1.0))
            cos_t, sin_t, gains = ln_cos, ln_sin, ones2
        else:
            w_all = df_w
            qcols = w_all.shape[2] // 3
            segments = ((qcols, "norm_rope_128", 0, DF_HD ** -0.5 * LOG2E), (qcols, "norm_rope_128", 1, 1.0),
                        (qcols, "plain", 0, 1.0))
            cos_t, sin_t, gains = ax_cos, ax_sin, df_qk_norm[j]

        proj = {"lat": _proj(h_lat, g1, mods, i, lat, w_all, j, segments, cos_t, sin_t, gains),
                "ctx": _proj(h_ctx, g1, mods, i, cst, w_all, j, segments, cos_t, sin_t, gains)}

        if kind == 0:
            y_lat, y_ctx = _ga_attention(proj["lat"], proj["ctx"], ga_sink[j], batch, seq, n_ctx, need_ctx)
            h_lat = _oproj(y_lat, ga_o, j, h_lat, mods, i, lat)
            if need_ctx:
                h_ctx = _oproj(y_ctx, ga_o, j, h_ctx, mods, i, cst)
        elif kind == 1:
            log_gamma = jax.nn.log_sigmoid(rt_decay[j].astype(F32))
            y_lat, y_ctx = _retention(proj["lat"], proj["ctx"], log_gamma, rt_gn[j], batch, seq, n_ctx)
            h_lat = _oproj_rt(y_lat[0], y_lat[1], proj["lat"], rt_o, j, h_lat, mods, i, lat)
            if need_ctx:
                h_ctx = _oproj_rt(y_ctx[0], y_ctx[1], proj["ctx"], rt_o, j, h_ctx, mods, i, cst)
        else:
            lambda_init = 0.8 - 0.6 * math.exp(-0.3 * i)
            y_lat, y_ctx = _df_attention(proj["lat"], proj["ctx"], df_lambda[j], df_subln[j], batch, seq, n_ctx,
                                         lambda_init, need_ctx)
            h_lat = _oproj(y_lat, df_o, j, h_lat, mods, i, lat)
            if need_ctx:
                h_ctx = _oproj(y_ctx, df_o, j, h_ctx, mods, i, cst)

        h_lat = _ffn(h_lat, g2, mods, i, lat, f_in, ffn_conv_w, f_cb, f_out, seq)
        if need_ctx:
            h_ctx = _ffn(h_ctx, g2, mods, i, cst, f_in, ffn_conv_w, f_cb, f_out, n_ctx)

    return h_lat.reshape(batch, seq, d)
```

```python
import functools
import math
from typing import Callable, NamedTuple

import jax
import jax.numpy as jnp
from jax import lax
from jax.experimental import pallas as pl
from jax.experimental.pallas import tpu as pltpu

F32 = jnp.float32
BF16 = jnp.bfloat16

EPS = 1e-6
NEG_INF = -1e30
LOG2E = math.log2(math.e)
ROPE_THETA = 10000.0
GRID_W = 64
WINDOW = 128
N_MIXERS = 3

GA_HD = 128
GA_GROUP = 4
RT_HEADS = 8
DF_HD = 128
CONV_W = 3

LANES = 128
MXU_COLS = 256
BF16_ROWS = 16
MOD_ROWS = 8
VMEM_LIMIT = 52 * 1024 * 1024


class _Stream(NamedTuple):
    group: int
    mod_row: Callable
    tab_block: Callable
    rope: bool


def _pick(n, candidates):
    for c in candidates:
        if n % c == 0:
            return c
    raise ValueError(f"no tile size in {candidates} divides {n}")


def _params(*sem):
    return pltpu.CompilerParams(dimension_semantics=sem, vmem_limit_bytes=VMEM_LIMIT)


def _silu(x):
    return x * (1.0 / (1.0 + jnp.exp(-x)))


def _dot(a, b):
    return jnp.dot(a, b, preferred_element_type=F32)


def _dot_nt(a, b):
    return lax.dot_general(a, b, (((1,), (1,)), ((), ())), preferred_element_type=F32)


def _dot_tn(a, b):
    return lax.dot_general(a, b, (((0,), (0,)), ((), ())), preferred_element_type=F32)


def _adaln_kernel(c_ref, w_ref, b_ref, o_ref):
    s = _silu(c_ref[...]).astype(BF16)
    o_ref[...] = _dot(s, w_ref[...].astype(BF16)) + b_ref[...]


def _adaln_table(cond, mod_w, mod_b):
    depth, d, _ = mod_w.shape
    bn = _pick(d, (1024, 512, 256, 128))
    nj = d // bn
    return pl.pallas_call(
        _adaln_kernel,
        grid=(depth, 6, nj),
        in_specs=[
            pl.BlockSpec((MOD_ROWS, d), lambda l, k, j: (0, 0)),
            pl.BlockSpec((None, d, bn), lambda l, k, j: (l, 0, k * nj + j)),
            pl.BlockSpec((None, 1, bn), lambda l, k, j: (l, 0, k * nj + j)),
        ],
        out_specs=pl.BlockSpec((None, None, MOD_ROWS, bn), lambda l, k, j: (l, k, 0, j)),
        out_shape=jax.ShapeDtypeStruct((depth, 6, MOD_ROWS, d), F32),
        compiler_params=_params("parallel", "parallel", "parallel"),
        name="adaln_table",
    )(cond, mod_w, mod_b.reshape(depth, 1, 6 * d))


def _mod_spec(layer, slot, row_fn, d):
    return pl.BlockSpec((None, None, None, 1, d), lambda i, j: (layer, slot, row_fn(i), 0, 0))


def _mod_spec_cols(layer, slot, row_fn, bn):
    return pl.BlockSpec((None, None, None, 1, bn), lambda i, j: (layer, slot, row_fn(i), 0, j))


def _norm_mod(x, g, shift, scale):
    ms = jnp.mean(x * x, axis=-1, keepdims=True)
    y = x * lax.rsqrt(ms + EPS) * g
    return y * (1.0 + scale) + shift


def _proj_kernel(h_ref, g_ref, sh_ref, sc_ref, w_ref, cos_ref, sin_ref, gain_ref, seg_ref, o_ref, a_scr, acc_scr,
                 *, plan, rope):
    j = pl.program_id(1)
    sub = acc_scr.shape[2]
    n_sub = o_ref.shape[1] // sub

    @pl.when(j == 0)
    def _():
        a_scr[...] = _norm_mod(h_ref[...], g_ref[...], sh_ref[...], sc_ref[...]).astype(BF16)

    def epilogue(s, kind, arg, scale):
        base = s * sub
        if kind == "plain":
            o_ref[:, base:base + sub] = acc_scr[s].astype(o_ref.dtype)
        elif kind == "silu":
            o_ref[:, base:base + sub] = _silu(acc_scr[s]).astype(o_ref.dtype)
        elif kind == "norm_rope_128":
            x = acc_scr[s]
            ms = _dot((x * x).astype(BF16), seg_ref[...])
            y = x * lax.rsqrt(ms + EPS) * (gain_ref[arg:arg + 1, :] * scale)
            for t in range(0, sub, LANES):
                yt = y[:, t:t + LANES]
                if rope:
                    yt = yt * cos_ref[...] + pltpu.roll(yt, LANES // 2, 1) * sin_ref[...]
                o_ref[:, base + t:base + t + LANES] = yt.astype(o_ref.dtype)
        else:
            assert kind == "rope_256" and sub == 2 * LANES
            x1 = acc_scr[s, :, 0:LANES] * scale
            x2 = acc_scr[s, :, LANES:sub] * scale
            if rope:
                c, sn = cos_ref[...], sin_ref[...]
                x1, x2 = x1 * c - x2 * sn, x1 * sn + x2 * c
            o_ref[:, base:base + LANES] = x1.astype(o_ref.dtype)
            o_ref[:, base + LANES:base + sub] = x2.astype(o_ref.dtype)

    for lo, hi, kinds in plan:
        @pl.when((j >= lo) & (j < hi))
        def _(kinds=kinds):
            a = a_scr[...]
            for s in range(n_sub):
                acc_scr[s] = _dot(a, w_ref[:, s * sub:(s + 1) * sub])
            for s, (kind, arg, scale) in enumerate(kinds):
                epilogue(s, kind, arg, scale)


def _proj(h, norm_g, mods, layer, st, w_all, w_idx, segments, cos_t, sin_t, gains):
    m, d = h.shape
    n = w_all.shape[2]
    bm = _pick(st.group, (1024, 512, 256))
    bn = _pick(n, (1024, 512, 256))
    sub = MXU_COLS
    n_sub = bn // sub
    per_sub = []
    for width, kind, arg, scale in segments:
        assert width % sub == 0
        per_sub += [(kind, arg, scale)] * (width // sub)
    assert len(per_sub) * sub == n
    tiles = [tuple(per_sub[t * n_sub:(t + 1) * n_sub]) for t in range(n // bn)]
    plan, lo = [], 0
    for t in range(1, len(tiles) + 1):
        if t == len(tiles) or tiles[t] != tiles[lo]:
            plan.append((lo, t, tiles[lo]))
            lo = t
    row_fn, tab_fn = st.mod_row, st.tab_block
    kernel = functools.partial(_proj_kernel, plan=tuple(plan), rope=st.rope)
    head = jnp.arange(sub) // LANES
    seg_mean = jnp.where(head[:, None] == head[None, :], 1.0 / LANES, 0.0).astype(BF16)
    gains = jnp.tile(gains, (1, sub // LANES))
    return pl.pallas_call(
        kernel,
        grid=(m // bm, n // bn),
        in_specs=[
            pl.BlockSpec((bm, d), lambda i, j: (i, 0)),
            pl.BlockSpec((1, d), lambda i, j: (0, 0)),
            _mod_spec(layer, 0, lambda i: row_fn(i, bm), d),
            _mod_spec(layer, 1, lambda i: row_fn(i, bm), d),
            pl.BlockSpec((None, d, bn), lambda i, j: (w_idx, 0, j)),
            pl.BlockSpec((bm, LANES), lambda i, j: (tab_fn(i, bm), 0)),
            pl.BlockSpec((bm, LANES), lambda i, j: (tab_fn(i, bm), 0)),
            pl.BlockSpec(gains.shape, lambda i, j: (0, 0)),
            pl.BlockSpec((sub, sub), lambda i, j: (0, 0)),
        ],
        out_specs=pl.BlockSpec((bm, bn), lambda i, j: (i, j)),
        out_shape=jax.ShapeDtypeStruct((m, n), BF16),
        scratch_shapes=[pltpu.VMEM((bm, d), BF16), pltpu.VMEM((n_sub, bm, sub), F32)],
        compiler_params=_params("parallel", "arbitrary"),
        name="proj",
    )(h, norm_g, mods, mods, w_all, cos_t, sin_t, gains, seg_mean)


def _oproj_kernel(y_ref, w_ref, h_ref, gate_ref, o_ref):
    o_ref[...] = h_ref[...] + gate_ref[...] * _dot(y_ref[...], w_ref[...])


def _oproj_rt_kernel(yf_ref, yb_ref, sf_ref, sb_ref, w_ref, h_ref, gate_ref, o_ref):
    c = pl.program_id(1)

    @pl.when(c == 0)
    def _():
        o_ref[...] = jnp.zeros_like(o_ref)

    ck = w_ref.shape[0]
    sub = min(ck, MXU_COLS)
    part = None
    for s in range(0, ck, sub):
        cols = slice(s, s + sub)
        z = sf_ref[:, cols] * yf_ref[:, cols] + sb_ref[:, cols] * yb_ref[:, cols]
        d = _dot(z, w_ref[cols, :])
        part = d if part is None else part + d
    o_ref[...] += part

    @pl.when(c == pl.num_programs(1) - 1)
    def _():
        o_ref[...] = h_ref[...] + gate_ref[...] * o_ref[...]


def _oproj(y, w_all, w_idx, h, mods, layer, st):
    m, k = y.shape
    d = w_all.shape[2]
    bm = _pick(st.group, (1024, 512, 256))
    row_fn = st.mod_row
    bn = _pick(d, (512, 256, 128))
    return pl.pallas_call(
        _oproj_kernel,
        grid=(m // bm, d // bn),
        in_specs=[
            pl.BlockSpec((bm, k), lambda i, j: (i, 0)),
            pl.BlockSpec((None, k, bn), lambda i, j: (w_idx, 0, j)),
            pl.BlockSpec((bm, bn), lambda i, j: (i, j)),
            _mod_spec_cols(layer, 2, lambda i: row_fn(i, bm), bn),
        ],
        out_specs=pl.BlockSpec((bm, bn), lambda i, j: (i, j)),
        out_shape=jax.ShapeDtypeStruct((m, d), F32),
        input_output_aliases={2: 0},
        compiler_params=_params("parallel", "arbitrary"),
        name="oproj",
    )(y, w_all, h, mods)


def _oproj_rt(yf, yb, proj, w_all, w_idx, h, mods, layer, st):
    m, v = yf.shape
    d = w_all.shape[2]
    bm = _pick(st.group, (512, 256))
    row_fn = st.mod_row
    ck = _pick(v, (1024, 512, 256, 128))
    nc = v // ck
    gcol = (proj.shape[1] // v - 2) * nc
    return pl.pallas_call(
        _oproj_rt_kernel,
        grid=(m // bm, nc),
        in_specs=[
            pl.BlockSpec((bm, ck), lambda i, c: (i, c)),
            pl.BlockSpec((bm, ck), lambda i, c: (i, c)),
            pl.BlockSpec((bm, ck), lambda i, c: (i, gcol + c)),
            pl.BlockSpec((bm, ck), lambda i, c: (i, gcol + nc + c)),
            pl.BlockSpec((None, ck, d), lambda i, c: (w_idx, c, 0)),
            pl.BlockSpec((bm, d), lambda i, c: (i, 0)),
            _mod_spec(layer, 2, lambda i: row_fn(i, bm), d),
        ],
        out_specs=pl.BlockSpec((bm, d), lambda i, c: (i, 0)),
        out_shape=jax.ShapeDtypeStruct((m, d), F32),
        input_output_aliases={5: 0},
        compiler_params=_params("parallel", "arbitrary"),
        name="oproj_rt",
    )(yf, yb, proj, proj, w_all, h, mods)


def _ffn_kernel(h_ref, hp_ref, hn_ref, g_ref, sh_ref, sc_ref, gate_ref, wa_ref, wb_ref,
                cwa_lo, cwb_lo, cba_lo, cbb_lo, wo_lo, cwa_hi, cwb_hi, cba_hi, cbb_hi, wo_hi,
                cwa_top, cwb_top, cba_top, cbb_top, wo_top,
                o_ref, a_scr, u_even, u_odd_a, u_odd_b, *, bm, seq_len, steps):
    i = pl.program_id(0)
    j = pl.program_id(1)
    halo = BF16_ROWS
    sub = u_even.shape[2]
    lo = (cwa_lo, cwb_lo, cba_lo, cbb_lo, wo_lo)
    hi = (cwa_hi, cwb_hi, cba_hi, cbb_hi, wo_hi)

    def up(half, u):
        cols = slice(half * sub, (half + 1) * sub)
        a = a_scr[...]
        u[0] = _dot(a, wa_ref[:, cols])
        u[1] = _dot(a, wb_ref[:, cols])

    def gate(u, cwa, cwb, cba, cbb, wo):
        def conv(idx, cw_ref, cb_ref):
            prev = u[idx, halo - 1:halo - 1 + bm, :]
            cur = u[idx, halo:halo + bm, :]
            nxt = u[idx, halo + 1:halo + 1 + bm, :]
            return cb_ref[...] + prev * cw_ref[0:1, :] + cur * cw_ref[1:2, :] + nxt * cw_ref[2:3, :]

        return (_silu(conv(0, cwa, cba)) * conv(1, cwb, cbb)).astype(BF16)

    def down(u, *params):
        o_ref[...] += _dot(gate(u, *params), params[-1][...])

    @pl.when(j == 0)
    def _():
        def nm(x):
            return _norm_mod(x, g_ref[...], sh_ref[...], sc_ref[...])

        keep_prev = jnp.where((i * bm) % seq_len != 0, 1.0, 0.0)
        keep_next = jnp.where(((i + 1) * bm) % seq_len != 0, 1.0, 0.0)
        a_scr[0:halo, :] = (nm(hp_ref[...]) * keep_prev).astype(BF16)
        a_scr[halo:halo + bm, :] = nm(h_ref[...]).astype(BF16)
        a_scr[halo + bm:, :] = (nm(hn_ref[...]) * keep_next).astype(BF16)
        o_ref[...] = jnp.zeros_like(o_ref)
        up(0, u_even)
        up(1, u_odd_a)
        down(u_even, *hi)

    def middle(u_new, u_old):
        up(0, u_even)
        up(1, u_new)
        g = jnp.concatenate([gate(u_old, *lo), gate(u_even, *hi)], axis=1)
        w = jnp.concatenate([wo_lo[...], wo_hi[...]], axis=0)
        o_ref[...] += _dot(g, w)

    inner = (j > 0) & (j < steps - 1)
    pl.when(inner & (j % 2 == 1))(lambda: middle(u_odd_b, u_odd_a))
    pl.when(inner & (j % 2 == 0))(lambda: middle(u_odd_a, u_odd_b))

    @pl.when(j == steps - 1)
    def _():
        u_new, u_old = (u_odd_a, u_odd_b) if (steps - 1) % 2 == 0 else (u_odd_b, u_odd_a)
        middle(u_new, u_old)
        down(u_new, cwa_top, cwb_top, cba_top, cbb_top, wo_top)
        o_ref[...] = h_ref[...] + gate_ref[...] * o_ref[...]


def _ffn(h, norm_g, mods, layer, st, w_in, conv_w, conv_b, w_out, seq_len):
    m, d = h.shape
    dff = w_out.shape[1]
    bm = _pick(seq_len, (512, 256))
    sub = MXU_COLS
    nsc = dff // sub
    assert dff % (2 * sub) == 0
    steps = nsc // 2
    assert steps >= 2
    halo = BF16_ROWS
    hb = bm // halo
    last = m // halo - 1
    kernel = functools.partial(_ffn_kernel, bm=bm, seq_len=seq_len, steps=steps)
    mrow = lambda i: st.mod_row(i, bm)
    chunk = (lambda j: jnp.maximum(2 * j - 1, 0), lambda j: 2 * j, lambda j: nsc - 1)

    def down_specs(k):
        return [
            pl.BlockSpec((None, CONV_W, sub), lambda i, j: (layer, 0, k(j))),
            pl.BlockSpec((None, CONV_W, sub), lambda i, j: (layer, 0, nsc + k(j))),
            pl.BlockSpec((None, 1, sub), lambda i, j: (layer, 0, k(j))),
            pl.BlockSpec((None, 1, sub), lambda i, j: (layer, 0, nsc + k(j))),
            pl.BlockSpec((None, sub, d), lambda i, j: (layer, k(j), 0)),
        ]

    down_args = [conv_w, conv_w, conv_b, conv_b, w_out]
    return pl.pallas_call(
        kernel,
        grid=(m // bm, steps),
        in_specs=[
            pl.BlockSpec((bm, d), lambda i, j: (i, 0)),
            pl.BlockSpec((halo, d), lambda i, j: (jnp.maximum(i * hb - 1, 0), 0)),
            pl.BlockSpec((halo, d), lambda i, j: (jnp.minimum((i + 1) * hb, last), 0)),
            pl.BlockSpec((1, d), lambda i, j: (0, 0)),
            _mod_spec(layer, 3, mrow, d),
            _mod_spec(layer, 4, mrow, d),
            _mod_spec(layer, 5, mrow, d),
            pl.BlockSpec((None, d, 2 * sub), lambda i, j: (layer, 0, j)),
            pl.BlockSpec((None, d, 2 * sub), lambda i, j: (layer, 0, steps + j)),
        ] + down_specs(chunk[0]) + down_specs(chunk[1]) + down_specs(chunk[2]),
        out_specs=pl.BlockSpec((bm, d), lambda i, j: (i, 0)),
        out_shape=jax.ShapeDtypeStruct((m, d), F32),
        scratch_shapes=[pltpu.VMEM((bm + 2 * halo, d), BF16),
                        pltpu.VMEM((2, bm + 2 * halo, sub), F32),
                        pltpu.VMEM((2, bm + 2 * halo, sub), F32),
                        pltpu.VMEM((2, bm + 2 * halo, sub), F32)],
        compiler_params=_params("parallel", "arbitrary"),
        name="conv_ffn",
    )(h, h, h, norm_g, mods, mods, mods, w_in, w_in, *down_args, *down_args, *down_args)


def _ga_kernel(sink_ref, q_ref, kvc_ref, *rest, seq, n_kv, local):
    if local:
        kvo_ref, kvp_ref, kvn_ref, o_ref, s_a, s_b = rest
    else:
        o_ref, s_a, s_b = rest
    hd = GA_HD
    kcols = n_kv * hd
    bq = q_ref.shape[0]
    sq = s_a.shape[1]
    n_ctx = kvc_ref.shape[0]
    bufs = (s_a, s_b)

    def band(u):
        if not local:
            return [], 0
        lo, hi = u * sq - WINDOW, (u + 1) * sq + WINDOW
        pieces = []
        if lo < 0:
            pieces.append((kvp_ref, WINDOW + lo, -lo))
        pieces.append((kvo_ref, max(lo, 0), min(hi, bq) - max(lo, 0)))
        if hi > bq:
            pieces.append((kvn_ref, 0, hi - bq))
        return pieces, lo

    def bias(u):
        pieces, lo = band(u)
        n_loc = sum(p[2] for p in pieces)
        kl = lax.broadcasted_iota(jnp.int32, (n_loc, sq), 0) + lo
        ql = lax.broadcasted_iota(jnp.int32, (n_loc, sq), 1) + u * sq
        pos = pl.program_id(1) * bq + kl
        ok = (jnp.abs(ql - kl) <= WINDOW) & (pos >= 0) & (pos < seq)
        return jnp.concatenate([jnp.where(ok, 0.0, NEG_INF).astype(F32), jnp.zeros((n_ctx, sq), F32)], axis=0)

    def keys_values(u, n):
        pieces, _ = band(u)
        ks, vs = slice(n * hd, (n + 1) * hd), slice(kcols + n * hd, kcols + (n + 1) * hd)
        k_all = jnp.concatenate([r[a:a + c, ks] for r, a, c in pieces] + [kvc_ref[:, ks]], axis=0)
        v_all = jnp.concatenate([r[a:a + c, vs] for r, a, c in pieces] + [kvc_ref[:, vs]], axis=0)
        return k_all, v_all.astype(F32).T.astype(BF16)

    chains = [(u, n, g) for u in range(bq // sq) for n in range(n_kv) for g in range(GA_GROUP)]
    cache = {}

    def operands(u, n):
        if (u, n) not in cache:
            cache.clear()
            cache[(u, n)] = keys_values(u, n)
        return cache[(u, n)]

    biases = {}

    def scores(idx):
        u, n, g = chains[idx]
        head = n * GA_GROUP + g
        k_all, _ = operands(u, n)
        s = _dot_nt(k_all, q_ref[u * sq:(u + 1) * sq, head * hd:(head + 1) * hd])
        if local:
            if u not in biases:
                biases[u] = bias(u)
            s = s + biases[u]
        bufs[idx % 2][...] = s

    scores(0)
    for idx, (u, n, g) in enumerate(chains):
        _, v_t = operands(u, n)
        if idx + 1 < len(chains):
            scores(idx + 1)
        buf = bufs[idx % 2]
        head = n * GA_GROUP + g
        sink = sink_ref[head] * LOG2E
        m = jnp.maximum(jnp.max(buf[...], axis=0, keepdims=True), sink)
        p = jnp.exp2(buf[...] - m)
        den = jnp.sum(p, axis=0, keepdims=True) + jnp.exp2(sink - m)
        o_t = _dot(v_t, p.astype(BF16)) / den
        o_ref[u * sq:(u + 1) * sq, head * hd:(head + 1) * hd] = o_t.T.astype(o_ref.dtype)


def _ga_attention(qkv_lat, qkv_ctx, sink, batch, seq, n_ctx, need_ctx):
    n_heads = sink.shape[0]
    qcols = n_heads * GA_HD
    n_kv = n_heads // GA_GROUP
    kvw = 2 * n_kv * GA_HD
    kvblk = qcols // kvw
    assert qcols % kvw == 0
    bq = _pick(seq, (512, 256, 128))
    sq = min(bq, MXU_COLS)
    nq = seq // bq
    wb = bq // WINDOW
    last_w = batch * seq // WINDOW - 1
    smem = pl.BlockSpec(memory_space=pltpu.SMEM)
    lat = pl.pallas_call(
        functools.partial(_ga_kernel, seq=seq, n_kv=n_kv, local=True),
        grid=(batch, nq),
        in_specs=[
            smem,
            pl.BlockSpec((bq, qcols), lambda b, i: (b * nq + i, 0)),
            pl.BlockSpec((n_ctx, kvw), lambda b, i: (b, kvblk)),
            pl.BlockSpec((bq, kvw), lambda b, i: (b * nq + i, kvblk)),
            pl.BlockSpec((WINDOW, kvw), lambda b, i: (jnp.maximum((b * nq + i) * wb - 1, 0), kvblk)),
            pl.BlockSpec((WINDOW, kvw), lambda b, i: (jnp.minimum((b * nq + i + 1) * wb, last_w), kvblk)),
        ],
        out_specs=pl.BlockSpec((bq, qcols), lambda b, i: (b * nq + i, 0)),
        out_shape=jax.ShapeDtypeStruct((batch * seq, qcols), BF16),
        scratch_shapes=[pltpu.VMEM((sq + 2 * WINDOW + n_ctx, sq), F32)] * 2,
        compiler_params=_params("parallel", "parallel"),
        name="ga_attention_latent",
    )(sink, qkv_lat, qkv_ctx, qkv_lat, qkv_lat, qkv_lat)
    ctx = None
    if need_ctx:
        cq = min(n_ctx, MXU_COLS)
        ctx = pl.pallas_call(
            functools.partial(_ga_kernel, seq=n_ctx, n_kv=n_kv, local=False),
            grid=(batch,),
            in_specs=[
                smem,
                pl.BlockSpec((n_ctx, qcols), lambda b: (b, 0)),
                pl.BlockSpec((n_ctx, kvw), lambda b: (b, kvblk)),
            ],
            out_specs=pl.BlockSpec((n_ctx, qcols), lambda b: (b, 0)),
            out_shape=jax.ShapeDtypeStruct((batch * n_ctx, qcols), BF16),
            scratch_shapes=[pltpu.VMEM((n_ctx, cq), F32)] * 2,
            compiler_params=_params("parallel"),
            name="ga_attention_context",
        )(sink, qkv_ctx, qkv_ctx)
    return lat, ctx


def _df_kernel(lam_ref, g_ref, q_ref, *rest, lambda_init, chunks, first_q_axis):
    n_seg = len(chunks)
    kv_refs = rest[:2 * n_seg]
    o_ref = rest[2 * n_seg]
    vt_scrs = rest[2 * n_seg + 1:2 * n_seg + 1 + n_seg]
    acc_scr, s_a, s_b = rest[-3:]
    hd = DF_HD
    bq = q_ref.shape[0]
    lam = lam_ref[...]
    lmbda = (jnp.exp(jnp.sum(lam[0:1] * lam[1:2], axis=-1, keepdims=True))
             - jnp.exp(jnp.sum(lam[2:3] * lam[3:4], axis=-1, keepdims=True)) + lambda_init)

    def transpose_values():
        for seg, ck in enumerate(chunks):
            v_ref, vt = kv_refs[2 * seg + 1], vt_scrs[seg]
            for c in range(v_ref.shape[0] // ck):
                vt[c] = v_ref[c * ck:(c + 1) * ck, :].astype(F32).T.astype(BF16)

    if first_q_axis is None:
        transpose_values()
    else:
        pl.when(pl.program_id(first_q_axis) == 0)(transpose_values)

    qs = (q_ref[:, 0:hd], q_ref[:, hd:2 * hd])
    acc_scr[...] = jnp.zeros_like(acc_scr)
    stat0 = (jnp.full((1, bq), NEG_INF, F32), jnp.zeros((1, bq), F32))
    carry = (stat0, stat0)

    def scores(k_ref, c, ck, dst):
        start = c * ck if isinstance(c, int) else pl.multiple_of(c * ck, ck)
        kblk = k_ref[pl.ds(start, ck), :]
        for r in range(2):
            dst[r] = _dot_nt(kblk[:, r * hd:(r + 1) * hd], qs[r])

    def update(src, v_t, carry):
        new = []
        for r in range(2):
            m_old, l_old = carry[r]
            m_new = jnp.maximum(m_old, jnp.max(src[r], axis=0, keepdims=True))
            alpha = jnp.exp2(m_old - m_new)
            p = jnp.exp2(src[r] - m_new)
            l_new = alpha * l_old + jnp.sum(p, axis=0, keepdims=True)
            acc_scr[r] = alpha * acc_scr[r] + _dot(v_t, p.astype(BF16))
            new.append((m_new, l_new))
        return tuple(new)

    for seg in range(n_seg - 1):
        k_ref, vt, ck = kv_refs[2 * seg], vt_scrs[seg], chunks[seg]
        for c in range(k_ref.shape[0] // ck):
            scores(k_ref, c, ck, s_a.at[:, 0:ck, :])
            carry = update(s_a.at[:, 0:ck, :], vt[c], carry)

    k_ref, vt, ck = kv_refs[2 * n_seg - 2], vt_scrs[n_seg - 1], chunks[n_seg - 1]
    n = k_ref.shape[0] // ck
    pairs = (n - 1) // 2
    scores(k_ref, 0, ck, s_a)

    def body(j, carry):
        scores(k_ref, 2 * j + 1, ck, s_b)
        carry = update(s_a, vt[2 * j], carry)
        scores(k_ref, 2 * j + 2, ck, s_a)
        return update(s_b, vt[2 * j + 1], carry)

    carry = lax.fori_loop(0, pairs, body, carry)
    if n - 2 * pairs == 2:
        scores(k_ref, n - 1, ck, s_b)
        carry = update(s_a, vt[n - 2], carry)
        carry = update(s_b, vt[n - 1], carry)
    else:
        carry = update(s_a, vt[n - 1], carry)
    (_, l0), (_, l1) = carry
    o = (acc_scr[0] / l0 - lmbda * (acc_scr[1] / l1)).T
    ms = jnp.mean(o * o, axis=-1, keepdims=True)
    y = o * lax.rsqrt(ms + EPS) * g_ref[...] * (1.0 - lambda_init)
    o_ref[...] = y.astype(o_ref.dtype)


def _df_attention(qkv_lat, qkv_ctx, lam, subln_g, batch, seq, n_ctx, lambda_init, need_ctx):
    hw = 2 * DF_HD
    n_heads = qkv_lat.shape[1] // (3 * hw)
    bq = _pick(seq, (512, 256, 128))
    nq = seq // bq
    ck_lat = _pick(seq, (512, 256, 128))
    ck_ctx = _pick(n_ctx, (512, 256, 128))
    g2 = subln_g.reshape(1, hw)
    const = lambda shape: pl.BlockSpec(shape, lambda *_: (0,) * len(shape))
    vt_ctx = pltpu.VMEM((n_ctx // ck_ctx, hw, ck_ctx), BF16)
    vt_lat = pltpu.VMEM((seq // ck_lat, hw, ck_lat), BF16)
    lat = pl.pallas_call(
        functools.partial(_df_kernel, lambda_init=lambda_init, chunks=(ck_ctx, ck_lat), first_q_axis=2),
        grid=(batch, n_heads, nq),
        in_specs=[
            const(lam.shape),
            const((1, hw)),
            pl.BlockSpec((bq, hw), lambda b, h, i: (b * nq + i, h)),
            pl.BlockSpec((n_ctx, hw), lambda b, h, i: (b, n_heads + h)),
            pl.BlockSpec((n_ctx, hw), lambda b, h, i: (b, 2 * n_heads + h)),
            pl.BlockSpec((seq, hw), lambda b, h, i: (b, n_heads + h)),
            pl.BlockSpec((seq, hw), lambda b, h, i: (b, 2 * n_heads + h)),
        ],
        out_specs=pl.BlockSpec((bq, hw), lambda b, h, i: (b * nq + i, h)),
        out_shape=jax.ShapeDtypeStruct((batch * seq, n_heads * hw), BF16),
        scratch_shapes=[vt_ctx, vt_lat, pltpu.VMEM((2, hw, bq), F32)]
        + [pltpu.VMEM((2, max(ck_lat, ck_ctx), bq), F32)] * 2,
        compiler_params=_params("parallel", "parallel", "arbitrary"),
        name="df_attention_latent",
    )(lam, g2, qkv_lat, qkv_ctx, qkv_ctx, qkv_lat, qkv_lat)
    ctx = None
    if need_ctx:
        ctx = pl.pallas_call(
            functools.partial(_df_kernel, lambda_init=lambda_init, chunks=(ck_ctx,), first_q_axis=None),
            grid=(batch, n_heads),
            in_specs=[
                const(lam.shape),
                const((1, hw)),
                pl.BlockSpec((n_ctx, hw), lambda b, h: (b, h)),
                pl.BlockSpec((n_ctx, hw), lambda b, h: (b, n_heads + h)),
                pl.BlockSpec((n_ctx, hw), lambda b, h: (b, 2 * n_heads + h)),
            ],
            out_specs=pl.BlockSpec((n_ctx, hw), lambda b, h: (b, h)),
            out_shape=jax.ShapeDtypeStruct((batch * n_ctx, n_heads * hw), BF16),
            scratch_shapes=[vt_ctx, pltpu.VMEM((2, hw, n_ctx), F32)]
            + [pltpu.VMEM((2, ck_ctx, n_ctx), F32)] * 2,
            compiler_params=_params("parallel", "parallel"),
            name="df_attention_context",
        )(lam, g2, qkv_ctx, qkv_ctx, qkv_ctx)
    return lat, ctx


def _rt_kernel(lg_ref, gn_ref, *refs, n_heads, chunk, ctx_steps):
    ctx_in, lat_in = refs[0:6], refs[6:12]
    ctx_out, lat_out = refs[12:14], refs[14:16]
    st_scr = refs[16]
    head0 = pl.program_id(1) * n_heads
    s_idx = pl.program_id(2)
    dk = ctx_in[0].shape[1] // n_heads
    dv = ctx_in[2].shape[1] // n_heads

    @pl.when(s_idx == 0)
    def _():
        st_scr[...] = jnp.zeros_like(st_scr)

    def step(ins, outs):
        qf_ref, kf_ref, vf_ref, qb_ref, kb_ref, vb_ref = ins
        of_ref, ob_ref = outs
        row = lax.broadcasted_iota(jnp.int32, (chunk, chunk), 0).astype(F32)
        col = lax.broadcasted_iota(jnp.int32, (chunk, chunk), 1).astype(F32)
        pos = lax.broadcasted_iota(jnp.int32, (chunk, 1), 0).astype(F32)
        dirs = (
            (qf_ref, kf_ref, vf_ref, of_ref, row - col, pos + 1.0, chunk - 1.0 - pos),
            (qb_ref, kb_ref, vb_ref, ob_ref, col - row, chunk - pos, pos),
        )
        for d, (q_ref, k_ref, v_ref, o_ref, rel, q_pow, k_pow) in enumerate(dirs):
            for h in range(n_heads):
                lg = lg_ref[d, head0 + h]
                decay = jnp.where(rel >= 0, jnp.exp(lg * jnp.maximum(rel, 0.0)), 0.0)
                q_decay = jnp.exp(lg * q_pow)
                k_decay = jnp.exp(lg * k_pow)
                chunk_decay = jnp.exp(lg * chunk)
                q = q_ref[:, h * dk:(h + 1) * dk]
                k = k_ref[:, h * dk:(h + 1) * dk]
                v = v_ref[:, h * dv:(h + 1) * dv]
                state = st_scr[d, h]
                inner = _dot_nt(q, k) * decay
                o = _dot(inner.astype(BF16), v) + _dot(q, state.astype(BF16)) * q_decay
                kd = (k.astype(F32) * k_decay).astype(BF16)
                st_scr[d, h] = state * chunk_decay + _dot_tn(kd, v)
                mu = jnp.mean(o, axis=-1, keepdims=True)
                dev = o - mu
                var = jnp.mean(dev * dev, axis=-1, keepdims=True)
                y = dev * lax.rsqrt(var + EPS) * gn_ref[d:d + 1, h * dv:(h + 1) * dv]
                o_ref[:, h * dv:(h + 1) * dv] = y.astype(o_ref.dtype)

    @pl.when(s_idx < ctx_steps)
    def _():
        step(ctx_in, ctx_out)

    @pl.when(s_idx >= ctx_steps)
    def _():
        step(lat_in, lat_out)


def _retention(proj_lat, proj_ctx, log_gamma, gn_g, batch, seq, n_ctx):
    n_heads = log_gamma.shape[1]
    qk = proj_lat.shape[1] // 8
    v = 2 * qk
    chunk = _pick(math.gcd(seq, n_ctx), (256, 128))
    ncc, nlc = n_ctx // chunk, seq // chunk
    dk, dv = qk // n_heads, v // n_heads
    groups = 2
    hpg = n_heads // groups
    qkw, vw = hpg * dk, hpg * dv

    def ctx_f(b, s):
        return b * ncc + jnp.minimum(s, ncc - 1)

    def ctx_b(b, s):
        return b * ncc + (ncc - 1 - jnp.minimum(s, ncc - 1))

    def lat_f(b, s):
        return b * nlc + jnp.maximum(s - ncc, 0)

    def lat_b(b, s):
        return b * nlc + (nlc - 1 - jnp.maximum(s - ncc, 0))

    def in_specs(row):
        return [
            pl.BlockSpec((chunk, qkw), lambda b, g, s: (row(b, s), g)),
            pl.BlockSpec((chunk, qkw), lambda b, g, s: (row(b, s), groups + g)),
            pl.BlockSpec((chunk, vw), lambda b, g, s: (row(b, s), groups + g)),
        ]

    def out_spec(row):
        return pl.BlockSpec((chunk, vw), lambda b, g, s: (row(b, s), g))

    out_ctx = jax.ShapeDtypeStruct((batch * n_ctx, v), BF16)
    out_lat = jax.ShapeDtypeStruct((batch * seq, v), BF16)
    yf_ctx, yb_ctx, yf_lat, yb_lat = pl.pallas_call(
        functools.partial(_rt_kernel, n_heads=hpg, chunk=chunk, ctx_steps=ncc),
        grid=(batch, groups, ncc + nlc),
        in_specs=[pl.BlockSpec(memory_space=pltpu.SMEM), pl.BlockSpec((2, vw), lambda b, g, s: (0, g))]
        + in_specs(ctx_f) + in_specs(ctx_b) + in_specs(lat_f) + in_specs(lat_b),
        out_specs=[out_spec(ctx_f), out_spec(ctx_b), out_spec(lat_f), out_spec(lat_b)],
        out_shape=[out_ctx, out_ctx, out_lat, out_lat],
        scratch_shapes=[pltpu.VMEM((2, hpg, dk, dv), F32)],
        compiler_params=_params("parallel", "parallel", "arbitrary"),
        name="retention",
    )(log_gamma, gn_g, *([proj_ctx] * 6), *([proj_lat] * 6))
    return (yf_lat, yb_lat), (yf_ctx, yb_ctx)


def _axial_tables(rows_count, head_dim):
    rows = jnp.repeat(jnp.arange(rows_count), GRID_W).astype(F32)
    cols = jnp.tile(jnp.arange(GRID_W), rows_count).astype(F32)
    n_freq = head_dim // 4
    inv = ROPE_THETA ** (-jnp.arange(n_freq, dtype=F32) / n_freq)
    ang = jnp.concatenate([rows[:, None] * inv, cols[:, None] * inv], -1)
    cos, sin = jnp.cos(ang), jnp.sin(ang)
    return jnp.concatenate([cos, cos], -1), jnp.concatenate([-sin, sin], -1)


def _linear_tables(n_tokens, head_dim):
    half = head_dim // 2
    inv = ROPE_THETA ** (-jnp.arange(half, dtype=F32) / half)
    ang = jnp.arange(n_tokens, dtype=F32)[:, None] * inv
    return jnp.cos(ang), jnp.sin(ang)


def kernel(x, c, ctx, c_ctx, mod_w, mod_b, norm_g, ffn_w_in, ffn_conv_w, ffn_conv_b, ffn_w_out, ga_wqkv, ga_sink, ga_qk_norm, ga_wo, rt_w_in, rt_decay, rt_gn, rt_wo, df_wqkv, df_lambda, df_qk_norm, df_subln, df_wo):
    batch, seq, d = x.shape
    n_ctx = ctx.shape[1]
    depth = mod_w.shape[0]
    assert batch + 1 <= MOD_ROWS and seq % GRID_W == 0

    cond = jnp.zeros((MOD_ROWS, d), F32).at[0].set(c_ctx).at[1:batch + 1].set(c)
    mods = _adaln_table(cond, mod_w, mod_b).reshape(depth, 6, MOD_ROWS, 1, d)

    ax_cos, ax_sin = _axial_tables(seq // GRID_W, GA_HD)
    assert DF_HD == GA_HD
    rt_dk = d // RT_HEADS
    ln_cos, ln_sin = _linear_tables(seq, rt_dk)
    ones2 = jnp.ones((2, LANES), F32)

    h_lat = x.reshape(batch * seq, d)
    h_ctx = ctx.reshape(batch * n_ctx, d)

    lat = _Stream(seq, lambda i, bm: 1 + i // (seq // bm), lambda i, bm: i % (seq // bm), True)
    cst = _Stream(batch * n_ctx, lambda i, bm: 0, lambda i, bm: 0, False)

    ga_w, ga_o = ga_wqkv.astype(BF16), ga_wo.astype(BF16)
    rt_w, rt_o = rt_w_in.astype(BF16), rt_wo.astype(BF16)
    df_w, df_o = df_wqkv.astype(BF16), df_wo.astype(BF16)
    f_in, f_out = ffn_w_in.astype(BF16), ffn_w_out.astype(BF16)
    f_cb = ffn_conv_b.reshape(depth, 1, -1)

    for i in range(depth):
        need_ctx = i < depth - 1
        kind, j = i % N_MIXERS, i // N_MIXERS
        g1 = norm_g[i, 0].reshape(1, d)
        g2 = norm_g[i, 1].reshape(1, d)

        if kind == 0:
            w_all = ga_w
            qcols = ga_sink.shape[1] * GA_HD
            kcols = (w_all.shape[2] - qcols) // 2
            segments = ((qcols, "norm_rope_128", 0, GA_HD ** -0.5 * LOG2E), (kcols, "norm_rope_128", 1, 1.0),
                        (kcols, "plain", 0, 1.0))
            cos_t, sin_t, gains = ax_cos, ax_sin, ga_qk_norm[j]
        elif kind == 1:
            w_all = rt_w
            vcols = (w_all.shape[2] - 2 * d) // 3
            segments = ((d, "rope_256", 0, rt_dk ** -0.5), (d, "rope_256", 0, 1.0),
                        (vcols, "plain", 0, 1.0), (2 * vcols, "silu", 0, 1.0))
            cos_t, sin_t, gains = ln_cos, ln_sin, ones2
        else:
            w_all = df_w
            qcols = w_all.shape[2] // 3
            segments = ((qcols, "norm_rope_128", 0, DF_HD ** -0.5 * LOG2E), (qcols, "norm_rope_128", 1, 1.0),
                        (qcols, "plain", 0, 1.0))
            cos_t, sin_t, gains = ax_cos, ax_sin, df_qk_norm[j]

        proj = {"lat": _proj(h_lat, g1, mods, i, lat, w_all, j, segments, cos_t, sin_t, gains),
                "ctx": _proj(h_ctx, g1, mods, i, cst, w_all, j, segments, cos_t, sin_t, gains)}

        if kind == 0:
            y_lat, y_ctx = _ga_attention(proj["lat"], proj["ctx"], ga_sink[j], batch, seq, n_ctx, need_ctx)
            h_lat = _oproj(y_lat, ga_o, j, h_lat, mods, i, lat)
            if need_ctx:
                h_ctx = _oproj(y_ctx, ga_o, j, h_ctx, mods, i, cst)
        elif kind == 1:
            log_gamma = jax.nn.log_sigmoid(rt_decay[j].astype(F32))
            y_lat, y_ctx = _retention(proj["lat"], proj["ctx"], log_gamma, rt_gn[j], batch, seq, n_ctx)
            h_lat = _oproj_rt(y_lat[0], y_lat[1], proj["lat"], rt_o, j, h_lat, mods, i, lat)
            if need_ctx:
                h_ctx = _oproj_rt(y_ctx[0], y_ctx[1], proj["ctx"], rt_o, j, h_ctx, mods, i, cst)
        else:
            lambda_init = 0.8 - 0.6 * math.exp(-0.3 * i)
            y_lat, y_ctx = _df_attention(proj["lat"], proj["ctx"], df_lambda[j], df_subln[j], batch, seq, n_ctx,
                                         lambda_init, need_ctx)
            h_lat = _oproj(y_lat, df_o, j, h_lat, mods, i, lat)
            if need_ctx:
                h_ctx = _oproj(y_ctx, df_o, j, h_ctx, mods, i, cst)

        h_lat = _ffn(h_lat, g2, mods, i, lat, f_in, ffn_conv_w, f_cb, f_out, seq)
        if need_ctx:
            h_ctx = _ffn(h_ctx, g2, mods, i, cst, f_in, ffn_conv_w, f_cb, f_out, n_ctx)

    return h_lat.reshape(batch, seq, d)
```

```python
import functools
import math
from typing import Callable, NamedTuple

import jax
import jax.numpy as jnp
from jax import lax
from jax.experimental import pallas as pl
from jax.experimental.pallas import tpu as pltpu

F32 = jnp.float32
BF16 = jnp.bfloat16

EPS = 1e-6
NEG_INF = -1e30
LOG2E = math.log2(math.e)
ROPE_THETA = 10000.0
GRID_W = 64
WINDOW = 128
N_MIXERS = 3

GA_HD = 128
GA_GROUP = 4
RT_HEADS = 8
DF_HD = 128
CONV_W = 3

LANES = 128
MXU_COLS = 256
BF16_ROWS = 16
ONES_ROWS = BF16_ROWS
MOD_ROWS = 8
VMEM_LIMIT = 52 * 1024 * 1024


class _Stream(NamedTuple):
    group: int
    mod_row: Callable
    tab_block: Callable
    rope: bool


def _pick(n, candidates):
    for c in candidates:
        if n % c == 0:
            return c
    raise ValueError(f"no tile size in {candidates} divides {n}")


def _params(*sem):
    return pltpu.CompilerParams(dimension_semantics=sem, vmem_limit_bytes=VMEM_LIMIT)


def _silu(x):
    return x * (1.0 / (1.0 + jnp.exp(-x)))


def _dot(a, b):
    return jnp.dot(a, b, preferred_element_type=F32)


def _dot_nt(a, b):
    return lax.dot_general(a, b, (((1,), (1,)), ((), ())), preferred_element_type=F32)


def _dot_tn(a, b):
    return lax.dot_general(a, b, (((0,), (0,)), ((), ())), preferred_element_type=F32)


def _adaln_kernel(c_ref, w_ref, b_ref, o_ref):
    s = _silu(c_ref[...]).astype(BF16)
    o_ref[...] = _dot(s, w_ref[...].astype(BF16)) + b_ref[...]


def _adaln_table(cond, mod_w, mod_b):
    depth, d, _ = mod_w.shape
    bn = _pick(d, (1024, 512, 256, 128))
    nj = d // bn
    return pl.pallas_call(
        _adaln_kernel,
        grid=(depth, 6, nj),
        in_specs=[
            pl.BlockSpec((MOD_ROWS, d), lambda l, k, j: (0, 0)),
            pl.BlockSpec((None, d, bn), lambda l, k, j: (l, 0, k * nj + j)),
            pl.BlockSpec((None, 1, bn), lambda l, k, j: (l, 0, k * nj + j)),
        ],
        out_specs=pl.BlockSpec((None, None, MOD_ROWS, bn), lambda l, k, j: (l, k, 0, j)),
        out_shape=jax.ShapeDtypeStruct((depth, 6, MOD_ROWS, d), F32),
        compiler_params=_params("parallel", "parallel", "parallel"),
        name="adaln_table",
    )(cond, mod_w, mod_b.reshape(depth, 1, 6 * d))


def _mod_spec(layer, slot, row_fn, d):
    return pl.BlockSpec((None, None, None, 1, d), lambda i, j: (layer, slot, row_fn(i), 0, 0))


def _mod_spec_cols(layer, slot, row_fn, bn):
    return pl.BlockSpec((None, None, None, 1, bn), lambda i, j: (layer, slot, row_fn(i), 0, j))


def _norm_mod(x, g, shift, scale):
    ms = jnp.mean(x * x, axis=-1, keepdims=True)
    y = x * lax.rsqrt(ms + EPS) * g
    return y * (1.0 + scale) + shift


def _proj_kernel(h_ref, g_ref, sh_ref, sc_ref, w_ref, cos_ref, sin_ref, gain_ref, seg_ref, o_ref, a_scr, acc_scr,
                 *, plan, rope):
    j = pl.program_id(1)
    sub = acc_scr.shape[2]
    n_sub = o_ref.shape[1] // sub

    @pl.when(j == 0)
    def _():
        a_scr[...] = _norm_mod(h_ref[...], g_ref[...], sh_ref[...], sc_ref[...]).astype(BF16)

    def epilogue(s, kind, arg, scale):
        base = s * sub
        if kind == "plain":
            o_ref[:, base:base + sub] = acc_scr[s].astype(o_ref.dtype)
        elif kind == "silu":
            o_ref[:, base:base + sub] = _silu(acc_scr[s]).astype(o_ref.dtype)
        elif kind == "norm_rope_128":
            x = acc_scr[s]
            ms = _dot((x * x).astype(BF16), seg_ref[...])
            y = x * lax.rsqrt(ms + EPS) * (gain_ref[arg:arg + 1, :] * scale)
            for t in range(0, sub, LANES):
                yt = y[:, t:t + LANES]
                if rope:
                    yt = yt * cos_ref[...] + pltpu.roll(yt, LANES // 2, 1) * sin_ref[...]
                o_ref[:, base + t:base + t + LANES] = yt.astype(o_ref.dtype)
        else:
            assert kind == "rope_256" and sub == 2 * LANES
            x1 = acc_scr[s, :, 0:LANES] * scale
            x2 = acc_scr[s, :, LANES:sub] * scale
            if rope:
                c, sn = cos_ref[...], sin_ref[...]
                x1, x2 = x1 * c - x2 * sn, x1 * sn + x2 * c
            o_ref[:, base:base + LANES] = x1.astype(o_ref.dtype)
            o_ref[:, base + LANES:base + sub] = x2.astype(o_ref.dtype)

    for lo, hi, kinds in plan:
        @pl.when((j >= lo) & (j < hi))
        def _(kinds=kinds):
            a = a_scr[...]
            for s in range(n_sub):
                acc_scr[s] = _dot(a, w_ref[:, s * sub:(s + 1) * sub])
            for s, (kind, arg, scale) in enumerate(kinds):
                epilogue(s, kind, arg, scale)


def _proj(h, norm_g, mods, layer, st, w_all, w_idx, segments, cos_t, sin_t, gains):
    m, d = h.shape
    n = w_all.shape[2]
    bm = _pick(st.group, (1024, 512, 256))
    bn = _pick(n, (1024, 512, 256))
    sub = MXU_COLS
    n_sub = bn // sub
    per_sub = []
    for width, kind, arg, scale in segments:
        assert width % sub == 0
        per_sub += [(kind, arg, scale)] * (width // sub)
    assert len(per_sub) * sub == n
    tiles = [tuple(per_sub[t * n_sub:(t + 1) * n_sub]) for t in range(n // bn)]
    plan, lo = [], 0
    for t in range(1, len(tiles) + 1):
        if t == len(tiles) or tiles[t] != tiles[lo]:
            plan.append((lo, t, tiles[lo]))
            lo = t
    row_fn, tab_fn = st.mod_row, st.tab_block
    kernel = functools.partial(_proj_kernel, plan=tuple(plan), rope=st.rope)
    head = jnp.arange(sub) // LANES
    seg_mean = jnp.where(head[:, None] == head[None, :], 1.0 / LANES, 0.0).astype(BF16)
    gains = jnp.tile(gains, (1, sub // LANES))
    return pl.pallas_call(
        kernel,
        grid=(m // bm, n // bn),
        in_specs=[
            pl.BlockSpec((bm, d), lambda i, j: (i, 0)),
            pl.BlockSpec((1, d), lambda i, j: (0, 0)),
            _mod_spec(layer, 0, lambda i: row_fn(i, bm), d),
            _mod_spec(layer, 1, lambda i: row_fn(i, bm), d),
            pl.BlockSpec((None, d, bn), lambda i, j: (w_idx, 0, j)),
            pl.BlockSpec((bm, LANES), lambda i, j: (tab_fn(i, bm), 0)),
            pl.BlockSpec((bm, LANES), lambda i, j: (tab_fn(i, bm), 0)),
            pl.BlockSpec(gains.shape, lambda i, j: (0, 0)),
            pl.BlockSpec((sub, sub), lambda i, j: (0, 0)),
        ],
        out_specs=pl.BlockSpec((bm, bn), lambda i, j: (i, j)),
        out_shape=jax.ShapeDtypeStruct((m, n), BF16),
        scratch_shapes=[pltpu.VMEM((bm, d), BF16), pltpu.VMEM((n_sub, bm, sub), F32)],
        compiler_params=_params("parallel", "arbitrary"),
        name="proj",
    )(h, norm_g, mods, mods, w_all, cos_t, sin_t, gains, seg_mean)


def _oproj_kernel(y_ref, w_ref, h_ref, gate_ref, o_ref):
    o_ref[...] = h_ref[...] + gate_ref[...] * _dot(y_ref[...], w_ref[...])


def _oproj_rt_kernel(yf_ref, yb_ref, sf_ref, sb_ref, w_ref, h_ref, gate_ref, o_ref):
    c = pl.program_id(1)

    @pl.when(c == 0)
    def _():
        o_ref[...] = jnp.zeros_like(o_ref)

    ck = w_ref.shape[0]
    sub = min(ck, MXU_COLS)
    part = None
    for s in range(0, ck, sub):
        cols = slice(s, s + sub)
        z = sf_ref[:, cols] * yf_ref[:, cols] + sb_ref[:, cols] * yb_ref[:, cols]
        d = _dot(z, w_ref[cols, :])
        part = d if part is None else part + d
    o_ref[...] += part

    @pl.when(c == pl.num_programs(1) - 1)
    def _():
        o_ref[...] = h_ref[...] + gate_ref[...] * o_ref[...]


def _oproj(y, w_all, w_idx, h, mods, layer, st):
    m, k = y.shape
    d = w_all.shape[2]
    bm = _pick(st.group, (1024, 512, 256))
    row_fn = st.mod_row
    bn = _pick(d, (512, 256, 128))
    return pl.pallas_call(
        _oproj_kernel,
        grid=(m // bm, d // bn),
        in_specs=[
            pl.BlockSpec((bm, k), lambda i, j: (i, 0)),
            pl.BlockSpec((None, k, bn), lambda i, j: (w_idx, 0, j)),
            pl.BlockSpec((bm, bn), lambda i, j: (i, j)),
            _mod_spec_cols(layer, 2, lambda i: row_fn(i, bm), bn),
        ],
        out_specs=pl.BlockSpec((bm, bn), lambda i, j: (i, j)),
        out_shape=jax.ShapeDtypeStruct((m, d), F32),
        input_output_aliases={2: 0},
        compiler_params=_params("parallel", "arbitrary"),
        name="oproj",
    )(y, w_all, h, mods)


def _oproj_rt(yf, yb, proj, w_all, w_idx, h, mods, layer, st):
    m, v = yf.shape
    d = w_all.shape[2]
    bm = _pick(st.group, (512, 256))
    row_fn = st.mod_row
    ck = _pick(v, (1024, 512, 256, 128))
    nc = v // ck
    gcol = (proj.shape[1] // v - 2) * nc
    return pl.pallas_call(
        _oproj_rt_kernel,
        grid=(m // bm, nc),
        in_specs=[
            pl.BlockSpec((bm, ck), lambda i, c: (i, c)),
            pl.BlockSpec((bm, ck), lambda i, c: (i, c)),
            pl.BlockSpec((bm, ck), lambda i, c: (i, gcol + c)),
            pl.BlockSpec((bm, ck), lambda i, c: (i, gcol + nc + c)),
            pl.BlockSpec((None, ck, d), lambda i, c: (w_idx, c, 0)),
            pl.BlockSpec((bm, d), lambda i, c: (i, 0)),
            _mod_spec(layer, 2, lambda i: row_fn(i, bm), d),
        ],
        out_specs=pl.BlockSpec((bm, d), lambda i, c: (i, 0)),
        out_shape=jax.ShapeDtypeStruct((m, d), F32),
        input_output_aliases={5: 0},
        compiler_params=_params("parallel", "arbitrary"),
        name="oproj_rt",
    )(yf, yb, proj, proj, w_all, h, mods)


def _ffn_kernel(h_ref, hp_ref, hn_ref, g_ref, sh_ref, sc_ref, gate_ref, wa_ref, wb_ref,
                cwa_lo, cwb_lo, cba_lo, cbb_lo, wo_lo, cwa_hi, cwb_hi, cba_hi, cbb_hi, wo_hi,
                cwa_top, cwb_top, cba_top, cbb_top, wo_top,
                o_ref, a_scr, u_even, u_odd_a, u_odd_b, *, bm, seq_len, steps):
    i = pl.program_id(0)
    j = pl.program_id(1)
    halo = BF16_ROWS
    sub = u_even.shape[2]
    lo = (cwa_lo, cwb_lo, cba_lo, cbb_lo, wo_lo)
    hi = (cwa_hi, cwb_hi, cba_hi, cbb_hi, wo_hi)

    def up(half, u):
        cols = slice(half * sub, (half + 1) * sub)
        a = a_scr[...]
        u[0] = _dot(a, wa_ref[:, cols])
        u[1] = _dot(a, wb_ref[:, cols])

    def gate(u, cwa, cwb, cba, cbb, wo):
        def conv(idx, cw_ref, cb_ref):
            prev = u[idx, halo - 1:halo - 1 + bm, :]
            cur = u[idx, halo:halo + bm, :]
            nxt = u[idx, halo + 1:halo + 1 + bm, :]
            return cb_ref[...] + prev * cw_ref[0:1, :] + cur * cw_ref[1:2, :] + nxt * cw_ref[2:3, :]

        return (_silu(conv(0, cwa, cba)) * conv(1, cwb, cbb)).astype(BF16)

    def down(u, *params):
        o_ref[...] += _dot(gate(u, *params), params[-1][...])

    @pl.when(j == 0)
    def _():
        def nm(x):
            return _norm_mod(x, g_ref[...], sh_ref[...], sc_ref[...])

        keep_prev = jnp.where((i * bm) % seq_len != 0, 1.0, 0.0)
        keep_next = jnp.where(((i + 1) * bm) % seq_len != 0, 1.0, 0.0)
        a_scr[0:halo, :] = (nm(hp_ref[...]) * keep_prev).astype(BF16)
        a_scr[halo:halo + bm, :] = nm(h_ref[...]).astype(BF16)
        a_scr[halo + bm:, :] = (nm(hn_ref[...]) * keep_next).astype(BF16)
        o_ref[...] = jnp.zeros_like(o_ref)
        up(0, u_even)
        up(1, u_odd_a)
        down(u_even, *hi)

    def middle(u_new, u_old):
        up(0, u_even)
        up(1, u_new)
        g = jnp.concatenate([gate(u_old, *lo), gate(u_even, *hi)], axis=1)
        w = jnp.concatenate([wo_lo[...], wo_hi[...]], axis=0)
        o_ref[...] += _dot(g, w)

    inner = (j > 0) & (j < steps - 1)
    pl.when(inner & (j % 2 == 1))(lambda: middle(u_odd_b, u_odd_a))
    pl.when(inner & (j % 2 == 0))(lambda: middle(u_odd_a, u_odd_b))

    @pl.when(j == steps - 1)
    def _():
        u_new, u_old = (u_odd_a, u_odd_b) if (steps - 1) % 2 == 0 else (u_odd_b, u_odd_a)
        middle(u_new, u_old)
        down(u_new, cwa_top, cwb_top, cba_top, cbb_top, wo_top)
        o_ref[...] = h_ref[...] + gate_ref[...] * o_ref[...]


def _ffn(h, norm_g, mods, layer, st, w_in, conv_w, conv_b, w_out, seq_len):
    m, d = h.shape
    dff = w_out.shape[1]
    bm = _pick(seq_len, (512, 256))
    sub = MXU_COLS
    nsc = dff // sub
    assert dff % (2 * sub) == 0
    steps = nsc // 2
    assert steps >= 2
    halo = BF16_ROWS
    hb = bm // halo
    last = m // halo - 1
    kernel = functools.partial(_ffn_kernel, bm=bm, seq_len=seq_len, steps=steps)
    mrow = lambda i: st.mod_row(i, bm)
    chunk = (lambda j: jnp.maximum(2 * j - 1, 0), lambda j: 2 * j, lambda j: nsc - 1)

    def down_specs(k):
        return [
            pl.BlockSpec((None, CONV_W, sub), lambda i, j: (layer, 0, k(j))),
            pl.BlockSpec((None, CONV_W, sub), lambda i, j: (layer, 0, nsc + k(j))),
            pl.BlockSpec((None, 1, sub), lambda i, j: (layer, 0, k(j))),
            pl.BlockSpec((None, 1, sub), lambda i, j: (layer, 0, nsc + k(j))),
            pl.BlockSpec((None, sub, d), lambda i, j: (layer, k(j), 0)),
        ]

    down_args = [conv_w, conv_w, conv_b, conv_b, w_out]
    return pl.pallas_call(
        kernel,
        grid=(m // bm, steps),
        in_specs=[
            pl.BlockSpec((bm, d), lambda i, j: (i, 0)),
            pl.BlockSpec((halo, d), lambda i, j: (jnp.maximum(i * hb - 1, 0), 0)),
            pl.BlockSpec((halo, d), lambda i, j: (jnp.minimum((i + 1) * hb, last), 0)),
            pl.BlockSpec((1, d), lambda i, j: (0, 0)),
            _mod_spec(layer, 3, mrow, d),
            _mod_spec(layer, 4, mrow, d),
            _mod_spec(layer, 5, mrow, d),
            pl.BlockSpec((None, d, 2 * sub), lambda i, j: (layer, 0, j)),
            pl.BlockSpec((None, d, 2 * sub), lambda i, j: (layer, 0, steps + j)),
        ] + down_specs(chunk[0]) + down_specs(chunk[1]) + down_specs(chunk[2]),
        out_specs=pl.BlockSpec((bm, d), lambda i, j: (i, 0)),
        out_shape=jax.ShapeDtypeStruct((m, d), F32),
        scratch_shapes=[pltpu.VMEM((bm + 2 * halo, d), BF16),
                        pltpu.VMEM((2, bm + 2 * halo, sub), F32),
                        pltpu.VMEM((2, bm + 2 * halo, sub), F32),
                        pltpu.VMEM((2, bm + 2 * halo, sub), F32)],
        compiler_params=_params("parallel", "arbitrary"),
        name="conv_ffn",
    )(h, h, h, norm_g, mods, mods, mods, w_in, w_in, *down_args, *down_args, *down_args)


def _ga_kernel(sink_ref, q_ref, kvc_ref, *rest, seq, n_kv, local):
    if local:
        kvo_ref, kvp_ref, kvn_ref, o_ref, s_a, s_b = rest
    else:
        o_ref, s_a, s_b = rest
    hd = GA_HD
    kcols = n_kv * hd
    bq = q_ref.shape[0]
    sq = s_a.shape[1]
    n_ctx = kvc_ref.shape[0]
    bufs = (s_a, s_b)

    def band(u):
        if not local:
            return [], 0
        lo, hi = u * sq - WINDOW, (u + 1) * sq + WINDOW
        pieces = []
        if lo < 0:
            pieces.append((kvp_ref, WINDOW + lo, -lo))
        pieces.append((kvo_ref, max(lo, 0), min(hi, bq) - max(lo, 0)))
        if hi > bq:
            pieces.append((kvn_ref, 0, hi - bq))
        return pieces, lo

    def bias(u):
        pieces, lo = band(u)
        n_loc = sum(p[2] for p in pieces)
        kl = lax.broadcasted_iota(jnp.int32, (n_loc, sq), 0) + lo
        ql = lax.broadcasted_iota(jnp.int32, (n_loc, sq), 1) + u * sq
        pos = pl.program_id(1) * bq + kl
        ok = (jnp.abs(ql - kl) <= WINDOW) & (pos >= 0) & (pos < seq)
        return jnp.concatenate([jnp.where(ok, 0.0, NEG_INF).astype(F32), jnp.zeros((n_ctx, sq), F32)], axis=0)

    def keys_values(u, n):
        pieces, _ = band(u)
        ks, vs = slice(n * hd, (n + 1) * hd), slice(kcols + n * hd, kcols + (n + 1) * hd)
        k_all = jnp.concatenate([r[a:a + c, ks] for r, a, c in pieces] + [kvc_ref[:, ks]], axis=0)
        v_all = jnp.concatenate([r[a:a + c, vs] for r, a, c in pieces] + [kvc_ref[:, vs]], axis=0)
        v_t = jnp.concatenate([v_all.astype(F32).T.astype(BF16), jnp.ones((ONES_ROWS, v_all.shape[0]), BF16)], axis=0)
        return k_all, v_t

    chains = [(u, n, g) for u in range(bq // sq) for n in range(n_kv) for g in range(GA_GROUP)]
    cache = {}

    def operands(u, n):
        if (u, n) not in cache:
            cache.clear()
            cache[(u, n)] = keys_values(u, n)
        return cache[(u, n)]

    biases = {}

    def scores(idx):
        u, n, g = chains[idx]
        head = n * GA_GROUP + g
        k_all, _ = operands(u, n)
        s = _dot_nt(k_all, q_ref[u * sq:(u + 1) * sq, head * hd:(head + 1) * hd])
        if local:
            if u not in biases:
                biases[u] = bias(u)
            s = s + biases[u]
        bufs[idx % 2][...] = s

    scores(0)
    for idx, (u, n, g) in enumerate(chains):
        _, v_t = operands(u, n)
        if idx + 1 < len(chains):
            scores(idx + 1)
        buf = bufs[idx % 2]
        head = n * GA_GROUP + g
        sink = sink_ref[head] * LOG2E
        m = jnp.maximum(jnp.max(buf[...], axis=0, keepdims=True), sink)
        p = jnp.exp2((buf[...] - m).astype(BF16))
        ov = _dot(v_t, p)
        o_t = ov[0:hd, :] / (ov[hd:hd + 1, :] + jnp.exp2(sink - m))
        o_ref[u * sq:(u + 1) * sq, head * hd:(head + 1) * hd] = o_t.T.astype(o_ref.dtype)


def _ga_attention(qkv_lat, qkv_ctx, sink, batch, seq, n_ctx, need_ctx):
    n_heads = sink.shape[0]
    qcols = n_heads * GA_HD
    n_kv = n_heads // GA_GROUP
    kvw = 2 * n_kv * GA_HD
    kvblk = qcols // kvw
    assert qcols % kvw == 0
    bq = _pick(seq, (512, 256, 128))
    sq = min(bq, MXU_COLS)
    nq = seq // bq
    wb = bq // WINDOW
    last_w = batch * seq // WINDOW - 1
    smem = pl.BlockSpec(memory_space=pltpu.SMEM)
    lat = pl.pallas_call(
        functools.partial(_ga_kernel, seq=seq, n_kv=n_kv, local=True),
        grid=(batch, nq),
        in_specs=[
            smem,
            pl.BlockSpec((bq, qcols), lambda b, i: (b * nq + i, 0)),
            pl.BlockSpec((n_ctx, kvw), lambda b, i: (b, kvblk)),
            pl.BlockSpec((bq, kvw), lambda b, i: (b * nq + i, kvblk)),
            pl.BlockSpec((WINDOW, kvw), lambda b, i: (jnp.maximum((b * nq + i) * wb - 1, 0), kvblk)),
            pl.BlockSpec((WINDOW, kvw), lambda b, i: (jnp.minimum((b * nq + i + 1) * wb, last_w), kvblk)),
        ],
        out_specs=pl.BlockSpec((bq, qcols), lambda b, i: (b * nq + i, 0)),
        out_shape=jax.ShapeDtypeStruct((batch * seq, qcols), BF16),
        scratch_shapes=[pltpu.VMEM((sq + 2 * WINDOW + n_ctx, sq), F32)] * 2,
        compiler_params=_params("parallel", "parallel"),
        name="ga_attention_latent",
    )(sink, qkv_lat, qkv_ctx, qkv_lat, qkv_lat, qkv_lat)
    ctx = None
    if need_ctx:
        cq = min(n_ctx, MXU_COLS)
        ctx = pl.pallas_call(
            functools.partial(_ga_kernel, seq=n_ctx, n_kv=n_kv, local=False),
            grid=(batch,),
            in_specs=[
                smem,
                pl.BlockSpec((n_ctx, qcols), lambda b: (b, 0)),
                pl.BlockSpec((n_ctx, kvw), lambda b: (b, kvblk)),
            ],
            out_specs=pl.BlockSpec((n_ctx, qcols), lambda b: (b, 0)),
            out_shape=jax.ShapeDtypeStruct((batch * n_ctx, qcols), BF16),
            scratch_shapes=[pltpu.VMEM((n_ctx, cq), F32)] * 2,
            compiler_params=_params("parallel"),
            name="ga_attention_context",
        )(sink, qkv_ctx, qkv_ctx)
    return lat, ctx


def _df_kernel(lam_ref, g_ref, q_ref, *rest, lambda_init, chunks, first_q_axis):
    n_seg = len(chunks)
    kv_refs = rest[:2 * n_seg]
    o_ref = rest[2 * n_seg]
    vt_scrs = rest[2 * n_seg + 1:2 * n_seg + 1 + n_seg]
    acc_scr, s_a, s_b = rest[-3:]
    hd = DF_HD
    bq = q_ref.shape[0]
    lam = lam_ref[...]
    lmbda = (jnp.exp(jnp.sum(lam[0:1] * lam[1:2], axis=-1, keepdims=True))
             - jnp.exp(jnp.sum(lam[2:3] * lam[3:4], axis=-1, keepdims=True)) + lambda_init)

    def transpose_values():
        for seg, ck in enumerate(chunks):
            v_ref, vt = kv_refs[2 * seg + 1], vt_scrs[seg]
            for c in range(v_ref.shape[0] // ck):
                vt[c, 0:2 * hd, :] = v_ref[c * ck:(c + 1) * ck, :].astype(F32).T.astype(BF16)
                vt[c, 2 * hd:, :] = jnp.ones((ONES_ROWS, ck), BF16)

    if first_q_axis is None:
        transpose_values()
    else:
        pl.when(pl.program_id(first_q_axis) == 0)(transpose_values)

    qs = (q_ref[:, 0:hd], q_ref[:, hd:2 * hd])
    acc_scr[...] = jnp.zeros_like(acc_scr)
    m0 = jnp.full((1, bq), NEG_INF, F32)
    carry = (m0, m0)

    def scores(k_ref, c, ck, dst):
        kblk = k_ref[c * ck:(c + 1) * ck, :]
        for r in range(2):
            dst[r] = _dot_nt(kblk[:, r * hd:(r + 1) * hd], qs[r])

    def update(src, v_t, carry):
        new = []
        for r in range(2):
            m_new = jnp.maximum(carry[r], jnp.max(src[r], axis=0, keepdims=True))
            alpha = jnp.exp2(carry[r] - m_new)
            p = jnp.exp2((src[r] - m_new).astype(BF16))
            acc_scr[r] = alpha * acc_scr[r] + _dot(v_t, p)
            new.append(m_new)
        return tuple(new)

    items = [(kv_refs[2 * seg], c, chunks[seg], vt_scrs[seg])
             for seg in range(n_seg) for c in range(kv_refs[2 * seg].shape[0] // chunks[seg])]
    bufs = (s_a, s_b)

    def view(buf, ck):
        return buf if ck == buf.shape[1] else buf.at[:, 0:ck, :]

    scores(items[0][0], items[0][1], items[0][2], view(bufs[0], items[0][2]))
    for idx, (_, c, ck, vt) in enumerate(items):
        if idx + 1 < len(items):
            k_next, c_next, ck_next, _ = items[idx + 1]
            scores(k_next, c_next, ck_next, view(bufs[(idx + 1) % 2], ck_next))
        carry = update(view(bufs[idx % 2], ck), vt[c], carry)
    dv = 2 * hd
    o = (acc_scr[0, 0:dv, :] / acc_scr[0, dv:dv + 1, :]
         - lmbda * (acc_scr[1, 0:dv, :] / acc_scr[1, dv:dv + 1, :])).T
    ms = jnp.mean(o * o, axis=-1, keepdims=True)
    y = o * lax.rsqrt(ms + EPS) * g_ref[...] * (1.0 - lambda_init)
    o_ref[...] = y.astype(o_ref.dtype)


def _df_attention(qkv_lat, qkv_ctx, lam, subln_g, batch, seq, n_ctx, lambda_init, need_ctx):
    hw = 2 * DF_HD
    n_heads = qkv_lat.shape[1] // (3 * hw)
    bq = _pick(seq, (512, 256, 128))
    nq = seq // bq
    ck_lat = _pick(seq, (512, 256, 128))
    ck_ctx = _pick(n_ctx, (512, 256, 128))
    g2 = subln_g.reshape(1, hw)
    const = lambda shape: pl.BlockSpec(shape, lambda *_: (0,) * len(shape))
    vt_ctx = pltpu.VMEM((n_ctx // ck_ctx, hw + ONES_ROWS, ck_ctx), BF16)
    vt_lat = pltpu.VMEM((seq // ck_lat, hw + ONES_ROWS, ck_lat), BF16)
    lat = pl.pallas_call(
        functools.partial(_df_kernel, lambda_init=lambda_init, chunks=(ck_ctx, ck_lat), first_q_axis=2),
        grid=(batch, n_heads, nq),
        in_specs=[
            const(lam.shape),
            const((1, hw)),
            pl.BlockSpec((bq, hw), lambda b, h, i: (b * nq + i, h)),
            pl.BlockSpec((n_ctx, hw), lambda b, h, i: (b, n_heads + h)),
            pl.BlockSpec((n_ctx, hw), lambda b, h, i: (b, 2 * n_heads + h)),
            pl.BlockSpec((seq, hw), lambda b, h, i: (b, n_heads + h)),
            pl.BlockSpec((seq, hw), lambda b, h, i: (b, 2 * n_heads + h)),
        ],
        out_specs=pl.BlockSpec((bq, hw), lambda b, h, i: (b * nq + i, h)),
        out_shape=jax.ShapeDtypeStruct((batch * seq, n_heads * hw), BF16),
        scratch_shapes=[vt_ctx, vt_lat, pltpu.VMEM((2, hw + ONES_ROWS, bq), F32)]
        + [pltpu.VMEM((2, max(ck_lat, ck_ctx), bq), F32)] * 2,
        compiler_params=_params("parallel", "parallel", "arbitrary"),
        name="df_attention_latent",
    )(lam, g2, qkv_lat, qkv_ctx, qkv_ctx, qkv_lat, qkv_lat)
    ctx = None
    if need_ctx:
        ctx = pl.pallas_call(
            functools.partial(_df_kernel, lambda_init=lambda_init, chunks=(ck_ctx,), first_q_axis=None),
            grid=(batch, n_heads),
            in_specs=[
                const(lam.shape),
                const((1, hw)),
                pl.BlockSpec((n_ctx, hw), lambda b, h: (b, h)),
                pl.BlockSpec((n_ctx, hw), lambda b, h: (b, n_heads + h)),
                pl.BlockSpec((n_ctx, hw), lambda b, h: (b, 2 * n_heads + h)),
            ],
            out_specs=pl.BlockSpec((n_ctx, hw), lambda b, h: (b, h)),
            out_shape=jax.ShapeDtypeStruct((batch * n_ctx, n_heads * hw), BF16),
            scratch_shapes=[vt_ctx, pltpu.VMEM((2, hw + ONES_ROWS, n_ctx), F32)]
            + [pltpu.VMEM((2, ck_ctx, n_ctx), F32)] * 2,
            compiler_params=_params("parallel", "parallel"),
            name="df_attention_context",
        )(lam, g2, qkv_ctx, qkv_ctx, qkv_ctx)
    return lat, ctx


def _rt_kernel(lg_ref, gn_ref, *refs, n_heads, chunk, ctx_steps):
    ctx_in, lat_in = refs[0:6], refs[6:12]
    ctx_out, lat_out = refs[12:14], refs[14:16]
    st_scr = refs[16]
    head0 = pl.program_id(1) * n_heads
    s_idx = pl.program_id(2)
    dk = ctx_in[0].shape[1] // n_heads
    dv = ctx_in[2].shape[1] // n_heads

    @pl.when(s_idx == 0)
    def _():
        st_scr[...] = jnp.zeros_like(st_scr)

    def step(ins, outs):
        qf_ref, kf_ref, vf_ref, qb_ref, kb_ref, vb_ref = ins
        of_ref, ob_ref = outs
        row = lax.broadcasted_iota(jnp.int32, (chunk, chunk), 0).astype(F32)
        col = lax.broadcasted_iota(jnp.int32, (chunk, chunk), 1).astype(F32)
        pos = lax.broadcasted_iota(jnp.int32, (chunk, 1), 0).astype(F32)
        dirs = (
            (qf_ref, kf_ref, vf_ref, of_ref, row - col, pos + 1.0, chunk - 1.0 - pos),
            (qb_ref, kb_ref, vb_ref, ob_ref, col - row, chunk - pos, pos),
        )
        for d, (q_ref, k_ref, v_ref, o_ref, rel, q_pow, k_pow) in enumerate(dirs):
            for h in range(n_heads):
                lg = lg_ref[d, head0 + h]
                decay = jnp.where(rel >= 0, jnp.exp(lg * jnp.maximum(rel, 0.0)), 0.0)
                q_decay = jnp.exp(lg * q_pow)
                k_decay = jnp.exp(lg * k_pow)
                chunk_decay = jnp.exp(lg * chunk)
                q = q_ref[:, h * dk:(h + 1) * dk]
                k = k_ref[:, h * dk:(h + 1) * dk]
                v = v_ref[:, h * dv:(h + 1) * dv]
                state = st_scr[d, h]
                inner = _dot_nt(q, k) * decay
                o = _dot(inner.astype(BF16), v) + _dot(q, state.astype(BF16)) * q_decay
                kd = (k.astype(F32) * k_decay).astype(BF16)
                st_scr[d, h] = state * chunk_decay + _dot_tn(kd, v)
                mu = jnp.mean(o, axis=-1, keepdims=True)
                dev = o - mu
                var = jnp.mean(dev * dev, axis=-1, keepdims=True)
                y = dev * lax.rsqrt(var + EPS) * gn_ref[d:d + 1, h * dv:(h + 1) * dv]
                o_ref[:, h * dv:(h + 1) * dv] = y.astype(o_ref.dtype)

    @pl.when(s_idx < ctx_steps)
    def _():
        step(ctx_in, ctx_out)

    @pl.when(s_idx >= ctx_steps)
    def _():
        step(lat_in, lat_out)


def _retention(proj_lat, proj_ctx, log_gamma, gn_g, batch, seq, n_ctx):
    n_heads = log_gamma.shape[1]
    qk = proj_lat.shape[1] // 8
    v = 2 * qk
    chunk = _pick(math.gcd(seq, n_ctx), (256, 128))
    ncc, nlc = n_ctx // chunk, seq // chunk
    dk, dv = qk // n_heads, v // n_heads
    groups = 2
    hpg = n_heads // groups
    qkw, vw = hpg * dk, hpg * dv

    def ctx_f(b, s):
        return b * ncc + jnp.minimum(s, ncc - 1)

    def ctx_b(b, s):
        return b * ncc + (ncc - 1 - jnp.minimum(s, ncc - 1))

    def lat_f(b, s):
        return b * nlc + jnp.maximum(s - ncc, 0)

    def lat_b(b, s):
        return b * nlc + (nlc - 1 - jnp.maximum(s - ncc, 0))

    def in_specs(row):
        return [
            pl.BlockSpec((chunk, qkw), lambda b, g, s: (row(b, s), g)),
            pl.BlockSpec((chunk, qkw), lambda b, g, s: (row(b, s), groups + g)),
            pl.BlockSpec((chunk, vw), lambda b, g, s: (row(b, s), groups + g)),
        ]

    def out_spec(row):
        return pl.BlockSpec((chunk, vw), lambda b, g, s: (row(b, s), g))

    out_ctx = jax.ShapeDtypeStruct((batch * n_ctx, v), BF16)
    out_lat = jax.ShapeDtypeStruct((batch * seq, v), BF16)
    yf_ctx, yb_ctx, yf_lat, yb_lat = pl.pallas_call(
        functools.partial(_rt_kernel, n_heads=hpg, chunk=chunk, ctx_steps=ncc),
        grid=(batch, groups, ncc + nlc),
        in_specs=[pl.BlockSpec(memory_space=pltpu.SMEM), pl.BlockSpec((2, vw), lambda b, g, s: (0, g))]
        + in_specs(ctx_f) + in_specs(ctx_b) + in_specs(lat_f) + in_specs(lat_b),
        out_specs=[out_spec(ctx_f), out_spec(ctx_b), out_spec(lat_f), out_spec(lat_b)],
        out_shape=[out_ctx, out_ctx, out_lat, out_lat],
        scratch_shapes=[pltpu.VMEM((2, hpg, dk, dv), F32)],
        compiler_params=_params("parallel", "parallel", "arbitrary"),
        name="retention",
    )(log_gamma, gn_g, *([proj_ctx] * 6), *([proj_lat] * 6))
    return (yf_lat, yb_lat), (yf_ctx, yb_ctx)


def _axial_tables(rows_count, head_dim):
    rows = jnp.repeat(jnp.arange(rows_count), GRID_W).astype(F32)
    cols = jnp.tile(jnp.arange(GRID_W), rows_count).astype(F32)
    n_freq = head_dim // 4
    inv = ROPE_THETA ** (-jnp.arange(n_freq, dtype=F32) / n_freq)
    ang = jnp.concatenate([rows[:, None] * inv, cols[:, None] * inv], -1)
    cos, sin = jnp.cos(ang), jnp.sin(ang)
    return jnp.concatenate([cos, cos], -1), jnp.concatenate([-sin, sin], -1)


def _linear_tables(n_tokens, head_dim):
    half = head_dim // 2
    inv = ROPE_THETA ** (-jnp.arange(half, dtype=F32) / half)
    ang = jnp.arange(n_tokens, dtype=F32)[:, None] * inv
    return jnp.cos(ang), jnp.sin(ang)


def kernel(x, c, ctx, c_ctx, mod_w, mod_b, norm_g, ffn_w_in, ffn_conv_w, ffn_conv_b, ffn_w_out, ga_wqkv, ga_sink, ga_qk_norm, ga_wo, rt_w_in, rt_decay, rt_gn, rt_wo, df_wqkv, df_lambda, df_qk_norm, df_subln, df_wo):
    batch, seq, d = x.shape
    n_ctx = ctx.shape[1]
    depth = mod_w.shape[0]
    assert batch + 1 <= MOD_ROWS and seq % GRID_W == 0

    cond = jnp.zeros((MOD_ROWS, d), F32).at[0].set(c_ctx).at[1:batch + 1].set(c)
    mods = _adaln_table(cond, mod_w, mod_b).reshape(depth, 6, MOD_ROWS, 1, d)

    ax_cos, ax_sin = _axial_tables(seq // GRID_W, GA_HD)
    assert DF_HD == GA_HD
    rt_dk = d // RT_HEADS
    ln_cos, ln_sin = _linear_tables(seq, rt_dk)
    ones2 = jnp.ones((2, LANES), F32)

    h_lat = x.reshape(batch * seq, d)
    h_ctx = ctx.reshape(batch * n_ctx, d)

    lat = _Stream(seq, lambda i, bm: 1 + i // (seq // bm), lambda i, bm: i % (seq // bm), True)
    cst = _Stream(batch * n_ctx, lambda i, bm: 0, lambda i, bm: 0, False)

    ga_w, ga_o = ga_wqkv.astype(BF16), ga_wo.astype(BF16)
    rt_w, rt_o = rt_w_in.astype(BF16), rt_wo.astype(BF16)
    df_w, df_o = df_wqkv.astype(BF16), df_wo.astype(BF16)
    f_in, f_out = ffn_w_in.astype(BF16), ffn_w_out.astype(BF16)
    f_cb = ffn_conv_b.reshape(depth, 1, -1)

    for i in range(depth):
        need_ctx = i < depth - 1
        kind, j = i % N_MIXERS, i // N_MIXERS
        g1 = norm_g[i, 0].reshape(1, d)
        g2 = norm_g[i, 1].reshape(1, d)

        if kind == 0:
            w_all = ga_w
            qcols = ga_sink.shape[1] * GA_HD
            kcols = (w_all.shape[2] - qcols) // 2
            segments = ((qcols, "norm_rope_128", 0, GA_HD ** -0.5 * LOG2E), (kcols, "norm_rope_128", 1, 1.0),
                        (kcols, "plain", 0, 1.0))
            cos_t, sin_t, gains = ax_cos, ax_sin, ga_qk_norm[j]
        elif kind == 1:
            w_all = rt_w
            vcols = (w_all.shape[2] - 2 * d) // 3
            segments = ((d, "rope_256", 0, rt_dk ** -0.5), (d, "rope_256", 0, 1.0),
                        (vcols, "plain", 0, 1.0), (2 * vcols, "silu", 0, 1.0))
            cos_t, sin_t, gains = ln_cos, ln_sin, ones2
        else:
            w_all = df_w
            qcols = w_all.shape[2] // 3
            segments = ((qcols, "norm_rope_128", 0, DF_HD ** -0.5 * LOG2E), (qcols, "norm_rope_128", 1, 1.0),
                        (qcols, "plain", 0, 1.0))
            cos_t, sin_t, gains = ax_cos, ax_sin, df_qk_norm[j]

        proj = {"lat": _proj(h_lat, g1, mods, i, lat, w_all, j, segments, cos_t, sin_t, gains),
                "ctx": _proj(h_ctx, g1, mods, i, cst, w_all, j, segments, cos_t, sin_t, gains)}

        if kind == 0:
            y_lat, y_ctx = _ga_attention(proj["lat"], proj["ctx"], ga_sink[j], batch, seq, n_ctx, need_ctx)
            h_lat = _oproj(y_lat, ga_o, j, h_lat, mods, i, lat)
            if need_ctx:
                h_ctx = _oproj(y_ctx, ga_o, j, h_ctx, mods, i, cst)
        elif kind == 1:
            log_gamma = jax.nn.log_sigmoid(rt_decay[j].astype(F32))
            y_lat, y_ctx = _retention(proj["lat"], proj["ctx"], log_gamma, rt_gn[j], batch, seq, n_ctx)
            h_lat = _oproj_rt(y_lat[0], y_lat[1], proj["lat"], rt_o, j, h_lat, mods, i, lat)
            if need_ctx:
                h_ctx = _oproj_rt(y_ctx[0], y_ctx[1], proj["ctx"], rt_o, j, h_ctx, mods, i, cst)
        else:
            lambda_init = 0.8 - 0.6 * math.exp(-0.3 * i)
            y_lat, y_ctx = _df_attention(proj["lat"], proj["ctx"], df_lambda[j], df_subln[j], batch, seq, n_ctx,
                                         lambda_init, need_ctx)
            h_lat = _oproj(y_lat, df_o, j, h_lat, mods, i, lat)
            if need_ctx:
                h_ctx = _oproj(y_ctx, df_o, j, h_ctx, mods, i, cst)

        h_lat = _ffn(h_lat, g2, mods, i, lat, f_in, ffn_conv_w, f_cb, f_out, seq)
        if need_ctx:
            h_ctx = _ffn(h_ctx, g2, mods, i, cst, f_in, ffn_conv_w, f_cb, f_out, n_ctx)

    return h_lat.reshape(batch, seq, d)
```

```python
import functools
import math
from typing import Callable, NamedTuple

import jax
import jax.numpy as jnp
from jax import lax
from jax.experimental import pallas as pl
from jax.experimental.pallas import tpu as pltpu

F32 = jnp.float32
BF16 = jnp.bfloat16

EPS = 1e-6
NEG_INF = -1e30
LOG2E = math.log2(math.e)
ROPE_THETA = 10000.0
GRID_W = 64
WINDOW = 128
N_MIXERS = 3

GA_HD = 128
GA_GROUP = 4
RT_HEADS = 8
DF_HD = 128
CONV_W = 3

LANES = 128
MXU_COLS = 256
BF16_ROWS = 16
ONES_ROWS = BF16_ROWS
MOD_ROWS = 8
VMEM_LIMIT = 52 * 1024 * 1024


class _Stream(NamedTuple):
    group: int
    mod_row: Callable
    tab_block: Callable
    rope: bool


def _pick(n, candidates):
    for c in candidates:
        if n % c == 0:
            return c
    raise ValueError(f"no tile size in {candidates} divides {n}")


def _params(*sem):
    return pltpu.CompilerParams(dimension_semantics=sem, vmem_limit_bytes=VMEM_LIMIT)


def _silu(x):
    return x * (1.0 / (1.0 + jnp.exp(-x)))


def _dot(a, b):
    return jnp.dot(a, b, preferred_element_type=F32)


def _dot_nt(a, b):
    return lax.dot_general(a, b, (((1,), (1,)), ((), ())), preferred_element_type=F32)


def _dot_tn(a, b):
    return lax.dot_general(a, b, (((0,), (0,)), ((), ())), preferred_element_type=F32)


def _adaln_kernel(c_ref, w_ref, b_ref, o_ref):
    s = _silu(c_ref[...]).astype(BF16)
    o_ref[...] = _dot(s, w_ref[...].astype(BF16)) + b_ref[...]


def _adaln_table(cond, mod_w, mod_b):
    depth, d, _ = mod_w.shape
    bn = _pick(d, (1024, 512, 256, 128))
    nj = d // bn
    return pl.pallas_call(
        _adaln_kernel,
        grid=(depth, 6, nj),
        in_specs=[
            pl.BlockSpec((MOD_ROWS, d), lambda l, k, j: (0, 0)),
            pl.BlockSpec((None, d, bn), lambda l, k, j: (l, 0, k * nj + j)),
            pl.BlockSpec((None, 1, bn), lambda l, k, j: (l, 0, k * nj + j)),
        ],
        out_specs=pl.BlockSpec((None, None, MOD_ROWS, bn), lambda l, k, j: (l, k, 0, j)),
        out_shape=jax.ShapeDtypeStruct((depth, 6, MOD_ROWS, d), F32),
        compiler_params=_params("parallel", "parallel", "parallel"),
        name="adaln_table",
    )(cond, mod_w, mod_b.reshape(depth, 1, 6 * d))


def _mod_spec(layer, slot, row_fn, d):
    return pl.BlockSpec((None, None, None, 1, d), lambda i, j: (layer, slot, row_fn(i), 0, 0))


def _mod_spec_cols(layer, slot, row_fn, bn):
    return pl.BlockSpec((None, None, None, 1, bn), lambda i, j: (layer, slot, row_fn(i), 0, j))


def _norm_mod(x, g, shift, scale, mean_mat=None):
    if mean_mat is None:
        r = lax.rsqrt(jnp.mean(x * x, axis=-1, keepdims=True) + EPS)
    else:
        ms = _dot((x * x).astype(BF16), mean_mat)
        r = pltpu.repeat(lax.rsqrt(ms + EPS), x.shape[1] // LANES, 1)
    return x * r * (g * (1.0 + scale)) + shift


def _mean_mat(d):
    return jnp.full((d, LANES), 1.0 / d, BF16)


def _proj_kernel(h_ref, g_ref, sh_ref, sc_ref, mm_ref, w_ref, cos_ref, sin_ref, gain_ref, seg_ref, o_ref, a_scr, acc_scr,
                 *, plan, rope):
    j = pl.program_id(1)
    sub = acc_scr.shape[2]
    n_sub = o_ref.shape[1] // sub

    @pl.when(j == 0)
    def _():
        a_scr[...] = _norm_mod(h_ref[...], g_ref[...], sh_ref[...], sc_ref[...], mm_ref[...]).astype(BF16)

    def epilogue(s, kind, arg, scale):
        base = s * sub
        if kind == "plain":
            o_ref[:, base:base + sub] = acc_scr[s].astype(o_ref.dtype)
        elif kind == "silu":
            o_ref[:, base:base + sub] = _silu(acc_scr[s]).astype(o_ref.dtype)
        elif kind == "norm_rope_128":
            x = acc_scr[s]
            ms = _dot((x * x).astype(BF16), seg_ref[...])
            y = x * lax.rsqrt(ms + EPS) * (gain_ref[arg:arg + 1, :] * scale)
            for t in range(0, sub, LANES):
                yt = y[:, t:t + LANES]
                if rope:
                    yt = yt * cos_ref[...] + pltpu.roll(yt, LANES // 2, 1) * sin_ref[...]
                o_ref[:, base + t:base + t + LANES] = yt.astype(o_ref.dtype)
        else:
            assert kind == "rope_256" and sub == 2 * LANES
            x1 = acc_scr[s, :, 0:LANES] * scale
            x2 = acc_scr[s, :, LANES:sub] * scale
            if rope:
                c, sn = cos_ref[...], sin_ref[...]
                x1, x2 = x1 * c - x2 * sn, x1 * sn + x2 * c
            o_ref[:, base:base + LANES] = x1.astype(o_ref.dtype)
            o_ref[:, base + LANES:base + sub] = x2.astype(o_ref.dtype)

    for lo, hi, kinds in plan:
        @pl.when((j >= lo) & (j < hi))
        def _(kinds=kinds):
            a = a_scr[...]
            for s in range(n_sub):
                acc_scr[s] = _dot(a, w_ref[:, s * sub:(s + 1) * sub])
            for s, (kind, arg, scale) in enumerate(kinds):
                epilogue(s, kind, arg, scale)


def _proj(h, norm_g, mods, layer, st, w_all, w_idx, segments, cos_t, sin_t, gains):
    m, d = h.shape
    n = w_all.shape[2]
    bm = _pick(st.group, (1024, 512, 256))
    bn = _pick(n, (1024, 512, 256))
    sub = MXU_COLS
    n_sub = bn // sub
    per_sub = []
    for width, kind, arg, scale in segments:
        assert width % sub == 0
        per_sub += [(kind, arg, scale)] * (width // sub)
    assert len(per_sub) * sub == n
    tiles = [tuple(per_sub[t * n_sub:(t + 1) * n_sub]) for t in range(n // bn)]
    plan, lo = [], 0
    for t in range(1, len(tiles) + 1):
        if t == len(tiles) or tiles[t] != tiles[lo]:
            plan.append((lo, t, tiles[lo]))
            lo = t
    row_fn, tab_fn = st.mod_row, st.tab_block
    kernel = functools.partial(_proj_kernel, plan=tuple(plan), rope=st.rope)
    head = jnp.arange(sub) // LANES
    seg_mean = jnp.where(head[:, None] == head[None, :], 1.0 / LANES, 0.0).astype(BF16)
    gains = jnp.tile(gains, (1, sub // LANES))
    return pl.pallas_call(
        kernel,
        grid=(m // bm, n // bn),
        in_specs=[
            pl.BlockSpec((bm, d), lambda i, j: (i, 0)),
            pl.BlockSpec((1, d), lambda i, j: (0, 0)),
            _mod_spec(layer, 0, lambda i: row_fn(i, bm), d),
            _mod_spec(layer, 1, lambda i: row_fn(i, bm), d),
            pl.BlockSpec((d, LANES), lambda i, j: (0, 0)),
            pl.BlockSpec((None, d, bn), lambda i, j: (w_idx, 0, j)),
            pl.BlockSpec((bm, LANES), lambda i, j: (tab_fn(i, bm), 0)),
            pl.BlockSpec((bm, LANES), lambda i, j: (tab_fn(i, bm), 0)),
            pl.BlockSpec(gains.shape, lambda i, j: (0, 0)),
            pl.BlockSpec((sub, sub), lambda i, j: (0, 0)),
        ],
        out_specs=pl.BlockSpec((bm, bn), lambda i, j: (i, j)),
        out_shape=jax.ShapeDtypeStruct((m, n), BF16),
        scratch_shapes=[pltpu.VMEM((bm, d), BF16), pltpu.VMEM((n_sub, bm, sub), F32)],
        compiler_params=_params("parallel", "arbitrary"),
        name="proj",
    )(h, norm_g, mods, mods, _mean_mat(d), w_all, cos_t, sin_t, gains, seg_mean)


def _oproj_kernel(y_ref, w_ref, h_ref, gate_ref, o_ref):
    o_ref[...] = h_ref[...] + gate_ref[...] * _dot(y_ref[...], w_ref[...])


def _oproj_rt_kernel(yf_ref, yb_ref, sf_ref, sb_ref, w_ref, h_ref, gate_ref, o_ref):
    c = pl.program_id(1)

    @pl.when(c == 0)
    def _():
        o_ref[...] = jnp.zeros_like(o_ref)

    ck = w_ref.shape[0]
    sub = min(ck, MXU_COLS)
    part = None
    for s in range(0, ck, sub):
        cols = slice(s, s + sub)
        z = sf_ref[:, cols] * yf_ref[:, cols] + sb_ref[:, cols] * yb_ref[:, cols]
        d = _dot(z, w_ref[cols, :])
        part = d if part is None else part + d
    o_ref[...] += part

    @pl.when(c == pl.num_programs(1) - 1)
    def _():
        o_ref[...] = h_ref[...] + gate_ref[...] * o_ref[...]


def _oproj(y, w_all, w_idx, h, mods, layer, st):
    m, k = y.shape
    d = w_all.shape[2]
    bm = _pick(st.group, (1024, 512, 256))
    row_fn = st.mod_row
    bn = _pick(d, (1024, 512, 256, 128))
    return pl.pallas_call(
        _oproj_kernel,
        grid=(m // bm, d // bn),
        in_specs=[
            pl.BlockSpec((bm, k), lambda i, j: (i, 0)),
            pl.BlockSpec((None, k, bn), lambda i, j: (w_idx, 0, j)),
            pl.BlockSpec((bm, bn), lambda i, j: (i, j)),
            _mod_spec_cols(layer, 2, lambda i: row_fn(i, bm), bn),
        ],
        out_specs=pl.BlockSpec((bm, bn), lambda i, j: (i, j)),
        out_shape=jax.ShapeDtypeStruct((m, d), F32),
        input_output_aliases={2: 0},
        compiler_params=_params("parallel", "arbitrary"),
        name="oproj",
    )(y, w_all, h, mods)


def _oproj_rt(yf, yb, proj, w_all, w_idx, h, mods, layer, st):
    m, v = yf.shape
    d = w_all.shape[2]
    bm = _pick(st.group, (512, 256))
    row_fn = st.mod_row
    ck = _pick(v, (1024, 512, 256, 128))
    nc = v // ck
    gcol = (proj.shape[1] // v - 2) * nc
    return pl.pallas_call(
        _oproj_rt_kernel,
        grid=(m // bm, nc),
        in_specs=[
            pl.BlockSpec((bm, ck), lambda i, c: (i, c)),
            pl.BlockSpec((bm, ck), lambda i, c: (i, c)),
            pl.BlockSpec((bm, ck), lambda i, c: (i, gcol + c)),
            pl.BlockSpec((bm, ck), lambda i, c: (i, gcol + nc + c)),
            pl.BlockSpec((None, ck, d), lambda i, c: (w_idx, c, 0)),
            pl.BlockSpec((bm, d), lambda i, c: (i, 0)),
            _mod_spec(layer, 2, lambda i: row_fn(i, bm), d),
        ],
        out_specs=pl.BlockSpec((bm, d), lambda i, c: (i, 0)),
        out_shape=jax.ShapeDtypeStruct((m, d), F32),
        input_output_aliases={5: 0},
        compiler_params=_params("parallel", "arbitrary"),
        name="oproj_rt",
    )(yf, yb, proj, proj, w_all, h, mods)


def _ffn_kernel(h_ref, hp_ref, hn_ref, g_ref, sh_ref, sc_ref, gate_ref, wa_ref, wb_ref,
                cwa_lo, cwb_lo, cba_lo, cbb_lo, wo_lo, cwa_hi, cwb_hi, cba_hi, cbb_hi, wo_hi,
                cwa_top, cwb_top, cba_top, cbb_top, wo_top,
                o_ref, a_scr, u_even, u_odd_a, u_odd_b, *, bm, seq_len, steps):
    i = pl.program_id(0)
    j = pl.program_id(1)
    halo = BF16_ROWS
    sub = u_even.shape[2]
    lo = (cwa_lo, cwb_lo, cba_lo, cbb_lo, wo_lo)
    hi = (cwa_hi, cwb_hi, cba_hi, cbb_hi, wo_hi)

    def up(half, u):
        cols = slice(half * sub, (half + 1) * sub)
        a = a_scr[...]
        u[0] = _dot(a, wa_ref[:, cols])
        u[1] = _dot(a, wb_ref[:, cols])

    def gate(u, cwa, cwb, cba, cbb, wo):
        def conv(idx, cw_ref, cb_ref):
            prev = u[idx, halo - 1:halo - 1 + bm, :]
            cur = u[idx, halo:halo + bm, :]
            nxt = u[idx, halo + 1:halo + 1 + bm, :]
            return cb_ref[...] + prev * cw_ref[0:1, :] + cur * cw_ref[1:2, :] + nxt * cw_ref[2:3, :]

        return (_silu(conv(0, cwa, cba)) * conv(1, cwb, cbb)).astype(BF16)

    def down(u, *params):
        o_ref[...] += _dot(gate(u, *params), params[-1][...])

    @pl.when(j == 0)
    def _():
        def nm(x):
            return _norm_mod(x, g_ref[...], sh_ref[...], sc_ref[...])

        keep_prev = jnp.where((i * bm) % seq_len != 0, 1.0, 0.0)
        keep_next = jnp.where(((i + 1) * bm) % seq_len != 0, 1.0, 0.0)
        a_scr[0:halo, :] = (nm(hp_ref[...]) * keep_prev).astype(BF16)
        a_scr[halo:halo + bm, :] = nm(h_ref[...]).astype(BF16)
        a_scr[halo + bm:, :] = (nm(hn_ref[...]) * keep_next).astype(BF16)
        o_ref[...] = jnp.zeros_like(o_ref)
        up(0, u_even)
        up(1, u_odd_a)
        down(u_even, *hi)

    def middle(u_new, u_old):
        up(0, u_even)
        up(1, u_new)
        g = jnp.concatenate([gate(u_old, *lo), gate(u_even, *hi)], axis=1)
        w = jnp.concatenate([wo_lo[...], wo_hi[...]], axis=0)
        o_ref[...] += _dot(g, w)

    inner = (j > 0) & (j < steps - 1)
    pl.when(inner & (j % 2 == 1))(lambda: middle(u_odd_b, u_odd_a))
    pl.when(inner & (j % 2 == 0))(lambda: middle(u_odd_a, u_odd_b))

    @pl.when(j == steps - 1)
    def _():
        u_new, u_old = (u_odd_a, u_odd_b) if (steps - 1) % 2 == 0 else (u_odd_b, u_odd_a)
        middle(u_new, u_old)
        down(u_new, cwa_top, cwb_top, cba_top, cbb_top, wo_top)
        o_ref[...] = h_ref[...] + gate_ref[...] * o_ref[...]


def _ffn(h, norm_g, mods, layer, st, w_in, conv_w, conv_b, w_out, seq_len):
    m, d = h.shape
    dff = w_out.shape[1]
    bm = _pick(seq_len, (512, 256))
    sub = MXU_COLS
    nsc = dff // sub
    assert dff % (2 * sub) == 0
    steps = nsc // 2
    assert steps >= 2
    halo = BF16_ROWS
    hb = bm // halo
    last = m // halo - 1
    kernel = functools.partial(_ffn_kernel, bm=bm, seq_len=seq_len, steps=steps)
    mrow = lambda i: st.mod_row(i, bm)
    chunk = (lambda j: jnp.maximum(2 * j - 1, 0), lambda j: 2 * j, lambda j: nsc - 1)

    def down_specs(k):
        return [
            pl.BlockSpec((None, CONV_W, sub), lambda i, j: (layer, 0, k(j))),
            pl.BlockSpec((None, CONV_W, sub), lambda i, j: (layer, 0, nsc + k(j))),
            pl.BlockSpec((None, 1, sub), lambda i, j: (layer, 0, k(j))),
            pl.BlockSpec((None, 1, sub), lambda i, j: (layer, 0, nsc + k(j))),
            pl.BlockSpec((None, sub, d), lambda i, j: (layer, k(j), 0)),
        ]

    down_args = [conv_w, conv_w, conv_b, conv_b, w_out]
    return pl.pallas_call(
        kernel,
        grid=(m // bm, steps),
        in_specs=[
            pl.BlockSpec((bm, d), lambda i, j: (i, 0)),
            pl.BlockSpec((halo, d), lambda i, j: (jnp.maximum(i * hb - 1, 0), 0)),
            pl.BlockSpec((halo, d), lambda i, j: (jnp.minimum((i + 1) * hb, last), 0)),
            pl.BlockSpec((1, d), lambda i, j: (0, 0)),
            _mod_spec(layer, 3, mrow, d),
            _mod_spec(layer, 4, mrow, d),
            _mod_spec(layer, 5, mrow, d),
            pl.BlockSpec((None, d, 2 * sub), lambda i, j: (layer, 0, j)),
            pl.BlockSpec((None, d, 2 * sub), lambda i, j: (layer, 0, steps + j)),
        ] + down_specs(chunk[0]) + down_specs(chunk[1]) + down_specs(chunk[2]),
        out_specs=pl.BlockSpec((bm, d), lambda i, j: (i, 0)),
        out_shape=jax.ShapeDtypeStruct((m, d), F32),
        scratch_shapes=[pltpu.VMEM((bm + 2 * halo, d), BF16),
                        pltpu.VMEM((2, bm + 2 * halo, sub), F32),
                        pltpu.VMEM((2, bm + 2 * halo, sub), F32),
                        pltpu.VMEM((2, bm + 2 * halo, sub), F32)],
        compiler_params=_params("parallel", "arbitrary"),
        name="conv_ffn",
    )(h, h, h, norm_g, mods, mods, mods, w_in, w_in, *down_args, *down_args, *down_args)


def _ga_kernel(sink_ref, q_ref, kvc_ref, *rest, seq, n_kv, local):
    if local:
        kvo_ref, kvp_ref, kvn_ref, o_ref, s_a, s_b = rest
    else:
        o_ref, s_a, s_b = rest
    hd = GA_HD
    kcols = n_kv * hd
    bq = q_ref.shape[0]
    sq = s_a.shape[1]
    n_ctx = kvc_ref.shape[0]
    bufs = (s_a, s_b)

    def band(u):
        if not local:
            return [], 0
        lo, hi = u * sq - WINDOW, (u + 1) * sq + WINDOW
        pieces = []
        if lo < 0:
            pieces.append((kvp_ref, WINDOW + lo, -lo))
        pieces.append((kvo_ref, max(lo, 0), min(hi, bq) - max(lo, 0)))
        if hi > bq:
            pieces.append((kvn_ref, 0, hi - bq))
        return pieces, lo

    def bias(u):
        pieces, lo = band(u)
        n_loc = sum(p[2] for p in pieces)
        kl = lax.broadcasted_iota(jnp.int32, (n_loc, sq), 0) + lo
        ql = lax.broadcasted_iota(jnp.int32, (n_loc, sq), 1) + u * sq
        pos = pl.program_id(1) * bq + kl
        ok = (jnp.abs(ql - kl) <= WINDOW) & (pos >= 0) & (pos < seq)
        return jnp.concatenate([jnp.where(ok, 0.0, NEG_INF).astype(F32), jnp.zeros((n_ctx, sq), F32)], axis=0)

    def keys_values(u, n):
        pieces, _ = band(u)
        ks, vs = slice(n * hd, (n + 1) * hd), slice(kcols + n * hd, kcols + (n + 1) * hd)
        k_all = jnp.concatenate([r[a:a + c, ks] for r, a, c in pieces] + [kvc_ref[:, ks]], axis=0)
        v_all = jnp.concatenate([r[a:a + c, vs] for r, a, c in pieces] + [kvc_ref[:, vs]], axis=0)
        v_t = jnp.concatenate([v_all.astype(F32).T.astype(BF16), jnp.ones((ONES_ROWS, v_all.shape[0]), BF16)], axis=0)
        return k_all, v_t

    chains = [(u, n, g) for u in range(bq // sq) for n in range(n_kv) for g in range(GA_GROUP)]
    cache = {}

    def operands(u, n):
        if (u, n) not in cache:
            cache.clear()
            cache[(u, n)] = keys_values(u, n)
        return cache[(u, n)]

    biases = {}

    def scores(idx):
        u, n, g = chains[idx]
        head = n * GA_GROUP + g
        k_all, _ = operands(u, n)
        s = _dot_nt(k_all, q_ref[u * sq:(u + 1) * sq, head * hd:(head + 1) * hd])
        if local:
            if u not in biases:
                biases[u] = bias(u)
            s = s + biases[u]
        bufs[idx % 2][...] = s

    scores(0)
    for idx, (u, n, g) in enumerate(chains):
        _, v_t = operands(u, n)
        if idx + 1 < len(chains):
            scores(idx + 1)
        buf = bufs[idx % 2]
        head = n * GA_GROUP + g
        sink = sink_ref[head] * LOG2E
        m = jnp.maximum(jnp.max(buf[...], axis=0, keepdims=True), sink)
        p = jnp.exp2((buf[...] - m).astype(BF16))
        ov = _dot(v_t, p)
        o_t = ov[0:hd, :] / (ov[hd:hd + 1, :] + jnp.exp2(sink - m))
        o_ref[u * sq:(u + 1) * sq, head * hd:(head + 1) * hd] = o_t.T.astype(o_ref.dtype)


def _ga_attention(qkv_lat, qkv_ctx, sink, batch, seq, n_ctx, need_ctx):
    n_heads = sink.shape[0]
    qcols = n_heads * GA_HD
    n_kv = n_heads // GA_GROUP
    kvw = 2 * n_kv * GA_HD
    kvblk = qcols // kvw
    assert qcols % kvw == 0
    bq = _pick(seq, (512, 256, 128))
    sq = min(bq, MXU_COLS)
    nq = seq // bq
    wb = bq // WINDOW
    last_w = batch * seq // WINDOW - 1
    smem = pl.BlockSpec(memory_space=pltpu.SMEM)
    lat = pl.pallas_call(
        functools.partial(_ga_kernel, seq=seq, n_kv=n_kv, local=True),
        grid=(batch, nq),
        in_specs=[
            smem,
            pl.BlockSpec((bq, qcols), lambda b, i: (b * nq + i, 0)),
            pl.BlockSpec((n_ctx, kvw), lambda b, i: (b, kvblk)),
            pl.BlockSpec((bq, kvw), lambda b, i: (b * nq + i, kvblk)),
            pl.BlockSpec((WINDOW, kvw), lambda b, i: (jnp.maximum((b * nq + i) * wb - 1, 0), kvblk)),
            pl.BlockSpec((WINDOW, kvw), lambda b, i: (jnp.minimum((b * nq + i + 1) * wb, last_w), kvblk)),
        ],
        out_specs=pl.BlockSpec((bq, qcols), lambda b, i: (b * nq + i, 0)),
        out_shape=jax.ShapeDtypeStruct((batch * seq, qcols), BF16),
        scratch_shapes=[pltpu.VMEM((sq + 2 * WINDOW + n_ctx, sq), F32)] * 2,
        compiler_params=_params("parallel", "parallel"),
        name="ga_attention_latent",
    )(sink, qkv_lat, qkv_ctx, qkv_lat, qkv_lat, qkv_lat)
    ctx = None
    if need_ctx:
        cq = min(n_ctx, MXU_COLS)
        ctx = pl.pallas_call(
            functools.partial(_ga_kernel, seq=n_ctx, n_kv=n_kv, local=False),
            grid=(batch,),
            in_specs=[
                smem,
                pl.BlockSpec((n_ctx, qcols), lambda b: (b, 0)),
                pl.BlockSpec((n_ctx, kvw), lambda b: (b, kvblk)),
            ],
            out_specs=pl.BlockSpec((n_ctx, qcols), lambda b: (b, 0)),
            out_shape=jax.ShapeDtypeStruct((batch * n_ctx, qcols), BF16),
            scratch_shapes=[pltpu.VMEM((n_ctx, cq), F32)] * 2,
            compiler_params=_params("parallel"),
            name="ga_attention_context",
        )(sink, qkv_ctx, qkv_ctx)
    return lat, ctx


def _df_kernel(lam_ref, g_ref, q_ref, *rest, lambda_init, chunks, first_q_axis):
    n_seg = len(chunks)
    kv_refs = rest[:2 * n_seg]
    o_ref = rest[2 * n_seg]
    vt_scrs = rest[2 * n_seg + 1:2 * n_seg + 1 + n_seg]
    acc_scr, s_a, s_b = rest[-3:]
    hd = DF_HD
    bq = q_ref.shape[0]
    lam = lam_ref[...]
    lmbda = (jnp.exp(jnp.sum(lam[0:1] * lam[1:2], axis=-1, keepdims=True))
             - jnp.exp(jnp.sum(lam[2:3] * lam[3:4], axis=-1, keepdims=True)) + lambda_init)

    def transpose_values():
        for seg, ck in enumerate(chunks):
            v_ref, vt = kv_refs[2 * seg + 1], vt_scrs[seg]
            for c in range(v_ref.shape[0] // ck):
                vt[c] = v_ref[c * ck:(c + 1) * ck, :].astype(F32).T.astype(BF16)

    if first_q_axis is None:
        transpose_values()
    else:
        pl.when(pl.program_id(first_q_axis) == 0)(transpose_values)

    qs = (q_ref[:, 0:hd], q_ref[:, hd:2 * hd])
    acc_scr[...] = jnp.zeros_like(acc_scr)
    stat0 = (jnp.full((1, bq), NEG_INF, F32), jnp.zeros((1, bq), F32))
    carry = (stat0, stat0)

    def scores(k_ref, c, ck, dst):
        kblk = k_ref[c * ck:(c + 1) * ck, :]
        for r in range(2):
            dst[r] = _dot_nt(kblk[:, r * hd:(r + 1) * hd], qs[r])

    def update(src, v_t, carry):
        new = []
        for r in range(2):
            m_old, l_old = carry[r]
            m_new = jnp.maximum(m_old, jnp.max(src[r], axis=0, keepdims=True))
            alpha = jnp.exp2(m_old - m_new)
            p = jnp.exp2(src[r] - m_new)
            l_new = alpha * l_old + jnp.sum(p, axis=0, keepdims=True)
            acc_scr[r] = alpha * acc_scr[r] + _dot(v_t, p.astype(BF16))
            new.append((m_new, l_new))
        return tuple(new)

    items = [(kv_refs[2 * seg], c, chunks[seg], vt_scrs[seg])
             for seg in range(n_seg) for c in range(kv_refs[2 * seg].shape[0] // chunks[seg])]
    bufs = (s_a, s_b)

    def view(buf, ck):
        return buf if ck == buf.shape[1] else buf.at[:, 0:ck, :]

    scores(items[0][0], items[0][1], items[0][2], view(bufs[0], items[0][2]))
    for idx, (_, c, ck, vt) in enumerate(items):
        if idx + 1 < len(items):
            k_next, c_next, ck_next, _ = items[idx + 1]
            scores(k_next, c_next, ck_next, view(bufs[(idx + 1) % 2], ck_next))
        carry = update(view(bufs[idx % 2], ck), vt[c], carry)
    (_, l0), (_, l1) = carry
    o = (acc_scr[0] / l0 - lmbda * (acc_scr[1] / l1)).T
    ms = jnp.mean(o * o, axis=-1, keepdims=True)
    y = o * lax.rsqrt(ms + EPS) * g_ref[...] * (1.0 - lambda_init)
    o_ref[...] = y.astype(o_ref.dtype)


def _df_attention(qkv_lat, qkv_ctx, lam, subln_g, batch, seq, n_ctx, lambda_init, need_ctx):
    hw = 2 * DF_HD
    n_heads = qkv_lat.shape[1] // (3 * hw)
    bq = _pick(seq, (512, 256, 128))
    nq = seq // bq
    ck_lat = _pick(seq, (512, 256, 128))
    ck_ctx = _pick(n_ctx, (512, 256, 128))
    g2 = subln_g.reshape(1, hw)
    const = lambda shape: pl.BlockSpec(shape, lambda *_: (0,) * len(shape))
    vt_ctx = pltpu.VMEM((n_ctx // ck_ctx, hw, ck_ctx), BF16)
    vt_lat = pltpu.VMEM((seq // ck_lat, hw, ck_lat), BF16)
    lat = pl.pallas_call(
        functools.partial(_df_kernel, lambda_init=lambda_init, chunks=(ck_ctx, ck_lat), first_q_axis=2),
        grid=(batch, n_heads, nq),
        in_specs=[
            const(lam.shape),
            const((1, hw)),
            pl.BlockSpec((bq, hw), lambda b, h, i: (b * nq + i, h)),
            pl.BlockSpec((n_ctx, hw), lambda b, h, i: (b, n_heads + h)),
            pl.BlockSpec((n_ctx, hw), lambda b, h, i: (b, 2 * n_heads + h)),
            pl.BlockSpec((seq, hw), lambda b, h, i: (b, n_heads + h)),
            pl.BlockSpec((seq, hw), lambda b, h, i: (b, 2 * n_heads + h)),
        ],
        out_specs=pl.BlockSpec((bq, hw), lambda b, h, i: (b * nq + i, h)),
        out_shape=jax.ShapeDtypeStruct((batch * seq, n_heads * hw), BF16),
        scratch_shapes=[vt_ctx, vt_lat, pltpu.VMEM((2, hw, bq), F32)]
        + [pltpu.VMEM((2, max(ck_lat, ck_ctx), bq), F32)] * 2,
        compiler_params=_params("parallel", "parallel", "arbitrary"),
        name="df_attention_latent",
    )(lam, g2, qkv_lat, qkv_ctx, qkv_ctx, qkv_lat, qkv_lat)
    ctx = None
    if need_ctx:
        ctx = pl.pallas_call(
            functools.partial(_df_kernel, lambda_init=lambda_init, chunks=(ck_ctx,), first_q_axis=None),
            grid=(batch, n_heads),
            in_specs=[
                const(lam.shape),
                const((1, hw)),
                pl.BlockSpec((n_ctx, hw), lambda b, h: (b, h)),
                pl.BlockSpec((n_ctx, hw), lambda b, h: (b, n_heads + h)),
                pl.BlockSpec((n_ctx, hw), lambda b, h: (b, 2 * n_heads + h)),
            ],
            out_specs=pl.BlockSpec((n_ctx, hw), lambda b, h: (b, h)),
            out_shape=jax.ShapeDtypeStruct((batch * n_ctx, n_heads * hw), BF16),
            scratch_shapes=[vt_ctx, pltpu.VMEM((2, hw, n_ctx), F32)]
            + [pltpu.VMEM((2, ck_ctx, n_ctx), F32)] * 2,
            compiler_params=_params("parallel", "parallel"),
            name="df_attention_context",
        )(lam, g2, qkv_ctx, qkv_ctx, qkv_ctx)
    return lat, ctx


def _rt_kernel(lg_ref, gn_ref, *refs, n_heads, chunk, ctx_steps):
    ctx_in, lat_in = refs[0:6], refs[6:12]
    ctx_out, lat_out = refs[12:14], refs[14:16]
    st_scr, dec_scr = refs[16], refs[17]
    head0 = pl.program_id(1) * n_heads
    s_idx = pl.program_id(2)
    dk = ctx_in[0].shape[1] // n_heads
    dv = ctx_in[2].shape[1] // n_heads

    @pl.when(s_idx == 0)
    def _():
        st_scr[...] = jnp.zeros_like(st_scr)
        row = lax.broadcasted_iota(jnp.int32, (chunk, chunk), 0).astype(F32)
        col = lax.broadcasted_iota(jnp.int32, (chunk, chunk), 1).astype(F32)
        for d, rel in enumerate((row - col, col - row)):
            for h in range(n_heads):
                lg = lg_ref[d, head0 + h]
                dec_scr[d, h] = jnp.where(rel >= 0, jnp.exp(lg * jnp.maximum(rel, 0.0)), 0.0)

    def step(ins, outs):
        qf_ref, kf_ref, vf_ref, qb_ref, kb_ref, vb_ref = ins
        of_ref, ob_ref = outs
        pos = lax.broadcasted_iota(jnp.int32, (chunk, 1), 0).astype(F32)
        dirs = (
            (qf_ref, kf_ref, vf_ref, of_ref, pos + 1.0, chunk - 1.0 - pos),
            (qb_ref, kb_ref, vb_ref, ob_ref, chunk - pos, pos),
        )
        for d, (q_ref, k_ref, v_ref, o_ref, q_pow, k_pow) in enumerate(dirs):
            for h in range(n_heads):
                lg = lg_ref[d, head0 + h]
                decay = dec_scr[d, h]
                q_decay = jnp.exp(lg * q_pow)
                k_decay = jnp.exp(lg * k_pow)
                chunk_decay = jnp.exp(lg * chunk)
                q = q_ref[:, h * dk:(h + 1) * dk]
                k = k_ref[:, h * dk:(h + 1) * dk]
                v = v_ref[:, h * dv:(h + 1) * dv]
                state = st_scr[d, h]
                inner = _dot_nt(q, k) * decay
                o = _dot(inner.astype(BF16), v) + _dot(q, state.astype(BF16)) * q_decay
                kd = (k.astype(F32) * k_decay).astype(BF16)
                st_scr[d, h] = state * chunk_decay + _dot_tn(kd, v)
                mu = jnp.mean(o, axis=-1, keepdims=True)
                dev = o - mu
                var = jnp.mean(dev * dev, axis=-1, keepdims=True)
                y = dev * lax.rsqrt(var + EPS) * gn_ref[d:d + 1, h * dv:(h + 1) * dv]
                o_ref[:, h * dv:(h + 1) * dv] = y.astype(o_ref.dtype)

    @pl.when(s_idx < ctx_steps)
    def _():
        step(ctx_in, ctx_out)

    @pl.when(s_idx >= ctx_steps)
    def _():
        step(lat_in, lat_out)


def _retention(proj_lat, proj_ctx, log_gamma, gn_g, batch, seq, n_ctx):
    n_heads = log_gamma.shape[1]
    qk = proj_lat.shape[1] // 8
    v = 2 * qk
    chunk = _pick(math.gcd(seq, n_ctx), (256, 128))
    ncc, nlc = n_ctx // chunk, seq // chunk
    dk, dv = qk // n_heads, v // n_heads
    groups = 2
    hpg = n_heads // groups
    qkw, vw = hpg * dk, hpg * dv

    def ctx_f(b, s):
        return b * ncc + jnp.minimum(s, ncc - 1)

    def ctx_b(b, s):
        return b * ncc + (ncc - 1 - jnp.minimum(s, ncc - 1))

    def lat_f(b, s):
        return b * nlc + jnp.maximum(s - ncc, 0)

    def lat_b(b, s):
        return b * nlc + (nlc - 1 - jnp.maximum(s - ncc, 0))

    def in_specs(row):
        return [
            pl.BlockSpec((chunk, qkw), lambda b, g, s: (row(b, s), g)),
            pl.BlockSpec((chunk, qkw), lambda b, g, s: (row(b, s), groups + g)),
            pl.BlockSpec((chunk, vw), lambda b, g, s: (row(b, s), groups + g)),
        ]

    def out_spec(row):
        return pl.BlockSpec((chunk, vw), lambda b, g, s: (row(b, s), g))

    out_ctx = jax.ShapeDtypeStruct((batch * n_ctx, v), BF16)
    out_lat = jax.ShapeDtypeStruct((batch * seq, v), BF16)
    yf_ctx, yb_ctx, yf_lat, yb_lat = pl.pallas_call(
        functools.partial(_rt_kernel, n_heads=hpg, chunk=chunk, ctx_steps=ncc),
        grid=(batch, groups, ncc + nlc),
        in_specs=[pl.BlockSpec(memory_space=pltpu.SMEM), pl.BlockSpec((2, vw), lambda b, g, s: (0, g))]
        + in_specs(ctx_f) + in_specs(ctx_b) + in_specs(lat_f) + in_specs(lat_b),
        out_specs=[out_spec(ctx_f), out_spec(ctx_b), out_spec(lat_f), out_spec(lat_b)],
        out_shape=[out_ctx, out_ctx, out_lat, out_lat],
        scratch_shapes=[pltpu.VMEM((2, hpg, dk, dv), F32), pltpu.VMEM((2, hpg, chunk, chunk), F32)],
        compiler_params=_params("parallel", "parallel", "arbitrary"),
        name="retention",
    )(log_gamma, gn_g, *([proj_ctx] * 6), *([proj_lat] * 6))
    return (yf_lat, yb_lat), (yf_ctx, yb_ctx)


def _axial_tables(rows_count, head_dim):
    rows = jnp.repeat(jnp.arange(rows_count), GRID_W).astype(F32)
    cols = jnp.tile(jnp.arange(GRID_W), rows_count).astype(F32)
    n_freq = head_dim // 4
    inv = ROPE_THETA ** (-jnp.arange(n_freq, dtype=F32) / n_freq)
    ang = jnp.concatenate([rows[:, None] * inv, cols[:, None] * inv], -1)
    cos, sin = jnp.cos(ang), jnp.sin(ang)
    return jnp.concatenate([cos, cos], -1), jnp.concatenate([-sin, sin], -1)


def _linear_tables(n_tokens, head_dim):
    half = head_dim // 2
    inv = ROPE_THETA ** (-jnp.arange(half, dtype=F32) / half)
    ang = jnp.arange(n_tokens, dtype=F32)[:, None] * inv
    return jnp.cos(ang), jnp.sin(ang)


def kernel(x, c, ctx, c_ctx, mod_w, mod_b, norm_g, ffn_w_in, ffn_conv_w, ffn_conv_b, ffn_w_out, ga_wqkv, ga_sink, ga_qk_norm, ga_wo, rt_w_in, rt_decay, rt_gn, rt_wo, df_wqkv, df_lambda, df_qk_norm, df_subln, df_wo):
    batch, seq, d = x.shape
    n_ctx = ctx.shape[1]
    depth = mod_w.shape[0]
    assert batch + 1 <= MOD_ROWS and seq % GRID_W == 0

    cond = jnp.zeros((MOD_ROWS, d), F32).at[0].set(c_ctx).at[1:batch + 1].set(c)
    mods = _adaln_table(cond, mod_w, mod_b).reshape(depth, 6, MOD_ROWS, 1, d)

    ax_cos, ax_sin = _axial_tables(seq // GRID_W, GA_HD)
    assert DF_HD == GA_HD
    rt_dk = d // RT_HEADS
    ln_cos, ln_sin = _linear_tables(seq, rt_dk)
    ones2 = jnp.ones((2, LANES), F32)

    h_lat = x.reshape(batch * seq, d)
    h_ctx = ctx.reshape(batch * n_ctx, d)

    lat = _Stream(seq, lambda i, bm: 1 + i // (seq // bm), lambda i, bm: i % (seq // bm), True)
    cst = _Stream(batch * n_ctx, lambda i, bm: 0, lambda i, bm: 0, False)

    ga_w, ga_o = ga_wqkv.astype(BF16), ga_wo.astype(BF16)
    rt_w, rt_o = rt_w_in.astype(BF16), rt_wo.astype(BF16)
    df_w, df_o = df_wqkv.astype(BF16), df_wo.astype(BF16)
    f_in, f_out = ffn_w_in.astype(BF16), ffn_w_out.astype(BF16)
    f_cb = ffn_conv_b.reshape(depth, 1, -1)

    for i in range(depth):
        need_ctx = i < depth - 1
        kind, j = i % N_MIXERS, i // N_MIXERS
        g1 = norm_g[i, 0].reshape(1, d)
        g2 = norm_g[i, 1].reshape(1, d)

        if kind == 0:
            w_all = ga_w
            qcols = ga_sink.shape[1] * GA_HD
            kcols = (w_all.shape[2] - qcols) // 2
            segments = ((qcols, "norm_rope_128", 0, GA_HD ** -0.5 * LOG2E), (kcols, "norm_rope_128", 1, 1.0),
                        (kcols, "plain", 0, 1.0))
            cos_t, sin_t, gains = ax_cos, ax_sin, ga_qk_norm[j]
        elif kind == 1:
            w_all = rt_w
            vcols = (w_all.shape[2] - 2 * d) // 3
            segments = ((d, "rope_256", 0, rt_dk ** -0.5), (d, "rope_256", 0, 1.0),
                        (vcols, "plain", 0, 1.0), (2 * vcols, "silu", 0, 1.0))
            cos_t, sin_t, gains = ln_cos, ln_sin, ones2
        else:
            w_all = df_w
            qcols = w_all.shape[2] // 3
            segments = ((qcols, "norm_rope_128", 0, DF_HD ** -0.5 * LOG2E), (qcols, "norm_rope_128", 1, 1.0),
                        (qcols, "plain", 0, 1.0))
            cos_t, sin_t, gains = ax_cos, ax_sin, df_qk_norm[j]

        proj = {"lat": _proj(h_lat, g1, mods, i, lat, w_all, j, segments, cos_t, sin_t, gains),
                "ctx": _proj(h_ctx, g1, mods, i, cst, w_all, j, segments, cos_t, sin_t, gains)}

        if kind == 0:
            y_lat, y_ctx = _ga_attention(proj["lat"], proj["ctx"], ga_sink[j], batch, seq, n_ctx, need_ctx)
            h_lat = _oproj(y_lat, ga_o, j, h_lat, mods, i, lat)
            if need_ctx:
                h_ctx = _oproj(y_ctx, ga_o, j, h_ctx, mods, i, cst)
        elif kind == 1:
            log_gamma = jax.nn.log_sigmoid(rt_decay[j].astype(F32))
            y_lat, y_ctx = _retention(proj["lat"], proj["ctx"], log_gamma, rt_gn[j], batch, seq, n_ctx)
            h_lat = _oproj_rt(y_lat[0], y_lat[1], proj["lat"], rt_o, j, h_lat, mods, i, lat)
            if need_ctx:
                h_ctx = _oproj_rt(y_ctx[0], y_ctx[1], proj["ctx"], rt_o, j, h_ctx, mods, i, cst)
        else:
            lambda_init = 0.8 - 0.6 * math.exp(-0.3 * i)
            y_lat, y_ctx = _df_attention(proj["lat"], proj["ctx"], df_lambda[j], df_subln[j], batch, seq, n_ctx,
                                         lambda_init, need_ctx)
            h_lat = _oproj(y_lat, df_o, j, h_lat, mods, i, lat)
            if need_ctx:
                h_ctx = _oproj(y_ctx, df_o, j, h_ctx, mods, i, cst)

        h_lat = _ffn(h_lat, g2, mods, i, lat, f_in, ffn_conv_w, f_cb, f_out, seq)
        if need_ctx:
            h_ctx = _ffn(h_ctx, g2, mods, i, cst, f_in, ffn_conv_w, f_cb, f_out, n_ctx)

    return h_lat.reshape(batch, seq, d)
```

```python
import functools
import math
from typing import Callable, NamedTuple

import jax
import jax.numpy as jnp
from jax import lax
from jax.experimental import pallas as pl
from jax.experimental.pallas import tpu as pltpu

F32 = jnp.float32
BF16 = jnp.bfloat16

EPS = 1e-6
NEG_INF = -1e30
LOG2E = math.log2(math.e)
ROPE_THETA = 10000.0
GRID_W = 64
WINDOW = 128
N_MIXERS = 3

GA_HD = 128
GA_GROUP = 4
RT_HEADS = 8
DF_HD = 128
CONV_W = 3

LANES = 128
MXU_COLS = 256
BF16_ROWS = 16
ONES_ROWS = BF16_ROWS
MOD_ROWS = 8
VMEM_LIMIT = 52 * 1024 * 1024


class _Stream(NamedTuple):
    group: int
    mod_row: Callable
    tab_block: Callable
    rope: bool


def _pick(n, candidates):
    for c in candidates:
        if n % c == 0:
            return c
    raise ValueError(f"no tile size in {candidates} divides {n}")


def _params(*sem):
    return pltpu.CompilerParams(dimension_semantics=sem, vmem_limit_bytes=VMEM_LIMIT)


def _silu(x):
    return x * (1.0 / (1.0 + jnp.exp(-x)))


def _dot(a, b):
    return jnp.dot(a, b, preferred_element_type=F32)


def _dot_nt(a, b):
    return lax.dot_general(a, b, (((1,), (1,)), ((), ())), preferred_element_type=F32)


def _dot_tn(a, b):
    return lax.dot_general(a, b, (((0,), (0,)), ((), ())), preferred_element_type=F32)


def _adaln_kernel(c_ref, w_ref, b_ref, o_ref):
    s = _silu(c_ref[...]).astype(BF16)
    o_ref[...] = _dot(s, w_ref[...].astype(BF16)) + b_ref[...]


def _adaln_table(cond, mod_w, mod_b):
    depth, d, _ = mod_w.shape
    bn = _pick(d, (1024, 512, 256, 128))
    nj = d // bn
    return pl.pallas_call(
        _adaln_kernel,
        grid=(depth, 6, nj),
        in_specs=[
            pl.BlockSpec((MOD_ROWS, d), lambda l, k, j: (0, 0)),
            pl.BlockSpec((None, d, bn), lambda l, k, j: (l, 0, k * nj + j)),
            pl.BlockSpec((None, 1, bn), lambda l, k, j: (l, 0, k * nj + j)),
        ],
        out_specs=pl.BlockSpec((None, None, MOD_ROWS, bn), lambda l, k, j: (l, k, 0, j)),
        out_shape=jax.ShapeDtypeStruct((depth, 6, MOD_ROWS, d), F32),
        compiler_params=_params("parallel", "parallel", "parallel"),
        name="adaln_table",
    )(cond, mod_w, mod_b.reshape(depth, 1, 6 * d))


def _mod_spec(layer, slot, row_fn, d):
    return pl.BlockSpec((None, None, None, 1, d), lambda i, j: (layer, slot, row_fn(i), 0, 0))


def _mod_spec_cols(layer, slot, row_fn, bn):
    return pl.BlockSpec((None, None, None, 1, bn), lambda i, j: (layer, slot, row_fn(i), 0, j))


def _norm_mod(x, g, shift, scale, mean_mat=None):
    if mean_mat is None:
        r = lax.rsqrt(jnp.mean(x * x, axis=-1, keepdims=True) + EPS)
    else:
        ms = _dot((x * x).astype(BF16), mean_mat)
        r = pltpu.repeat(lax.rsqrt(ms + EPS), x.shape[1] // LANES, 1)
    return x * r * (g * (1.0 + scale)) + shift


def _mean_mat(d):
    return jnp.full((d, LANES), 1.0 / d, BF16)


def _proj_kernel(h_ref, g_ref, sh_ref, sc_ref, mm_ref, w_ref, cos_ref, sin_ref, gain_ref, seg_ref, o_ref, a_scr, acc_scr,
                 *, plan, rope):
    j = pl.program_id(1)
    sub = acc_scr.shape[2]
    n_sub = o_ref.shape[1] // sub

    @pl.when(j == 0)
    def _():
        a_scr[...] = _norm_mod(h_ref[...], g_ref[...], sh_ref[...], sc_ref[...], mm_ref[...]).astype(BF16)

    def epilogue(s, kind, arg, scale):
        base = s * sub
        if kind == "plain":
            o_ref[:, base:base + sub] = acc_scr[s].astype(o_ref.dtype)
        elif kind == "silu":
            o_ref[:, base:base + sub] = _silu(acc_scr[s]).astype(o_ref.dtype)
        elif kind == "norm_rope_128":
            x = acc_scr[s]
            ms = _dot((x * x).astype(BF16), seg_ref[...])
            y = x * lax.rsqrt(ms + EPS) * (gain_ref[arg:arg + 1, :] * scale)
            for t in range(0, sub, LANES):
                yt = y[:, t:t + LANES]
                if rope:
                    yt = yt * cos_ref[...] + pltpu.roll(yt, LANES // 2, 1) * sin_ref[...]
                o_ref[:, base + t:base + t + LANES] = yt.astype(o_ref.dtype)
        else:
            assert kind == "rope_256" and sub == 2 * LANES
            x1 = acc_scr[s, :, 0:LANES] * scale
            x2 = acc_scr[s, :, LANES:sub] * scale
            if rope:
                c, sn = cos_ref[...], sin_ref[...]
                x1, x2 = x1 * c - x2 * sn, x1 * sn + x2 * c
            o_ref[:, base:base + LANES] = x1.astype(o_ref.dtype)
            o_ref[:, base + LANES:base + sub] = x2.astype(o_ref.dtype)

    for lo, hi, kinds in plan:
        @pl.when((j >= lo) & (j < hi))
        def _(kinds=kinds):
            a = a_scr[...]
            for s in range(n_sub):
                acc_scr[s] = _dot(a, w_ref[:, s * sub:(s + 1) * sub])
            for s, (kind, arg, scale) in enumerate(kinds):
                epilogue(s, kind, arg, scale)


def _proj(h, norm_g, mods, layer, st, w_all, w_idx, segments, cos_t, sin_t, gains):
    m, d = h.shape
    n = w_all.shape[2]
    bm = _pick(st.group, (1024, 512, 256))
    bn = _pick(n, (1024, 512, 256))
    sub = MXU_COLS
    n_sub = bn // sub
    per_sub = []
    for width, kind, arg, scale in segments:
        assert width % sub == 0
        per_sub += [(kind, arg, scale)] * (width // sub)
    assert len(per_sub) * sub == n
    tiles = [tuple(per_sub[t * n_sub:(t + 1) * n_sub]) for t in range(n // bn)]
    plan, lo = [], 0
    for t in range(1, len(tiles) + 1):
        if t == len(tiles) or tiles[t] != tiles[lo]:
            plan.append((lo, t, tiles[lo]))
            lo = t
    row_fn, tab_fn = st.mod_row, st.tab_block
    kernel = functools.partial(_proj_kernel, plan=tuple(plan), rope=st.rope)
    head = jnp.arange(sub) // LANES
    seg_mean = jnp.where(head[:, None] == head[None, :], 1.0 / LANES, 0.0).astype(BF16)
    gains = jnp.tile(gains, (1, sub // LANES))
    return pl.pallas_call(
        kernel,
        grid=(m // bm, n // bn),
        in_specs=[
            pl.BlockSpec((bm, d), lambda i, j: (i, 0)),
            pl.BlockSpec((1, d), lambda i, j: (0, 0)),
            _mod_spec(layer, 0, lambda i: row_fn(i, bm), d),
            _mod_spec(layer, 1, lambda i: row_fn(i, bm), d),
            pl.BlockSpec((d, LANES), lambda i, j: (0, 0)),
            pl.BlockSpec((None, d, bn), lambda i, j: (w_idx, 0, j)),
            pl.BlockSpec((bm, LANES), lambda i, j: (tab_fn(i, bm), 0)),
            pl.BlockSpec((bm, LANES), lambda i, j: (tab_fn(i, bm), 0)),
            pl.BlockSpec(gains.shape, lambda i, j: (0, 0)),
            pl.BlockSpec((sub, sub), lambda i, j: (0, 0)),
        ],
        out_specs=pl.BlockSpec((bm, bn), lambda i, j: (i, j)),
        out_shape=jax.ShapeDtypeStruct((m, n), BF16),
        scratch_shapes=[pltpu.VMEM((bm, d), BF16), pltpu.VMEM((n_sub, bm, sub), F32)],
        compiler_params=_params("parallel", "arbitrary"),
        name="proj",
    )(h, norm_g, mods, mods, _mean_mat(d), w_all, cos_t, sin_t, gains, seg_mean)


def _oproj_kernel(y_ref, w_ref, h_ref, gate_ref, o_ref):
    o_ref[...] = h_ref[...] + gate_ref[...] * _dot(y_ref[...], w_ref[...])


def _oproj_rt_kernel(yf_ref, yb_ref, sf_ref, sb_ref, w_ref, h_ref, gate_ref, o_ref):
    c = pl.program_id(1)

    @pl.when(c == 0)
    def _():
        o_ref[...] = jnp.zeros_like(o_ref)

    ck = w_ref.shape[0]
    sub = min(ck, MXU_COLS)
    part = None
    for s in range(0, ck, sub):
        cols = slice(s, s + sub)
        z = sf_ref[:, cols] * yf_ref[:, cols] + sb_ref[:, cols] * yb_ref[:, cols]
        d = _dot(z, w_ref[cols, :])
        part = d if part is None else part + d
    o_ref[...] += part

    @pl.when(c == pl.num_programs(1) - 1)
    def _():
        o_ref[...] = h_ref[...] + gate_ref[...] * o_ref[...]


def _oproj(y, w_all, w_idx, h, mods, layer, st):
    m, k = y.shape
    d = w_all.shape[2]
    bm = _pick(st.group, (1024, 512, 256))
    row_fn = st.mod_row
    bn = _pick(d, (1024, 512, 256, 128))
    return pl.pallas_call(
        _oproj_kernel,
        grid=(m // bm, d // bn),
        in_specs=[
            pl.BlockSpec((bm, k), lambda i, j: (i, 0)),
            pl.BlockSpec((None, k, bn), lambda i, j: (w_idx, 0, j)),
            pl.BlockSpec((bm, bn), lambda i, j: (i, j)),
            _mod_spec_cols(layer, 2, lambda i: row_fn(i, bm), bn),
        ],
        out_specs=pl.BlockSpec((bm, bn), lambda i, j: (i, j)),
        out_shape=jax.ShapeDtypeStruct((m, d), F32),
        input_output_aliases={2: 0},
        compiler_params=_params("parallel", "arbitrary"),
        name="oproj",
    )(y, w_all, h, mods)


def _oproj_rt(yf, yb, proj, w_all, w_idx, h, mods, layer, st):
    m, v = yf.shape
    d = w_all.shape[2]
    bm = _pick(st.group, (512, 256))
    row_fn = st.mod_row
    ck = _pick(v, (1024, 512, 256, 128))
    nc = v // ck
    gcol = (proj.shape[1] // v - 2) * nc
    return pl.pallas_call(
        _oproj_rt_kernel,
        grid=(m // bm, nc),
        in_specs=[
            pl.BlockSpec((bm, ck), lambda i, c: (i, c)),
            pl.BlockSpec((bm, ck), lambda i, c: (i, c)),
            pl.BlockSpec((bm, ck), lambda i, c: (i, gcol + c)),
            pl.BlockSpec((bm, ck), lambda i, c: (i, gcol + nc + c)),
            pl.BlockSpec((None, ck, d), lambda i, c: (w_idx, c, 0)),
            pl.BlockSpec((bm, d), lambda i, c: (i, 0)),
            _mod_spec(layer, 2, lambda i: row_fn(i, bm), d),
        ],
        out_specs=pl.BlockSpec((bm, d), lambda i, c: (i, 0)),
        out_shape=jax.ShapeDtypeStruct((m, d), F32),
        input_output_aliases={5: 0},
        compiler_params=_params("parallel", "arbitrary"),
        name="oproj_rt",
    )(yf, yb, proj, proj, w_all, h, mods)


def _ffn_kernel(h_ref, hp_ref, hn_ref, g_ref, sh_ref, sc_ref, gate_ref, wa_ref, wb_ref, cw_ref, cb_ref,
                wo_lo, wo_hi, wo_top, o_ref, a_scr, u_even, u_odd_a, u_odd_b, *, bm, seq_len, steps):
    i = pl.program_id(0)
    j = pl.program_id(1)
    halo = BF16_ROWS
    sub = u_even.shape[2]
    nsc = 2 * steps

    def up(half, u):
        cols = slice(half * sub, (half + 1) * sub)
        a = a_scr[...]
        u[0] = _dot(a, wa_ref[:, cols])
        u[1] = _dot(a, wb_ref[:, cols])

    def gate(u, k):
        def conv(idx, kk):
            cw = cw_ref[kk]
            prev = u[idx, halo - 1:halo - 1 + bm, :]
            cur = u[idx, halo:halo + bm, :]
            nxt = u[idx, halo + 1:halo + 1 + bm, :]
            return cb_ref[kk] + prev * cw[0:1, :] + cur * cw[1:2, :] + nxt * cw[2:3, :]

        return (_silu(conv(0, k)) * conv(1, nsc + k)).astype(BF16)

    @pl.when(j == 0)
    def _():
        def nm(x):
            return _norm_mod(x, g_ref[...], sh_ref[...], sc_ref[...])

        keep_prev = jnp.where((i * bm) % seq_len != 0, 1.0, 0.0)
        keep_next = jnp.where(((i + 1) * bm) % seq_len != 0, 1.0, 0.0)
        a_scr[0:halo, :] = (nm(hp_ref[...]) * keep_prev).astype(BF16)
        a_scr[halo:halo + bm, :] = nm(h_ref[...]).astype(BF16)
        a_scr[halo + bm:, :] = (nm(hn_ref[...]) * keep_next).astype(BF16)
        o_ref[...] = jnp.zeros_like(o_ref)
        up(0, u_even)
        up(1, u_odd_a)
        o_ref[...] += _dot(gate(u_even, 0), wo_hi[...])

    def middle(u_new, u_old):
        up(0, u_even)
        up(1, u_new)
        g = jnp.concatenate([gate(u_old, 2 * j - 1), gate(u_even, 2 * j)], axis=1)
        w = jnp.concatenate([wo_lo[...], wo_hi[...]], axis=0)
        o_ref[...] += _dot(g, w)

    inner = (j > 0) & (j < steps - 1)
    pl.when(inner & (j % 2 == 1))(lambda: middle(u_odd_b, u_odd_a))
    pl.when(inner & (j % 2 == 0))(lambda: middle(u_odd_a, u_odd_b))

    @pl.when(j == steps - 1)
    def _():
        u_new, u_old = (u_odd_a, u_odd_b) if (steps - 1) % 2 == 0 else (u_odd_b, u_odd_a)
        middle(u_new, u_old)
        o_ref[...] += _dot(gate(u_new, nsc - 1), wo_top[...])
        o_ref[...] = h_ref[...] + gate_ref[...] * o_ref[...]


def _ffn(h, norm_g, mods, layer, st, w_in, conv_w, conv_b, w_out, seq_len):
    m, d = h.shape
    dff = w_out.shape[1]
    bm = _pick(seq_len, (512, 256))
    sub = MXU_COLS
    nsc = dff // sub
    assert dff % (2 * sub) == 0 and conv_w.shape[1:] == (2 * nsc, CONV_W, sub)
    steps = nsc // 2
    assert steps >= 2
    halo = BF16_ROWS
    hb = bm // halo
    last = m // halo - 1
    kernel = functools.partial(_ffn_kernel, bm=bm, seq_len=seq_len, steps=steps)
    mrow = lambda i: st.mod_row(i, bm)
    wo_rows = (lambda j: jnp.maximum(2 * j - 1, 0), lambda j: 2 * j, lambda j: nsc - 1)
    return pl.pallas_call(
        kernel,
        grid=(m // bm, steps),
        in_specs=[
            pl.BlockSpec((bm, d), lambda i, j: (i, 0)),
            pl.BlockSpec((halo, d), lambda i, j: (jnp.maximum(i * hb - 1, 0), 0)),
            pl.BlockSpec((halo, d), lambda i, j: (jnp.minimum((i + 1) * hb, last), 0)),
            pl.BlockSpec((1, d), lambda i, j: (0, 0)),
            _mod_spec(layer, 3, mrow, d),
            _mod_spec(layer, 4, mrow, d),
            _mod_spec(layer, 5, mrow, d),
            pl.BlockSpec((None, d, 2 * sub), lambda i, j: (layer, 0, j)),
            pl.BlockSpec((None, d, 2 * sub), lambda i, j: (layer, 0, steps + j)),
            pl.BlockSpec((None, 2 * nsc, CONV_W, sub), lambda i, j: (layer, 0, 0, 0)),
            pl.BlockSpec((None, 2 * nsc, 1, sub), lambda i, j: (layer, 0, 0, 0)),
        ] + [pl.BlockSpec((None, sub, d), lambda i, j, k=k: (layer, k(j), 0)) for k in wo_rows],
        out_specs=pl.BlockSpec((bm, d), lambda i, j: (i, 0)),
        out_shape=jax.ShapeDtypeStruct((m, d), F32),
        scratch_shapes=[pltpu.VMEM((bm + 2 * halo, d), BF16),
                        pltpu.VMEM((2, bm + 2 * halo, sub), F32),
                        pltpu.VMEM((2, bm + 2 * halo, sub), F32),
                        pltpu.VMEM((2, bm + 2 * halo, sub), F32)],
        compiler_params=_params("parallel", "arbitrary"),
        name="conv_ffn",
    )(h, h, h, norm_g, mods, mods, mods, w_in, w_in, conv_w, conv_b, w_out, w_out, w_out)


def _ga_kernel(sink_ref, q_ref, kvc_ref, *rest, seq, n_kv, local):
    if local:
        kvo_ref, kvp_ref, kvn_ref, o_ref, s_a, s_b = rest
    else:
        o_ref, s_a, s_b = rest
    hd = GA_HD
    kcols = n_kv * hd
    bq = q_ref.shape[0]
    sq = s_a.shape[1]
    n_ctx = kvc_ref.shape[0]
    bufs = (s_a, s_b)

    def band(u):
        if not local:
            return [], 0
        lo, hi = u * sq - WINDOW, (u + 1) * sq + WINDOW
        pieces = []
        if lo < 0:
            pieces.append((kvp_ref, WINDOW + lo, -lo))
        pieces.append((kvo_ref, max(lo, 0), min(hi, bq) - max(lo, 0)))
        if hi > bq:
            pieces.append((kvn_ref, 0, hi - bq))
        return pieces, lo

    def bias(u):
        pieces, lo = band(u)
        n_loc = sum(p[2] for p in pieces)
        kl = lax.broadcasted_iota(jnp.int32, (n_loc, sq), 0) + lo
        ql = lax.broadcasted_iota(jnp.int32, (n_loc, sq), 1) + u * sq
        pos = pl.program_id(1) * bq + kl
        ok = (jnp.abs(ql - kl) <= WINDOW) & (pos >= 0) & (pos < seq)
        return jnp.concatenate([jnp.where(ok, 0.0, NEG_INF).astype(F32), jnp.zeros((n_ctx, sq), F32)], axis=0)

    def keys_values(u, n):
        pieces, _ = band(u)
        ks, vs = slice(n * hd, (n + 1) * hd), slice(kcols + n * hd, kcols + (n + 1) * hd)
        k_all = jnp.concatenate([r[a:a + c, ks] for r, a, c in pieces] + [kvc_ref[:, ks]], axis=0)
        v_all = jnp.concatenate([r[a:a + c, vs] for r, a, c in pieces] + [kvc_ref[:, vs]], axis=0)
        v_t = jnp.concatenate([v_all.astype(F32).T.astype(BF16), jnp.ones((ONES_ROWS, v_all.shape[0]), BF16)], axis=0)
        return k_all, v_t

    chains = [(u, n, g) for u in range(bq // sq) for n in range(n_kv) for g in range(GA_GROUP)]
    cache = {}

    def operands(u, n):
        if (u, n) not in cache:
            cache.clear()
            cache[(u, n)] = keys_values(u, n)
        return cache[(u, n)]

    biases = {}

    def scores(idx):
        u, n, g = chains[idx]
        head = n * GA_GROUP + g
        k_all, _ = operands(u, n)
        s = _dot_nt(k_all, q_ref[u * sq:(u + 1) * sq, head * hd:(head + 1) * hd])
        if local:
            if u not in biases:
                biases[u] = bias(u)
            s = s + biases[u]
        bufs[idx % 2][...] = s

    scores(0)
    for idx, (u, n, g) in enumerate(chains):
        _, v_t = operands(u, n)
        if idx + 1 < len(chains):
            scores(idx + 1)
        buf = bufs[idx % 2]
        head = n * GA_GROUP + g
        sink = sink_ref[head] * LOG2E
        m = jnp.maximum(jnp.max(buf[...], axis=0, keepdims=True), sink)
        p = jnp.exp2((buf[...] - m).astype(BF16))
        ov = _dot(v_t, p)
        o_t = ov[0:hd, :] / (ov[hd:hd + 1, :] + jnp.exp2(sink - m))
        o_ref[u * sq:(u + 1) * sq, head * hd:(head + 1) * hd] = o_t.T.astype(o_ref.dtype)


def _ga_attention(qkv_lat, qkv_ctx, sink, batch, seq, n_ctx, need_ctx):
    n_heads = sink.shape[0]
    qcols = n_heads * GA_HD
    n_kv = n_heads // GA_GROUP
    kvw = 2 * n_kv * GA_HD
    kvblk = qcols // kvw
    assert qcols % kvw == 0
    bq = _pick(seq, (512, 256, 128))
    sq = min(bq, MXU_COLS)
    nq = seq // bq
    wb = bq // WINDOW
    last_w = batch * seq // WINDOW - 1
    smem = pl.BlockSpec(memory_space=pltpu.SMEM)
    lat = pl.pallas_call(
        functools.partial(_ga_kernel, seq=seq, n_kv=n_kv, local=True),
        grid=(batch, nq),
        in_specs=[
            smem,
            pl.BlockSpec((bq, qcols), lambda b, i: (b * nq + i, 0)),
            pl.BlockSpec((n_ctx, kvw), lambda b, i: (b, kvblk)),
            pl.BlockSpec((bq, kvw), lambda b, i: (b * nq + i, kvblk)),
            pl.BlockSpec((WINDOW, kvw), lambda b, i: (jnp.maximum((b * nq + i) * wb - 1, 0), kvblk)),
            pl.BlockSpec((WINDOW, kvw), lambda b, i: (jnp.minimum((b * nq + i + 1) * wb, last_w), kvblk)),
        ],
        out_specs=pl.BlockSpec((bq, qcols), lambda b, i: (b * nq + i, 0)),
        out_shape=jax.ShapeDtypeStruct((batch * seq, qcols), BF16),
        scratch_shapes=[pltpu.VMEM((sq + 2 * WINDOW + n_ctx, sq), F32)] * 2,
        compiler_params=_params("parallel", "parallel"),
        name="ga_attention_latent",
    )(sink, qkv_lat, qkv_ctx, qkv_lat, qkv_lat, qkv_lat)
    ctx = None
    if need_ctx:
        cq = min(n_ctx, MXU_COLS)
        ctx = pl.pallas_call(
            functools.partial(_ga_kernel, seq=n_ctx, n_kv=n_kv, local=False),
            grid=(batch,),
            in_specs=[
                smem,
                pl.BlockSpec((n_ctx, qcols), lambda b: (b, 0)),
                pl.BlockSpec((n_ctx, kvw), lambda b: (b, kvblk)),
            ],
            out_specs=pl.BlockSpec((n_ctx, qcols), lambda b: (b, 0)),
            out_shape=jax.ShapeDtypeStruct((batch * n_ctx, qcols), BF16),
            scratch_shapes=[pltpu.VMEM((n_ctx, cq), F32)] * 2,
            compiler_params=_params("parallel"),
            name="ga_attention_context",
        )(sink, qkv_ctx, qkv_ctx)
    return lat, ctx


def _df_kernel(lam_ref, g_ref, q_ref, *rest, lambda_init, chunks, first_q_axis):
    n_seg = len(chunks)
    kv_refs = rest[:2 * n_seg]
    o_ref = rest[2 * n_seg]
    vt_scrs = rest[2 * n_seg + 1:2 * n_seg + 1 + n_seg]
    acc_scr, s_a, s_b = rest[-3:]
    hd = DF_HD
    bq = q_ref.shape[0]
    lam = lam_ref[...]
    lmbda = (jnp.exp(jnp.sum(lam[0:1] * lam[1:2], axis=-1, keepdims=True))
             - jnp.exp(jnp.sum(lam[2:3] * lam[3:4], axis=-1, keepdims=True)) + lambda_init)

    def transpose_values():
        for seg, ck in enumerate(chunks):
            v_ref, vt = kv_refs[2 * seg + 1], vt_scrs[seg]
            for c in range(v_ref.shape[0] // ck):
                vt[c] = v_ref[c * ck:(c + 1) * ck, :].astype(F32).T.astype(BF16)

    if first_q_axis is None:
        transpose_values()
    else:
        pl.when(pl.program_id(first_q_axis) == 0)(transpose_values)

    qs = (q_ref[:, 0:hd], q_ref[:, hd:2 * hd])
    acc_scr[...] = jnp.zeros_like(acc_scr)
    stat0 = (jnp.full((1, bq), NEG_INF, F32), jnp.zeros((1, bq), F32))
    carry = (stat0, stat0)

    def scores(k_ref, c, ck, dst):
        kblk = k_ref[c * ck:(c + 1) * ck, :]
        for r in range(2):
            dst[r] = _dot_nt(kblk[:, r * hd:(r + 1) * hd], qs[r])

    def update(src, v_t, carry):
        new = []
        for r in range(2):
            m_old, l_old = carry[r]
            m_new = jnp.maximum(m_old, jnp.max(src[r], axis=0, keepdims=True))
            alpha = jnp.exp2(m_old - m_new)
            p = jnp.exp2(src[r] - m_new)
            l_new = alpha * l_old + jnp.sum(p, axis=0, keepdims=True)
            acc_scr[r] = alpha * acc_scr[r] + _dot(v_t, p.astype(BF16))
            new.append((m_new, l_new))
        return tuple(new)

    items = [(kv_refs[2 * seg], c, chunks[seg], vt_scrs[seg])
             for seg in range(n_seg) for c in range(kv_refs[2 * seg].shape[0] // chunks[seg])]
    bufs = (s_a, s_b)

    def view(buf, ck):
        return buf if ck == buf.shape[1] else buf.at[:, 0:ck, :]

    scores(items[0][0], items[0][1], items[0][2], view(bufs[0], items[0][2]))
    for idx, (_, c, ck, vt) in enumerate(items):
        if idx + 1 < len(items):
            k_next, c_next, ck_next, _ = items[idx + 1]
            scores(k_next, c_next, ck_next, view(bufs[(idx + 1) % 2], ck_next))
        carry = update(view(bufs[idx % 2], ck), vt[c], carry)
    (_, l0), (_, l1) = carry
    o = (acc_scr[0] / l0 - lmbda * (acc_scr[1] / l1)).T
    ms = jnp.mean(o * o, axis=-1, keepdims=True)
    y = o * lax.rsqrt(ms + EPS) * g_ref[...] * (1.0 - lambda_init)
    o_ref[...] = y.astype(o_ref.dtype)


def _df_attention(qkv_lat, qkv_ctx, lam, subln_g, batch, seq, n_ctx, lambda_init, need_ctx):
    hw = 2 * DF_HD
    n_heads = qkv_lat.shape[1] // (3 * hw)
    bq = _pick(seq, (512, 256, 128))
    nq = seq // bq
    ck_lat = _pick(seq, (512, 256, 128))
    ck_ctx = _pick(n_ctx, (512, 256, 128))
    g2 = subln_g.reshape(1, hw)
    const = lambda shape: pl.BlockSpec(shape, lambda *_: (0,) * len(shape))
    vt_ctx = pltpu.VMEM((n_ctx // ck_ctx, hw, ck_ctx), BF16)
    vt_lat = pltpu.VMEM((seq // ck_lat, hw, ck_lat), BF16)
    lat = pl.pallas_call(
        functools.partial(_df_kernel, lambda_init=lambda_init, chunks=(ck_ctx, ck_lat), first_q_axis=2),
        grid=(batch, n_heads, nq),
        in_specs=[
            const(lam.shape),
            const((1, hw)),
            pl.BlockSpec((bq, hw), lambda b, h, i: (b * nq + i, h)),
            pl.BlockSpec((n_ctx, hw), lambda b, h, i: (b, n_heads + h)),
            pl.BlockSpec((n_ctx, hw), lambda b, h, i: (b, 2 * n_heads + h)),
            pl.BlockSpec((seq, hw), lambda b, h, i: (b, n_heads + h)),
            pl.BlockSpec((seq, hw), lambda b, h, i: (b, 2 * n_heads + h)),
        ],
        out_specs=pl.BlockSpec((bq, hw), lambda b, h, i: (b * nq + i, h)),
        out_shape=jax.ShapeDtypeStruct((batch * seq, n_heads * hw), BF16),
        scratch_shapes=[vt_ctx, vt_lat, pltpu.VMEM((2, hw, bq), F32)]
        + [pltpu.VMEM((2, max(ck_lat, ck_ctx), bq), F32)] * 2,
        compiler_params=_params("parallel", "parallel", "arbitrary"),
        name="df_attention_latent",
    )(lam, g2, qkv_lat, qkv_ctx, qkv_ctx, qkv_lat, qkv_lat)
    ctx = None
    if need_ctx:
        ctx = pl.pallas_call(
            functools.partial(_df_kernel, lambda_init=lambda_init, chunks=(ck_ctx,), first_q_axis=None),
            grid=(batch, n_heads),
            in_specs=[
                const(lam.shape),
                const((1, hw)),
                pl.BlockSpec((n_ctx, hw), lambda b, h: (b, h)),
                pl.BlockSpec((n_ctx, hw), lambda b, h: (b, n_heads + h)),
                pl.BlockSpec((n_ctx, hw), lambda b, h: (b, 2 * n_heads + h)),
            ],
            out_specs=pl.BlockSpec((n_ctx, hw), lambda b, h: (b, h)),
            out_shape=jax.ShapeDtypeStruct((batch * n_ctx, n_heads * hw), BF16),
            scratch_shapes=[vt_ctx, pltpu.VMEM((2, hw, n_ctx), F32)]
            + [pltpu.VMEM((2, ck_ctx, n_ctx), F32)] * 2,
            compiler_params=_params("parallel", "parallel"),
            name="df_attention_context",
        )(lam, g2, qkv_ctx, qkv_ctx, qkv_ctx)
    return lat, ctx


def _rt_kernel(lg_ref, gn_ref, *refs, n_heads, chunk, ctx_steps):
    ctx_in, lat_in = refs[0:6], refs[6:12]
    ctx_out, lat_out = refs[12:14], refs[14:16]
    st_scr, dec_scr = refs[16], refs[17]
    head0 = pl.program_id(1) * n_heads
    s_idx = pl.program_id(2)
    dk = ctx_in[0].shape[1] // n_heads
    dv = ctx_in[2].shape[1] // n_heads

    @pl.when(s_idx == 0)
    def _():
        st_scr[...] = jnp.zeros_like(st_scr)
        row = lax.broadcasted_iota(jnp.int32, (chunk, chunk), 0).astype(F32)
        col = lax.broadcasted_iota(jnp.int32, (chunk, chunk), 1).astype(F32)
        for d, rel in enumerate((row - col, col - row)):
            for h in range(n_heads):
                lg = lg_ref[d, head0 + h]
                dec_scr[d, h] = jnp.where(rel >= 0, jnp.exp(lg * jnp.maximum(rel, 0.0)), 0.0)

    def step(ins, outs):
        qf_ref, kf_ref, vf_ref, qb_ref, kb_ref, vb_ref = ins
        of_ref, ob_ref = outs
        pos = lax.broadcasted_iota(jnp.int32, (chunk, 1), 0).astype(F32)
        dirs = (
            (qf_ref, kf_ref, vf_ref, of_ref, pos + 1.0, chunk - 1.0 - pos),
            (qb_ref, kb_ref, vb_ref, ob_ref, chunk - pos, pos),
        )
        for d, (q_ref, k_ref, v_ref, o_ref, q_pow, k_pow) in enumerate(dirs):
            for h in range(n_heads):
                lg = lg_ref[d, head0 + h]
                decay = dec_scr[d, h]
                q_decay = jnp.exp(lg * q_pow)
                k_decay = jnp.exp(lg * k_pow)
                chunk_decay = jnp.exp(lg * chunk)
                q = q_ref[:, h * dk:(h + 1) * dk]
                k = k_ref[:, h * dk:(h + 1) * dk]
                v = v_ref[:, h * dv:(h + 1) * dv]
                state = st_scr[d, h]
                inner = _dot_nt(q, k) * decay
                o = _dot(inner.astype(BF16), v) + _dot(q, state.astype(BF16)) * q_decay
                kd = (k.astype(F32) * k_decay).astype(BF16)
                st_scr[d, h] = state * chunk_decay + _dot_tn(kd, v)
                mu = jnp.mean(o, axis=-1, keepdims=True)
                dev = o - mu
                var = jnp.mean(dev * dev, axis=-1, keepdims=True)
                y = dev * lax.rsqrt(var + EPS) * gn_ref[d:d + 1, h * dv:(h + 1) * dv]
                o_ref[:, h * dv:(h + 1) * dv] = y.astype(o_ref.dtype)

    @pl.when(s_idx < ctx_steps)
    def _():
        step(ctx_in, ctx_out)

    @pl.when(s_idx >= ctx_steps)
    def _():
        step(lat_in, lat_out)


def _retention(proj_lat, proj_ctx, log_gamma, gn_g, batch, seq, n_ctx):
    n_heads = log_gamma.shape[1]
    qk = proj_lat.shape[1] // 8
    v = 2 * qk
    chunk = _pick(math.gcd(seq, n_ctx), (256, 128))
    ncc, nlc = n_ctx // chunk, seq // chunk
    dk, dv = qk // n_heads, v // n_heads
    groups = 2
    hpg = n_heads // groups
    qkw, vw = hpg * dk, hpg * dv

    def ctx_f(b, s):
        return b * ncc + jnp.minimum(s, ncc - 1)

    def ctx_b(b, s):
        return b * ncc + (ncc - 1 - jnp.minimum(s, ncc - 1))

    def lat_f(b, s):
        return b * nlc + jnp.maximum(s - ncc, 0)

    def lat_b(b, s):
        return b * nlc + (nlc - 1 - jnp.maximum(s - ncc, 0))

    def in_specs(row):
        return [
            pl.BlockSpec((chunk, qkw), lambda b, g, s: (row(b, s), g)),
            pl.BlockSpec((chunk, qkw), lambda b, g, s: (row(b, s), groups + g)),
            pl.BlockSpec((chunk, vw), lambda b, g, s: (row(b, s), groups + g)),
        ]

    def out_spec(row):
        return pl.BlockSpec((chunk, vw), lambda b, g, s: (row(b, s), g))

    out_ctx = jax.ShapeDtypeStruct((batch * n_ctx, v), BF16)
    out_lat = jax.ShapeDtypeStruct((batch * seq, v), BF16)
    yf_ctx, yb_ctx, yf_lat, yb_lat = pl.pallas_call(
        functools.partial(_rt_kernel, n_heads=hpg, chunk=chunk, ctx_steps=ncc),
        grid=(batch, groups, ncc + nlc),
        in_specs=[pl.BlockSpec(memory_space=pltpu.SMEM), pl.BlockSpec((2, vw), lambda b, g, s: (0, g))]
        + in_specs(ctx_f) + in_specs(ctx_b) + in_specs(lat_f) + in_specs(lat_b),
        out_specs=[out_spec(ctx_f), out_spec(ctx_b), out_spec(lat_f), out_spec(lat_b)],
        out_shape=[out_ctx, out_ctx, out_lat, out_lat],
        scratch_shapes=[pltpu.VMEM((2, hpg, dk, dv), F32), pltpu.VMEM((2, hpg, chunk, chunk), F32)],
        compiler_params=_params("parallel", "parallel", "arbitrary"),
        name="retention",
    )(log_gamma, gn_g, *([proj_ctx] * 6), *([proj_lat] * 6))
    return (yf_lat, yb_lat), (yf_ctx, yb_ctx)


def _axial_tables(rows_count, head_dim):
    rows = jnp.repeat(jnp.arange(rows_count), GRID_W).astype(F32)
    cols = jnp.tile(jnp.arange(GRID_W), rows_count).astype(F32)
    n_freq = head_dim // 4
    inv = ROPE_THETA ** (-jnp.arange(n_freq, dtype=F32) / n_freq)
    ang = jnp.concatenate([rows[:, None] * inv, cols[:, None] * inv], -1)
    cos, sin = jnp.cos(ang), jnp.sin(ang)
    return jnp.concatenate([cos, cos], -1), jnp.concatenate([-sin, sin], -1)


def _linear_tables(n_tokens, head_dim):
    half = head_dim // 2
    inv = ROPE_THETA ** (-jnp.arange(half, dtype=F32) / half)
    ang = jnp.arange(n_tokens, dtype=F32)[:, None] * inv
    return jnp.cos(ang), jnp.sin(ang)


def kernel(x, c, ctx, c_ctx, mod_w, mod_b, norm_g, ffn_w_in, ffn_conv_w, ffn_conv_b, ffn_w_out, ga_wqkv, ga_sink, ga_qk_norm, ga_wo, rt_w_in, rt_decay, rt_gn, rt_wo, df_wqkv, df_lambda, df_qk_norm, df_subln, df_wo):
    batch, seq, d = x.shape
    n_ctx = ctx.shape[1]
    depth = mod_w.shape[0]
    assert batch + 1 <= MOD_ROWS and seq % GRID_W == 0

    cond = jnp.zeros((MOD_ROWS, d), F32).at[0].set(c_ctx).at[1:batch + 1].set(c)
    mods = _adaln_table(cond, mod_w, mod_b).reshape(depth, 6, MOD_ROWS, 1, d)

    ax_cos, ax_sin = _axial_tables(seq // GRID_W, GA_HD)
    assert DF_HD == GA_HD
    rt_dk = d // RT_HEADS
    ln_cos, ln_sin = _linear_tables(seq, rt_dk)
    ones2 = jnp.ones((2, LANES), F32)

    h_lat = x.reshape(batch * seq, d)
    h_ctx = ctx.reshape(batch * n_ctx, d)

    lat = _Stream(seq, lambda i, bm: 1 + i // (seq // bm), lambda i, bm: i % (seq // bm), True)
    cst = _Stream(batch * n_ctx, lambda i, bm: 0, lambda i, bm: 0, False)

    ga_w, ga_o = ga_wqkv.astype(BF16), ga_wo.astype(BF16)
    rt_w, rt_o = rt_w_in.astype(BF16), rt_wo.astype(BF16)
    df_w, df_o = df_wqkv.astype(BF16), df_wo.astype(BF16)
    f_in, f_out = ffn_w_in.astype(BF16), ffn_w_out.astype(BF16)
    n_cw = ffn_conv_w.shape[2] // MXU_COLS
    f_cw = ffn_conv_w.reshape(depth, CONV_W, n_cw, MXU_COLS).transpose(0, 2, 1, 3)
    f_cb = ffn_conv_b.reshape(depth, n_cw, 1, MXU_COLS)

    for i in range(depth):
        need_ctx = i < depth - 1
        kind, j = i % N_MIXERS, i // N_MIXERS
        g1 = norm_g[i, 0].reshape(1, d)
        g2 = norm_g[i, 1].reshape(1, d)

        if kind == 0:
            w_all = ga_w
            qcols = ga_sink.shape[1] * GA_HD
            kcols = (w_all.shape[2] - qcols) // 2
            segments = ((qcols, "norm_rope_128", 0, GA_HD ** -0.5 * LOG2E), (kcols, "norm_rope_128", 1, 1.0),
                        (kcols, "plain", 0, 1.0))
            cos_t, sin_t, gains = ax_cos, ax_sin, ga_qk_norm[j]
        elif kind == 1:
            w_all = rt_w
            vcols = (w_all.shape[2] - 2 * d) // 3
            segments = ((d, "rope_256", 0, rt_dk ** -0.5), (d, "rope_256", 0, 1.0),
                        (vcols, "plain", 0, 1.0), (2 * vcols, "silu", 0, 1.0))
            cos_t, sin_t, gains = ln_cos, ln_sin, ones2
        else:
            w_all = df_w
            qcols = w_all.shape[2] // 3
            segments = ((qcols, "norm_rope_128", 0, DF_HD ** -0.5 * LOG2E), (qcols, "norm_rope_128", 1, 1.0),
                        (qcols, "plain", 0, 1.0))
            cos_t, sin_t, gains = ax_cos, ax_sin, df_qk_norm[j]

        proj = {"lat": _proj(h_lat, g1, mods, i, lat, w_all, j, segments, cos_t, sin_t, gains),
                "ctx": _proj(h_ctx, g1, mods, i, cst, w_all, j, segments, cos_t, sin_t, gains)}

        if kind == 0:
            y_lat, y_ctx = _ga_attention(proj["lat"], proj["ctx"], ga_sink[j], batch, seq, n_ctx, need_ctx)
            h_lat = _oproj(y_lat, ga_o, j, h_lat, mods, i, lat)
            if need_ctx:
                h_ctx = _oproj(y_ctx, ga_o, j, h_ctx, mods, i, cst)
        elif kind == 1:
            log_gamma = jax.nn.log_sigmoid(rt_decay[j].astype(F32))
            y_lat, y_ctx = _retention(proj["lat"], proj["ctx"], log_gamma, rt_gn[j], batch, seq, n_ctx)
            h_lat = _oproj_rt(y_lat[0], y_lat[1], proj["lat"], rt_o, j, h_lat, mods, i, lat)
            if need_ctx:
                h_ctx = _oproj_rt(y_ctx[0], y_ctx[1], proj["ctx"], rt_o, j, h_ctx, mods, i, cst)
        else:
            lambda_init = 0.8 - 0.6 * math.exp(-0.3 * i)
            y_lat, y_ctx = _df_attention(proj["lat"], proj["ctx"], df_lambda[j], df_subln[j], batch, seq, n_ctx,
                                         lambda_init, need_ctx)
            h_lat = _oproj(y_lat, df_o, j, h_lat, mods, i, lat)
            if need_ctx:
                h_ctx = _oproj(y_ctx, df_o, j, h_ctx, mods, i, cst)

        h_lat = _ffn(h_lat, g2, mods, i, lat, f_in, f_cw, f_cb, f_out, seq)
        if need_ctx:
            h_ctx = _ffn(h_ctx, g2, mods, i, cst, f_in, f_cw, f_cb, f_out, n_ctx)

    return h_lat.reshape(batch, seq, d)
```

```python
import functools
import math
from typing import Callable, NamedTuple

import jax
import jax.numpy as jnp
from jax import lax
from jax.experimental import pallas as pl
from jax.experimental.pallas import tpu as pltpu

F32 = jnp.float32
BF16 = jnp.bfloat16

EPS = 1e-6
NEG_INF = -1e30
LOG2E = math.log2(math.e)
ROPE_THETA = 10000.0
GRID_W = 64
WINDOW = 128
N_MIXERS = 3

GA_HD = 128
GA_GROUP = 4
RT_HEADS = 8
DF_HD = 128
CONV_W = 3

LANES = 128
MXU_COLS = 256
BF16_ROWS = 16
ONES_ROWS = BF16_ROWS
MOD_ROWS = 8
VMEM_LIMIT = 52 * 1024 * 1024


class _Stream(NamedTuple):
    group: int
    mod_row: Callable
    tab_block: Callable
    rope: bool


def _pick(n, candidates):
    for c in candidates:
        if n % c == 0:
            return c
    raise ValueError(f"no tile size in {candidates} divides {n}")


def _params(*sem):
    return pltpu.CompilerParams(dimension_semantics=sem, vmem_limit_bytes=VMEM_LIMIT)


def _silu(x):
    return x * (1.0 / (1.0 + jnp.exp(-x)))


def _dot(a, b):
    return jnp.dot(a, b, preferred_element_type=F32)


def _dot_nt(a, b):
    return lax.dot_general(a, b, (((1,), (1,)), ((), ())), preferred_element_type=F32)


def _dot_tn(a, b):
    return lax.dot_general(a, b, (((0,), (0,)), ((), ())), preferred_element_type=F32)


def _adaln_kernel(c_ref, w_ref, b_ref, o_ref):
    s = _silu(c_ref[...]).astype(BF16)
    o_ref[...] = _dot(s, w_ref[...].astype(BF16)) + b_ref[...]


def _adaln_table(cond, mod_w, mod_b):
    depth, d, _ = mod_w.shape
    bn = _pick(d, (1024, 512, 256, 128))
    nj = d // bn
    return pl.pallas_call(
        _adaln_kernel,
        grid=(depth, 6, nj),
        in_specs=[
            pl.BlockSpec((MOD_ROWS, d), lambda l, k, j: (0, 0)),
            pl.BlockSpec((None, d, bn), lambda l, k, j: (l, 0, k * nj + j)),
            pl.BlockSpec((None, 1, bn), lambda l, k, j: (l, 0, k * nj + j)),
        ],
        out_specs=pl.BlockSpec((None, None, MOD_ROWS, bn), lambda l, k, j: (l, k, 0, j)),
        out_shape=jax.ShapeDtypeStruct((depth, 6, MOD_ROWS, d), F32),
        compiler_params=_params("parallel", "parallel", "parallel"),
        name="adaln_table",
    )(cond, mod_w, mod_b.reshape(depth, 1, 6 * d))


def _mod_spec(layer, slot, row_fn, d):
    return pl.BlockSpec((None, None, None, 1, d), lambda i, j: (layer, slot, row_fn(i), 0, 0))


def _mod_spec_cols(layer, slot, row_fn, bn):
    return pl.BlockSpec((None, None, None, 1, bn), lambda i, j: (layer, slot, row_fn(i), 0, j))


def _norm_mod(x, g, shift, scale, mean_mat=None):
    if mean_mat is None:
        r = lax.rsqrt(jnp.mean(x * x, axis=-1, keepdims=True) + EPS)
    else:
        ms = _dot((x * x).astype(BF16), mean_mat)
        r = jnp.concatenate([lax.rsqrt(ms + EPS)] * (x.shape[1] // LANES), axis=1)
    return x * r * (g * (1.0 + scale)) + shift


def _mean_mat(d):
    return jnp.full((d, LANES), 1.0 / d, BF16)


def _proj_kernel(h_ref, g_ref, sh_ref, sc_ref, mm_ref, w_ref, cos_ref, sin_ref, gain_ref, seg_ref, o_ref, a_scr, acc_scr,
                 *, plan, rope):
    j = pl.program_id(1)
    sub = acc_scr.shape[2]
    n_sub = o_ref.shape[1] // sub

    @pl.when(j == 0)
    def _():
        a_scr[...] = _norm_mod(h_ref[...], g_ref[...], sh_ref[...], sc_ref[...], mm_ref[...]).astype(BF16)

    def epilogue(s, kind, arg, scale):
        base = s * sub
        if kind == "plain":
            o_ref[:, base:base + sub] = acc_scr[s].astype(o_ref.dtype)
        elif kind == "silu":
            o_ref[:, base:base + sub] = _silu(acc_scr[s]).astype(o_ref.dtype)
        elif kind == "norm_rope_128":
            x = acc_scr[s]
            ms = _dot((x * x).astype(BF16), seg_ref[...])
            y = x * lax.rsqrt(ms + EPS) * (gain_ref[arg:arg + 1, :] * scale)
            for t in range(0, sub, LANES):
                yt = y[:, t:t + LANES]
                if rope:
                    yt = yt * cos_ref[...] + pltpu.roll(yt, LANES // 2, 1) * sin_ref[...]
                o_ref[:, base + t:base + t + LANES] = yt.astype(o_ref.dtype)
        else:
            assert kind == "rope_256" and sub == 2 * LANES
            x1 = acc_scr[s, :, 0:LANES] * scale
            x2 = acc_scr[s, :, LANES:sub] * scale
            if rope:
                c, sn = cos_ref[...], sin_ref[...]
                x1, x2 = x1 * c - x2 * sn, x1 * sn + x2 * c
            o_ref[:, base:base + LANES] = x1.astype(o_ref.dtype)
            o_ref[:, base + LANES:base + sub] = x2.astype(o_ref.dtype)

    for lo, hi, kinds in plan:
        @pl.when((j >= lo) & (j < hi))
        def _(kinds=kinds):
            a = a_scr[...]
            for s in range(n_sub):
                acc_scr[s] = _dot(a, w_ref[:, s * sub:(s + 1) * sub])
            for s, (kind, arg, scale) in enumerate(kinds):
                epilogue(s, kind, arg, scale)


def _proj(h, norm_g, mods, layer, st, w_all, w_idx, segments, cos_t, sin_t, gains):
    m, d = h.shape
    n = w_all.shape[2]
    bm = _pick(st.group, (1024, 512, 256))
    bn = _pick(n, (1024, 512, 256))
    sub = MXU_COLS
    n_sub = bn // sub
    per_sub = []
    for width, kind, arg, scale in segments:
        assert width % sub == 0
        per_sub += [(kind, arg, scale)] * (width // sub)
    assert len(per_sub) * sub == n
    tiles = [tuple(per_sub[t * n_sub:(t + 1) * n_sub]) for t in range(n // bn)]
    plan, lo = [], 0
    for t in range(1, len(tiles) + 1):
        if t == len(tiles) or tiles[t] != tiles[lo]:
            plan.append((lo, t, tiles[lo]))
            lo = t
    row_fn, tab_fn = st.mod_row, st.tab_block
    kernel = functools.partial(_proj_kernel, plan=tuple(plan), rope=st.rope)
    head = jnp.arange(sub) // LANES
    seg_mean = jnp.where(head[:, None] == head[None, :], 1.0 / LANES, 0.0).astype(BF16)
    gains = jnp.tile(gains, (1, sub // LANES))
    return pl.pallas_call(
        kernel,
        grid=(m // bm, n // bn),
        in_specs=[
            pl.BlockSpec((bm, d), lambda i, j: (i, 0)),
            pl.BlockSpec((1, d), lambda i, j: (0, 0)),
            _mod_spec(layer, 0, lambda i: row_fn(i, bm), d),
            _mod_spec(layer, 1, lambda i: row_fn(i, bm), d),
            pl.BlockSpec((d, LANES), lambda i, j: (0, 0)),
            pl.BlockSpec((None, d, bn), lambda i, j: (w_idx, 0, j)),
            pl.BlockSpec((bm, LANES), lambda i, j: (tab_fn(i, bm), 0)),
            pl.BlockSpec((bm, LANES), lambda i, j: (tab_fn(i, bm), 0)),
            pl.BlockSpec(gains.shape, lambda i, j: (0, 0)),
            pl.BlockSpec((sub, sub), lambda i, j: (0, 0)),
        ],
        out_specs=pl.BlockSpec((bm, bn), lambda i, j: (i, j)),
        out_shape=jax.ShapeDtypeStruct((m, n), BF16),
        scratch_shapes=[pltpu.VMEM((bm, d), BF16), pltpu.VMEM((n_sub, bm, sub), F32)],
        compiler_params=_params("parallel", "arbitrary"),
        name="proj",
    )(h, norm_g, mods, mods, _mean_mat(d), w_all, cos_t, sin_t, gains, seg_mean)


def _oproj_kernel(y_ref, w_ref, h_ref, gate_ref, o_ref):
    o_ref[...] = h_ref[...] + gate_ref[...] * _dot(y_ref[...], w_ref[...])


def _oproj_rt_kernel(yf_ref, yb_ref, sf_ref, sb_ref, w_ref, h_ref, gate_ref, o_ref):
    c = pl.program_id(1)

    @pl.when(c == 0)
    def _():
        o_ref[...] = jnp.zeros_like(o_ref)

    ck = w_ref.shape[0]
    sub = min(ck, MXU_COLS)
    part = None
    for s in range(0, ck, sub):
        cols = slice(s, s + sub)
        z = sf_ref[:, cols] * yf_ref[:, cols] + sb_ref[:, cols] * yb_ref[:, cols]
        d = _dot(z, w_ref[cols, :])
        part = d if part is None else part + d
    o_ref[...] += part

    @pl.when(c == pl.num_programs(1) - 1)
    def _():
        o_ref[...] = h_ref[...] + gate_ref[...] * o_ref[...]


def _oproj(y, w_all, w_idx, h, mods, layer, st, in_place):
    m, k = y.shape
    d = w_all.shape[2]
    bm = _pick(st.group, (1024, 512, 256))
    row_fn = st.mod_row
    bn = _pick(d, (1024, 512, 256, 128))
    return pl.pallas_call(
        _oproj_kernel,
        grid=(m // bm, d // bn),
        in_specs=[
            pl.BlockSpec((bm, k), lambda i, j: (i, 0)),
            pl.BlockSpec((None, k, bn), lambda i, j: (w_idx, 0, j)),
            pl.BlockSpec((bm, bn), lambda i, j: (i, j)),
            _mod_spec_cols(layer, 2, lambda i: row_fn(i, bm), bn),
        ],
        out_specs=pl.BlockSpec((bm, bn), lambda i, j: (i, j)),
        out_shape=jax.ShapeDtypeStruct((m, d), F32),
        input_output_aliases={2: 0} if in_place else {},
        compiler_params=_params("parallel", "arbitrary"),
        name="oproj",
    )(y, w_all, h, mods)


def _oproj_rt(yf, yb, proj, w_all, w_idx, h, mods, layer, st, in_place):
    m, v = yf.shape
    d = w_all.shape[2]
    bm = _pick(st.group, (512, 256))
    row_fn = st.mod_row
    ck = _pick(v, (1024, 512, 256, 128))
    nc = v // ck
    gcol = (proj.shape[1] // v - 2) * nc
    return pl.pallas_call(
        _oproj_rt_kernel,
        grid=(m // bm, nc),
        in_specs=[
            pl.BlockSpec((bm, ck), lambda i, c: (i, c)),
            pl.BlockSpec((bm, ck), lambda i, c: (i, c)),
            pl.BlockSpec((bm, ck), lambda i, c: (i, gcol + c)),
            pl.BlockSpec((bm, ck), lambda i, c: (i, gcol + nc + c)),
            pl.BlockSpec((None, ck, d), lambda i, c: (w_idx, c, 0)),
            pl.BlockSpec((bm, d), lambda i, c: (i, 0)),
            _mod_spec(layer, 2, lambda i: row_fn(i, bm), d),
        ],
        out_specs=pl.BlockSpec((bm, d), lambda i, c: (i, 0)),
        out_shape=jax.ShapeDtypeStruct((m, d), F32),
        input_output_aliases={5: 0} if in_place else {},
        compiler_params=_params("parallel", "arbitrary"),
        name="oproj_rt",
    )(yf, yb, proj, proj, w_all, h, mods)


def _ffn_kernel(h_ref, hp_ref, hn_ref, g_ref, sh_ref, sc_ref, gate_ref, wa_ref, wb_ref, cw_ref, cb_ref,
                wo_lo, wo_hi, wo_top, o_ref, a_scr, u_even, u_odd_a, u_odd_b, *, bm, seq_len, steps):
    i = pl.program_id(0)
    j = pl.program_id(1)
    halo = BF16_ROWS
    sub = u_even.shape[2]
    nsc = 2 * steps

    def up(half, u):
        cols = slice(half * sub, (half + 1) * sub)
        a = a_scr[...]
        u[0] = _dot(a, wa_ref[:, cols])
        u[1] = _dot(a, wb_ref[:, cols])

    def gate(u, k):
        def conv(idx, kk):
            cw = cw_ref[kk]
            prev = u[idx, halo - 1:halo - 1 + bm, :]
            cur = u[idx, halo:halo + bm, :]
            nxt = u[idx, halo + 1:halo + 1 + bm, :]
            return cb_ref[kk] + prev * cw[0:1, :] + cur * cw[1:2, :] + nxt * cw[2:3, :]

        return (_silu(conv(0, k)) * conv(1, nsc + k)).astype(BF16)

    @pl.when(j == 0)
    def _():
        def nm(x):
            return _norm_mod(x, g_ref[...], sh_ref[...], sc_ref[...])

        keep_prev = jnp.where((i * bm) % seq_len != 0, 1.0, 0.0)
        keep_next = jnp.where(((i + 1) * bm) % seq_len != 0, 1.0, 0.0)
        a_scr[0:halo, :] = (nm(hp_ref[...]) * keep_prev).astype(BF16)
        a_scr[halo:halo + bm, :] = nm(h_ref[...]).astype(BF16)
        a_scr[halo + bm:, :] = (nm(hn_ref[...]) * keep_next).astype(BF16)
        o_ref[...] = jnp.zeros_like(o_ref)
        up(0, u_even)
        up(1, u_odd_a)
        o_ref[...] += _dot(gate(u_even, 0), wo_hi[...])

    def middle(u_new, u_old):
        up(0, u_even)
        up(1, u_new)
        g = jnp.concatenate([gate(u_old, 2 * j - 1), gate(u_even, 2 * j)], axis=1)
        w = jnp.concatenate([wo_lo[...], wo_hi[...]], axis=0)
        o_ref[...] += _dot(g, w)

    inner = (j > 0) & (j < steps - 1)
    pl.when(inner & (j % 2 == 1))(lambda: middle(u_odd_b, u_odd_a))
    pl.when(inner & (j % 2 == 0))(lambda: middle(u_odd_a, u_odd_b))

    @pl.when(j == steps - 1)
    def _():
        u_new, u_old = (u_odd_a, u_odd_b) if (steps - 1) % 2 == 0 else (u_odd_b, u_odd_a)
        middle(u_new, u_old)
        o_ref[...] += _dot(gate(u_new, nsc - 1), wo_top[...])
        o_ref[...] = h_ref[...] + gate_ref[...] * o_ref[...]


def _ffn(h, norm_g, mods, layer, st, w_in, conv_w, conv_b, w_out, seq_len):
    m, d = h.shape
    dff = w_out.shape[1]
    bm = _pick(seq_len, (512, 256))
    sub = MXU_COLS
    nsc = dff // sub
    assert dff % (2 * sub) == 0 and conv_w.shape[1:] == (2 * nsc, CONV_W, sub)
    steps = nsc // 2
    assert steps >= 2
    halo = BF16_ROWS
    hb = bm // halo
    last = m // halo - 1
    kernel = functools.partial(_ffn_kernel, bm=bm, seq_len=seq_len, steps=steps)
    mrow = lambda i: st.mod_row(i, bm)
    wo_rows = (lambda j: jnp.maximum(2 * j - 1, 0), lambda j: 2 * j, lambda j: nsc - 1)
    return pl.pallas_call(
        kernel,
        grid=(m // bm, steps),
        in_specs=[
            pl.BlockSpec((bm, d), lambda i, j: (i, 0)),
            pl.BlockSpec((halo, d), lambda i, j: (jnp.maximum(i * hb - 1, 0), 0)),
            pl.BlockSpec((halo, d), lambda i, j: (jnp.minimum((i + 1) * hb, last), 0)),
            pl.BlockSpec((1, d), lambda i, j: (0, 0)),
            _mod_spec(layer, 3, mrow, d),
            _mod_spec(layer, 4, mrow, d),
            _mod_spec(layer, 5, mrow, d),
            pl.BlockSpec((None, d, 2 * sub), lambda i, j: (layer, 0, j)),
            pl.BlockSpec((None, d, 2 * sub), lambda i, j: (layer, 0, steps + j)),
            pl.BlockSpec((None, 2 * nsc, CONV_W, sub), lambda i, j: (layer, 0, 0, 0)),
            pl.BlockSpec((None, 2 * nsc, 1, sub), lambda i, j: (layer, 0, 0, 0)),
        ] + [pl.BlockSpec((None, sub, d), lambda i, j, k=k: (layer, k(j), 0)) for k in wo_rows],
        out_specs=pl.BlockSpec((bm, d), lambda i, j: (i, 0)),
        out_shape=jax.ShapeDtypeStruct((m, d), F32),
        scratch_shapes=[pltpu.VMEM((bm + 2 * halo, d), BF16),
                        pltpu.VMEM((2, bm + 2 * halo, sub), F32),
                        pltpu.VMEM((2, bm + 2 * halo, sub), F32),
                        pltpu.VMEM((2, bm + 2 * halo, sub), F32)],
        compiler_params=_params("parallel", "arbitrary"),
        name="conv_ffn",
    )(h, h, h, norm_g, mods, mods, mods, w_in, w_in, conv_w, conv_b, w_out, w_out, w_out)


def _ga_kernel(sink_ref, q_ref, kvc_ref, *rest, seq, n_kv, local):
    if local:
        kvo_ref, kvp_ref, kvn_ref, o_ref, s_a, s_b = rest
    else:
        o_ref, s_a, s_b = rest
    hd = GA_HD
    kcols = n_kv * hd
    bq = q_ref.shape[0]
    sq = s_a.shape[1]
    n_ctx = kvc_ref.shape[0]
    bufs = (s_a, s_b)

    def band(u):
        if not local:
            return [], 0
        lo, hi = u * sq - WINDOW, (u + 1) * sq + WINDOW
        pieces = []
        if lo < 0:
            pieces.append((kvp_ref, WINDOW + lo, -lo))
        pieces.append((kvo_ref, max(lo, 0), min(hi, bq) - max(lo, 0)))
        if hi > bq:
            pieces.append((kvn_ref, 0, hi - bq))
        return pieces, lo

    def bias(u):
        pieces, lo = band(u)
        n_loc = sum(p[2] for p in pieces)
        kl = lax.broadcasted_iota(jnp.int32, (n_loc, sq), 0) + lo
        ql = lax.broadcasted_iota(jnp.int32, (n_loc, sq), 1) + u * sq
        pos = pl.program_id(1) * bq + kl
        ok = (jnp.abs(ql - kl) <= WINDOW) & (pos >= 0) & (pos < seq)
        return jnp.concatenate([jnp.where(ok, 0.0, NEG_INF).astype(F32), jnp.zeros((n_ctx, sq), F32)], axis=0)

    def keys_values(u, n):
        pieces, _ = band(u)
        ks, vs = slice(n * hd, (n + 1) * hd), slice(kcols + n * hd, kcols + (n + 1) * hd)
        k_all = jnp.concatenate([r[a:a + c, ks] for r, a, c in pieces] + [kvc_ref[:, ks]], axis=0)
        v_all = jnp.concatenate([r[a:a + c, vs] for r, a, c in pieces] + [kvc_ref[:, vs]], axis=0)
        v_t = jnp.concatenate([v_all.astype(F32).T.astype(BF16), jnp.ones((ONES_ROWS, v_all.shape[0]), BF16)], axis=0)
        return k_all, v_t

    chains = [(u, n, g) for u in range(bq // sq) for n in range(n_kv) for g in range(GA_GROUP)]
    cache = {}

    def operands(u, n):
        if (u, n) not in cache:
            cache.clear()
            cache[(u, n)] = keys_values(u, n)
        return cache[(u, n)]

    biases = {}

    def scores(idx):
        u, n, g = chains[idx]
        head = n * GA_GROUP + g
        k_all, _ = operands(u, n)
        s = _dot_nt(k_all, q_ref[u * sq:(u + 1) * sq, head * hd:(head + 1) * hd])
        if local:
            if u not in biases:
                biases[u] = bias(u)
            s = s + biases[u]
        bufs[idx % 2][...] = s

    scores(0)
    for idx, (u, n, g) in enumerate(chains):
        _, v_t = operands(u, n)
        if idx + 1 < len(chains):
            scores(idx + 1)
        buf = bufs[idx % 2]
        head = n * GA_GROUP + g
        sink = sink_ref[head] * LOG2E
        m = jnp.maximum(jnp.max(buf[...], axis=0, keepdims=True), sink)
        p = jnp.exp2((buf[...] - m).astype(BF16))
        ov = _dot(v_t, p)
        o_t = ov[0:hd, :] / (ov[hd:hd + 1, :] + jnp.exp2(sink - m))
        o_ref[u * sq:(u + 1) * sq, head * hd:(head + 1) * hd] = o_t.T.astype(o_ref.dtype)


def _ga_attention(qkv_lat, qkv_ctx, sink, batch, seq, n_ctx, need_ctx):
    n_heads = sink.shape[0]
    qcols = n_heads * GA_HD
    n_kv = n_heads // GA_GROUP
    kvw = 2 * n_kv * GA_HD
    kvblk = qcols // kvw
    assert qcols % kvw == 0
    bq = _pick(seq, (512, 256, 128))
    sq = min(bq, MXU_COLS)
    nq = seq // bq
    wb = bq // WINDOW
    last_w = batch * seq // WINDOW - 1
    smem = pl.BlockSpec(memory_space=pltpu.SMEM)
    lat = pl.pallas_call(
        functools.partial(_ga_kernel, seq=seq, n_kv=n_kv, local=True),
        grid=(batch, nq),
        in_specs=[
            smem,
            pl.BlockSpec((bq, qcols), lambda b, i: (b * nq + i, 0)),
            pl.BlockSpec((n_ctx, kvw), lambda b, i: (b, kvblk)),
            pl.BlockSpec((bq, kvw), lambda b, i: (b * nq + i, kvblk)),
            pl.BlockSpec((WINDOW, kvw), lambda b, i: (jnp.maximum((b * nq + i) * wb - 1, 0), kvblk)),
            pl.BlockSpec((WINDOW, kvw), lambda b, i: (jnp.minimum((b * nq + i + 1) * wb, last_w), kvblk)),
        ],
        out_specs=pl.BlockSpec((bq, qcols), lambda b, i: (b * nq + i, 0)),
        out_shape=jax.ShapeDtypeStruct((batch * seq, qcols), BF16),
        scratch_shapes=[pltpu.VMEM((sq + 2 * WINDOW + n_ctx, sq), F32)] * 2,
        compiler_params=_params("parallel", "parallel"),
        name="ga_attention_latent",
    )(sink, qkv_lat, qkv_ctx, qkv_lat, qkv_lat, qkv_lat)
    ctx = None
    if need_ctx:
        cq = min(n_ctx, MXU_COLS)
        ctx = pl.pallas_call(
            functools.partial(_ga_kernel, seq=n_ctx, n_kv=n_kv, local=False),
            grid=(batch,),
            in_specs=[
                smem,
                pl.BlockSpec((n_ctx, qcols), lambda b: (b, 0)),
                pl.BlockSpec((n_ctx, kvw), lambda b: (b, kvblk)),
            ],
            out_specs=pl.BlockSpec((n_ctx, qcols), lambda b: (b, 0)),
            out_shape=jax.ShapeDtypeStruct((batch * n_ctx, qcols), BF16),
            scratch_shapes=[pltpu.VMEM((n_ctx, cq), F32)] * 2,
            compiler_params=_params("parallel"),
            name="ga_attention_context",
        )(sink, qkv_ctx, qkv_ctx)
    return lat, ctx


def _df_kernel(lam_ref, g_ref, q_ref, *rest, lambda_init, chunks, first_q_axis):
    n_seg = len(chunks)
    kv_refs = rest[:2 * n_seg]
    o_ref = rest[2 * n_seg]
    vt_scrs = rest[2 * n_seg + 1:2 * n_seg + 1 + n_seg]
    acc_scr, s_a, s_b = rest[-3:]
    hd = DF_HD
    bq = q_ref.shape[0]
    lam = lam_ref[...]
    lmbda = (jnp.exp(jnp.sum(lam[0:1] * lam[1:2], axis=-1, keepdims=True))
             - jnp.exp(jnp.sum(lam[2:3] * lam[3:4], axis=-1, keepdims=True)) + lambda_init)

    def transpose_values():
        for seg, ck in enumerate(chunks):
            v_ref, vt = kv_refs[2 * seg + 1], vt_scrs[seg]
            for c in range(v_ref.shape[0] // ck):
                vt[c] = v_ref[c * ck:(c + 1) * ck, :].astype(F32).T.astype(BF16)

    if first_q_axis is None:
        transpose_values()
    else:
        pl.when(pl.program_id(first_q_axis) == 0)(transpose_values)

    qs = (q_ref[:, 0:hd], q_ref[:, hd:2 * hd])
    acc_scr[...] = jnp.zeros_like(acc_scr)
    stat0 = (jnp.full((1, bq), NEG_INF, F32), jnp.zeros((1, bq), F32))
    carry = (stat0, stat0)

    def scores(k_ref, c, ck, dst):
        kblk = k_ref[c * ck:(c + 1) * ck, :]
        for r in range(2):
            dst[r] = _dot_nt(kblk[:, r * hd:(r + 1) * hd], qs[r])

    def update(src, v_t, carry):
        new = []
        for r in range(2):
            m_old, l_old = carry[r]
            m_new = jnp.maximum(m_old, jnp.max(src[r], axis=0, keepdims=True))
            alpha = jnp.exp2(m_old - m_new)
            p = jnp.exp2(src[r] - m_new)
            l_new = alpha * l_old + jnp.sum(p, axis=0, keepdims=True)
            acc_scr[r] = alpha * acc_scr[r] + _dot(v_t, p.astype(BF16))
            new.append((m_new, l_new))
        return tuple(new)

    items = [(kv_refs[2 * seg], c, chunks[seg], vt_scrs[seg])
             for seg in range(n_seg) for c in range(kv_refs[2 * seg].shape[0] // chunks[seg])]
    bufs = (s_a, s_b)

    def view(buf, ck):
        return buf if ck == buf.shape[1] else buf.at[:, 0:ck, :]

    scores(items[0][0], items[0][1], items[0][2], view(bufs[0], items[0][2]))
    for idx, (_, c, ck, vt) in enumerate(items):
        if idx + 1 < len(items):
            k_next, c_next, ck_next, _ = items[idx + 1]
            scores(k_next, c_next, ck_next, view(bufs[(idx + 1) % 2], ck_next))
        carry = update(view(bufs[idx % 2], ck), vt[c], carry)
    (_, l0), (_, l1) = carry
    o = (acc_scr[0] / l0 - lmbda * (acc_scr[1] / l1)).T
    ms = jnp.mean(o * o, axis=-1, keepdims=True)
    y = o * lax.rsqrt(ms + EPS) * g_ref[...] * (1.0 - lambda_init)
    o_ref[...] = y.astype(o_ref.dtype)


def _df_attention(qkv_lat, qkv_ctx, lam, subln_g, batch, seq, n_ctx, lambda_init, need_ctx):
    hw = 2 * DF_HD
    n_heads = qkv_lat.shape[1] // (3 * hw)
    bq = _pick(seq, (512, 256, 128))
    nq = seq // bq
    ck_lat = _pick(seq, (512, 256, 128))
    ck_ctx = _pick(n_ctx, (512, 256, 128))
    g2 = subln_g.reshape(1, hw)
    const = lambda shape: pl.BlockSpec(shape, lambda *_: (0,) * len(shape))
    vt_ctx = pltpu.VMEM((n_ctx // ck_ctx, hw, ck_ctx), BF16)
    vt_lat = pltpu.VMEM((seq // ck_lat, hw, ck_lat), BF16)
    lat = pl.pallas_call(
        functools.partial(_df_kernel, lambda_init=lambda_init, chunks=(ck_ctx, ck_lat), first_q_axis=2),
        grid=(batch, n_heads, nq),
        in_specs=[
            const(lam.shape),
            const((1, hw)),
            pl.BlockSpec((bq, hw), lambda b, h, i: (b * nq + i, h)),
            pl.BlockSpec((n_ctx, hw), lambda b, h, i: (b, n_heads + h)),
            pl.BlockSpec((n_ctx, hw), lambda b, h, i: (b, 2 * n_heads + h)),
            pl.BlockSpec((seq, hw), lambda b, h, i: (b, n_heads + h)),
            pl.BlockSpec((seq, hw), lambda b, h, i: (b, 2 * n_heads + h)),
        ],
        out_specs=pl.BlockSpec((bq, hw), lambda b, h, i: (b * nq + i, h)),
        out_shape=jax.ShapeDtypeStruct((batch * seq, n_heads * hw), BF16),
        scratch_shapes=[vt_ctx, vt_lat, pltpu.VMEM((2, hw, bq), F32)]
        + [pltpu.VMEM((2, max(ck_lat, ck_ctx), bq), F32)] * 2,
        compiler_params=_params("parallel", "parallel", "arbitrary"),
        name="df_attention_latent",
    )(lam, g2, qkv_lat, qkv_ctx, qkv_ctx, qkv_lat, qkv_lat)
    ctx = None
    if need_ctx:
        ctx = pl.pallas_call(
            functools.partial(_df_kernel, lambda_init=lambda_init, chunks=(ck_ctx,), first_q_axis=None),
            grid=(batch, n_heads),
            in_specs=[
                const(lam.shape),
                const((1, hw)),
                pl.BlockSpec((n_ctx, hw), lambda b, h: (b, h)),
                pl.BlockSpec((n_ctx, hw), lambda b, h: (b, n_heads + h)),
                pl.BlockSpec((n_ctx, hw), lambda b, h: (b, 2 * n_heads + h)),
            ],
            out_specs=pl.BlockSpec((n_ctx, hw), lambda b, h: (b, h)),
            out_shape=jax.ShapeDtypeStruct((batch * n_ctx, n_heads * hw), BF16),
            scratch_shapes=[vt_ctx, pltpu.VMEM((2, hw, n_ctx), F32)]
            + [pltpu.VMEM((2, ck_ctx, n_ctx), F32)] * 2,
            compiler_params=_params("parallel", "parallel"),
            name="df_attention_context",
        )(lam, g2, qkv_ctx, qkv_ctx, qkv_ctx)
    return lat, ctx


def _rt_kernel(lg_ref, gn_ref, *refs, n_heads, chunk, ctx_steps):
    ctx_in, lat_in = refs[0:6], refs[6:12]
    ctx_out, lat_out = refs[12:14], refs[14:16]
    st_scr, dec_scr = refs[16], refs[17]
    head0 = pl.program_id(1) * n_heads
    s_idx = pl.program_id(2)
    dk = ctx_in[0].shape[1] // n_heads
    dv = ctx_in[2].shape[1] // n_heads

    @pl.when(s_idx == 0)
    def _():
        st_scr[...] = jnp.zeros_like(st_scr)
        row = lax.broadcasted_iota(jnp.int32, (chunk, chunk), 0).astype(F32)
        col = lax.broadcasted_iota(jnp.int32, (chunk, chunk), 1).astype(F32)
        for d, rel in enumerate((row - col, col - row)):
            for h in range(n_heads):
                lg = lg_ref[d, head0 + h]
                dec_scr[d, h] = jnp.where(rel >= 0, jnp.exp(lg * jnp.maximum(rel, 0.0)), 0.0)

    def step(ins, outs):
        qf_ref, kf_ref, vf_ref, qb_ref, kb_ref, vb_ref = ins
        of_ref, ob_ref = outs
        pos = lax.broadcasted_iota(jnp.int32, (chunk, 1), 0).astype(F32)
        dirs = (
            (qf_ref, kf_ref, vf_ref, of_ref, pos + 1.0, chunk - 1.0 - pos),
            (qb_ref, kb_ref, vb_ref, ob_ref, chunk - pos, pos),
        )
        for d, (q_ref, k_ref, v_ref, o_ref, q_pow, k_pow) in enumerate(dirs):
            for h in range(n_heads):
                lg = lg_ref[d, head0 + h]
                decay = dec_scr[d, h]
                q_decay = jnp.exp(lg * q_pow)
                k_decay = jnp.exp(lg * k_pow)
                chunk_decay = jnp.exp(lg * chunk)
                q = q_ref[:, h * dk:(h + 1) * dk]
                k = k_ref[:, h * dk:(h + 1) * dk]
                v = v_ref[:, h * dv:(h + 1) * dv]
                state = st_scr[d, h]
                inner = _dot_nt(q, k) * decay
                o = _dot(inner.astype(BF16), v) + _dot(q, state.astype(BF16)) * q_decay
                kd = (k.astype(F32) * k_decay).astype(BF16)
                st_scr[d, h] = state * chunk_decay + _dot_tn(kd, v)
                mu = jnp.mean(o, axis=-1, keepdims=True)
                dev = o - mu
                var = jnp.mean(dev * dev, axis=-1, keepdims=True)
                y = dev * lax.rsqrt(var + EPS) * gn_ref[d:d + 1, h * dv:(h + 1) * dv]
                o_ref[:, h * dv:(h + 1) * dv] = y.astype(o_ref.dtype)

    @pl.when(s_idx < ctx_steps)
    def _():
        step(ctx_in, ctx_out)

    @pl.when(s_idx >= ctx_steps)
    def _():
        step(lat_in, lat_out)


def _retention(proj_lat, proj_ctx, log_gamma, gn_g, batch, seq, n_ctx):
    n_heads = log_gamma.shape[1]
    qk = proj_lat.shape[1] // 8
    v = 2 * qk
    chunk = _pick(math.gcd(seq, n_ctx), (256, 128))
    ncc, nlc = n_ctx // chunk, seq // chunk
    dk, dv = qk // n_heads, v // n_heads
    groups = 2
    hpg = n_heads // groups
    qkw, vw = hpg * dk, hpg * dv

    def ctx_f(b, s):
        return b * ncc + jnp.minimum(s, ncc - 1)

    def ctx_b(b, s):
        return b * ncc + (ncc - 1 - jnp.minimum(s, ncc - 1))

    def lat_f(b, s):
        return b * nlc + jnp.maximum(s - ncc, 0)

    def lat_b(b, s):
        return b * nlc + (nlc - 1 - jnp.maximum(s - ncc, 0))

    def in_specs(row):
        return [
            pl.BlockSpec((chunk, qkw), lambda b, g, s: (row(b, s), g)),
            pl.BlockSpec((chunk, qkw), lambda b, g, s: (row(b, s), groups + g)),
            pl.BlockSpec((chunk, vw), lambda b, g, s: (row(b, s), groups + g)),
        ]

    def out_spec(row):
        return pl.BlockSpec((chunk, vw), lambda b, g, s: (row(b, s), g))

    out_ctx = jax.ShapeDtypeStruct((batch * n_ctx, v), BF16)
    out_lat = jax.ShapeDtypeStruct((batch * seq, v), BF16)
    yf_ctx, yb_ctx, yf_lat, yb_lat = pl.pallas_call(
        functools.partial(_rt_kernel, n_heads=hpg, chunk=chunk, ctx_steps=ncc),
        grid=(batch, groups, ncc + nlc),
        in_specs=[pl.BlockSpec(memory_space=pltpu.SMEM), pl.BlockSpec((2, vw), lambda b, g, s: (0, g))]
        + in_specs(ctx_f) + in_specs(ctx_b) + in_specs(lat_f) + in_specs(lat_b),
        out_specs=[out_spec(ctx_f), out_spec(ctx_b), out_spec(lat_f), out_spec(lat_b)],
        out_shape=[out_ctx, out_ctx, out_lat, out_lat],
        scratch_shapes=[pltpu.VMEM((2, hpg, dk, dv), F32), pltpu.VMEM((2, hpg, chunk, chunk), F32)],
        compiler_params=_params("parallel", "parallel", "arbitrary"),
        name="retention",
    )(log_gamma, gn_g, *([proj_ctx] * 6), *([proj_lat] * 6))
    return (yf_lat, yb_lat), (yf_ctx, yb_ctx)


def _axial_tables(rows_count, head_dim):
    rows = jnp.repeat(jnp.arange(rows_count), GRID_W).astype(F32)
    cols = jnp.tile(jnp.arange(GRID_W), rows_count).astype(F32)
    n_freq = head_dim // 4
    inv = ROPE_THETA ** (-jnp.arange(n_freq, dtype=F32) / n_freq)
    ang = jnp.concatenate([rows[:, None] * inv, cols[:, None] * inv], -1)
    cos, sin = jnp.cos(ang), jnp.sin(ang)
    return jnp.concatenate([cos, cos], -1), jnp.concatenate([-sin, sin], -1)


def _linear_tables(n_tokens, head_dim):
    half = head_dim // 2
    inv = ROPE_THETA ** (-jnp.arange(half, dtype=F32) / half)
    ang = jnp.arange(n_tokens, dtype=F32)[:, None] * inv
    return jnp.cos(ang), jnp.sin(ang)


def kernel(x, c, ctx, c_ctx, mod_w, mod_b, norm_g, ffn_w_in, ffn_conv_w, ffn_conv_b, ffn_w_out, ga_wqkv, ga_sink, ga_qk_norm, ga_wo, rt_w_in, rt_decay, rt_gn, rt_wo, df_wqkv, df_lambda, df_qk_norm, df_subln, df_wo):
    batch, seq, d = x.shape
    n_ctx = ctx.shape[1]
    depth = mod_w.shape[0]
    assert batch + 1 <= MOD_ROWS and seq % GRID_W == 0

    cond = jnp.zeros((MOD_ROWS, d), F32).at[0].set(c_ctx).at[1:batch + 1].set(c)
    mods = _adaln_table(cond, mod_w, mod_b).reshape(depth, 6, MOD_ROWS, 1, d)

    ax_cos, ax_sin = _axial_tables(seq // GRID_W, GA_HD)
    assert DF_HD == GA_HD
    rt_dk = d // RT_HEADS
    ln_cos, ln_sin = _linear_tables(seq, rt_dk)
    ones2 = jnp.ones((2, LANES), F32)

    h_lat = x.reshape(batch * seq, d)
    h_ctx = ctx.reshape(batch * n_ctx, d)

    lat = _Stream(seq, lambda i, bm: 1 + i // (seq // bm), lambda i, bm: i % (seq // bm), True)
    cst = _Stream(batch * n_ctx, lambda i, bm: 0, lambda i, bm: 0, False)

    ga_w, ga_o = ga_wqkv.astype(BF16), ga_wo.astype(BF16)
    rt_w, rt_o = rt_w_in.astype(BF16), rt_wo.astype(BF16)
    df_w, df_o = df_wqkv.astype(BF16), df_wo.astype(BF16)
    f_in, f_out = ffn_w_in.astype(BF16), ffn_w_out.astype(BF16)
    n_cw = ffn_conv_w.shape[2] // MXU_COLS
    f_cw = ffn_conv_w.reshape(depth, CONV_W, n_cw, MXU_COLS).transpose(0, 2, 1, 3)
    f_cb = ffn_conv_b.reshape(depth, n_cw, 1, MXU_COLS)

    for i in range(depth):
        need_ctx = i < depth - 1
        in_place = i > 0
        kind, j = i % N_MIXERS, i // N_MIXERS
        g1 = norm_g[i, 0].reshape(1, d)
        g2 = norm_g[i, 1].reshape(1, d)

        if kind == 0:
            w_all = ga_w
            qcols = ga_sink.shape[1] * GA_HD
            kcols = (w_all.shape[2] - qcols) // 2
            segments = ((qcols, "norm_rope_128", 0, GA_HD ** -0.5 * LOG2E), (kcols, "norm_rope_128", 1, 1.0),
                        (kcols, "plain", 0, 1.0))
            cos_t, sin_t, gains = ax_cos, ax_sin, ga_qk_norm[j]
        elif kind == 1:
            w_all = rt_w
            vcols = (w_all.shape[2] - 2 * d) // 3
            segments = ((d, "rope_256", 0, rt_dk ** -0.5), (d, "rope_256", 0, 1.0),
                        (vcols, "plain", 0, 1.0), (2 * vcols, "silu", 0, 1.0))
            cos_t, sin_t, gains = ln_cos, ln_sin, ones2
        else:
            w_all = df_w
            qcols = w_all.shape[2] // 3
            segments = ((qcols, "norm_rope_128", 0, DF_HD ** -0.5 * LOG2E), (qcols, "norm_rope_128", 1, 1.0),
                        (qcols, "plain", 0, 1.0))
            cos_t, sin_t, gains = ax_cos, ax_sin, df_qk_norm[j]

        proj = {"lat": _proj(h_lat, g1, mods, i, lat, w_all, j, segments, cos_t, sin_t, gains),
                "ctx": _proj(h_ctx, g1, mods, i, cst, w_all, j, segments, cos_t, sin_t, gains)}

        if kind == 0:
            y_lat, y_ctx = _ga_attention(proj["lat"], proj["ctx"], ga_sink[j], batch, seq, n_ctx, need_ctx)
            h_lat = _oproj(y_lat, ga_o, j, h_lat, mods, i, lat, in_place)
            if need_ctx:
                h_ctx = _oproj(y_ctx, ga_o, j, h_ctx, mods, i, cst, in_place)
        elif kind == 1:
            log_gamma = jax.nn.log_sigmoid(rt_decay[j].astype(F32))
            y_lat, y_ctx = _retention(proj["lat"], proj["ctx"], log_gamma, rt_gn[j], batch, seq, n_ctx)
            h_lat = _oproj_rt(y_lat[0], y_lat[1], proj["lat"], rt_o, j, h_lat, mods, i, lat, in_place)
            if need_ctx:
                h_ctx = _oproj_rt(y_ctx[0], y_ctx[1], proj["ctx"], rt_o, j, h_ctx, mods, i, cst, in_place)
        else:
            lambda_init = 0.8 - 0.6 * math.exp(-0.3 * i)
            y_lat, y_ctx = _df_attention(proj["lat"], proj["ctx"], df_lambda[j], df_subln[j], batch, seq, n_ctx,
                                         lambda_init, need_ctx)
            h_lat = _oproj(y_lat, df_o, j, h_lat, mods, i, lat, in_place)
            if need_ctx:
                h_ctx = _oproj(y_ctx, df_o, j, h_ctx, mods, i, cst, in_place)

        h_lat = _ffn(h_lat, g2, mods, i, lat, f_in, f_cw, f_cb, f_out, seq)
        if need_ctx:
            h_ctx = _ffn(h_ctx, g2, mods, i, cst, f_in, f_cw, f_cb, f_out, n_ctx)

    return h_lat.reshape(batch, seq, d)
```

```python
import functools
import math
from typing import Callable, NamedTuple

import jax
import jax.numpy as jnp
from jax import lax
from jax.experimental import pallas as pl
from jax.experimental.pallas import tpu as pltpu

F32 = jnp.float32
BF16 = jnp.bfloat16

EPS = 1e-6
NEG_INF = -1e30
LOG2E = math.log2(math.e)
ROPE_THETA = 10000.0
GRID_W = 64
WINDOW = 128
N_MIXERS = 3

GA_HD = 128
GA_GROUP = 4
RT_HEADS = 8
DF_HD = 128
CONV_W = 3

LANES = 128
MXU_COLS = 256
BF16_ROWS = 16
ONES_ROWS = BF16_ROWS
MOD_ROWS = 8
VMEM_LIMIT = 52 * 1024 * 1024


class _Stream(NamedTuple):
    group: int
    mod_row: Callable
    tab_block: Callable
    rope: bool


def _pick(n, candidates):
    for c in candidates:
        if n % c == 0:
            return c
    raise ValueError(f"no tile size in {candidates} divides {n}")


def _params(*sem):
    return pltpu.CompilerParams(dimension_semantics=sem, vmem_limit_bytes=VMEM_LIMIT)


def _silu(x):
    return x * (1.0 / (1.0 + jnp.exp(-x)))


def _dot(a, b):
    return jnp.dot(a, b, preferred_element_type=F32)


def _dot_nt(a, b):
    return lax.dot_general(a, b, (((1,), (1,)), ((), ())), preferred_element_type=F32)


def _dot_tn(a, b):
    return lax.dot_general(a, b, (((0,), (0,)), ((), ())), preferred_element_type=F32)


def _adaln_kernel(c_ref, w_ref, b_ref, o_ref):
    s = _silu(c_ref[...]).astype(BF16)
    o_ref[...] = _dot(s, w_ref[...].astype(BF16)) + b_ref[...]


def _adaln_table(cond, mod_w, mod_b):
    depth, d, _ = mod_w.shape
    bn = _pick(d, (1024, 512, 256, 128))
    nj = d // bn
    return pl.pallas_call(
        _adaln_kernel,
        grid=(depth, 6, nj),
        in_specs=[
            pl.BlockSpec((MOD_ROWS, d), lambda l, k, j: (0, 0)),
            pl.BlockSpec((None, d, bn), lambda l, k, j: (l, 0, k * nj + j)),
            pl.BlockSpec((None, 1, bn), lambda l, k, j: (l, 0, k * nj + j)),
        ],
        out_specs=pl.BlockSpec((None, None, MOD_ROWS, bn), lambda l, k, j: (l, k, 0, j)),
        out_shape=jax.ShapeDtypeStruct((depth, 6, MOD_ROWS, d), F32),
        compiler_params=_params("parallel", "parallel", "parallel"),
        name="adaln_table",
    )(cond, mod_w, mod_b.reshape(depth, 1, 6 * d))


def _mod_spec(layer, slot, row_fn, d):
    return pl.BlockSpec((None, None, None, 1, d), lambda i, j: (layer, slot, row_fn(i), 0, 0))


def _mod_spec_cols(layer, slot, row_fn, bn):
    return pl.BlockSpec((None, None, None, 1, bn), lambda i, j: (layer, slot, row_fn(i), 0, j))


def _norm_mod(x, g, shift, scale, mean_mat=None):
    if mean_mat is None:
        r = lax.rsqrt(jnp.mean(x * x, axis=-1, keepdims=True) + EPS)
    else:
        ms = _dot((x * x).astype(BF16), mean_mat)
        r = jnp.concatenate([lax.rsqrt(ms + EPS)] * (x.shape[1] // LANES), axis=1)
    return x * r * (g * (1.0 + scale)) + shift


def _mean_mat(d):
    return jnp.full((d, LANES), 1.0 / d, BF16)


def _proj_kernel(h_ref, g_ref, sh_ref, sc_ref, mm_ref, w_ref, cos_ref, sin_ref, gain_ref, seg_ref, o_ref, a_scr, acc_scr,
                 *, plan, rope):
    j = pl.program_id(1)
    sub = acc_scr.shape[2]
    n_sub = o_ref.shape[1] // sub

    @pl.when(j == 0)
    def _():
        a_scr[...] = _norm_mod(h_ref[...], g_ref[...], sh_ref[...], sc_ref[...], mm_ref[...]).astype(BF16)

    def epilogue(s, kind, arg, scale):
        base = s * sub
        if kind == "plain":
            o_ref[:, base:base + sub] = acc_scr[s].astype(o_ref.dtype)
        elif kind == "silu":
            o_ref[:, base:base + sub] = _silu(acc_scr[s]).astype(o_ref.dtype)
        elif kind == "norm_rope_128":
            x = acc_scr[s]
            ms = _dot((x * x).astype(BF16), seg_ref[...])
            y = x * lax.rsqrt(ms + EPS) * (gain_ref[arg:arg + 1, :] * scale)
            for t in range(0, sub, LANES):
                yt = y[:, t:t + LANES]
                if rope:
                    yt = yt * cos_ref[...] + pltpu.roll(yt, LANES // 2, 1) * sin_ref[...]
                o_ref[:, base + t:base + t + LANES] = yt.astype(o_ref.dtype)
        else:
            assert kind == "rope_256" and sub == 2 * LANES
            x1 = acc_scr[s, :, 0:LANES] * scale
            x2 = acc_scr[s, :, LANES:sub] * scale
            if rope:
                c, sn = cos_ref[...], sin_ref[...]
                x1, x2 = x1 * c - x2 * sn, x1 * sn + x2 * c
            o_ref[:, base:base + LANES] = x1.astype(o_ref.dtype)
            o_ref[:, base + LANES:base + sub] = x2.astype(o_ref.dtype)

    for lo, hi, kinds in plan:
        @pl.when((j >= lo) & (j < hi))
        def _(kinds=kinds):
            a = a_scr[...]
            for s in range(n_sub):
                acc_scr[s] = _dot(a, w_ref[:, s * sub:(s + 1) * sub])
            for s, (kind, arg, scale) in enumerate(kinds):
                epilogue(s, kind, arg, scale)


def _proj(h, norm_g, mods, layer, st, w_all, w_idx, segments, cos_t, sin_t, gains):
    m, d = h.shape
    n = w_all.shape[2]
    bm = _pick(st.group, (1024, 512, 256))
    bn = _pick(n, (1024, 512, 256))
    sub = MXU_COLS
    n_sub = bn // sub
    per_sub = []
    for width, kind, arg, scale in segments:
        assert width % sub == 0
        per_sub += [(kind, arg, scale)] * (width // sub)
    assert len(per_sub) * sub == n
    tiles = [tuple(per_sub[t * n_sub:(t + 1) * n_sub]) for t in range(n // bn)]
    plan, lo = [], 0
    for t in range(1, len(tiles) + 1):
        if t == len(tiles) or tiles[t] != tiles[lo]:
            plan.append((lo, t, tiles[lo]))
            lo = t
    row_fn, tab_fn = st.mod_row, st.tab_block
    kernel = functools.partial(_proj_kernel, plan=tuple(plan), rope=st.rope)
    head = jnp.arange(sub) // LANES
    seg_mean = jnp.where(head[:, None] == head[None, :], 1.0 / LANES, 0.0).astype(BF16)
    gains = jnp.tile(gains, (1, sub // LANES))
    return pl.pallas_call(
        kernel,
        grid=(m // bm, n // bn),
        in_specs=[
            pl.BlockSpec((bm, d), lambda i, j: (i, 0)),
            pl.BlockSpec((1, d), lambda i, j: (0, 0)),
            _mod_spec(layer, 0, lambda i: row_fn(i, bm), d),
            _mod_spec(layer, 1, lambda i: row_fn(i, bm), d),
            pl.BlockSpec((d, LANES), lambda i, j: (0, 0)),
            pl.BlockSpec((None, d, bn), lambda i, j: (w_idx, 0, j)),
            pl.BlockSpec((bm, LANES), lambda i, j: (tab_fn(i, bm), 0)),
            pl.BlockSpec((bm, LANES), lambda i, j: (tab_fn(i, bm), 0)),
            pl.BlockSpec(gains.shape, lambda i, j: (0, 0)),
            pl.BlockSpec((sub, sub), lambda i, j: (0, 0)),
        ],
        out_specs=pl.BlockSpec((bm, bn), lambda i, j: (i, j)),
        out_shape=jax.ShapeDtypeStruct((m, n), BF16),
        scratch_shapes=[pltpu.VMEM((bm, d), BF16), pltpu.VMEM((n_sub, bm, sub), F32)],
        compiler_params=_params("parallel", "arbitrary"),
        name="proj",
    )(h, norm_g, mods, mods, _mean_mat(d), w_all, cos_t, sin_t, gains, seg_mean)


def _oproj_kernel(y_ref, w_ref, h_ref, gate_ref, o_ref):
    o_ref[...] = h_ref[...] + gate_ref[...] * _dot(y_ref[...], w_ref[...])


def _oproj_rt_kernel(yf_ref, yb_ref, sf_ref, sb_ref, w_ref, h_ref, gate_ref, o_ref):
    c = pl.program_id(1)

    @pl.when(c == 0)
    def _():
        o_ref[...] = jnp.zeros_like(o_ref)

    ck = w_ref.shape[0]
    sub = min(ck, MXU_COLS)
    part = None
    for s in range(0, ck, sub):
        cols = slice(s, s + sub)
        z = sf_ref[:, cols] * yf_ref[:, cols] + sb_ref[:, cols] * yb_ref[:, cols]
        d = _dot(z, w_ref[cols, :])
        part = d if part is None else part + d
    o_ref[...] += part

    @pl.when(c == pl.num_programs(1) - 1)
    def _():
        o_ref[...] = h_ref[...] + gate_ref[...] * o_ref[...]


def _oproj(y, w_all, w_idx, h, mods, layer, st, in_place):
    m, k = y.shape
    d = w_all.shape[2]
    bm = _pick(st.group, (1024, 512, 256))
    row_fn = st.mod_row
    bn = _pick(d, (1024, 512, 256, 128))
    return pl.pallas_call(
        _oproj_kernel,
        grid=(m // bm, d // bn),
        in_specs=[
            pl.BlockSpec((bm, k), lambda i, j: (i, 0)),
            pl.BlockSpec((None, k, bn), lambda i, j: (w_idx, 0, j)),
            pl.BlockSpec((bm, bn), lambda i, j: (i, j)),
            _mod_spec_cols(layer, 2, lambda i: row_fn(i, bm), bn),
        ],
        out_specs=pl.BlockSpec((bm, bn), lambda i, j: (i, j)),
        out_shape=jax.ShapeDtypeStruct((m, d), F32),
        input_output_aliases={2: 0} if in_place else {},
        compiler_params=_params("parallel", "arbitrary"),
        name="oproj",
    )(y, w_all, h, mods)


def _oproj_rt(yf, yb, proj, w_all, w_idx, h, mods, layer, st, in_place):
    m, v = yf.shape
    d = w_all.shape[2]
    bm = _pick(st.group, (512, 256))
    row_fn = st.mod_row
    ck = _pick(v, (1024, 512, 256, 128))
    nc = v // ck
    gcol = (proj.shape[1] // v - 2) * nc
    return pl.pallas_call(
        _oproj_rt_kernel,
        grid=(m // bm, nc),
        in_specs=[
            pl.BlockSpec((bm, ck), lambda i, c: (i, c)),
            pl.BlockSpec((bm, ck), lambda i, c: (i, c)),
            pl.BlockSpec((bm, ck), lambda i, c: (i, gcol + c)),
            pl.BlockSpec((bm, ck), lambda i, c: (i, gcol + nc + c)),
            pl.BlockSpec((None, ck, d), lambda i, c: (w_idx, c, 0)),
            pl.BlockSpec((bm, d), lambda i, c: (i, 0)),
            _mod_spec(layer, 2, lambda i: row_fn(i, bm), d),
        ],
        out_specs=pl.BlockSpec((bm, d), lambda i, c: (i, 0)),
        out_shape=jax.ShapeDtypeStruct((m, d), F32),
        input_output_aliases={5: 0} if in_place else {},
        compiler_params=_params("parallel", "arbitrary"),
        name="oproj_rt",
    )(yf, yb, proj, proj, w_all, h, mods)


def _ffn_kernel(h_ref, hp_ref, hn_ref, g_ref, sh_ref, sc_ref, gate_ref, wa_ref, wb_ref, cw_ref, cb_ref,
                wo_lo, wo_hi, wo_top, o_ref, a_scr, u_even, u_odd_a, u_odd_b, *, bm, seq_len, steps):
    i = pl.program_id(0)
    j = pl.program_id(1)
    halo = BF16_ROWS
    sub = u_even.shape[2]
    nsc = 2 * steps

    def up(half, u):
        cols = slice(half * sub, (half + 1) * sub)
        a = a_scr[...]
        u[0] = _dot(a, wa_ref[:, cols])
        u[1] = _dot(a, wb_ref[:, cols])

    def gate(u, k):
        if bm > seq_len:
            pos = lax.broadcasted_iota(jnp.int32, (bm, sub), 0) % seq_len
            first, final = pos == 0, pos == seq_len - 1

        def conv(idx, kk):
            cw = cw_ref[kk]
            prev = u[idx, halo - 1:halo - 1 + bm, :]
            cur = u[idx, halo:halo + bm, :]
            nxt = u[idx, halo + 1:halo + 1 + bm, :]
            if bm > seq_len:
                prev, nxt = jnp.where(first, 0.0, prev), jnp.where(final, 0.0, nxt)
            return cb_ref[kk] + prev * cw[0:1, :] + cur * cw[1:2, :] + nxt * cw[2:3, :]

        return (_silu(conv(0, k)) * conv(1, nsc + k)).astype(BF16)

    @pl.when(j == 0)
    def _():
        def nm(x):
            return _norm_mod(x, g_ref[...], sh_ref[...], sc_ref[...])

        keep_prev = jnp.where((i * bm) % seq_len != 0, 1.0, 0.0)
        keep_next = jnp.where(((i + 1) * bm) % seq_len != 0, 1.0, 0.0)
        a_scr[0:halo, :] = (nm(hp_ref[...]) * keep_prev).astype(BF16)
        a_scr[halo:halo + bm, :] = nm(h_ref[...]).astype(BF16)
        a_scr[halo + bm:, :] = (nm(hn_ref[...]) * keep_next).astype(BF16)
        o_ref[...] = jnp.zeros_like(o_ref)
        up(0, u_even)
        up(1, u_odd_a)
        o_ref[...] += _dot(gate(u_even, 0), wo_hi[...])

    def middle(u_new, u_old):
        up(0, u_even)
        up(1, u_new)
        g = jnp.concatenate([gate(u_old, 2 * j - 1), gate(u_even, 2 * j)], axis=1)
        w = jnp.concatenate([wo_lo[...], wo_hi[...]], axis=0)
        o_ref[...] += _dot(g, w)

    inner = (j > 0) & (j < steps - 1)
    pl.when(inner & (j % 2 == 1))(lambda: middle(u_odd_b, u_odd_a))
    pl.when(inner & (j % 2 == 0))(lambda: middle(u_odd_a, u_odd_b))

    @pl.when(j == steps - 1)
    def _():
        u_new, u_old = (u_odd_a, u_odd_b) if (steps - 1) % 2 == 0 else (u_odd_b, u_odd_a)
        middle(u_new, u_old)
        o_ref[...] += _dot(gate(u_new, nsc - 1), wo_top[...])
        o_ref[...] = h_ref[...] + gate_ref[...] * o_ref[...]


def _ffn(h, norm_g, mods, layer, st, w_in, conv_w, conv_b, w_out, seq_len):
    m, d = h.shape
    dff = w_out.shape[1]
    bm = _pick(m, (512, 256))
    assert bm % seq_len == 0 or seq_len % bm == 0
    sub = MXU_COLS
    nsc = dff // sub
    assert dff % (2 * sub) == 0 and conv_w.shape[1:] == (2 * nsc, CONV_W, sub)
    steps = nsc // 2
    assert steps >= 2
    halo = BF16_ROWS
    hb = bm // halo
    last = m // halo - 1
    kernel = functools.partial(_ffn_kernel, bm=bm, seq_len=seq_len, steps=steps)
    mrow = lambda i: st.mod_row(i, bm)
    wo_rows = (lambda j: jnp.maximum(2 * j - 1, 0), lambda j: 2 * j, lambda j: nsc - 1)
    return pl.pallas_call(
        kernel,
        grid=(m // bm, steps),
        in_specs=[
            pl.BlockSpec((bm, d), lambda i, j: (i, 0)),
            pl.BlockSpec((halo, d), lambda i, j: (jnp.maximum(i * hb - 1, 0), 0)),
            pl.BlockSpec((halo, d), lambda i, j: (jnp.minimum((i + 1) * hb, last), 0)),
            pl.BlockSpec((1, d), lambda i, j: (0, 0)),
            _mod_spec(layer, 3, mrow, d),
            _mod_spec(layer, 4, mrow, d),
            _mod_spec(layer, 5, mrow, d),
            pl.BlockSpec((None, d, 2 * sub), lambda i, j: (layer, 0, j)),
            pl.BlockSpec((None, d, 2 * sub), lambda i, j: (layer, 0, steps + j)),
            pl.BlockSpec((None, 2 * nsc, CONV_W, sub), lambda i, j: (layer, 0, 0, 0)),
            pl.BlockSpec((None, 2 * nsc, 1, sub), lambda i, j: (layer, 0, 0, 0)),
        ] + [pl.BlockSpec((None, sub, d), lambda i, j, k=k: (layer, k(j), 0)) for k in wo_rows],
        out_specs=pl.BlockSpec((bm, d), lambda i, j: (i, 0)),
        out_shape=jax.ShapeDtypeStruct((m, d), F32),
        scratch_shapes=[pltpu.VMEM((bm + 2 * halo, d), BF16),
                        pltpu.VMEM((2, bm + 2 * halo, sub), F32),
                        pltpu.VMEM((2, bm + 2 * halo, sub), F32),
                        pltpu.VMEM((2, bm + 2 * halo, sub), F32)],
        compiler_params=_params("parallel", "arbitrary"),
        name="conv_ffn",
    )(h, h, h, norm_g, mods, mods, mods, w_in, w_in, conv_w, conv_b, w_out, w_out, w_out)


def _ga_kernel(sink_ref, q_ref, kvc_ref, *rest, seq, n_kv, local):
    if local:
        kvo_ref, kvp_ref, kvn_ref, o_ref, s_a, s_b = rest
    else:
        o_ref, s_a, s_b = rest
    hd = GA_HD
    kcols = n_kv * hd
    bq = q_ref.shape[0]
    sq = s_a.shape[1]
    n_ctx = kvc_ref.shape[0]
    bufs = (s_a, s_b)

    def band(u):
        if not local:
            return [], 0
        lo, hi = u * sq - WINDOW, (u + 1) * sq + WINDOW
        pieces = []
        if lo < 0:
            pieces.append((kvp_ref, WINDOW + lo, -lo))
        pieces.append((kvo_ref, max(lo, 0), min(hi, bq) - max(lo, 0)))
        if hi > bq:
            pieces.append((kvn_ref, 0, hi - bq))
        return pieces, lo

    def bias(u):
        pieces, lo = band(u)
        n_loc = sum(p[2] for p in pieces)
        kl = lax.broadcasted_iota(jnp.int32, (n_loc, sq), 0) + lo
        ql = lax.broadcasted_iota(jnp.int32, (n_loc, sq), 1) + u * sq
        pos = pl.program_id(1) * bq + kl
        ok = (jnp.abs(ql - kl) <= WINDOW) & (pos >= 0) & (pos < seq)
        return jnp.concatenate([jnp.where(ok, 0.0, NEG_INF).astype(F32), jnp.zeros((n_ctx, sq), F32)], axis=0)

    def keys_values(u, n):
        pieces, _ = band(u)
        ks, vs = slice(n * hd, (n + 1) * hd), slice(kcols + n * hd, kcols + (n + 1) * hd)
        k_all = jnp.concatenate([r[a:a + c, ks] for r, a, c in pieces] + [kvc_ref[:, ks]], axis=0)
        v_all = jnp.concatenate([r[a:a + c, vs] for r, a, c in pieces] + [kvc_ref[:, vs]], axis=0)
        v_t = jnp.concatenate([v_all.astype(F32).T.astype(BF16), jnp.ones((ONES_ROWS, v_all.shape[0]), BF16)], axis=0)
        return k_all, v_t

    chains = [(u, n, g) for u in range(bq // sq) for n in range(n_kv) for g in range(GA_GROUP)]
    cache = {}

    def operands(u, n):
        if (u, n) not in cache:
            cache.clear()
            cache[(u, n)] = keys_values(u, n)
        return cache[(u, n)]

    biases = {}

    def scores(idx):
        u, n, g = chains[idx]
        head = n * GA_GROUP + g
        k_all, _ = operands(u, n)
        s = _dot_nt(k_all, q_ref[u * sq:(u + 1) * sq, head * hd:(head + 1) * hd])
        if local:
            if u not in biases:
                biases[u] = bias(u)
            s = s + biases[u]
        bufs[idx % 2][...] = s

    scores(0)
    for idx, (u, n, g) in enumerate(chains):
        _, v_t = operands(u, n)
        if idx + 1 < len(chains):
            scores(idx + 1)
        buf = bufs[idx % 2]
        head = n * GA_GROUP + g
        sink = sink_ref[head] * LOG2E
        m = jnp.maximum(jnp.max(buf[...], axis=0, keepdims=True), sink)
        p = jnp.exp2((buf[...] - m).astype(BF16))
        ov = _dot(v_t, p)
        o_t = ov[0:hd, :] / (ov[hd:hd + 1, :] + jnp.exp2(sink - m))
        o_ref[u * sq:(u + 1) * sq, head * hd:(head + 1) * hd] = o_t.T.astype(o_ref.dtype)


def _ga_attention(qkv_lat, qkv_ctx, sink, batch, seq, n_ctx, need_ctx):
    n_heads = sink.shape[0]
    qcols = n_heads * GA_HD
    n_kv = n_heads // GA_GROUP
    kvw = 2 * n_kv * GA_HD
    kvblk = qcols // kvw
    assert qcols % kvw == 0
    bq = _pick(seq, (512, 256, 128))
    sq = min(bq, MXU_COLS)
    nq = seq // bq
    wb = bq // WINDOW
    last_w = batch * seq // WINDOW - 1
    smem = pl.BlockSpec(memory_space=pltpu.SMEM)
    lat = pl.pallas_call(
        functools.partial(_ga_kernel, seq=seq, n_kv=n_kv, local=True),
        grid=(batch, nq),
        in_specs=[
            smem,
            pl.BlockSpec((bq, qcols), lambda b, i: (b * nq + i, 0)),
            pl.BlockSpec((n_ctx, kvw), lambda b, i: (b, kvblk)),
            pl.BlockSpec((bq, kvw), lambda b, i: (b * nq + i, kvblk)),
            pl.BlockSpec((WINDOW, kvw), lambda b, i: (jnp.maximum((b * nq + i) * wb - 1, 0), kvblk)),
            pl.BlockSpec((WINDOW, kvw), lambda b, i: (jnp.minimum((b * nq + i + 1) * wb, last_w), kvblk)),
        ],
        out_specs=pl.BlockSpec((bq, qcols), lambda b, i: (b * nq + i, 0)),
        out_shape=jax.ShapeDtypeStruct((batch * seq, qcols), BF16),
        scratch_shapes=[pltpu.VMEM((sq + 2 * WINDOW + n_ctx, sq), F32)] * 2,
        compiler_params=_params("parallel", "parallel"),
        name="ga_attention_latent",
    )(sink, qkv_lat, qkv_ctx, qkv_lat, qkv_lat, qkv_lat)
    ctx = None
    if need_ctx:
        cq = min(n_ctx, MXU_COLS)
        ctx = pl.pallas_call(
            functools.partial(_ga_kernel, seq=n_ctx, n_kv=n_kv, local=False),
            grid=(batch,),
            in_specs=[
                smem,
                pl.BlockSpec((n_ctx, qcols), lambda b: (b, 0)),
                pl.BlockSpec((n_ctx, kvw), lambda b: (b, kvblk)),
            ],
            out_specs=pl.BlockSpec((n_ctx, qcols), lambda b: (b, 0)),
            out_shape=jax.ShapeDtypeStruct((batch * n_ctx, qcols), BF16),
            scratch_shapes=[pltpu.VMEM((n_ctx, cq), F32)] * 2,
            compiler_params=_params("parallel"),
            name="ga_attention_context",
        )(sink, qkv_ctx, qkv_ctx)
    return lat, ctx


def _df_kernel(lam_ref, g_ref, q_ref, *rest, lambda_init, chunks, first_q_axis):
    n_seg = len(chunks)
    kv_refs = rest[:2 * n_seg]
    o_ref = rest[2 * n_seg]
    vt_scrs = rest[2 * n_seg + 1:2 * n_seg + 1 + n_seg]
    acc_scr, s_a, s_b = rest[-3:]
    hd = DF_HD
    bq = q_ref.shape[0]
    lam = lam_ref[...]
    lmbda = (jnp.exp(jnp.sum(lam[0:1] * lam[1:2], axis=-1, keepdims=True))
             - jnp.exp(jnp.sum(lam[2:3] * lam[3:4], axis=-1, keepdims=True)) + lambda_init)

    def transpose_values():
        for seg, ck in enumerate(chunks):
            v_ref, vt = kv_refs[2 * seg + 1], vt_scrs[seg]
            for c in range(v_ref.shape[0] // ck):
                vt[c] = v_ref[c * ck:(c + 1) * ck, :].astype(F32).T.astype(BF16)

    if first_q_axis is None:
        transpose_values()
    else:
        pl.when(pl.program_id(first_q_axis) == 0)(transpose_values)

    qs = (q_ref[:, 0:hd], q_ref[:, hd:2 * hd])
    acc_scr[...] = jnp.zeros_like(acc_scr)
    stat0 = (jnp.full((1, bq), NEG_INF, F32), jnp.zeros((1, bq), F32))
    carry = (stat0, stat0)

    def scores(k_ref, c, ck, dst):
        kblk = k_ref[c * ck:(c + 1) * ck, :]
        for r in range(2):
            dst[r] = _dot_nt(kblk[:, r * hd:(r + 1) * hd], qs[r])

    def update(src, v_t, carry):
        new = []
        for r in range(2):
            m_old, l_old = carry[r]
            m_new = jnp.maximum(m_old, jnp.max(src[r], axis=0, keepdims=True))
            alpha = jnp.exp2(m_old - m_new)
            p = jnp.exp2(src[r] - m_new)
            l_new = alpha * l_old + jnp.sum(p, axis=0, keepdims=True)
            acc_scr[r] = alpha * acc_scr[r] + _dot(v_t, p.astype(BF16))
            new.append((m_new, l_new))
        return tuple(new)

    items = [(kv_refs[2 * seg], c, chunks[seg], vt_scrs[seg])
             for seg in range(n_seg) for c in range(kv_refs[2 * seg].shape[0] // chunks[seg])]
    bufs = (s_a, s_b)

    def view(buf, ck):
        return buf if ck == buf.shape[1] else buf.at[:, 0:ck, :]

    scores(items[0][0], items[0][1], items[0][2], view(bufs[0], items[0][2]))
    for idx, (_, c, ck, vt) in enumerate(items):
        if idx + 1 < len(items):
            k_next, c_next, ck_next, _ = items[idx + 1]
            scores(k_next, c_next, ck_next, view(bufs[(idx + 1) % 2], ck_next))
        carry = update(view(bufs[idx % 2], ck), vt[c], carry)
    (_, l0), (_, l1) = carry
    o = (acc_scr[0] / l0 - lmbda * (acc_scr[1] / l1)).T
    ms = jnp.mean(o * o, axis=-1, keepdims=True)
    y = o * lax.rsqrt(ms + EPS) * g_ref[...] * (1.0 - lambda_init)
    o_ref[...] = y.astype(o_ref.dtype)


def _df_attention(qkv_lat, qkv_ctx, lam, subln_g, batch, seq, n_ctx, lambda_init, need_ctx):
    hw = 2 * DF_HD
    n_heads = qkv_lat.shape[1] // (3 * hw)
    bq = _pick(seq, (512, 256, 128))
    nq = seq // bq
    ck_lat = _pick(seq, (512, 256, 128))
    ck_ctx = _pick(n_ctx, (512, 256, 128))
    g2 = subln_g.reshape(1, hw)
    const = lambda shape: pl.BlockSpec(shape, lambda *_: (0,) * len(shape))
    vt_ctx = pltpu.VMEM((n_ctx // ck_ctx, hw, ck_ctx), BF16)
    vt_lat = pltpu.VMEM((seq // ck_lat, hw, ck_lat), BF16)
    lat = pl.pallas_call(
        functools.partial(_df_kernel, lambda_init=lambda_init, chunks=(ck_ctx, ck_lat), first_q_axis=2),
        grid=(batch, n_heads, nq),
        in_specs=[
            const(lam.shape),
            const((1, hw)),
            pl.BlockSpec((bq, hw), lambda b, h, i: (b * nq + i, h)),
            pl.BlockSpec((n_ctx, hw), lambda b, h, i: (b, n_heads + h)),
            pl.BlockSpec((n_ctx, hw), lambda b, h, i: (b, 2 * n_heads + h)),
            pl.BlockSpec((seq, hw), lambda b, h, i: (b, n_heads + h)),
            pl.BlockSpec((seq, hw), lambda b, h, i: (b, 2 * n_heads + h)),
        ],
        out_specs=pl.BlockSpec((bq, hw), lambda b, h, i: (b * nq + i, h)),
        out_shape=jax.ShapeDtypeStruct((batch * seq, n_heads * hw), BF16),
        scratch_shapes=[vt_ctx, vt_lat, pltpu.VMEM((2, hw, bq), F32)]
        + [pltpu.VMEM((2, max(ck_lat, ck_ctx), bq), F32)] * 2,
        compiler_params=_params("parallel", "parallel", "arbitrary"),
        name="df_attention_latent",
    )(lam, g2, qkv_lat, qkv_ctx, qkv_ctx, qkv_lat, qkv_lat)
    ctx = None
    if need_ctx:
        ctx = pl.pallas_call(
            functools.partial(_df_kernel, lambda_init=lambda_init, chunks=(ck_ctx,), first_q_axis=None),
            grid=(batch, n_heads),
            in_specs=[
                const(lam.shape),
                const((1, hw)),
                pl.BlockSpec((n_ctx, hw), lambda b, h: (b, h)),
                pl.BlockSpec((n_ctx, hw), lambda b, h: (b, n_heads + h)),
                pl.BlockSpec((n_ctx, hw), lambda b, h: (b, 2 * n_heads + h)),
            ],
            out_specs=pl.BlockSpec((n_ctx, hw), lambda b, h: (b, h)),
            out_shape=jax.ShapeDtypeStruct((batch * n_ctx, n_heads * hw), BF16),
            scratch_shapes=[vt_ctx, pltpu.VMEM((2, hw, n_ctx), F32)]
            + [pltpu.VMEM((2, ck_ctx, n_ctx), F32)] * 2,
            compiler_params=_params("parallel", "parallel"),
            name="df_attention_context",
        )(lam, g2, qkv_ctx, qkv_ctx, qkv_ctx)
    return lat, ctx


def _rt_kernel(lg_ref, gn_ref, *refs, n_heads, chunk, ctx_steps):
    ctx_in, lat_in = refs[0:6], refs[6:12]
    ctx_out, lat_out = refs[12:14], refs[14:16]
    st_scr, dec_scr = refs[16], refs[17]
    head0 = pl.program_id(1) * n_heads
    s_idx = pl.program_id(2)
    dk = ctx_in[0].shape[1] // n_heads
    dv = ctx_in[2].shape[1] // n_heads

    @pl.when(s_idx == 0)
    def _():
        st_scr[...] = jnp.zeros_like(st_scr)
        row = lax.broadcasted_iota(jnp.int32, (chunk, chunk), 0).astype(F32)
        col = lax.broadcasted_iota(jnp.int32, (chunk, chunk), 1).astype(F32)
        for d, rel in enumerate((row - col, col - row)):
            for h in range(n_heads):
                lg = lg_ref[d, head0 + h]
                dec_scr[d, h] = jnp.where(rel >= 0, jnp.exp(lg * jnp.maximum(rel, 0.0)), 0.0)

    def step(ins, outs):
        qf_ref, kf_ref, vf_ref, qb_ref, kb_ref, vb_ref = ins
        of_ref, ob_ref = outs
        pos = lax.broadcasted_iota(jnp.int32, (chunk, 1), 0).astype(F32)
        dirs = (
            (qf_ref, kf_ref, vf_ref, of_ref, pos + 1.0, chunk - 1.0 - pos),
            (qb_ref, kb_ref, vb_ref, ob_ref, chunk - pos, pos),
        )
        for d, (q_ref, k_ref, v_ref, o_ref, q_pow, k_pow) in enumerate(dirs):
            for h in range(n_heads):
                lg = lg_ref[d, head0 + h]
                decay = dec_scr[d, h]
                q_decay = jnp.exp(lg * q_pow)
                k_decay = jnp.exp(lg * k_pow)
                chunk_decay = jnp.exp(lg * chunk)
                q = q_ref[:, h * dk:(h + 1) * dk]
                k = k_ref[:, h * dk:(h + 1) * dk]
                v = v_ref[:, h * dv:(h + 1) * dv]
                state = st_scr[d, h]
                inner = _dot_nt(q, k) * decay
                o = _dot(inner.astype(BF16), v) + _dot(q, state.astype(BF16)) * q_decay
                kd = (k.astype(F32) * k_decay).astype(BF16)
                st_scr[d, h] = state * chunk_decay + _dot_tn(kd, v)
                mu = jnp.mean(o, axis=-1, keepdims=True)
                dev = o - mu
                var = jnp.mean(dev * dev, axis=-1, keepdims=True)
                y = dev * lax.rsqrt(var + EPS) * gn_ref[d:d + 1, h * dv:(h + 1) * dv]
                o_ref[:, h * dv:(h + 1) * dv] = y.astype(o_ref.dtype)

    @pl.when(s_idx < ctx_steps)
    def _():
        step(ctx_in, ctx_out)

    @pl.when(s_idx >= ctx_steps)
    def _():
        step(lat_in, lat_out)


def _retention(proj_lat, proj_ctx, log_gamma, gn_g, batch, seq, n_ctx):
    n_heads = log_gamma.shape[1]
    qk = proj_lat.shape[1] // 8
    v = 2 * qk
    chunk = _pick(math.gcd(seq, n_ctx), (256, 128))
    ncc, nlc = n_ctx // chunk, seq // chunk
    dk, dv = qk // n_heads, v // n_heads
    groups = 2
    hpg = n_heads // groups
    qkw, vw = hpg * dk, hpg * dv

    def ctx_f(b, s):
        return b * ncc + jnp.minimum(s, ncc - 1)

    def ctx_b(b, s):
        return b * ncc + (ncc - 1 - jnp.minimum(s, ncc - 1))

    def lat_f(b, s):
        return b * nlc + jnp.maximum(s - ncc, 0)

    def lat_b(b, s):
        return b * nlc + (nlc - 1 - jnp.maximum(s - ncc, 0))

    def in_specs(row):
        return [
            pl.BlockSpec((chunk, qkw), lambda b, g, s: (row(b, s), g)),
            pl.BlockSpec((chunk, qkw), lambda b, g, s: (row(b, s), groups + g)),
            pl.BlockSpec((chunk, vw), lambda b, g, s: (row(b, s), groups + g)),
        ]

    def out_spec(row):
        return pl.BlockSpec((chunk, vw), lambda b, g, s: (row(b, s), g))

    out_ctx = jax.ShapeDtypeStruct((batch * n_ctx, v), BF16)
    out_lat = jax.ShapeDtypeStruct((batch * seq, v), BF16)
    yf_ctx, yb_ctx, yf_lat, yb_lat = pl.pallas_call(
        functools.partial(_rt_kernel, n_heads=hpg, chunk=chunk, ctx_steps=ncc),
        grid=(batch, groups, ncc + nlc),
        in_specs=[pl.BlockSpec(memory_space=pltpu.SMEM), pl.BlockSpec((2, vw), lambda b, g, s: (0, g))]
        + in_specs(ctx_f) + in_specs(ctx_b) + in_specs(lat_f) + in_specs(lat_b),
        out_specs=[out_spec(ctx_f), out_spec(ctx_b), out_spec(lat_f), out_spec(lat_b)],
        out_shape=[out_ctx, out_ctx, out_lat, out_lat],
        scratch_shapes=[pltpu.VMEM((2, hpg, dk, dv), F32), pltpu.VMEM((2, hpg, chunk, chunk), F32)],
        compiler_params=_params("parallel", "parallel", "arbitrary"),
        name="retention",
    )(log_gamma, gn_g, *([proj_ctx] * 6), *([proj_lat] * 6))
    return (yf_lat, yb_lat), (yf_ctx, yb_ctx)


def _axial_tables(rows_count, head_dim):
    rows = jnp.repeat(jnp.arange(rows_count), GRID_W).astype(F32)
    cols = jnp.tile(jnp.arange(GRID_W), rows_count).astype(F32)
    n_freq = head_dim // 4
    inv = ROPE_THETA ** (-jnp.arange(n_freq, dtype=F32) / n_freq)
    ang = jnp.concatenate([rows[:, None] * inv, cols[:, None] * inv], -1)
    cos, sin = jnp.cos(ang), jnp.sin(ang)
    return jnp.concatenate([cos, cos], -1), jnp.concatenate([-sin, sin], -1)


def _linear_tables(n_tokens, head_dim):
    half = head_dim // 2
    inv = ROPE_THETA ** (-jnp.arange(half, dtype=F32) / half)
    ang = jnp.arange(n_tokens, dtype=F32)[:, None] * inv
    return jnp.cos(ang), jnp.sin(ang)


def kernel(x, c, ctx, c_ctx, mod_w, mod_b, norm_g, ffn_w_in, ffn_conv_w, ffn_conv_b, ffn_w_out, ga_wqkv, ga_sink, ga_qk_norm, ga_wo, rt_w_in, rt_decay, rt_gn, rt_wo, df_wqkv, df_lambda, df_qk_norm, df_subln, df_wo):
    batch, seq, d = x.shape
    n_ctx = ctx.shape[1]
    depth = mod_w.shape[0]
    assert batch + 1 <= MOD_ROWS and seq % GRID_W == 0

    cond = jnp.zeros((MOD_ROWS, d), F32).at[0].set(c_ctx).at[1:batch + 1].set(c)
    mods = _adaln_table(cond, mod_w, mod_b).reshape(depth, 6, MOD_ROWS, 1, d)

    ax_cos, ax_sin = _axial_tables(seq // GRID_W, GA_HD)
    assert DF_HD == GA_HD
    rt_dk = d // RT_HEADS
    ln_cos, ln_sin = _linear_tables(seq, rt_dk)
    ones2 = jnp.ones((2, LANES), F32)

    h_lat = x.reshape(batch * seq, d)
    h_ctx = ctx.reshape(batch * n_ctx, d)

    lat = _Stream(seq, lambda i, bm: 1 + i // (seq // bm), lambda i, bm: i % (seq // bm), True)
    cst = _Stream(batch * n_ctx, lambda i, bm: 0, lambda i, bm: 0, False)

    ga_w, ga_o = ga_wqkv.astype(BF16), ga_wo.astype(BF16)
    rt_w, rt_o = rt_w_in.astype(BF16), rt_wo.astype(BF16)
    df_w, df_o = df_wqkv.astype(BF16), df_wo.astype(BF16)
    f_in, f_out = ffn_w_in.astype(BF16), ffn_w_out.astype(BF16)
    n_cw = ffn_conv_w.shape[2] // MXU_COLS
    f_cw = ffn_conv_w.reshape(depth, CONV_W, n_cw, MXU_COLS).transpose(0, 2, 1, 3)
    f_cb = ffn_conv_b.reshape(depth, n_cw, 1, MXU_COLS)

    for i in range(depth):
        need_ctx = i < depth - 1
        in_place = i > 0
        kind, j = i % N_MIXERS, i // N_MIXERS
        g1 = norm_g[i, 0].reshape(1, d)
        g2 = norm_g[i, 1].reshape(1, d)

        if kind == 0:
            w_all = ga_w
            qcols = ga_sink.shape[1] * GA_HD
            kcols = (w_all.shape[2] - qcols) // 2
            segments = ((qcols, "norm_rope_128", 0, GA_HD ** -0.5 * LOG2E), (kcols, "norm_rope_128", 1, 1.0),
                        (kcols, "plain", 0, 1.0))
            cos_t, sin_t, gains = ax_cos, ax_sin, ga_qk_norm[j]
        elif kind == 1:
            w_all = rt_w
            vcols = (w_all.shape[2] - 2 * d) // 3
            segments = ((d, "rope_256", 0, rt_dk ** -0.5), (d, "rope_256", 0, 1.0),
                        (vcols, "plain", 0, 1.0), (2 * vcols, "silu", 0, 1.0))
            cos_t, sin_t, gains = ln_cos, ln_sin, ones2
        else:
            w_all = df_w
            qcols = w_all.shape[2] // 3
            segments = ((qcols, "norm_rope_128", 0, DF_HD ** -0.5 * LOG2E), (qcols, "norm_rope_128", 1, 1.0),
                        (qcols, "plain", 0, 1.0))
            cos_t, sin_t, gains = ax_cos, ax_sin, df_qk_norm[j]

        proj = {"lat": _proj(h_lat, g1, mods, i, lat, w_all, j, segments, cos_t, sin_t, gains),
                "ctx": _proj(h_ctx, g1, mods, i, cst, w_all, j, segments, cos_t, sin_t, gains)}

        if kind == 0:
            y_lat, y_ctx = _ga_attention(proj["lat"], proj["ctx"], ga_sink[j], batch, seq, n_ctx, need_ctx)
            h_lat = _oproj(y_lat, ga_o, j, h_lat, mods, i, lat, in_place)
            if need_ctx:
                h_ctx = _oproj(y_ctx, ga_o, j, h_ctx, mods, i, cst, in_place)
        elif kind == 1:
            log_gamma = jax.nn.log_sigmoid(rt_decay[j].astype(F32))
            y_lat, y_ctx = _retention(proj["lat"], proj["ctx"], log_gamma, rt_gn[j], batch, seq, n_ctx)
            h_lat = _oproj_rt(y_lat[0], y_lat[1], proj["lat"], rt_o, j, h_lat, mods, i, lat, in_place)
            if need_ctx:
                h_ctx = _oproj_rt(y_ctx[0], y_ctx[1], proj["ctx"], rt_o, j, h_ctx, mods, i, cst, in_place)
        else:
            lambda_init = 0.8 - 0.6 * math.exp(-0.3 * i)
            y_lat, y_ctx = _df_attention(proj["lat"], proj["ctx"], df_lambda[j], df_subln[j], batch, seq, n_ctx,
                                         lambda_init, need_ctx)
            h_lat = _oproj(y_lat, df_o, j, h_lat, mods, i, lat, in_place)
            if need_ctx:
                h_ctx = _oproj(y_ctx, df_o, j, h_ctx, mods, i, cst, in_place)

        h_lat = _ffn(h_lat, g2, mods, i, lat, f_in, f_cw, f_cb, f_out, seq)
        if need_ctx:
            h_ctx = _ffn(h_ctx, g2, mods, i, cst, f_in, f_cw, f_cb, f_out, n_ctx)

    return h_lat.reshape(batch, seq, d)
```

```python
import functools
import math
from typing import Callable, NamedTuple

import jax
import jax.numpy as jnp
from jax import lax
from jax.experimental import pallas as pl
from jax.experimental.pallas import tpu as pltpu

F32 = jnp.float32
BF16 = jnp.bfloat16

EPS = 1e-6
NEG_INF = -1e30
LOG2E = math.log2(math.e)
ROPE_THETA = 10000.0
GRID_W = 64
WINDOW = 128
N_MIXERS = 3

GA_HD = 128
GA_GROUP = 4
RT_HEADS = 8
DF_HD = 128
CONV_W = 3

LANES = 128
MXU_COLS = 256
BF16_ROWS = 16
ONES_ROWS = BF16_ROWS
MOD_ROWS = 8
VMEM_LIMIT = 52 * 1024 * 1024


class _Stream(NamedTuple):
    group: int
    mod_row: Callable
    tab_block: Callable
    rope: bool


def _pick(n, candidates):
    for c in candidates:
        if n % c == 0:
            return c
    raise ValueError(f"no tile size in {candidates} divides {n}")


def _params(*sem):
    return pltpu.CompilerParams(dimension_semantics=sem, vmem_limit_bytes=VMEM_LIMIT)


def _silu(x):
    return x * (1.0 / (1.0 + jnp.exp(-x)))


def _dot(a, b):
    return jnp.dot(a, b, preferred_element_type=F32)


def _dot_nt(a, b):
    return lax.dot_general(a, b, (((1,), (1,)), ((), ())), preferred_element_type=F32)


def _dot_tn(a, b):
    return lax.dot_general(a, b, (((0,), (0,)), ((), ())), preferred_element_type=F32)


def _adaln_kernel(c_ref, w_ref, b_ref, o_ref):
    s = _silu(c_ref[...]).astype(BF16)
    o_ref[...] = _dot(s, w_ref[...].astype(BF16)) + b_ref[...]


def _adaln_table(cond, mod_w, mod_b):
    depth, d, _ = mod_w.shape
    bn = _pick(d, (1024, 512, 256, 128))
    nj = d // bn
    return pl.pallas_call(
        _adaln_kernel,
        grid=(depth, 6, nj),
        in_specs=[
            pl.BlockSpec((MOD_ROWS, d), lambda l, k, j: (0, 0)),
            pl.BlockSpec((None, d, bn), lambda l, k, j: (l, 0, k * nj + j)),
            pl.BlockSpec((None, 1, bn), lambda l, k, j: (l, 0, k * nj + j)),
        ],
        out_specs=pl.BlockSpec((None, None, MOD_ROWS, bn), lambda l, k, j: (l, k, 0, j)),
        out_shape=jax.ShapeDtypeStruct((depth, 6, MOD_ROWS, d), F32),
        compiler_params=_params("parallel", "parallel", "parallel"),
        name="adaln_table",
    )(cond, mod_w, mod_b.reshape(depth, 1, 6 * d))


def _mod_spec(layer, slot, row_fn, d):
    return pl.BlockSpec((None, None, None, 1, d), lambda i, j: (layer, slot, row_fn(i), 0, 0))


def _mod_spec_cols(layer, slot, row_fn, bn):
    return pl.BlockSpec((None, None, None, 1, bn), lambda i, j: (layer, slot, row_fn(i), 0, j))


def _norm_mod(x, g, shift, scale, mean_mat=None):
    if mean_mat is None:
        r = lax.rsqrt(jnp.mean(x * x, axis=-1, keepdims=True) + EPS)
    else:
        ms = _dot((x * x).astype(BF16), mean_mat)
        r = jnp.concatenate([lax.rsqrt(ms + EPS)] * (x.shape[1] // LANES), axis=1)
    return x * r * (g * (1.0 + scale)) + shift


def _mean_mat(d):
    return jnp.full((d, LANES), 1.0 / d, BF16)


def _proj_kernel(h_ref, g_ref, sh_ref, sc_ref, mm_ref, w_ref, cos_ref, sin_ref, gain_ref, seg_ref, o_ref, a_scr, acc_scr,
                 *, plan, rope):
    j = pl.program_id(1)
    sub = acc_scr.shape[2]
    n_sub = o_ref.shape[1] // sub

    @pl.when(j == 0)
    def _():
        a_scr[...] = _norm_mod(h_ref[...], g_ref[...], sh_ref[...], sc_ref[...], mm_ref[...]).astype(BF16)

    def epilogue(s, kind, arg, scale):
        base = s * sub
        if kind == "plain":
            o_ref[:, base:base + sub] = acc_scr[s].astype(o_ref.dtype)
        elif kind == "silu":
            o_ref[:, base:base + sub] = _silu(acc_scr[s]).astype(o_ref.dtype)
        elif kind == "norm_rope_128":
            x = acc_scr[s]
            ms = _dot((x * x).astype(BF16), seg_ref[...])
            y = x * lax.rsqrt(ms + EPS) * (gain_ref[arg:arg + 1, :] * scale)
            for t in range(0, sub, LANES):
                yt = y[:, t:t + LANES]
                if rope:
                    yt = yt * cos_ref[...] + pltpu.roll(yt, LANES // 2, 1) * sin_ref[...]
                o_ref[:, base + t:base + t + LANES] = yt.astype(o_ref.dtype)
        else:
            assert kind == "rope_256" and sub == 2 * LANES
            x1 = acc_scr[s, :, 0:LANES] * scale
            x2 = acc_scr[s, :, LANES:sub] * scale
            if rope:
                c, sn = cos_ref[...], sin_ref[...]
                x1, x2 = x1 * c - x2 * sn, x1 * sn + x2 * c
            o_ref[:, base:base + LANES] = x1.astype(o_ref.dtype)
            o_ref[:, base + LANES:base + sub] = x2.astype(o_ref.dtype)

    for lo, hi, kinds in plan:
        @pl.when((j >= lo) & (j < hi))
        def _(kinds=kinds):
            a = a_scr[...]
            for s in range(n_sub):
                acc_scr[s] = _dot(a, w_ref[:, s * sub:(s + 1) * sub])
            for s, (kind, arg, scale) in enumerate(kinds):
                epilogue(s, kind, arg, scale)


def _proj(h, norm_g, mods, layer, st, w_all, w_idx, segments, cos_t, sin_t, gains):
    m, d = h.shape
    n = w_all.shape[2]
    bm = _pick(st.group, (1024, 512, 256))
    bn = _pick(n, (1024, 512, 256))
    sub = MXU_COLS
    n_sub = bn // sub
    per_sub = []
    for width, kind, arg, scale in segments:
        assert width % sub == 0
        per_sub += [(kind, arg, scale)] * (width // sub)
    assert len(per_sub) * sub == n
    tiles = [tuple(per_sub[t * n_sub:(t + 1) * n_sub]) for t in range(n // bn)]
    plan, lo = [], 0
    for t in range(1, len(tiles) + 1):
        if t == len(tiles) or tiles[t] != tiles[lo]:
            plan.append((lo, t, tiles[lo]))
            lo = t
    row_fn, tab_fn = st.mod_row, st.tab_block
    kernel = functools.partial(_proj_kernel, plan=tuple(plan), rope=st.rope)
    head = jnp.arange(sub) // LANES
    seg_mean = jnp.where(head[:, None] == head[None, :], 1.0 / LANES, 0.0).astype(BF16)
    gains = jnp.tile(gains, (1, sub // LANES))
    return pl.pallas_call(
        kernel,
        grid=(m // bm, n // bn),
        in_specs=[
            pl.BlockSpec((bm, d), lambda i, j: (i, 0)),
            pl.BlockSpec((1, d), lambda i, j: (0, 0)),
            _mod_spec(layer, 0, lambda i: row_fn(i, bm), d),
            _mod_spec(layer, 1, lambda i: row_fn(i, bm), d),
            pl.BlockSpec((d, LANES), lambda i, j: (0, 0)),
            pl.BlockSpec((None, d, bn), lambda i, j: (w_idx, 0, j)),
            pl.BlockSpec((bm, LANES), lambda i, j: (tab_fn(i, bm), 0)),
            pl.BlockSpec((bm, LANES), lambda i, j: (tab_fn(i, bm), 0)),
            pl.BlockSpec(gains.shape, lambda i, j: (0, 0)),
            pl.BlockSpec((sub, sub), lambda i, j: (0, 0)),
        ],
        out_specs=pl.BlockSpec((bm, bn), lambda i, j: (i, j)),
        out_shape=jax.ShapeDtypeStruct((m, n), BF16),
        scratch_shapes=[pltpu.VMEM((bm, d), BF16), pltpu.VMEM((n_sub, bm, sub), F32)],
        compiler_params=_params("parallel", "arbitrary"),
        name="proj",
    )(h, norm_g, mods, mods, _mean_mat(d), w_all, cos_t, sin_t, gains, seg_mean)


def _oproj_kernel(y_ref, w_ref, h_ref, gate_ref, o_ref):
    o_ref[...] = h_ref[...] + gate_ref[...] * _dot(y_ref[...], w_ref[...])


def _oproj_rt_kernel(yf_ref, yb_ref, w_ref, h_ref, gate_ref, o_ref):
    c = pl.program_id(1)

    @pl.when(c == 0)
    def _():
        o_ref[...] = jnp.zeros_like(o_ref)

    o_ref[...] += _dot(yf_ref[...] + yb_ref[...], w_ref[...])

    @pl.when(c == pl.num_programs(1) - 1)
    def _():
        o_ref[...] = h_ref[...] + gate_ref[...] * o_ref[...]


def _oproj(y, w_all, w_idx, h, mods, layer, st, in_place):
    m, k = y.shape
    d = w_all.shape[2]
    bm = _pick(st.group, (1024, 512, 256))
    row_fn = st.mod_row
    bn = _pick(d, (1024, 512, 256, 128))
    return pl.pallas_call(
        _oproj_kernel,
        grid=(m // bm, d // bn),
        in_specs=[
            pl.BlockSpec((bm, k), lambda i, j: (i, 0)),
            pl.BlockSpec((None, k, bn), lambda i, j: (w_idx, 0, j)),
            pl.BlockSpec((bm, bn), lambda i, j: (i, j)),
            _mod_spec_cols(layer, 2, lambda i: row_fn(i, bm), bn),
        ],
        out_specs=pl.BlockSpec((bm, bn), lambda i, j: (i, j)),
        out_shape=jax.ShapeDtypeStruct((m, d), F32),
        input_output_aliases={2: 0} if in_place else {},
        compiler_params=_params("parallel", "arbitrary"),
        name="oproj",
    )(y, w_all, h, mods)


def _oproj_rt(yf, yb, w_all, w_idx, h, mods, layer, st, in_place):
    m, v = yf.shape
    d = w_all.shape[2]
    bm = _pick(st.group, (1024, 512, 256))
    row_fn = st.mod_row
    ck = _pick(v, (1024, 512, 256, 128))
    return pl.pallas_call(
        _oproj_rt_kernel,
        grid=(m // bm, v // ck),
        in_specs=[
            pl.BlockSpec((bm, ck), lambda i, c: (i, c)),
            pl.BlockSpec((bm, ck), lambda i, c: (i, c)),
            pl.BlockSpec((None, ck, d), lambda i, c: (w_idx, c, 0)),
            pl.BlockSpec((bm, d), lambda i, c: (i, 0)),
            _mod_spec(layer, 2, lambda i: row_fn(i, bm), d),
        ],
        out_specs=pl.BlockSpec((bm, d), lambda i, c: (i, 0)),
        out_shape=jax.ShapeDtypeStruct((m, d), F32),
        input_output_aliases={3: 0} if in_place else {},
        compiler_params=_params("parallel", "arbitrary"),
        name="oproj_rt",
    )(yf, yb, w_all, h, mods)


def _ffn_kernel(h_ref, hp_ref, hn_ref, g_ref, sh_ref, sc_ref, gate_ref, wa_ref, wb_ref, cw_ref, cb_ref,
                wo_lo, wo_hi, wo_top, o_ref, a_scr, u_even, u_odd_a, u_odd_b, *, bm, seq_len, steps):
    i = pl.program_id(0)
    j = pl.program_id(1)
    halo = BF16_ROWS
    sub = u_even.shape[2]
    nsc = 2 * steps

    def up(half, u):
        cols = slice(half * sub, (half + 1) * sub)
        a = a_scr[...]
        u[0] = _dot(a, wa_ref[:, cols])
        u[1] = _dot(a, wb_ref[:, cols])

    def gate(u, k):
        if bm > seq_len:
            pos = lax.broadcasted_iota(jnp.int32, (bm, sub), 0) % seq_len
            first, final = pos == 0, pos == seq_len - 1

        def conv(idx, kk):
            cw = cw_ref[kk]
            prev = u[idx, halo - 1:halo - 1 + bm, :]
            cur = u[idx, halo:halo + bm, :]
            nxt = u[idx, halo + 1:halo + 1 + bm, :]
            if bm > seq_len:
                prev, nxt = jnp.where(first, 0.0, prev), jnp.where(final, 0.0, nxt)
            return cb_ref[kk] + prev * cw[0:1, :] + cur * cw[1:2, :] + nxt * cw[2:3, :]

        return (_silu(conv(0, k)) * conv(1, nsc + k)).astype(BF16)

    @pl.when(j == 0)
    def _():
        def nm(x):
            return _norm_mod(x, g_ref[...], sh_ref[...], sc_ref[...])

        keep_prev = jnp.where((i * bm) % seq_len != 0, 1.0, 0.0)
        keep_next = jnp.where(((i + 1) * bm) % seq_len != 0, 1.0, 0.0)
        a_scr[0:halo, :] = (nm(hp_ref[...]) * keep_prev).astype(BF16)
        a_scr[halo:halo + bm, :] = nm(h_ref[...]).astype(BF16)
        a_scr[halo + bm:, :] = (nm(hn_ref[...]) * keep_next).astype(BF16)
        o_ref[...] = jnp.zeros_like(o_ref)
        up(0, u_even)
        up(1, u_odd_a)
        o_ref[...] += _dot(gate(u_even, 0), wo_hi[...])

    def middle(u_new, u_old):
        up(0, u_even)
        up(1, u_new)
        g = jnp.concatenate([gate(u_old, 2 * j - 1), gate(u_even, 2 * j)], axis=1)
        w = jnp.concatenate([wo_lo[...], wo_hi[...]], axis=0)
        o_ref[...] += _dot(g, w)

    inner = (j > 0) & (j < steps - 1)
    pl.when(inner & (j % 2 == 1))(lambda: middle(u_odd_b, u_odd_a))
    pl.when(inner & (j % 2 == 0))(lambda: middle(u_odd_a, u_odd_b))

    @pl.when(j == steps - 1)
    def _():
        u_new, u_old = (u_odd_a, u_odd_b) if (steps - 1) % 2 == 0 else (u_odd_b, u_odd_a)
        middle(u_new, u_old)
        o_ref[...] += _dot(gate(u_new, nsc - 1), wo_top[...])
        o_ref[...] = h_ref[...] + gate_ref[...] * o_ref[...]


def _ffn(h, norm_g, mods, layer, st, w_in, conv_w, conv_b, w_out, seq_len):
    m, d = h.shape
    dff = w_out.shape[1]
    bm = _pick(m, (512, 256))
    assert bm % seq_len == 0 or seq_len % bm == 0
    sub = MXU_COLS
    nsc = dff // sub
    assert dff % (2 * sub) == 0 and conv_w.shape[1:] == (2 * nsc, CONV_W, sub)
    steps = nsc // 2
    assert steps >= 2
    halo = BF16_ROWS
    hb = bm // halo
    last = m // halo - 1
    kernel = functools.partial(_ffn_kernel, bm=bm, seq_len=seq_len, steps=steps)
    mrow = lambda i: st.mod_row(i, bm)
    wo_rows = (lambda j: jnp.maximum(2 * j - 1, 0), lambda j: 2 * j, lambda j: nsc - 1)
    return pl.pallas_call(
        kernel,
        grid=(m // bm, steps),
        in_specs=[
            pl.BlockSpec((bm, d), lambda i, j: (i, 0)),
            pl.BlockSpec((halo, d), lambda i, j: (jnp.maximum(i * hb - 1, 0), 0)),
            pl.BlockSpec((halo, d), lambda i, j: (jnp.minimum((i + 1) * hb, last), 0)),
            pl.BlockSpec((1, d), lambda i, j: (0, 0)),
            _mod_spec(layer, 3, mrow, d),
            _mod_spec(layer, 4, mrow, d),
            _mod_spec(layer, 5, mrow, d),
            pl.BlockSpec((None, d, 2 * sub), lambda i, j: (layer, 0, j)),
            pl.BlockSpec((None, d, 2 * sub), lambda i, j: (layer, 0, steps + j)),
            pl.BlockSpec((None, 2 * nsc, CONV_W, sub), lambda i, j: (layer, 0, 0, 0)),
            pl.BlockSpec((None, 2 * nsc, 1, sub), lambda i, j: (layer, 0, 0, 0)),
        ] + [pl.BlockSpec((None, sub, d), lambda i, j, k=k: (layer, k(j), 0)) for k in wo_rows],
        out_specs=pl.BlockSpec((bm, d), lambda i, j: (i, 0)),
        out_shape=jax.ShapeDtypeStruct((m, d), F32),
        scratch_shapes=[pltpu.VMEM((bm + 2 * halo, d), BF16),
                        pltpu.VMEM((2, bm + 2 * halo, sub), F32),
                        pltpu.VMEM((2, bm + 2 * halo, sub), F32),
                        pltpu.VMEM((2, bm + 2 * halo, sub), F32)],
        compiler_params=_params("parallel", "arbitrary"),
        name="conv_ffn",
    )(h, h, h, norm_g, mods, mods, mods, w_in, w_in, conv_w, conv_b, w_out, w_out, w_out)


def _ga_kernel(sink_ref, q_ref, kvc_ref, *rest, seq, n_kv, local):
    if local:
        kvo_ref, kvp_ref, kvn_ref, o_ref, s_a, s_b = rest
    else:
        o_ref, s_a, s_b = rest
    hd = GA_HD
    kcols = n_kv * hd
    bq = q_ref.shape[0]
    sq = s_a.shape[1]
    n_ctx = kvc_ref.shape[0]
    bufs = (s_a, s_b)

    def band(u):
        if not local:
            return [], 0
        lo, hi = u * sq - WINDOW, (u + 1) * sq + WINDOW
        pieces = []
        if lo < 0:
            pieces.append((kvp_ref, WINDOW + lo, -lo))
        pieces.append((kvo_ref, max(lo, 0), min(hi, bq) - max(lo, 0)))
        if hi > bq:
            pieces.append((kvn_ref, 0, hi - bq))
        return pieces, lo

    def bias(u):
        pieces, lo = band(u)
        n_loc = sum(p[2] for p in pieces)
        kl = lax.broadcasted_iota(jnp.int32, (n_loc, sq), 0) + lo
        ql = lax.broadcasted_iota(jnp.int32, (n_loc, sq), 1) + u * sq
        pos = pl.program_id(1) * bq + kl
        ok = (jnp.abs(ql - kl) <= WINDOW) & (pos >= 0) & (pos < seq)
        return jnp.concatenate([jnp.where(ok, 0.0, NEG_INF).astype(F32), jnp.zeros((n_ctx, sq), F32)], axis=0)

    def keys_values(u, n):
        pieces, _ = band(u)
        ks, vs = slice(n * hd, (n + 1) * hd), slice(kcols + n * hd, kcols + (n + 1) * hd)
        k_all = jnp.concatenate([r[a:a + c, ks] for r, a, c in pieces] + [kvc_ref[:, ks]], axis=0)
        v_all = jnp.concatenate([r[a:a + c, vs] for r, a, c in pieces] + [kvc_ref[:, vs]], axis=0)
        v_t = jnp.concatenate([v_all.astype(F32).T.astype(BF16), jnp.ones((ONES_ROWS, v_all.shape[0]), BF16)], axis=0)
        return k_all, v_t

    chains = [(u, n, g) for u in range(bq // sq) for n in range(n_kv) for g in range(GA_GROUP)]
    cache = {}

    def operands(u, n):
        if (u, n) not in cache:
            cache.clear()
            cache[(u, n)] = keys_values(u, n)
        return cache[(u, n)]

    biases = {}

    def scores(idx):
        u, n, g = chains[idx]
        head = n * GA_GROUP + g
        k_all, _ = operands(u, n)
        s = _dot_nt(k_all, q_ref[u * sq:(u + 1) * sq, head * hd:(head + 1) * hd])
        if local:
            if u not in biases:
                biases[u] = bias(u)
            s = s + biases[u]
        bufs[idx % 2][...] = s

    scores(0)
    for idx, (u, n, g) in enumerate(chains):
        _, v_t = operands(u, n)
        if idx + 1 < len(chains):
            scores(idx + 1)
        buf = bufs[idx % 2]
        head = n * GA_GROUP + g
        sink = sink_ref[head] * LOG2E
        m = jnp.maximum(jnp.max(buf[...], axis=0, keepdims=True), sink)
        p = jnp.exp2((buf[...] - m).astype(BF16))
        ov = _dot(v_t, p)
        o_t = ov[0:hd, :] / (ov[hd:hd + 1, :] + jnp.exp2(sink - m))
        o_ref[u * sq:(u + 1) * sq, head * hd:(head + 1) * hd] = o_t.T.astype(o_ref.dtype)


def _ga_attention(qkv_lat, qkv_ctx, sink, batch, seq, n_ctx, need_ctx):
    n_heads = sink.shape[0]
    qcols = n_heads * GA_HD
    n_kv = n_heads // GA_GROUP
    kvw = 2 * n_kv * GA_HD
    kvblk = qcols // kvw
    assert qcols % kvw == 0
    bq = _pick(seq, (512, 256, 128))
    sq = min(bq, MXU_COLS)
    nq = seq // bq
    wb = bq // WINDOW
    last_w = batch * seq // WINDOW - 1
    smem = pl.BlockSpec(memory_space=pltpu.SMEM)
    lat = pl.pallas_call(
        functools.partial(_ga_kernel, seq=seq, n_kv=n_kv, local=True),
        grid=(batch, nq),
        in_specs=[
            smem,
            pl.BlockSpec((bq, qcols), lambda b, i: (b * nq + i, 0)),
            pl.BlockSpec((n_ctx, kvw), lambda b, i: (b, kvblk)),
            pl.BlockSpec((bq, kvw), lambda b, i: (b * nq + i, kvblk)),
            pl.BlockSpec((WINDOW, kvw), lambda b, i: (jnp.maximum((b * nq + i) * wb - 1, 0), kvblk)),
            pl.BlockSpec((WINDOW, kvw), lambda b, i: (jnp.minimum((b * nq + i + 1) * wb, last_w), kvblk)),
        ],
        out_specs=pl.BlockSpec((bq, qcols), lambda b, i: (b * nq + i, 0)),
        out_shape=jax.ShapeDtypeStruct((batch * seq, qcols), BF16),
        scratch_shapes=[pltpu.VMEM((sq + 2 * WINDOW + n_ctx, sq), F32)] * 2,
        compiler_params=_params("parallel", "parallel"),
        name="ga_attention_latent",
    )(sink, qkv_lat, qkv_ctx, qkv_lat, qkv_lat, qkv_lat)
    ctx = None
    if need_ctx:
        cq = min(n_ctx, MXU_COLS)
        ctx = pl.pallas_call(
            functools.partial(_ga_kernel, seq=n_ctx, n_kv=n_kv, local=False),
            grid=(batch,),
            in_specs=[
                smem,
                pl.BlockSpec((n_ctx, qcols), lambda b: (b, 0)),
                pl.BlockSpec((n_ctx, kvw), lambda b: (b, kvblk)),
            ],
            out_specs=pl.BlockSpec((n_ctx, qcols), lambda b: (b, 0)),
            out_shape=jax.ShapeDtypeStruct((batch * n_ctx, qcols), BF16),
            scratch_shapes=[pltpu.VMEM((n_ctx, cq), F32)] * 2,
            compiler_params=_params("parallel"),
            name="ga_attention_context",
        )(sink, qkv_ctx, qkv_ctx)
    return lat, ctx


def _df_kernel(lam_ref, g_ref, q_ref, *rest, lambda_init, chunks, first_q_axis):
    n_seg = len(chunks)
    kv_refs = rest[:2 * n_seg]
    o_ref = rest[2 * n_seg]
    vt_scrs = rest[2 * n_seg + 1:2 * n_seg + 1 + n_seg]
    acc_scr, s_a, s_b = rest[-3:]
    hd = DF_HD
    bq = q_ref.shape[0]
    lam = lam_ref[...]
    lmbda = (jnp.exp(jnp.sum(lam[0:1] * lam[1:2], axis=-1, keepdims=True))
             - jnp.exp(jnp.sum(lam[2:3] * lam[3:4], axis=-1, keepdims=True)) + lambda_init)

    def transpose_values():
        for seg, ck in enumerate(chunks):
            v_ref, vt = kv_refs[2 * seg + 1], vt_scrs[seg]
            for c in range(v_ref.shape[0] // ck):
                vt[c] = v_ref[c * ck:(c + 1) * ck, :].astype(F32).T.astype(BF16)

    if first_q_axis is None:
        transpose_values()
    else:
        pl.when(pl.program_id(first_q_axis) == 0)(transpose_values)

    qs = (q_ref[:, 0:hd], q_ref[:, hd:2 * hd])
    acc_scr[...] = jnp.zeros_like(acc_scr)
    stat0 = (jnp.full((1, bq), NEG_INF, F32), jnp.zeros((1, bq), F32))
    carry = (stat0, stat0)

    def scores(k_ref, c, ck, dst):
        kblk = k_ref[c * ck:(c + 1) * ck, :]
        for r in range(2):
            dst[r] = _dot_nt(kblk[:, r * hd:(r + 1) * hd], qs[r])

    def update(src, v_t, carry):
        new = []
        for r in range(2):
            m_old, l_old = carry[r]
            m_new = jnp.maximum(m_old, jnp.max(src[r], axis=0, keepdims=True))
            alpha = jnp.exp2(m_old - m_new)
            p = jnp.exp2(src[r] - m_new)
            l_new = alpha * l_old + jnp.sum(p, axis=0, keepdims=True)
            acc_scr[r] = alpha * acc_scr[r] + _dot(v_t, p.astype(BF16))
            new.append((m_new, l_new))
        return tuple(new)

    items = [(kv_refs[2 * seg], c, chunks[seg], vt_scrs[seg])
             for seg in range(n_seg) for c in range(kv_refs[2 * seg].shape[0] // chunks[seg])]
    bufs = (s_a, s_b)

    def view(buf, ck):
        return buf if ck == buf.shape[1] else buf.at[:, 0:ck, :]

    scores(items[0][0], items[0][1], items[0][2], view(bufs[0], items[0][2]))
    for idx, (_, c, ck, vt) in enumerate(items):
        if idx + 1 < len(items):
            k_next, c_next, ck_next, _ = items[idx + 1]
            scores(k_next, c_next, ck_next, view(bufs[(idx + 1) % 2], ck_next))
        carry = update(view(bufs[idx % 2], ck), vt[c], carry)
    (_, l0), (_, l1) = carry
    o = (acc_scr[0] / l0 - lmbda * (acc_scr[1] / l1)).T
    ms = jnp.mean(o * o, axis=-1, keepdims=True)
    y = o * lax.rsqrt(ms + EPS) * g_ref[...] * (1.0 - lambda_init)
    o_ref[...] = y.astype(o_ref.dtype)


def _df_attention(qkv_lat, qkv_ctx, lam, subln_g, batch, seq, n_ctx, lambda_init, need_ctx):
    hw = 2 * DF_HD
    n_heads = qkv_lat.shape[1] // (3 * hw)
    bq = _pick(seq, (512, 256, 128))
    nq = seq // bq
    ck_lat = _pick(seq, (512, 256, 128))
    ck_ctx = _pick(n_ctx, (512, 256, 128))
    g2 = subln_g.reshape(1, hw)
    const = lambda shape: pl.BlockSpec(shape, lambda *_: (0,) * len(shape))
    vt_ctx = pltpu.VMEM((n_ctx // ck_ctx, hw, ck_ctx), BF16)
    vt_lat = pltpu.VMEM((seq // ck_lat, hw, ck_lat), BF16)
    lat = pl.pallas_call(
        functools.partial(_df_kernel, lambda_init=lambda_init, chunks=(ck_ctx, ck_lat), first_q_axis=2),
        grid=(batch, n_heads, nq),
        in_specs=[
            const(lam.shape),
            const((1, hw)),
            pl.BlockSpec((bq, hw), lambda b, h, i: (b * nq + i, h)),
            pl.BlockSpec((n_ctx, hw), lambda b, h, i: (b, n_heads + h)),
            pl.BlockSpec((n_ctx, hw), lambda b, h, i: (b, 2 * n_heads + h)),
            pl.BlockSpec((seq, hw), lambda b, h, i: (b, n_heads + h)),
            pl.BlockSpec((seq, hw), lambda b, h, i: (b, 2 * n_heads + h)),
        ],
        out_specs=pl.BlockSpec((bq, hw), lambda b, h, i: (b * nq + i, h)),
        out_shape=jax.ShapeDtypeStruct((batch * seq, n_heads * hw), BF16),
        scratch_shapes=[vt_ctx, vt_lat, pltpu.VMEM((2, hw, bq), F32)]
        + [pltpu.VMEM((2, max(ck_lat, ck_ctx), bq), F32)] * 2,
        compiler_params=_params("parallel", "parallel", "arbitrary"),
        name="df_attention_latent",
    )(lam, g2, qkv_lat, qkv_ctx, qkv_ctx, qkv_lat, qkv_lat)
    ctx = None
    if need_ctx:
        ctx = pl.pallas_call(
            functools.partial(_df_kernel, lambda_init=lambda_init, chunks=(ck_ctx,), first_q_axis=None),
            grid=(batch, n_heads),
            in_specs=[
                const(lam.shape),
                const((1, hw)),
                pl.BlockSpec((n_ctx, hw), lambda b, h: (b, h)),
                pl.BlockSpec((n_ctx, hw), lambda b, h: (b, n_heads + h)),
                pl.BlockSpec((n_ctx, hw), lambda b, h: (b, 2 * n_heads + h)),
            ],
            out_specs=pl.BlockSpec((n_ctx, hw), lambda b, h: (b, h)),
            out_shape=jax.ShapeDtypeStruct((batch * n_ctx, n_heads * hw), BF16),
            scratch_shapes=[vt_ctx, pltpu.VMEM((2, hw, n_ctx), F32)]
            + [pltpu.VMEM((2, ck_ctx, n_ctx), F32)] * 2,
            compiler_params=_params("parallel", "parallel"),
            name="df_attention_context",
        )(lam, g2, qkv_ctx, qkv_ctx, qkv_ctx)
    return lat, ctx


def _rt_kernel(lg_ref, gn_ref, *refs, n_heads, chunk, ctx_steps):
    ctx_in, lat_in = refs[0:8], refs[8:16]
    ctx_out, lat_out = refs[16:18], refs[18:20]
    st_scr, dec_scr = refs[20], refs[21]
    head0 = pl.program_id(1) * n_heads
    s_idx = pl.program_id(2)
    dk = ctx_in[0].shape[1] // n_heads
    dv = ctx_in[2].shape[1] // n_heads

    @pl.when(s_idx == 0)
    def _():
        st_scr[...] = jnp.zeros_like(st_scr)
        row = lax.broadcasted_iota(jnp.int32, (chunk, chunk), 0).astype(F32)
        col = lax.broadcasted_iota(jnp.int32, (chunk, chunk), 1).astype(F32)
        for d, rel in enumerate((row - col, col - row)):
            for h in range(n_heads):
                lg = lg_ref[d, head0 + h]
                dec_scr[d, h] = jnp.where(rel >= 0, jnp.exp(lg * jnp.maximum(rel, 0.0)), 0.0)

    def step(ins, outs):
        qf_ref, kf_ref, vf_ref, sf_ref, qb_ref, kb_ref, vb_ref, sb_ref = ins
        of_ref, ob_ref = outs
        pos = lax.broadcasted_iota(jnp.int32, (chunk, 1), 0).astype(F32)
        dirs = (
            (qf_ref, kf_ref, vf_ref, sf_ref, of_ref, pos + 1.0, chunk - 1.0 - pos),
            (qb_ref, kb_ref, vb_ref, sb_ref, ob_ref, chunk - pos, pos),
        )
        for d, (q_ref, k_ref, v_ref, s_ref, o_ref, q_pow, k_pow) in enumerate(dirs):
            for h in range(n_heads):
                lg = lg_ref[d, head0 + h]
                decay = dec_scr[d, h]
                q_decay = jnp.exp(lg * q_pow)
                k_decay = jnp.exp(lg * k_pow)
                chunk_decay = jnp.exp(lg * chunk)
                q = q_ref[:, h * dk:(h + 1) * dk]
                k = k_ref[:, h * dk:(h + 1) * dk]
                v = v_ref[:, h * dv:(h + 1) * dv]
                state = st_scr[d, h]
                inner = _dot_nt(q, k) * decay
                o = _dot(inner.astype(BF16), v) + _dot(q, state.astype(BF16)) * q_decay
                kd = (k.astype(F32) * k_decay).astype(BF16)
                st_scr[d, h] = state * chunk_decay + _dot_tn(kd, v)
                mu = jnp.mean(o, axis=-1, keepdims=True)
                dev = o - mu
                var = jnp.mean(dev * dev, axis=-1, keepdims=True)
                y = dev * lax.rsqrt(var + EPS) * gn_ref[d:d + 1, h * dv:(h + 1) * dv]
                y = y * s_ref[:, h * dv:(h + 1) * dv].astype(F32)
                o_ref[:, h * dv:(h + 1) * dv] = y.astype(o_ref.dtype)

    @pl.when(s_idx < ctx_steps)
    def _():
        step(ctx_in, ctx_out)

    @pl.when(s_idx >= ctx_steps)
    def _():
        step(lat_in, lat_out)


def _retention(proj_lat, proj_ctx, log_gamma, gn_g, batch, seq, n_ctx):
    n_heads = log_gamma.shape[1]
    qk = proj_lat.shape[1] // 8
    v = 2 * qk
    chunk = _pick(math.gcd(seq, n_ctx), (256, 128))
    ncc, nlc = n_ctx // chunk, seq // chunk
    dk, dv = qk // n_heads, v // n_heads
    groups = 2
    hpg = n_heads // groups
    qkw, vw = hpg * dk, hpg * dv

    def ctx_f(b, s):
        return b * ncc + jnp.minimum(s, ncc - 1)

    def ctx_b(b, s):
        return b * ncc + (ncc - 1 - jnp.minimum(s, ncc - 1))

    def lat_f(b, s):
        return b * nlc + jnp.maximum(s - ncc, 0)

    def lat_b(b, s):
        return b * nlc + (nlc - 1 - jnp.maximum(s - ncc, 0))

    def in_specs(row, direction):
        return [
            pl.BlockSpec((chunk, qkw), lambda b, g, s: (row(b, s), g)),
            pl.BlockSpec((chunk, qkw), lambda b, g, s: (row(b, s), groups + g)),
            pl.BlockSpec((chunk, vw), lambda b, g, s: (row(b, s), groups + g)),
            pl.BlockSpec((chunk, vw), lambda b, g, s: (row(b, s), (2 + direction) * groups + g)),
        ]

    def out_spec(row):
        return pl.BlockSpec((chunk, vw), lambda b, g, s: (row(b, s), g))

    out_ctx = jax.ShapeDtypeStruct((batch * n_ctx, v), BF16)
    out_lat = jax.ShapeDtypeStruct((batch * seq, v), BF16)
    yf_ctx, yb_ctx, yf_lat, yb_lat = pl.pallas_call(
        functools.partial(_rt_kernel, n_heads=hpg, chunk=chunk, ctx_steps=ncc),
        grid=(batch, groups, ncc + nlc),
        in_specs=[pl.BlockSpec(memory_space=pltpu.SMEM), pl.BlockSpec((2, vw), lambda b, g, s: (0, g))]
        + in_specs(ctx_f, 0) + in_specs(ctx_b, 1) + in_specs(lat_f, 0) + in_specs(lat_b, 1),
        out_specs=[out_spec(ctx_f), out_spec(ctx_b), out_spec(lat_f), out_spec(lat_b)],
        out_shape=[out_ctx, out_ctx, out_lat, out_lat],
        scratch_shapes=[pltpu.VMEM((2, hpg, dk, dv), F32), pltpu.VMEM((2, hpg, chunk, chunk), F32)],
        compiler_params=_params("parallel", "parallel", "arbitrary"),
        name="retention",
    )(log_gamma, gn_g, *([proj_ctx] * 8), *([proj_lat] * 8))
    return (yf_lat, yb_lat), (yf_ctx, yb_ctx)


def _axial_tables(rows_count, head_dim):
    rows = jnp.repeat(jnp.arange(rows_count), GRID_W).astype(F32)
    cols = jnp.tile(jnp.arange(GRID_W), rows_count).astype(F32)
    n_freq = head_dim // 4
    inv = ROPE_THETA ** (-jnp.arange(n_freq, dtype=F32) / n_freq)
    ang = jnp.concatenate([rows[:, None] * inv, cols[:, None] * inv], -1)
    cos, sin = jnp.cos(ang), jnp.sin(ang)
    return jnp.concatenate([cos, cos], -1), jnp.concatenate([-sin, sin], -1)


def _linear_tables(n_tokens, head_dim):
    half = head_dim // 2
    inv = ROPE_THETA ** (-jnp.arange(half, dtype=F32) / half)
    ang = jnp.arange(n_tokens, dtype=F32)[:, None] * inv
    return jnp.cos(ang), jnp.sin(ang)


def kernel(x, c, ctx, c_ctx, mod_w, mod_b, norm_g, ffn_w_in, ffn_conv_w, ffn_conv_b, ffn_w_out, ga_wqkv, ga_sink, ga_qk_norm, ga_wo, rt_w_in, rt_decay, rt_gn, rt_wo, df_wqkv, df_lambda, df_qk_norm, df_subln, df_wo):
    batch, seq, d = x.shape
    n_ctx = ctx.shape[1]
    depth = mod_w.shape[0]
    assert batch + 1 <= MOD_ROWS and seq % GRID_W == 0

    cond = jnp.zeros((MOD_ROWS, d), F32).at[0].set(c_ctx).at[1:batch + 1].set(c)
    mods = _adaln_table(cond, mod_w, mod_b).reshape(depth, 6, MOD_ROWS, 1, d)

    ax_cos, ax_sin = _axial_tables(seq // GRID_W, GA_HD)
    assert DF_HD == GA_HD
    rt_dk = d // RT_HEADS
    ln_cos, ln_sin = _linear_tables(seq, rt_dk)
    ones2 = jnp.ones((2, LANES), F32)

    h_lat = x.reshape(batch * seq, d)
    h_ctx = ctx.reshape(batch * n_ctx, d)

    lat = _Stream(seq, lambda i, bm: 1 + i // (seq // bm), lambda i, bm: i % (seq // bm), True)
    cst = _Stream(batch * n_ctx, lambda i, bm: 0, lambda i, bm: 0, False)

    ga_w, ga_o = ga_wqkv.astype(BF16), ga_wo.astype(BF16)
    rt_w, rt_o = rt_w_in.astype(BF16), rt_wo.astype(BF16)
    df_w, df_o = df_wqkv.astype(BF16), df_wo.astype(BF16)
    f_in, f_out = ffn_w_in.astype(BF16), ffn_w_out.astype(BF16)
    n_cw = ffn_conv_w.shape[2] // MXU_COLS
    f_cw = ffn_conv_w.reshape(depth, CONV_W, n_cw, MXU_COLS).transpose(0, 2, 1, 3)
    f_cb = ffn_conv_b.reshape(depth, n_cw, 1, MXU_COLS)

    for i in range(depth):
        need_ctx = i < depth - 1
        in_place = i > 0
        kind, j = i % N_MIXERS, i // N_MIXERS
        g1 = norm_g[i, 0].reshape(1, d)
        g2 = norm_g[i, 1].reshape(1, d)

        if kind == 0:
            w_all = ga_w
            qcols = ga_sink.shape[1] * GA_HD
            kcols = (w_all.shape[2] - qcols) // 2
            segments = ((qcols, "norm_rope_128", 0, GA_HD ** -0.5 * LOG2E), (kcols, "norm_rope_128", 1, 1.0),
                        (kcols, "plain", 0, 1.0))
            cos_t, sin_t, gains = ax_cos, ax_sin, ga_qk_norm[j]
        elif kind == 1:
            w_all = rt_w
            vcols = (w_all.shape[2] - 2 * d) // 3
            segments = ((d, "rope_256", 0, rt_dk ** -0.5), (d, "rope_256", 0, 1.0),
                        (vcols, "plain", 0, 1.0), (2 * vcols, "silu", 0, 1.0))
            cos_t, sin_t, gains = ln_cos, ln_sin, ones2
        else:
            w_all = df_w
            qcols = w_all.shape[2] // 3
            segments = ((qcols, "norm_rope_128", 0, DF_HD ** -0.5 * LOG2E), (qcols, "norm_rope_128", 1, 1.0),
                        (qcols, "plain", 0, 1.0))
            cos_t, sin_t, gains = ax_cos, ax_sin, df_qk_norm[j]

        proj = {"lat": _proj(h_lat, g1, mods, i, lat, w_all, j, segments, cos_t, sin_t, gains),
                "ctx": _proj(h_ctx, g1, mods, i, cst, w_all, j, segments, cos_t, sin_t, gains)}

        if kind == 0:
            y_lat, y_ctx = _ga_attention(proj["lat"], proj["ctx"], ga_sink[j], batch, seq, n_ctx, need_ctx)
            h_lat = _oproj(y_lat, ga_o, j, h_lat, mods, i, lat, in_place)
            if need_ctx:
                h_ctx = _oproj(y_ctx, ga_o, j, h_ctx, mods, i, cst, in_place)
        elif kind == 1:
            log_gamma = jax.nn.log_sigmoid(rt_decay[j].astype(F32))
            y_lat, y_ctx = _retention(proj["lat"], proj["ctx"], log_gamma, rt_gn[j], batch, seq, n_ctx)
            h_lat = _oproj_rt(y_lat[0], y_lat[1], rt_o, j, h_lat, mods, i, lat, in_place)
            if need_ctx:
                h_ctx = _oproj_rt(y_ctx[0], y_ctx[1], rt_o, j, h_ctx, mods, i, cst, in_place)
        else:
            lambda_init = 0.8 - 0.6 * math.exp(-0.3 * i)
            y_lat, y_ctx = _df_attention(proj["lat"], proj["ctx"], df_lambda[j], df_subln[j], batch, seq, n_ctx,
                                         lambda_init, need_ctx)
            h_lat = _oproj(y_lat, df_o, j, h_lat, mods, i, lat, in_place)
            if need_ctx:
                h_ctx = _oproj(y_ctx, df_o, j, h_ctx, mods, i, cst, in_place)

        h_lat = _ffn(h_lat, g2, mods, i, lat, f_in, f_cw, f_cb, f_out, seq)
        if need_ctx:
            h_ctx = _ffn(h_ctx, g2, mods, i, cst, f_in, f_cw, f_cb, f_out, n_ctx)

    return h_lat.reshape(batch, seq, d)
```

```python
import functools
import math
from typing import Callable, NamedTuple

import jax
import jax.numpy as jnp
from jax import lax
from jax.experimental import pallas as pl
from jax.experimental.pallas import tpu as pltpu

F32 = jnp.float32
BF16 = jnp.bfloat16

EPS = 1e-6
NEG_INF = -1e30
LOG2E = math.log2(math.e)
ROPE_THETA = 10000.0
GRID_W = 64
WINDOW = 128
N_MIXERS = 3

GA_HD = 128
GA_GROUP = 4
RT_HEADS = 8
DF_HD = 128
CONV_W = 3

LANES = 128
MXU_COLS = 256
BF16_ROWS = 16
ONES_ROWS = BF16_ROWS
MOD_ROWS = 8
VMEM_LIMIT = 52 * 1024 * 1024


class _Stream(NamedTuple):
    group: int
    mod_row: Callable
    tab_block: Callable
    rope: bool


def _pick(n, candidates):
    for c in candidates:
        if n % c == 0:
            return c
    raise ValueError(f"no tile size in {candidates} divides {n}")


def _params(*sem):
    return pltpu.CompilerParams(dimension_semantics=sem, vmem_limit_bytes=VMEM_LIMIT)


def _silu(x):
    return x * (1.0 / (1.0 + jnp.exp(-x)))


def _dot(a, b):
    return jnp.dot(a, b, preferred_element_type=F32)


def _dot_nt(a, b):
    return lax.dot_general(a, b, (((1,), (1,)), ((), ())), preferred_element_type=F32)


def _dot_tn(a, b):
    return lax.dot_general(a, b, (((0,), (0,)), ((), ())), preferred_element_type=F32)


def _adaln_kernel(c_ref, w_ref, b_ref, o_ref):
    s = _silu(c_ref[...]).astype(BF16)
    o_ref[...] = _dot(s, w_ref[...].astype(BF16)) + b_ref[...]


def _adaln_table(cond, mod_w, mod_b):
    depth, d, _ = mod_w.shape
    bn = _pick(d, (1024, 512, 256, 128))
    nj = d // bn
    return pl.pallas_call(
        _adaln_kernel,
        grid=(depth, 6, nj),
        in_specs=[
            pl.BlockSpec((MOD_ROWS, d), lambda l, k, j: (0, 0)),
            pl.BlockSpec((None, d, bn), lambda l, k, j: (l, 0, k * nj + j)),
            pl.BlockSpec((None, 1, bn), lambda l, k, j: (l, 0, k * nj + j)),
        ],
        out_specs=pl.BlockSpec((None, None, MOD_ROWS, bn), lambda l, k, j: (l, k, 0, j)),
        out_shape=jax.ShapeDtypeStruct((depth, 6, MOD_ROWS, d), F32),
        compiler_params=_params("parallel", "parallel", "parallel"),
        name="adaln_table",
    )(cond, mod_w, mod_b.reshape(depth, 1, 6 * d))


def _mod_spec(layer, slot, row_fn, d):
    return pl.BlockSpec((None, None, None, 1, d), lambda i, j: (layer, slot, row_fn(i), 0, 0))


def _mod_spec_cols(layer, slot, row_fn, bn):
    return pl.BlockSpec((None, None, None, 1, bn), lambda i, j: (layer, slot, row_fn(i), 0, j))


def _norm_mod(x, g, shift, scale, mean_mat=None):
    if mean_mat is None:
        r = lax.rsqrt(jnp.mean(x * x, axis=-1, keepdims=True) + EPS)
    else:
        ms = _dot((x * x).astype(BF16), mean_mat)
        r = jnp.concatenate([lax.rsqrt(ms + EPS)] * (x.shape[1] // LANES), axis=1)
    return x * r * (g * (1.0 + scale)) + shift


def _mean_mat(d):
    return jnp.full((d, LANES), 1.0 / d, BF16)


def _proj_kernel(h_ref, g_ref, sh_ref, sc_ref, mm_ref, w_ref, cos_ref, sin_ref, gain_ref, seg_ref, o_ref, a_scr, acc_scr,
                 *, plan, rope):
    j = pl.program_id(1)
    sub = acc_scr.shape[2]
    n_sub = o_ref.shape[1] // sub

    @pl.when(j == 0)
    def _():
        a_scr[...] = _norm_mod(h_ref[...], g_ref[...], sh_ref[...], sc_ref[...], mm_ref[...]).astype(BF16)

    def epilogue(s, kind, arg, scale):
        base = s * sub
        if kind == "plain":
            o_ref[:, base:base + sub] = acc_scr[s].astype(o_ref.dtype)
        elif kind == "silu":
            o_ref[:, base:base + sub] = _silu(acc_scr[s]).astype(o_ref.dtype)
        elif kind == "norm_rope_128":
            x = acc_scr[s]
            ms = _dot((x * x).astype(BF16), seg_ref[...])
            y = x * lax.rsqrt(ms + EPS) * (gain_ref[arg:arg + 1, :] * scale)
            for t in range(0, sub, LANES):
                yt = y[:, t:t + LANES]
                if rope:
                    yt = yt * cos_ref[...] + pltpu.roll(yt, LANES // 2, 1) * sin_ref[...]
                o_ref[:, base + t:base + t + LANES] = yt.astype(o_ref.dtype)
        else:
            assert kind == "rope_256" and sub == 2 * LANES
            x1 = acc_scr[s, :, 0:LANES] * scale
            x2 = acc_scr[s, :, LANES:sub] * scale
            if rope:
                c, sn = cos_ref[...], sin_ref[...]
                x1, x2 = x1 * c - x2 * sn, x1 * sn + x2 * c
            o_ref[:, base:base + LANES] = x1.astype(o_ref.dtype)
            o_ref[:, base + LANES:base + sub] = x2.astype(o_ref.dtype)

    for lo, hi, kinds in plan:
        @pl.when((j >= lo) & (j < hi))
        def _(kinds=kinds):
            a = a_scr[...]
            for s in range(n_sub):
                acc_scr[s] = _dot(a, w_ref[:, s * sub:(s + 1) * sub])
            for s, (kind, arg, scale) in enumerate(kinds):
                epilogue(s, kind, arg, scale)


def _proj(h, norm_g, mods, layer, st, w_all, w_idx, segments, cos_t, sin_t, gains):
    m, d = h.shape
    n = w_all.shape[2]
    bm = _pick(st.group, (1024, 512, 256))
    bn = _pick(n, (1024, 512, 256))
    sub = MXU_COLS
    n_sub = bn // sub
    per_sub = []
    for width, kind, arg, scale in segments:
        assert width % sub == 0
        per_sub += [(kind, arg, scale)] * (width // sub)
    assert len(per_sub) * sub == n
    tiles = [tuple(per_sub[t * n_sub:(t + 1) * n_sub]) for t in range(n // bn)]
    plan, lo = [], 0
    for t in range(1, len(tiles) + 1):
        if t == len(tiles) or tiles[t] != tiles[lo]:
            plan.append((lo, t, tiles[lo]))
            lo = t
    row_fn, tab_fn = st.mod_row, st.tab_block
    kernel = functools.partial(_proj_kernel, plan=tuple(plan), rope=st.rope)
    head = jnp.arange(sub) // LANES
    seg_mean = jnp.where(head[:, None] == head[None, :], 1.0 / LANES, 0.0).astype(BF16)
    gains = jnp.tile(gains, (1, sub // LANES))
    return pl.pallas_call(
        kernel,
        grid=(m // bm, n // bn),
        in_specs=[
            pl.BlockSpec((bm, d), lambda i, j: (i, 0)),
            pl.BlockSpec((1, d), lambda i, j: (0, 0)),
            _mod_spec(layer, 0, lambda i: row_fn(i, bm), d),
            _mod_spec(layer, 1, lambda i: row_fn(i, bm), d),
            pl.BlockSpec((d, LANES), lambda i, j: (0, 0)),
            pl.BlockSpec((None, d, bn), lambda i, j: (w_idx, 0, j)),
            pl.BlockSpec((bm, LANES), lambda i, j: (tab_fn(i, bm), 0)),
            pl.BlockSpec((bm, LANES), lambda i, j: (tab_fn(i, bm), 0)),
            pl.BlockSpec(gains.shape, lambda i, j: (0, 0)),
            pl.BlockSpec((sub, sub), lambda i, j: (0, 0)),
        ],
        out_specs=pl.BlockSpec((bm, bn), lambda i, j: (i, j)),
        out_shape=jax.ShapeDtypeStruct((m, n), BF16),
        scratch_shapes=[pltpu.VMEM((bm, d), BF16), pltpu.VMEM((n_sub, bm, sub), F32)],
        compiler_params=_params("parallel", "arbitrary"),
        name="proj",
    )(h, norm_g, mods, mods, _mean_mat(d), w_all, cos_t, sin_t, gains, seg_mean)


def _oproj_kernel(y_ref, w_ref, h_ref, gate_ref, o_ref):
    o_ref[...] = h_ref[...] + gate_ref[...] * _dot(y_ref[...], w_ref[...])


def _oproj_rt_kernel(yf_ref, yb_ref, w_ref, h_ref, gate_ref, o_ref):
    c = pl.program_id(1)

    @pl.when(c == 0)
    def _():
        o_ref[...] = jnp.zeros_like(o_ref)

    o_ref[...] += _dot(yf_ref[...] + yb_ref[...], w_ref[...])

    @pl.when(c == pl.num_programs(1) - 1)
    def _():
        o_ref[...] = h_ref[...] + gate_ref[...] * o_ref[...]


def _oproj(y, w_all, w_idx, h, mods, layer, st, in_place):
    m, k = y.shape
    d = w_all.shape[2]
    bm = _pick(st.group, (1024, 512, 256))
    row_fn = st.mod_row
    bn = _pick(d, (1024, 512, 256, 128))
    return pl.pallas_call(
        _oproj_kernel,
        grid=(m // bm, d // bn),
        in_specs=[
            pl.BlockSpec((bm, k), lambda i, j: (i, 0)),
            pl.BlockSpec((None, k, bn), lambda i, j: (w_idx, 0, j)),
            pl.BlockSpec((bm, bn), lambda i, j: (i, j)),
            _mod_spec_cols(layer, 2, lambda i: row_fn(i, bm), bn),
        ],
        out_specs=pl.BlockSpec((bm, bn), lambda i, j: (i, j)),
        out_shape=jax.ShapeDtypeStruct((m, d), F32),
        input_output_aliases={2: 0} if in_place else {},
        compiler_params=_params("parallel", "arbitrary"),
        name="oproj",
    )(y, w_all, h, mods)


def _oproj_rt(yf, yb, w_all, w_idx, h, mods, layer, st, in_place):
    m, v = yf.shape
    d = w_all.shape[2]
    bm = _pick(st.group, (1024, 512, 256))
    row_fn = st.mod_row
    ck = _pick(v, (1024, 512, 256, 128))
    return pl.pallas_call(
        _oproj_rt_kernel,
        grid=(m // bm, v // ck),
        in_specs=[
            pl.BlockSpec((bm, ck), lambda i, c: (i, c)),
            pl.BlockSpec((bm, ck), lambda i, c: (i, c)),
            pl.BlockSpec((None, ck, d), lambda i, c: (w_idx, c, 0)),
            pl.BlockSpec((bm, d), lambda i, c: (i, 0)),
            _mod_spec(layer, 2, lambda i: row_fn(i, bm), d),
        ],
        out_specs=pl.BlockSpec((bm, d), lambda i, c: (i, 0)),
        out_shape=jax.ShapeDtypeStruct((m, d), F32),
        input_output_aliases={3: 0} if in_place else {},
        compiler_params=_params("parallel", "arbitrary"),
        name="oproj_rt",
    )(yf, yb, w_all, h, mods)


def _ffn_kernel(h_ref, hp_ref, hn_ref, g_ref, sh_ref, sc_ref, gate_ref, wa_ref, wb_ref, cw_ref, cb_ref,
                wo_lo, wo_hi, wo_top, *rest, bm, seq_len, steps, n_cast):
    o_ref = rest[n_cast]
    a_scr, u_even, u_odd_a, u_odd_b = rest[-4:]
    _cast_carried(rest[:n_cast], rest[n_cast + 1:2 * n_cast + 1])
    i = pl.program_id(0)
    j = pl.program_id(1)
    halo = BF16_ROWS
    sub = u_even.shape[2]
    nsc = 2 * steps

    def up(half, u):
        cols = slice(half * sub, (half + 1) * sub)
        a = a_scr[...]
        u[0] = _dot(a, wa_ref[:, cols])
        u[1] = _dot(a, wb_ref[:, cols])

    def gate(u, k):
        if bm > seq_len:
            pos = lax.broadcasted_iota(jnp.int32, (bm, sub), 0) % seq_len
            first, final = pos == 0, pos == seq_len - 1

        def conv(idx, kk):
            cw = cw_ref[kk]
            prev = u[idx, halo - 1:halo - 1 + bm, :]
            cur = u[idx, halo:halo + bm, :]
            nxt = u[idx, halo + 1:halo + 1 + bm, :]
            if bm > seq_len:
                prev, nxt = jnp.where(first, 0.0, prev), jnp.where(final, 0.0, nxt)
            return cb_ref[kk] + prev * cw[0:1, :] + cur * cw[1:2, :] + nxt * cw[2:3, :]

        return (_silu(conv(0, k)) * conv(1, nsc + k)).astype(BF16)

    @pl.when(j == 0)
    def _():
        def nm(x):
            return _norm_mod(x, g_ref[...], sh_ref[...], sc_ref[...])

        keep_prev = jnp.where((i * bm) % seq_len != 0, 1.0, 0.0)
        keep_next = jnp.where(((i + 1) * bm) % seq_len != 0, 1.0, 0.0)
        a_scr[0:halo, :] = (nm(hp_ref[...]) * keep_prev).astype(BF16)
        a_scr[halo:halo + bm, :] = nm(h_ref[...]).astype(BF16)
        a_scr[halo + bm:, :] = (nm(hn_ref[...]) * keep_next).astype(BF16)
        o_ref[...] = jnp.zeros_like(o_ref)
        up(0, u_even)
        up(1, u_odd_a)
        o_ref[...] += _dot(gate(u_even, 0), wo_hi[...])

    def middle(u_new, u_old):
        up(0, u_even)
        up(1, u_new)
        g = jnp.concatenate([gate(u_old, 2 * j - 1), gate(u_even, 2 * j)], axis=1)
        w = jnp.concatenate([wo_lo[...], wo_hi[...]], axis=0)
        o_ref[...] += _dot(g, w)

    inner = (j > 0) & (j < steps - 1)
    pl.when(inner & (j % 2 == 1))(lambda: middle(u_odd_b, u_odd_a))
    pl.when(inner & (j % 2 == 0))(lambda: middle(u_odd_a, u_odd_b))

    @pl.when(j == steps - 1)
    def _():
        u_new, u_old = (u_odd_a, u_odd_b) if (steps - 1) % 2 == 0 else (u_odd_b, u_odd_a)
        middle(u_new, u_old)
        o_ref[...] += _dot(gate(u_new, nsc - 1), wo_top[...])
        o_ref[...] = h_ref[...] + gate_ref[...] * o_ref[...]


def _ffn(h, norm_g, mods, layer, w_idx, st, w_in, conv_w, conv_b, w_out, seq_len, cast=None):
    m, d = h.shape
    dff = w_out.shape[1]
    bm = _pick(m, (512, 256))
    assert bm % seq_len == 0 or seq_len % bm == 0
    sub = MXU_COLS
    nsc = dff // sub
    assert dff % (2 * sub) == 0 and conv_w.shape[1:] == (2 * nsc, CONV_W, sub)
    steps = nsc // 2
    assert steps >= 2
    halo = BF16_ROWS
    hb = bm // halo
    last = m // halo - 1
    c_ins, c_in_specs, c_out_specs, c_out_shapes = _cast_plan(
        cast or [], (m // bm) * steps, lambda i, j: i * steps + j)
    kernel = functools.partial(_ffn_kernel, bm=bm, seq_len=seq_len, steps=steps, n_cast=len(c_ins))
    mrow = lambda i: st.mod_row(i, bm)
    wo_rows = (lambda j: jnp.maximum(2 * j - 1, 0), lambda j: 2 * j, lambda j: nsc - 1)
    out = pl.pallas_call(
        kernel,
        grid=(m // bm, steps),
        in_specs=[
            pl.BlockSpec((bm, d), lambda i, j: (i, 0)),
            pl.BlockSpec((halo, d), lambda i, j: (jnp.maximum(i * hb - 1, 0), 0)),
            pl.BlockSpec((halo, d), lambda i, j: (jnp.minimum((i + 1) * hb, last), 0)),
            pl.BlockSpec((1, d), lambda i, j: (0, 0)),
            _mod_spec(layer, 3, mrow, d),
            _mod_spec(layer, 4, mrow, d),
            _mod_spec(layer, 5, mrow, d),
            pl.BlockSpec((None, d, 2 * sub), lambda i, j: (w_idx, 0, j)),
            pl.BlockSpec((None, d, 2 * sub), lambda i, j: (w_idx, 0, steps + j)),
            pl.BlockSpec((None, 2 * nsc, CONV_W, sub), lambda i, j: (layer, 0, 0, 0)),
            pl.BlockSpec((None, 2 * nsc, 1, sub), lambda i, j: (layer, 0, 0, 0)),
        ] + [pl.BlockSpec((None, sub, d), lambda i, j, k=k: (w_idx, k(j), 0)) for k in wo_rows] + c_in_specs,
        out_specs=[pl.BlockSpec((bm, d), lambda i, j: (i, 0))] + c_out_specs,
        out_shape=[jax.ShapeDtypeStruct((m, d), F32)] + c_out_shapes,
        scratch_shapes=[pltpu.VMEM((bm + 2 * halo, d), BF16),
                        pltpu.VMEM((2, bm + 2 * halo, sub), F32),
                        pltpu.VMEM((2, bm + 2 * halo, sub), F32),
                        pltpu.VMEM((2, bm + 2 * halo, sub), F32)],
        compiler_params=_params("arbitrary", "arbitrary"),
        name="conv_ffn",
    )(h, h, h, norm_g, mods, mods, mods, w_in, w_in, conv_w, conv_b, w_out, w_out, w_out, *c_ins)
    return out[0], out[1:]


def _cast_plan(items, n_steps, step_fn):
    ins, in_specs, out_specs, out_shapes = [], [], [], []
    for w, layer0, n_layers in items:
        depth, r, c = w.shape
        rows = n_layers * r

        def fits(n):
            b = rows // n
            return rows % n == 0 and b % BF16_ROWS == 0 and (r % b == 0 or (b % r == 0 and layer0 * r % b == 0))

        nblk = max(n for n in range(1, n_steps + 1) if fits(n))
        br = rows // nblk
        first = layer0 * r // br

        def blk(*g, nblk=nblk):
            return jnp.minimum(step_fn(*g), nblk - 1)

        ins.append(w.reshape(depth * r, c))
        in_specs.append(pl.BlockSpec((br, c), lambda *g, blk=blk, first=first: (first + blk(*g), 0)))
        out_specs.append(pl.BlockSpec((br, c), lambda *g, blk=blk: (blk(*g), 0)))
        out_shapes.append(jax.ShapeDtypeStruct((rows, c), BF16))
    return ins, in_specs, out_specs, out_shapes


def _cast_carried(in_refs, out_refs):
    for src, dst in zip(in_refs, out_refs):
        dst[...] = src[...].astype(dst.dtype)


def _ga_kernel(sink_ref, q_ref, kvc_ref, *rest, seq, n_kv, local, n_cast=0):
    if local:
        kvo_ref, kvp_ref, kvn_ref = rest[:3]
        o_ref, s_a, s_b = rest[3 + n_cast], rest[-2], rest[-1]
        _cast_carried(rest[3:3 + n_cast], rest[4 + n_cast:4 + 2 * n_cast])
    else:
        o_ref, s_a, s_b = rest
    hd = GA_HD
    kcols = n_kv * hd
    bq = q_ref.shape[0]
    sq = s_a.shape[1]
    n_ctx = kvc_ref.shape[0]
    bufs = (s_a, s_b)

    def band(u):
        if not local:
            return [], 0
        lo, hi = u * sq - WINDOW, (u + 1) * sq + WINDOW
        pieces = []
        if lo < 0:
            pieces.append((kvp_ref, WINDOW + lo, -lo))
        pieces.append((kvo_ref, max(lo, 0), min(hi, bq) - max(lo, 0)))
        if hi > bq:
            pieces.append((kvn_ref, 0, hi - bq))
        return pieces, lo

    def bias(u):
        pieces, lo = band(u)
        n_loc = sum(p[2] for p in pieces)
        kl = lax.broadcasted_iota(jnp.int32, (n_loc, sq), 0) + lo
        ql = lax.broadcasted_iota(jnp.int32, (n_loc, sq), 1) + u * sq
        pos = pl.program_id(1) * bq + kl
        ok = (jnp.abs(ql - kl) <= WINDOW) & (pos >= 0) & (pos < seq)
        return jnp.concatenate([jnp.where(ok, 0.0, NEG_INF).astype(F32), jnp.zeros((n_ctx, sq), F32)], axis=0)

    def keys_values(u, n):
        pieces, _ = band(u)
        ks, vs = slice(n * hd, (n + 1) * hd), slice(kcols + n * hd, kcols + (n + 1) * hd)
        k_all = jnp.concatenate([r[a:a + c, ks] for r, a, c in pieces] + [kvc_ref[:, ks]], axis=0)
        v_all = jnp.concatenate([r[a:a + c, vs] for r, a, c in pieces] + [kvc_ref[:, vs]], axis=0)
        v_t = jnp.concatenate([v_all.astype(F32).T.astype(BF16), jnp.ones((ONES_ROWS, v_all.shape[0]), BF16)], axis=0)
        return k_all, v_t

    chains = [(u, n, g) for u in range(bq // sq) for n in range(n_kv) for g in range(GA_GROUP)]
    cache = {}

    def operands(u, n):
        if (u, n) not in cache:
            cache.clear()
            cache[(u, n)] = keys_values(u, n)
        return cache[(u, n)]

    biases = {}

    def scores(idx):
        u, n, g = chains[idx]
        head = n * GA_GROUP + g
        k_all, _ = operands(u, n)
        s = _dot_nt(k_all, q_ref[u * sq:(u + 1) * sq, head * hd:(head + 1) * hd])
        if local:
            if u not in biases:
                biases[u] = bias(u)
            s = s + biases[u]
        bufs[idx % 2][...] = s

    scores(0)
    for idx, (u, n, g) in enumerate(chains):
        _, v_t = operands(u, n)
        if idx + 1 < len(chains):
            scores(idx + 1)
        buf = bufs[idx % 2]
        head = n * GA_GROUP + g
        sink = sink_ref[head] * LOG2E
        m = jnp.maximum(jnp.max(buf[...], axis=0, keepdims=True), sink)
        p = jnp.exp2((buf[...] - m).astype(BF16))
        ov = _dot(v_t, p)
        o_t = ov[0:hd, :] / (ov[hd:hd + 1, :] + jnp.exp2(sink - m))
        o_ref[u * sq:(u + 1) * sq, head * hd:(head + 1) * hd] = o_t.T.astype(o_ref.dtype)


def _ga_attention(qkv_lat, qkv_ctx, sink, batch, seq, n_ctx, need_ctx, cast=None):
    n_heads = sink.shape[0]
    qcols = n_heads * GA_HD
    n_kv = n_heads // GA_GROUP
    kvw = 2 * n_kv * GA_HD
    kvblk = qcols // kvw
    assert qcols % kvw == 0
    bq = _pick(seq, (512, 256, 128))
    sq = min(bq, MXU_COLS)
    nq = seq // bq
    wb = bq // WINDOW
    last_w = batch * seq // WINDOW - 1
    smem = pl.BlockSpec(memory_space=pltpu.SMEM)
    c_ins, c_in_specs, c_out_specs, c_out_shapes = _cast_plan(cast or [], batch * nq, lambda b, i: b * nq + i)
    lat = pl.pallas_call(
        functools.partial(_ga_kernel, seq=seq, n_kv=n_kv, local=True, n_cast=len(c_ins)),
        grid=(batch, nq),
        in_specs=[
            smem,
            pl.BlockSpec((bq, qcols), lambda b, i: (b * nq + i, 0)),
            pl.BlockSpec((n_ctx, kvw), lambda b, i: (b, kvblk)),
            pl.BlockSpec((bq, kvw), lambda b, i: (b * nq + i, kvblk)),
            pl.BlockSpec((WINDOW, kvw), lambda b, i: (jnp.maximum((b * nq + i) * wb - 1, 0), kvblk)),
            pl.BlockSpec((WINDOW, kvw), lambda b, i: (jnp.minimum((b * nq + i + 1) * wb, last_w), kvblk)),
        ] + c_in_specs,
        out_specs=[pl.BlockSpec((bq, qcols), lambda b, i: (b * nq + i, 0))] + c_out_specs,
        out_shape=[jax.ShapeDtypeStruct((batch * seq, qcols), BF16)] + c_out_shapes,
        scratch_shapes=[pltpu.VMEM((sq + 2 * WINDOW + n_ctx, sq), F32)] * 2,
        compiler_params=_params("arbitrary", "arbitrary"),
        name="ga_attention_latent",
    )(sink, qkv_lat, qkv_ctx, qkv_lat, qkv_lat, qkv_lat, *c_ins)
    lat, casts = lat[0], lat[1:]
    ctx = None
    if need_ctx:
        cq = min(n_ctx, MXU_COLS)
        ctx = pl.pallas_call(
            functools.partial(_ga_kernel, seq=n_ctx, n_kv=n_kv, local=False),
            grid=(batch,),
            in_specs=[
                smem,
                pl.BlockSpec((n_ctx, qcols), lambda b: (b, 0)),
                pl.BlockSpec((n_ctx, kvw), lambda b: (b, kvblk)),
            ],
            out_specs=pl.BlockSpec((n_ctx, qcols), lambda b: (b, 0)),
            out_shape=jax.ShapeDtypeStruct((batch * n_ctx, qcols), BF16),
            scratch_shapes=[pltpu.VMEM((n_ctx, cq), F32)] * 2,
            compiler_params=_params("parallel"),
            name="ga_attention_context",
        )(sink, qkv_ctx, qkv_ctx)
    return lat, ctx, casts


def _df_kernel(lam_ref, g_ref, q_ref, *rest, lambda_init, chunks, first_q_axis, n_cast=0):
    n_seg = len(chunks)
    kv_refs = rest[:2 * n_seg]
    o_ref = rest[2 * n_seg + n_cast]
    _cast_carried(rest[2 * n_seg:2 * n_seg + n_cast], rest[2 * n_seg + n_cast + 1:2 * n_seg + 2 * n_cast + 1])
    vt_scrs = rest[-3 - n_seg:-3]
    acc_scr, s_a, s_b = rest[-3:]
    hd = DF_HD
    bq = q_ref.shape[0]
    lam = lam_ref[...]
    lmbda = (jnp.exp(jnp.sum(lam[0:1] * lam[1:2], axis=-1, keepdims=True))
             - jnp.exp(jnp.sum(lam[2:3] * lam[3:4], axis=-1, keepdims=True)) + lambda_init)

    def transpose_values():
        for seg, ck in enumerate(chunks):
            v_ref, vt = kv_refs[2 * seg + 1], vt_scrs[seg]
            for c in range(v_ref.shape[0] // ck):
                vt[c] = v_ref[c * ck:(c + 1) * ck, :].astype(F32).T.astype(BF16)

    if first_q_axis is None:
        transpose_values()
    else:
        pl.when(pl.program_id(first_q_axis) == 0)(transpose_values)

    qs = (q_ref[:, 0:hd], q_ref[:, hd:2 * hd])
    acc_scr[...] = jnp.zeros_like(acc_scr)
    stat0 = (jnp.full((1, bq), NEG_INF, F32), jnp.zeros((1, bq), F32))
    carry = (stat0, stat0)

    def scores(k_ref, c, ck, dst):
        kblk = k_ref[c * ck:(c + 1) * ck, :]
        for r in range(2):
            dst[r] = _dot_nt(kblk[:, r * hd:(r + 1) * hd], qs[r])

    def update(src, v_t, carry):
        new = []
        for r in range(2):
            m_old, l_old = carry[r]
            m_new = jnp.maximum(m_old, jnp.max(src[r], axis=0, keepdims=True))
            alpha = jnp.exp2(m_old - m_new)
            p = jnp.exp2(src[r] - m_new)
            l_new = alpha * l_old + jnp.sum(p, axis=0, keepdims=True)
            acc_scr[r] = alpha * acc_scr[r] + _dot(v_t, p.astype(BF16))
            new.append((m_new, l_new))
        return tuple(new)

    items = [(kv_refs[2 * seg], c, chunks[seg], vt_scrs[seg])
             for seg in range(n_seg) for c in range(kv_refs[2 * seg].shape[0] // chunks[seg])]
    bufs = (s_a, s_b)

    def view(buf, ck):
        return buf if ck == buf.shape[1] else buf.at[:, 0:ck, :]

    scores(items[0][0], items[0][1], items[0][2], view(bufs[0], items[0][2]))
    for idx, (_, c, ck, vt) in enumerate(items):
        if idx + 1 < len(items):
            k_next, c_next, ck_next, _ = items[idx + 1]
            scores(k_next, c_next, ck_next, view(bufs[(idx + 1) % 2], ck_next))
        carry = update(view(bufs[idx % 2], ck), vt[c], carry)
    (_, l0), (_, l1) = carry
    o = (acc_scr[0] / l0 - lmbda * (acc_scr[1] / l1)).T
    ms = jnp.mean(o * o, axis=-1, keepdims=True)
    y = o * lax.rsqrt(ms + EPS) * g_ref[...] * (1.0 - lambda_init)
    o_ref[...] = y.astype(o_ref.dtype)


def _df_attention(qkv_lat, qkv_ctx, lam, subln_g, batch, seq, n_ctx, lambda_init, need_ctx, cast=None):
    hw = 2 * DF_HD
    n_heads = qkv_lat.shape[1] // (3 * hw)
    bq = _pick(seq, (512, 256, 128))
    nq = seq // bq
    ck_lat = _pick(seq, (512, 256, 128))
    ck_ctx = _pick(n_ctx, (512, 256, 128))
    g2 = subln_g.reshape(1, hw)
    const = lambda shape: pl.BlockSpec(shape, lambda *_: (0,) * len(shape))
    vt_ctx = pltpu.VMEM((n_ctx // ck_ctx, hw, ck_ctx), BF16)
    vt_lat = pltpu.VMEM((seq // ck_lat, hw, ck_lat), BF16)
    c_ins, c_in_specs, c_out_specs, c_out_shapes = _cast_plan(
        cast or [], batch * n_heads * nq, lambda b, h, i: (b * n_heads + h) * nq + i)
    lat = pl.pallas_call(
        functools.partial(_df_kernel, lambda_init=lambda_init, chunks=(ck_ctx, ck_lat), first_q_axis=2,
                          n_cast=len(c_ins)),
        grid=(batch, n_heads, nq),
        in_specs=[
            const(lam.shape),
            const((1, hw)),
            pl.BlockSpec((bq, hw), lambda b, h, i: (b * nq + i, h)),
            pl.BlockSpec((n_ctx, hw), lambda b, h, i: (b, n_heads + h)),
            pl.BlockSpec((n_ctx, hw), lambda b, h, i: (b, 2 * n_heads + h)),
            pl.BlockSpec((seq, hw), lambda b, h, i: (b, n_heads + h)),
            pl.BlockSpec((seq, hw), lambda b, h, i: (b, 2 * n_heads + h)),
        ] + c_in_specs,
        out_specs=[pl.BlockSpec((bq, hw), lambda b, h, i: (b * nq + i, h))] + c_out_specs,
        out_shape=[jax.ShapeDtypeStruct((batch * seq, n_heads * hw), BF16)] + c_out_shapes,
        scratch_shapes=[vt_ctx, vt_lat, pltpu.VMEM((2, hw, bq), F32)]
        + [pltpu.VMEM((2, max(ck_lat, ck_ctx), bq), F32)] * 2,
        compiler_params=_params("arbitrary", "arbitrary", "arbitrary"),
        name="df_attention_latent",
    )(lam, g2, qkv_lat, qkv_ctx, qkv_ctx, qkv_lat, qkv_lat, *c_ins)
    lat, casts = lat[0], lat[1:]
    ctx = None
    if need_ctx:
        ctx = pl.pallas_call(
            functools.partial(_df_kernel, lambda_init=lambda_init, chunks=(ck_ctx,), first_q_axis=None),
            grid=(batch, n_heads),
            in_specs=[
                const(lam.shape),
                const((1, hw)),
                pl.BlockSpec((n_ctx, hw), lambda b, h: (b, h)),
                pl.BlockSpec((n_ctx, hw), lambda b, h: (b, n_heads + h)),
                pl.BlockSpec((n_ctx, hw), lambda b, h: (b, 2 * n_heads + h)),
            ],
            out_specs=pl.BlockSpec((n_ctx, hw), lambda b, h: (b, h)),
            out_shape=jax.ShapeDtypeStruct((batch * n_ctx, n_heads * hw), BF16),
            scratch_shapes=[vt_ctx, pltpu.VMEM((2, hw, n_ctx), F32)]
            + [pltpu.VMEM((2, ck_ctx, n_ctx), F32)] * 2,
            compiler_params=_params("parallel", "parallel"),
            name="df_attention_context",
        )(lam, g2, qkv_ctx, qkv_ctx, qkv_ctx)
    return lat, ctx, casts


def _rt_kernel(lg_ref, gn_ref, *refs, n_heads, chunk, ctx_steps):
    ctx_in, lat_in = refs[0:8], refs[8:16]
    ctx_out, lat_out = refs[16:18], refs[18:20]
    st_scr, dec_scr = refs[20], refs[21]
    head0 = pl.program_id(1) * n_heads
    s_idx = pl.program_id(2)
    dk = ctx_in[0].shape[1] // n_heads
    dv = ctx_in[2].shape[1] // n_heads

    @pl.when(s_idx == 0)
    def _():
        st_scr[...] = jnp.zeros_like(st_scr)
        row = lax.broadcasted_iota(jnp.int32, (chunk, chunk), 0).astype(F32)
        col = lax.broadcasted_iota(jnp.int32, (chunk, chunk), 1).astype(F32)
        for d, rel in enumerate((row - col, col - row)):
            for h in range(n_heads):
                lg = lg_ref[d, head0 + h]
                dec_scr[d, h] = jnp.where(rel >= 0, jnp.exp(lg * jnp.maximum(rel, 0.0)), 0.0)

    def step(ins, outs):
        qf_ref, kf_ref, vf_ref, sf_ref, qb_ref, kb_ref, vb_ref, sb_ref = ins
        of_ref, ob_ref = outs
        pos = lax.broadcasted_iota(jnp.int32, (chunk, 1), 0).astype(F32)
        dirs = (
            (qf_ref, kf_ref, vf_ref, sf_ref, of_ref, pos + 1.0, chunk - 1.0 - pos),
            (qb_ref, kb_ref, vb_ref, sb_ref, ob_ref, chunk - pos, pos),
        )
        for d, (q_ref, k_ref, v_ref, s_ref, o_ref, q_pow, k_pow) in enumerate(dirs):
            for h in range(n_heads):
                lg = lg_ref[d, head0 + h]
                decay = dec_scr[d, h]
                q_decay = jnp.exp(lg * q_pow)
                k_decay = jnp.exp(lg * k_pow)
                chunk_decay = jnp.exp(lg * chunk)
                q = q_ref[:, h * dk:(h + 1) * dk]
                k = k_ref[:, h * dk:(h + 1) * dk]
                v = v_ref[:, h * dv:(h + 1) * dv]
                state = st_scr[d, h]
                inner = _dot_nt(q, k) * decay
                o = _dot(inner.astype(BF16), v) + _dot(q, state.astype(BF16)) * q_decay
                kd = (k.astype(F32) * k_decay).astype(BF16)
                st_scr[d, h] = state * chunk_decay + _dot_tn(kd, v)
                mu = jnp.mean(o, axis=-1, keepdims=True)
                dev = o - mu
                var = jnp.mean(dev * dev, axis=-1, keepdims=True)
                y = dev * lax.rsqrt(var + EPS) * gn_ref[d:d + 1, h * dv:(h + 1) * dv]
                y = y * s_ref[:, h * dv:(h + 1) * dv].astype(F32)
                o_ref[:, h * dv:(h + 1) * dv] = y.astype(o_ref.dtype)

    @pl.when(s_idx < ctx_steps)
    def _():
        step(ctx_in, ctx_out)

    @pl.when(s_idx >= ctx_steps)
    def _():
        step(lat_in, lat_out)


def _retention(proj_lat, proj_ctx, log_gamma, gn_g, batch, seq, n_ctx):
    n_heads = log_gamma.shape[1]
    qk = proj_lat.shape[1] // 8
    v = 2 * qk
    chunk = _pick(math.gcd(seq, n_ctx), (256, 128))
    ncc, nlc = n_ctx // chunk, seq // chunk
    dk, dv = qk // n_heads, v // n_heads
    groups = 2
    hpg = n_heads // groups
    qkw, vw = hpg * dk, hpg * dv

    def ctx_f(b, s):
        return b * ncc + jnp.minimum(s, ncc - 1)

    def ctx_b(b, s):
        return b * ncc + (ncc - 1 - jnp.minimum(s, ncc - 1))

    def lat_f(b, s):
        return b * nlc + jnp.maximum(s - ncc, 0)

    def lat_b(b, s):
        return b * nlc + (nlc - 1 - jnp.maximum(s - ncc, 0))

    def in_specs(row, direction):
        return [
            pl.BlockSpec((chunk, qkw), lambda b, g, s: (row(b, s), g)),
            pl.BlockSpec((chunk, qkw), lambda b, g, s: (row(b, s), groups + g)),
            pl.BlockSpec((chunk, vw), lambda b, g, s: (row(b, s), groups + g)),
            pl.BlockSpec((chunk, vw), lambda b, g, s: (row(b, s), (2 + direction) * groups + g)),
        ]

    def out_spec(row):
        return pl.BlockSpec((chunk, vw), lambda b, g, s: (row(b, s), g))

    out_ctx = jax.ShapeDtypeStruct((batch * n_ctx, v), BF16)
    out_lat = jax.ShapeDtypeStruct((batch * seq, v), BF16)
    yf_ctx, yb_ctx, yf_lat, yb_lat = pl.pallas_call(
        functools.partial(_rt_kernel, n_heads=hpg, chunk=chunk, ctx_steps=ncc),
        grid=(batch, groups, ncc + nlc),
        in_specs=[pl.BlockSpec(memory_space=pltpu.SMEM), pl.BlockSpec((2, vw), lambda b, g, s: (0, g))]
        + in_specs(ctx_f, 0) + in_specs(ctx_b, 1) + in_specs(lat_f, 0) + in_specs(lat_b, 1),
        out_specs=[out_spec(ctx_f), out_spec(ctx_b), out_spec(lat_f), out_spec(lat_b)],
        out_shape=[out_ctx, out_ctx, out_lat, out_lat],
        scratch_shapes=[pltpu.VMEM((2, hpg, dk, dv), F32), pltpu.VMEM((2, hpg, chunk, chunk), F32)],
        compiler_params=_params("parallel", "parallel", "arbitrary"),
        name="retention",
    )(log_gamma, gn_g, *([proj_ctx] * 8), *([proj_lat] * 8))
    return (yf_lat, yb_lat), (yf_ctx, yb_ctx)


def _axial_tables(rows_count, head_dim):
    rows = jnp.repeat(jnp.arange(rows_count), GRID_W).astype(F32)
    cols = jnp.tile(jnp.arange(GRID_W), rows_count).astype(F32)
    n_freq = head_dim // 4
    inv = ROPE_THETA ** (-jnp.arange(n_freq, dtype=F32) / n_freq)
    ang = jnp.concatenate([rows[:, None] * inv, cols[:, None] * inv], -1)
    cos, sin = jnp.cos(ang), jnp.sin(ang)
    return jnp.concatenate([cos, cos], -1), jnp.concatenate([-sin, sin], -1)


def _linear_tables(n_tokens, head_dim):
    half = head_dim // 2
    inv = ROPE_THETA ** (-jnp.arange(half, dtype=F32) / half)
    ang = jnp.arange(n_tokens, dtype=F32)[:, None] * inv
    return jnp.cos(ang), jnp.sin(ang)


def kernel(x, c, ctx, c_ctx, mod_w, mod_b, norm_g, ffn_w_in, ffn_conv_w, ffn_conv_b, ffn_w_out, ga_wqkv, ga_sink, ga_qk_norm, ga_wo, rt_w_in, rt_decay, rt_gn, rt_wo, df_wqkv, df_lambda, df_qk_norm, df_subln, df_wo):
    batch, seq, d = x.shape
    n_ctx = ctx.shape[1]
    depth = mod_w.shape[0]
    assert batch + 1 <= MOD_ROWS and seq % GRID_W == 0

    cond = jnp.zeros((MOD_ROWS, d), F32).at[0].set(c_ctx).at[1:batch + 1].set(c)
    mods = _adaln_table(cond, mod_w, mod_b).reshape(depth, 6, MOD_ROWS, 1, d)

    ax_cos, ax_sin = _axial_tables(seq // GRID_W, GA_HD)
    assert DF_HD == GA_HD
    rt_dk = d // RT_HEADS
    ln_cos, ln_sin = _linear_tables(seq, rt_dk)
    ones2 = jnp.ones((2, LANES), F32)

    h_lat = x.reshape(batch * seq, d)
    h_ctx = ctx.reshape(batch * n_ctx, d)

    lat = _Stream(seq, lambda i, bm: 1 + i // (seq // bm), lambda i, bm: i % (seq // bm), True)
    cst = _Stream(batch * n_ctx, lambda i, bm: 0, lambda i, bm: 0, False)

    ga_w, ga_o = ga_wqkv.astype(BF16), ga_wo.astype(BF16)
    late = [w for w in (rt_w_in, rt_wo, df_wqkv, df_wo) if w.shape[0] > 0]
    rt_w = rt_o = df_w = df_o = None
    ffn_bf16 = {}
    n_cw = ffn_conv_w.shape[2] // MXU_COLS
    f_cw = ffn_conv_w.reshape(depth, CONV_W, n_cw, MXU_COLS).transpose(0, 2, 1, 3)
    f_cb = ffn_conv_b.reshape(depth, n_cw, 1, MXU_COLS)

    for i in range(depth):
        need_ctx = i < depth - 1
        in_place = i > 0
        kind, j = i % N_MIXERS, i // N_MIXERS
        g1 = norm_g[i, 0].reshape(1, d)
        g2 = norm_g[i, 1].reshape(1, d)

        if kind == 0:
            w_all = ga_w
            qcols = ga_sink.shape[1] * GA_HD
            kcols = (w_all.shape[2] - qcols) // 2
            segments = ((qcols, "norm_rope_128", 0, GA_HD ** -0.5 * LOG2E), (kcols, "norm_rope_128", 1, 1.0),
                        (kcols, "plain", 0, 1.0))
            cos_t, sin_t, gains = ax_cos, ax_sin, ga_qk_norm[j]
        elif kind == 1:
            w_all = rt_w
            vcols = (w_all.shape[2] - 2 * d) // 3
            segments = ((d, "rope_256", 0, rt_dk ** -0.5), (d, "rope_256", 0, 1.0),
                        (vcols, "plain", 0, 1.0), (2 * vcols, "silu", 0, 1.0))
            cos_t, sin_t, gains = ln_cos, ln_sin, ones2
        else:
            w_all = df_w
            qcols = w_all.shape[2] // 3
            segments = ((qcols, "norm_rope_128", 0, DF_HD ** -0.5 * LOG2E), (qcols, "norm_rope_128", 1, 1.0),
                        (qcols, "plain", 0, 1.0))
            cos_t, sin_t, gains = ax_cos, ax_sin, df_qk_norm[j]

        proj = {"lat": _proj(h_lat, g1, mods, i, lat, w_all, j, segments, cos_t, sin_t, gains),
                "ctx": _proj(h_ctx, g1, mods, i, cst, w_all, j, segments, cos_t, sin_t, gains)}

        todo = [l for l in range(i, min(i + 2, depth)) if l not in ffn_bf16]
        cast = [(ffn_w_in, todo[0], len(todo)), (ffn_w_out, todo[0], len(todo))] if todo and kind != 1 else None
        if kind == 0:
            y_lat, y_ctx, casts = _ga_attention(proj["lat"], proj["ctx"], ga_sink[j], batch, seq, n_ctx, need_ctx,
                                                cast)
            h_lat = _oproj(y_lat, ga_o, j, h_lat, mods, i, lat, in_place)
            if need_ctx:
                h_ctx = _oproj(y_ctx, ga_o, j, h_ctx, mods, i, cst, in_place)
        elif kind == 1:
            log_gamma = jax.nn.log_sigmoid(rt_decay[j].astype(F32))
            y_lat, y_ctx = _retention(proj["lat"], proj["ctx"], log_gamma, rt_gn[j], batch, seq, n_ctx)
            h_lat = _oproj_rt(y_lat[0], y_lat[1], rt_o, j, h_lat, mods, i, lat, in_place)
            if need_ctx:
                h_ctx = _oproj_rt(y_ctx[0], y_ctx[1], rt_o, j, h_ctx, mods, i, cst, in_place)
        else:
            lambda_init = 0.8 - 0.6 * math.exp(-0.3 * i)
            y_lat, y_ctx, casts = _df_attention(proj["lat"], proj["ctx"], df_lambda[j], df_subln[j], batch, seq,
                                                n_ctx, lambda_init, need_ctx, cast)
            h_lat = _oproj(y_lat, df_o, j, h_lat, mods, i, lat, in_place)
            if need_ctx:
                h_ctx = _oproj(y_ctx, df_o, j, h_ctx, mods, i, cst, in_place)

        if cast is not None:
            w_in_c = casts[0].reshape(len(todo), d, -1)
            w_out_c = casts[1].reshape(len(todo), -1, d)
            for n, l in enumerate(todo):
                ffn_bf16[l] = (w_in_c, w_out_c, n)
        if i not in ffn_bf16:
            ffn_bf16[i] = (ffn_w_in[i:i + 1].astype(BF16), ffn_w_out[i:i + 1].astype(BF16), 0)
        f_in, f_out, f_idx = ffn_bf16[i]
        side = [(w, 0, w.shape[0]) for w in late] if i == 0 else None
        h_lat, side_out = _ffn(h_lat, g2, mods, i, f_idx, lat, f_in, f_cw, f_cb, f_out, seq, side)
        if side:
            done = {id(w): o.reshape(w.shape) for w, o in zip(late, side_out)}
            rt_w, rt_o, df_w, df_o = (done.get(id(w)) for w in (rt_w_in, rt_wo, df_wqkv, df_wo))
        if need_ctx:
            h_ctx, _ = _ffn(h_ctx, g2, mods, i, f_idx, cst, f_in, f_cw, f_cb, f_out, n_ctx)

    return h_lat.reshape(batch, seq, d)
```

```python
import functools
import math
from typing import Callable, NamedTuple

import jax
import jax.numpy as jnp
from jax import lax
from jax.experimental import pallas as pl
from jax.experimental.pallas import tpu as pltpu

F32 = jnp.float32
BF16 = jnp.bfloat16

EPS = 1e-6
NEG_INF = -1e30
LOG2E = math.log2(math.e)
ROPE_THETA = 10000.0
GRID_W = 64
WINDOW = 128
N_MIXERS = 3

GA_HD = 128
GA_GROUP = 4
RT_HEADS = 8
DF_HD = 128
CONV_W = 3

LANES = 128
MXU_COLS = 256
BF16_ROWS = 16
ONES_ROWS = BF16_ROWS
MOD_ROWS = 8
VMEM_LIMIT = 52 * 1024 * 1024


class _Stream(NamedTuple):
    group: int
    mod_row: Callable
    tab_block: Callable
    rope: bool


def _pick(n, candidates):
    for c in candidates:
        if n % c == 0:
            return c
    raise ValueError(f"no tile size in {candidates} divides {n}")


def _params(*sem):
    return pltpu.CompilerParams(dimension_semantics=sem, vmem_limit_bytes=VMEM_LIMIT)


def _silu(x):
    return x * (1.0 / (1.0 + jnp.exp(-x)))


def _dot(a, b):
    return jnp.dot(a, b, preferred_element_type=F32)


def _dot_nt(a, b):
    return lax.dot_general(a, b, (((1,), (1,)), ((), ())), preferred_element_type=F32)


def _dot_tn(a, b):
    return lax.dot_general(a, b, (((0,), (0,)), ((), ())), preferred_element_type=F32)


def _adaln_kernel(c_ref, w_ref, b_ref, o_ref):
    s = _silu(c_ref[...]).astype(BF16)
    o_ref[...] = _dot(s, w_ref[...].astype(BF16)) + b_ref[...]


def _adaln_table(cond, mod_w, mod_b):
    depth, d, _ = mod_w.shape
    bn = _pick(d, (1024, 512, 256, 128))
    nj = d // bn
    return pl.pallas_call(
        _adaln_kernel,
        grid=(depth, 6, nj),
        in_specs=[
            pl.BlockSpec((MOD_ROWS, d), lambda l, k, j: (0, 0)),
            pl.BlockSpec((None, d, bn), lambda l, k, j: (l, 0, k * nj + j)),
            pl.BlockSpec((None, 1, bn), lambda l, k, j: (l, 0, k * nj + j)),
        ],
        out_specs=pl.BlockSpec((None, None, MOD_ROWS, bn), lambda l, k, j: (l, k, 0, j)),
        out_shape=jax.ShapeDtypeStruct((depth, 6, MOD_ROWS, d), F32),
        compiler_params=_params("parallel", "parallel", "parallel"),
        name="adaln_table",
    )(cond, mod_w, mod_b.reshape(depth, 1, 6 * d))


def _mod_spec(layer, slot, row_fn, d):
    return pl.BlockSpec((None, None, None, 1, d), lambda i, j: (layer, slot, row_fn(i), 0, 0))


def _mod_spec_cols(layer, slot, row_fn, bn):
    return pl.BlockSpec((None, None, None, 1, bn), lambda i, j: (layer, slot, row_fn(i), 0, j))


def _norm_mod(x, g, shift, scale, mean_mat=None):
    if mean_mat is None:
        r = lax.rsqrt(jnp.mean(x * x, axis=-1, keepdims=True) + EPS)
    else:
        ms = _dot((x * x).astype(BF16), mean_mat)
        r = jnp.concatenate([lax.rsqrt(ms + EPS)] * (x.shape[1] // LANES), axis=1)
    return x * r * (g * (1.0 + scale)) + shift


def _mean_mat(d):
    return jnp.full((d, LANES), 1.0 / d, BF16)


def _proj_kernel(h_ref, g_ref, sh_ref, sc_ref, mm_ref, w_ref, cos_ref, sin_ref, gain_ref, seg_ref, o_ref, a_scr, acc_scr,
                 *, plan, rope):
    j = pl.program_id(1)
    sub = acc_scr.shape[2]
    n_sub = o_ref.shape[1] // sub

    @pl.when(j == 0)
    def _():
        a_scr[...] = _norm_mod(h_ref[...], g_ref[...], sh_ref[...], sc_ref[...], mm_ref[...]).astype(BF16)

    def epilogue(s, kind, arg, scale):
        base = s * sub
        if kind == "plain":
            o_ref[:, base:base + sub] = acc_scr[s].astype(o_ref.dtype)
        elif kind == "silu":
            o_ref[:, base:base + sub] = _silu(acc_scr[s]).astype(o_ref.dtype)
        elif kind == "norm_rope_128":
            x = acc_scr[s]
            ms = _dot((x * x).astype(BF16), seg_ref[...])
            y = x * lax.rsqrt(ms + EPS) * (gain_ref[arg:arg + 1, :] * scale)
            for t in range(0, sub, LANES):
                yt = y[:, t:t + LANES]
                if rope:
                    yt = yt * cos_ref[...] + pltpu.roll(yt, LANES // 2, 1) * sin_ref[...]
                o_ref[:, base + t:base + t + LANES] = yt.astype(o_ref.dtype)
        else:
            assert kind == "rope_256" and sub == 2 * LANES
            x1 = acc_scr[s, :, 0:LANES] * scale
            x2 = acc_scr[s, :, LANES:sub] * scale
            if rope:
                c, sn = cos_ref[...], sin_ref[...]
                x1, x2 = x1 * c - x2 * sn, x1 * sn + x2 * c
            o_ref[:, base:base + LANES] = x1.astype(o_ref.dtype)
            o_ref[:, base + LANES:base + sub] = x2.astype(o_ref.dtype)

    for lo, hi, kinds in plan:
        @pl.when((j >= lo) & (j < hi))
        def _(kinds=kinds):
            a = a_scr[...]
            for s in range(n_sub):
                acc_scr[s] = _dot(a, w_ref[:, s * sub:(s + 1) * sub])
            for s, (kind, arg, scale) in enumerate(kinds):
                epilogue(s, kind, arg, scale)


def _proj(h, norm_g, mods, layer, st, w_all, w_idx, segments, cos_t, sin_t, gains):
    m, d = h.shape
    n = w_all.shape[2]
    bm = _pick(st.group, (1024, 512, 256))
    bn = _pick(n, (1024, 512, 256))
    sub = MXU_COLS
    n_sub = bn // sub
    per_sub = []
    for width, kind, arg, scale in segments:
        assert width % sub == 0
        per_sub += [(kind, arg, scale)] * (width // sub)
    assert len(per_sub) * sub == n
    tiles = [tuple(per_sub[t * n_sub:(t + 1) * n_sub]) for t in range(n // bn)]
    plan, lo = [], 0
    for t in range(1, len(tiles) + 1):
        if t == len(tiles) or tiles[t] != tiles[lo]:
            plan.append((lo, t, tiles[lo]))
            lo = t
    row_fn, tab_fn = st.mod_row, st.tab_block
    kernel = functools.partial(_proj_kernel, plan=tuple(plan), rope=st.rope)
    head = jnp.arange(sub) // LANES
    seg_mean = jnp.where(head[:, None] == head[None, :], 1.0 / LANES, 0.0).astype(BF16)
    gains = jnp.tile(gains, (1, sub // LANES))
    return pl.pallas_call(
        kernel,
        grid=(m // bm, n // bn),
        in_specs=[
            pl.BlockSpec((bm, d), lambda i, j: (i, 0)),
            pl.BlockSpec((1, d), lambda i, j: (0, 0)),
            _mod_spec(layer, 0, lambda i: row_fn(i, bm), d),
            _mod_spec(layer, 1, lambda i: row_fn(i, bm), d),
            pl.BlockSpec((d, LANES), lambda i, j: (0, 0)),
            pl.BlockSpec((None, d, bn), lambda i, j: (w_idx, 0, j)),
            pl.BlockSpec((bm, LANES), lambda i, j: (tab_fn(i, bm), 0)),
            pl.BlockSpec((bm, LANES), lambda i, j: (tab_fn(i, bm), 0)),
            pl.BlockSpec(gains.shape, lambda i, j: (0, 0)),
            pl.BlockSpec((sub, sub), lambda i, j: (0, 0)),
        ],
        out_specs=pl.BlockSpec((bm, bn), lambda i, j: (i, j)),
        out_shape=jax.ShapeDtypeStruct((m, n), BF16),
        scratch_shapes=[pltpu.VMEM((bm, d), BF16), pltpu.VMEM((n_sub, bm, sub), F32)],
        compiler_params=_params("parallel", "arbitrary"),
        name="proj",
    )(h, norm_g, mods, mods, _mean_mat(d), w_all, cos_t, sin_t, gains, seg_mean)


def _oproj_kernel(y_ref, w_ref, h_ref, gate_ref, o_ref):
    o_ref[...] = h_ref[...] + gate_ref[...] * _dot(y_ref[...], w_ref[...])


def _oproj_rt_kernel(yf_ref, yb_ref, w_ref, h_ref, gate_ref, o_ref):
    c = pl.program_id(1)

    @pl.when(c == 0)
    def _():
        o_ref[...] = jnp.zeros_like(o_ref)

    o_ref[...] += _dot(yf_ref[...] + yb_ref[...], w_ref[...])

    @pl.when(c == pl.num_programs(1) - 1)
    def _():
        o_ref[...] = h_ref[...] + gate_ref[...] * o_ref[...]


def _oproj(y, w_all, w_idx, h, mods, layer, st, in_place):
    m, k = y.shape
    d = w_all.shape[2]
    bm = _pick(st.group, (1024, 512, 256))
    row_fn = st.mod_row
    bn = _pick(d, (1024, 512, 256, 128))
    return pl.pallas_call(
        _oproj_kernel,
        grid=(m // bm, d // bn),
        in_specs=[
            pl.BlockSpec((bm, k), lambda i, j: (i, 0)),
            pl.BlockSpec((None, k, bn), lambda i, j: (w_idx, 0, j)),
            pl.BlockSpec((bm, bn), lambda i, j: (i, j)),
            _mod_spec_cols(layer, 2, lambda i: row_fn(i, bm), bn),
        ],
        out_specs=pl.BlockSpec((bm, bn), lambda i, j: (i, j)),
        out_shape=jax.ShapeDtypeStruct((m, d), F32),
        input_output_aliases={2: 0} if in_place else {},
        compiler_params=_params("parallel", "arbitrary"),
        name="oproj",
    )(y, w_all, h, mods)


def _oproj_rt(yf, yb, w_all, w_idx, h, mods, layer, st, in_place):
    m, v = yf.shape
    d = w_all.shape[2]
    bm = _pick(st.group, (1024, 512, 256))
    row_fn = st.mod_row
    ck = _pick(v, (1024, 512, 256, 128))
    return pl.pallas_call(
        _oproj_rt_kernel,
        grid=(m // bm, v // ck),
        in_specs=[
            pl.BlockSpec((bm, ck), lambda i, c: (i, c)),
            pl.BlockSpec((bm, ck), lambda i, c: (i, c)),
            pl.BlockSpec((None, ck, d), lambda i, c: (w_idx, c, 0)),
            pl.BlockSpec((bm, d), lambda i, c: (i, 0)),
            _mod_spec(layer, 2, lambda i: row_fn(i, bm), d),
        ],
        out_specs=pl.BlockSpec((bm, d), lambda i, c: (i, 0)),
        out_shape=jax.ShapeDtypeStruct((m, d), F32),
        input_output_aliases={3: 0} if in_place else {},
        compiler_params=_params("parallel", "arbitrary"),
        name="oproj_rt",
    )(yf, yb, w_all, h, mods)


def _ffn_kernel(h_ref, hp_ref, hn_ref, g_ref, sh_ref, sc_ref, gate_ref, wa_ref, wb_ref, cw_ref, cb_ref,
                wo_lo, wo_hi, wo_top, *rest, bm, seq_len, steps, n_cast):
    o_ref = rest[n_cast]
    a_scr, u_even, u_odd_a, u_odd_b = rest[-4:]
    _cast_carried(rest[:n_cast], rest[n_cast + 1:2 * n_cast + 1])
    i = pl.program_id(0)
    j = pl.program_id(1)
    halo = BF16_ROWS
    sub = u_even.shape[2]
    nsc = 2 * steps

    def up(half, u):
        cols = slice(half * sub, (half + 1) * sub)
        a = a_scr[...]
        u[0] = _dot(a, wa_ref[:, cols])
        u[1] = _dot(a, wb_ref[:, cols])

    def gate(u, k):
        if bm > seq_len:
            pos = lax.broadcasted_iota(jnp.int32, (bm, sub), 0) % seq_len
            first, final = pos == 0, pos == seq_len - 1

        def conv(idx, kk):
            cw = cw_ref[kk]
            prev = u[idx, halo - 1:halo - 1 + bm, :]
            cur = u[idx, halo:halo + bm, :]
            nxt = u[idx, halo + 1:halo + 1 + bm, :]
            if bm > seq_len:
                prev, nxt = jnp.where(first, 0.0, prev), jnp.where(final, 0.0, nxt)
            return cb_ref[kk] + prev * cw[0:1, :] + cur * cw[1:2, :] + nxt * cw[2:3, :]

        return (_silu(conv(0, k)) * conv(1, nsc + k)).astype(BF16)

    @pl.when(j == 0)
    def _():
        def nm(x):
            return _norm_mod(x, g_ref[...], sh_ref[...], sc_ref[...])

        keep_prev = jnp.where((i * bm) % seq_len != 0, 1.0, 0.0)
        keep_next = jnp.where(((i + 1) * bm) % seq_len != 0, 1.0, 0.0)
        a_scr[0:halo, :] = (nm(hp_ref[...]) * keep_prev).astype(BF16)
        a_scr[halo:halo + bm, :] = nm(h_ref[...]).astype(BF16)
        a_scr[halo + bm:, :] = (nm(hn_ref[...]) * keep_next).astype(BF16)
        o_ref[...] = jnp.zeros_like(o_ref)
        up(0, u_even)
        up(1, u_odd_a)
        o_ref[...] += _dot(gate(u_even, 0), wo_hi[...])

    def middle(u_new, u_old):
        up(0, u_even)
        up(1, u_new)
        g = jnp.concatenate([gate(u_old, 2 * j - 1), gate(u_even, 2 * j)], axis=1)
        w = jnp.concatenate([wo_lo[...], wo_hi[...]], axis=0)
        o_ref[...] += _dot(g, w)

    inner = (j > 0) & (j < steps - 1)
    pl.when(inner & (j % 2 == 1))(lambda: middle(u_odd_b, u_odd_a))
    pl.when(inner & (j % 2 == 0))(lambda: middle(u_odd_a, u_odd_b))

    @pl.when(j == steps - 1)
    def _():
        u_new, u_old = (u_odd_a, u_odd_b) if (steps - 1) % 2 == 0 else (u_odd_b, u_odd_a)
        middle(u_new, u_old)
        o_ref[...] += _dot(gate(u_new, nsc - 1), wo_top[...])
        o_ref[...] = h_ref[...] + gate_ref[...] * o_ref[...]


def _ffn(h, norm_g, mods, layer, w_idx, st, w_in, conv_w, conv_b, w_out, seq_len, cast=None):
    m, d = h.shape
    dff = w_out.shape[1]
    bm = _pick(m, (512, 256))
    assert bm % seq_len == 0 or seq_len % bm == 0
    sub = MXU_COLS
    nsc = dff // sub
    assert dff % (2 * sub) == 0 and conv_w.shape[1:] == (2 * nsc, CONV_W, sub)
    steps = nsc // 2
    assert steps >= 2
    halo = BF16_ROWS
    hb = bm // halo
    last = m // halo - 1
    c_ins, c_in_specs, c_out_specs, c_out_shapes = _cast_plan(
        cast or [], (m // bm) * steps, lambda i, j: i * steps + j)
    kernel = functools.partial(_ffn_kernel, bm=bm, seq_len=seq_len, steps=steps, n_cast=len(c_ins))
    mrow = lambda i: st.mod_row(i, bm)
    wo_rows = (lambda j: jnp.maximum(2 * j - 1, 0), lambda j: 2 * j, lambda j: nsc - 1)
    out = pl.pallas_call(
        kernel,
        grid=(m // bm, steps),
        in_specs=[
            pl.BlockSpec((bm, d), lambda i, j: (i, 0)),
            pl.BlockSpec((halo, d), lambda i, j: (jnp.maximum(i * hb - 1, 0), 0)),
            pl.BlockSpec((halo, d), lambda i, j: (jnp.minimum((i + 1) * hb, last), 0)),
            pl.BlockSpec((1, d), lambda i, j: (0, 0)),
            _mod_spec(layer, 3, mrow, d),
            _mod_spec(layer, 4, mrow, d),
            _mod_spec(layer, 5, mrow, d),
            pl.BlockSpec((None, d, 2 * sub), lambda i, j: (w_idx, 0, j)),
            pl.BlockSpec((None, d, 2 * sub), lambda i, j: (w_idx, 0, steps + j)),
            pl.BlockSpec((None, 2 * nsc, CONV_W, sub), lambda i, j: (layer, 0, 0, 0)),
            pl.BlockSpec((None, 2 * nsc, 1, sub), lambda i, j: (layer, 0, 0, 0)),
        ] + [pl.BlockSpec((None, sub, d), lambda i, j, k=k: (w_idx, k(j), 0)) for k in wo_rows] + c_in_specs,
        out_specs=[pl.BlockSpec((bm, d), lambda i, j: (i, 0))] + c_out_specs,
        out_shape=[jax.ShapeDtypeStruct((m, d), F32)] + c_out_shapes,
        scratch_shapes=[pltpu.VMEM((bm + 2 * halo, d), BF16),
                        pltpu.VMEM((2, bm + 2 * halo, sub), F32),
                        pltpu.VMEM((2, bm + 2 * halo, sub), F32),
                        pltpu.VMEM((2, bm + 2 * halo, sub), F32)],
        compiler_params=_params("arbitrary", "arbitrary"),
        name="conv_ffn",
    )(h, h, h, norm_g, mods, mods, mods, w_in, w_in, conv_w, conv_b, w_out, w_out, w_out, *c_ins)
    return out[0], out[1:]


def _cast_plan(items, n_steps, step_fn):
    ins, in_specs, out_specs, out_shapes = [], [], [], []
    for w, layer0, n_layers in items:
        depth, r, c = w.shape
        rows = n_layers * r

        def fits(n):
            b = rows // n
            return rows % n == 0 and b % BF16_ROWS == 0 and (r % b == 0 or (b % r == 0 and layer0 * r % b == 0))

        nblk = max(n for n in range(1, n_steps + 1) if fits(n))
        br = rows // nblk
        first = layer0 * r // br

        def blk(*g, nblk=nblk):
            return jnp.minimum(step_fn(*g), nblk - 1)

        ins.append(w.reshape(depth * r, c))
        in_specs.append(pl.BlockSpec((br, c), lambda *g, blk=blk, first=first: (first + blk(*g), 0)))
        out_specs.append(pl.BlockSpec((br, c), lambda *g, blk=blk: (blk(*g), 0)))
        out_shapes.append(jax.ShapeDtypeStruct((rows, c), BF16))
    return ins, in_specs, out_specs, out_shapes


def _cast_carried(in_refs, out_refs):
    for src, dst in zip(in_refs, out_refs):
        dst[...] = src[...].astype(dst.dtype)


def _ga_kernel(sink_ref, q_ref, kvc_ref, *rest, seq, n_kv, local, n_cast=0):
    if local:
        kvo_ref, kvp_ref, kvn_ref = rest[:3]
        o_ref, s_a, s_b = rest[3 + n_cast], rest[-2], rest[-1]
        _cast_carried(rest[3:3 + n_cast], rest[4 + n_cast:4 + 2 * n_cast])
    else:
        o_ref, s_a, s_b = rest
    hd = GA_HD
    kcols = n_kv * hd
    bq = q_ref.shape[0]
    sq = s_a.shape[1]
    n_ctx = kvc_ref.shape[0]
    bufs = (s_a, s_b)

    def band(u):
        if not local:
            return [], 0
        lo, hi = u * sq - WINDOW, (u + 1) * sq + WINDOW
        pieces = []
        if lo < 0:
            pieces.append((kvp_ref, WINDOW + lo, -lo))
        pieces.append((kvo_ref, max(lo, 0), min(hi, bq) - max(lo, 0)))
        if hi > bq:
            pieces.append((kvn_ref, 0, hi - bq))
        return pieces, lo

    def bias(u):
        pieces, lo = band(u)
        n_loc = sum(p[2] for p in pieces)
        kl = lax.broadcasted_iota(jnp.int32, (n_loc, sq), 0) + lo
        ql = lax.broadcasted_iota(jnp.int32, (n_loc, sq), 1) + u * sq
        pos = pl.program_id(1) * bq + kl
        ok = (jnp.abs(ql - kl) <= WINDOW) & (pos >= 0) & (pos < seq)
        return jnp.concatenate([jnp.where(ok, 0.0, NEG_INF).astype(F32), jnp.zeros((n_ctx, sq), F32)], axis=0)

    def keys_values(u, n):
        pieces, _ = band(u)
        ks, vs = slice(n * hd, (n + 1) * hd), slice(kcols + n * hd, kcols + (n + 1) * hd)
        k_all = jnp.concatenate([r[a:a + c, ks] for r, a, c in pieces] + [kvc_ref[:, ks]], axis=0)
        v_all = jnp.concatenate([r[a:a + c, vs] for r, a, c in pieces] + [kvc_ref[:, vs]], axis=0)
        v_t = jnp.concatenate([v_all.astype(F32).T.astype(BF16), jnp.ones((ONES_ROWS, v_all.shape[0]), BF16)], axis=0)
        return k_all, v_t

    chains = [(u, n, g) for u in range(bq // sq) for n in range(n_kv) for g in range(GA_GROUP)]
    cache = {}

    def operands(u, n):
        if (u, n) not in cache:
            cache.clear()
            cache[(u, n)] = keys_values(u, n)
        return cache[(u, n)]

    biases = {}

    def scores(idx):
        u, n, g = chains[idx]
        head = n * GA_GROUP + g
        k_all, _ = operands(u, n)
        s = _dot_nt(k_all, q_ref[u * sq:(u + 1) * sq, head * hd:(head + 1) * hd])
        if local:
            if u not in biases:
                biases[u] = bias(u)
            s = s + biases[u]
        bufs[idx % 2][...] = s

    scores(0)
    for idx, (u, n, g) in enumerate(chains):
        _, v_t = operands(u, n)
        if idx + 1 < len(chains):
            scores(idx + 1)
        buf = bufs[idx % 2]
        head = n * GA_GROUP + g
        sink = sink_ref[head] * LOG2E
        m = jnp.maximum(jnp.max(buf[...], axis=0, keepdims=True), sink)
        p = jnp.exp2((buf[...] - m).astype(BF16))
        ov = _dot(v_t, p)
        o_t = ov[0:hd, :] / (ov[hd:hd + 1, :] + jnp.exp2(sink - m))
        o_ref[u * sq:(u + 1) * sq, head * hd:(head + 1) * hd] = o_t.T.astype(o_ref.dtype)


def _ga_attention(qkv_lat, qkv_ctx, sink, batch, seq, n_ctx, need_ctx, cast=None):
    n_heads = sink.shape[0]
    qcols = n_heads * GA_HD
    n_kv = n_heads // GA_GROUP
    kvw = 2 * n_kv * GA_HD
    kvblk = qcols // kvw
    assert qcols % kvw == 0
    bq = _pick(seq, (512, 256, 128))
    sq = min(bq, MXU_COLS)
    nq = seq // bq
    wb = bq // WINDOW
    last_w = batch * seq // WINDOW - 1
    smem = pl.BlockSpec(memory_space=pltpu.SMEM)
    c_ins, c_in_specs, c_out_specs, c_out_shapes = _cast_plan(cast or [], batch * nq, lambda b, i: b * nq + i)
    lat = pl.pallas_call(
        functools.partial(_ga_kernel, seq=seq, n_kv=n_kv, local=True, n_cast=len(c_ins)),
        grid=(batch, nq),
        in_specs=[
            smem,
            pl.BlockSpec((bq, qcols), lambda b, i: (b * nq + i, 0)),
            pl.BlockSpec((n_ctx, kvw), lambda b, i: (b, kvblk)),
            pl.BlockSpec((bq, kvw), lambda b, i: (b * nq + i, kvblk)),
            pl.BlockSpec((WINDOW, kvw), lambda b, i: (jnp.maximum((b * nq + i) * wb - 1, 0), kvblk)),
            pl.BlockSpec((WINDOW, kvw), lambda b, i: (jnp.minimum((b * nq + i + 1) * wb, last_w), kvblk)),
        ] + c_in_specs,
        out_specs=[pl.BlockSpec((bq, qcols), lambda b, i: (b * nq + i, 0))] + c_out_specs,
        out_shape=[jax.ShapeDtypeStruct((batch * seq, qcols), BF16)] + c_out_shapes,
        scratch_shapes=[pltpu.VMEM((sq + 2 * WINDOW + n_ctx, sq), F32)] * 2,
        compiler_params=_params("arbitrary", "arbitrary"),
        name="ga_attention_latent",
    )(sink, qkv_lat, qkv_ctx, qkv_lat, qkv_lat, qkv_lat, *c_ins)
    lat, casts = lat[0], lat[1:]
    ctx = None
    if need_ctx:
        cq = min(n_ctx, MXU_COLS)
        ctx = pl.pallas_call(
            functools.partial(_ga_kernel, seq=n_ctx, n_kv=n_kv, local=False),
            grid=(batch,),
            in_specs=[
                smem,
                pl.BlockSpec((n_ctx, qcols), lambda b: (b, 0)),
                pl.BlockSpec((n_ctx, kvw), lambda b: (b, kvblk)),
            ],
            out_specs=pl.BlockSpec((n_ctx, qcols), lambda b: (b, 0)),
            out_shape=jax.ShapeDtypeStruct((batch * n_ctx, qcols), BF16),
            scratch_shapes=[pltpu.VMEM((n_ctx, cq), F32)] * 2,
            compiler_params=_params("parallel"),
            name="ga_attention_context",
        )(sink, qkv_ctx, qkv_ctx)
    return lat, ctx, casts


def _df_kernel(lam_ref, g_ref, q_ref, *rest, lambda_init, chunks, first_q_axis, n_cast=0):
    n_seg = len(chunks)
    kv_refs = rest[:2 * n_seg]
    o_ref = rest[2 * n_seg + n_cast]
    _cast_carried(rest[2 * n_seg:2 * n_seg + n_cast], rest[2 * n_seg + n_cast + 1:2 * n_seg + 2 * n_cast + 1])
    vt_scrs = rest[-3 - n_seg:-3]
    acc_scr, s_a, s_b = rest[-3:]
    hd = DF_HD
    bq = q_ref.shape[0]
    lam = lam_ref[...]
    lmbda = (jnp.exp(jnp.sum(lam[0:1] * lam[1:2], axis=-1, keepdims=True))
             - jnp.exp(jnp.sum(lam[2:3] * lam[3:4], axis=-1, keepdims=True)) + lambda_init)

    def transpose_values():
        for seg, ck in enumerate(chunks):
            v_ref, vt = kv_refs[2 * seg + 1], vt_scrs[seg]
            for c in range(v_ref.shape[0] // ck):
                vt[c] = v_ref[c * ck:(c + 1) * ck, :].astype(F32).T.astype(BF16)

    if first_q_axis is None:
        transpose_values()
    else:
        pl.when(pl.program_id(first_q_axis) == 0)(transpose_values)

    qs = (q_ref[:, 0:hd], q_ref[:, hd:2 * hd])
    acc_scr[...] = jnp.zeros_like(acc_scr)
    stat0 = (jnp.full((1, bq), NEG_INF, F32), jnp.zeros((1, bq), F32))
    carry = (stat0, stat0)

    def scores(k_ref, c, ck, dst):
        kblk = k_ref[c * ck:(c + 1) * ck, :]
        for r in range(2):
            dst[r] = _dot_nt(kblk[:, r * hd:(r + 1) * hd], qs[r])

    def update(src, v_t, carry):
        new = []
        for r in range(2):
            m_old, l_old = carry[r]
            m_new = jnp.maximum(m_old, jnp.max(src[r], axis=0, keepdims=True))
            alpha = jnp.exp2(m_old - m_new)
            p = jnp.exp2(src[r] - m_new)
            l_new = alpha * l_old + jnp.sum(p, axis=0, keepdims=True)
            acc_scr[r] = alpha * acc_scr[r] + _dot(v_t, p.astype(BF16))
            new.append((m_new, l_new))
        return tuple(new)

    items = [(kv_refs[2 * seg], c, chunks[seg], vt_scrs[seg])
             for seg in range(n_seg) for c in range(kv_refs[2 * seg].shape[0] // chunks[seg])]
    bufs = (s_a, s_b)

    def view(buf, ck):
        return buf if ck == buf.shape[1] else buf.at[:, 0:ck, :]

    scores(items[0][0], items[0][1], items[0][2], view(bufs[0], items[0][2]))
    for idx, (_, c, ck, vt) in enumerate(items):
        if idx + 1 < len(items):
            k_next, c_next, ck_next, _ = items[idx + 1]
            scores(k_next, c_next, ck_next, view(bufs[(idx + 1) % 2], ck_next))
        carry = update(view(bufs[idx % 2], ck), vt[c], carry)
    (_, l0), (_, l1) = carry
    o = (acc_scr[0] / l0 - lmbda * (acc_scr[1] / l1)).T
    ms = jnp.mean(o * o, axis=-1, keepdims=True)
    y = o * lax.rsqrt(ms + EPS) * g_ref[...] * (1.0 - lambda_init)
    o_ref[...] = y.astype(o_ref.dtype)


def _df_attention(qkv_lat, qkv_ctx, lam, subln_g, batch, seq, n_ctx, lambda_init, need_ctx, cast=None):
    hw = 2 * DF_HD
    n_heads = qkv_lat.shape[1] // (3 * hw)
    bq = _pick(seq, (512, 256, 128))
    nq = seq // bq
    ck_lat = _pick(seq, (2048, 1024, 512, 256, 128))
    ck_ctx = _pick(n_ctx, (512, 256, 128))
    g2 = subln_g.reshape(1, hw)
    const = lambda shape: pl.BlockSpec(shape, lambda *_: (0,) * len(shape))
    vt_ctx = pltpu.VMEM((n_ctx // ck_ctx, hw, ck_ctx), BF16)
    vt_lat = pltpu.VMEM((seq // ck_lat, hw, ck_lat), BF16)
    c_ins, c_in_specs, c_out_specs, c_out_shapes = _cast_plan(
        cast or [], batch * n_heads * nq, lambda b, h, i: (b * n_heads + h) * nq + i)
    lat = pl.pallas_call(
        functools.partial(_df_kernel, lambda_init=lambda_init, chunks=(ck_ctx, ck_lat), first_q_axis=2,
                          n_cast=len(c_ins)),
        grid=(batch, n_heads, nq),
        in_specs=[
            const(lam.shape),
            const((1, hw)),
            pl.BlockSpec((bq, hw), lambda b, h, i: (b * nq + i, h)),
            pl.BlockSpec((n_ctx, hw), lambda b, h, i: (b, n_heads + h)),
            pl.BlockSpec((n_ctx, hw), lambda b, h, i: (b, 2 * n_heads + h)),
            pl.BlockSpec((seq, hw), lambda b, h, i: (b, n_heads + h)),
            pl.BlockSpec((seq, hw), lambda b, h, i: (b, 2 * n_heads + h)),
        ] + c_in_specs,
        out_specs=[pl.BlockSpec((bq, hw), lambda b, h, i: (b * nq + i, h))] + c_out_specs,
        out_shape=[jax.ShapeDtypeStruct((batch * seq, n_heads * hw), BF16)] + c_out_shapes,
        scratch_shapes=[vt_ctx, vt_lat, pltpu.VMEM((2, hw, bq), F32)]
        + [pltpu.VMEM((2, max(ck_lat, ck_ctx), bq), F32)] * 2,
        compiler_params=_params("arbitrary", "arbitrary", "arbitrary"),
        name="df_attention_latent",
    )(lam, g2, qkv_lat, qkv_ctx, qkv_ctx, qkv_lat, qkv_lat, *c_ins)
    lat, casts = lat[0], lat[1:]
    ctx = None
    if need_ctx:
        ctx = pl.pallas_call(
            functools.partial(_df_kernel, lambda_init=lambda_init, chunks=(ck_ctx,), first_q_axis=None),
            grid=(batch, n_heads),
            in_specs=[
                const(lam.shape),
                const((1, hw)),
                pl.BlockSpec((n_ctx, hw), lambda b, h: (b, h)),
                pl.BlockSpec((n_ctx, hw), lambda b, h: (b, n_heads + h)),
                pl.BlockSpec((n_ctx, hw), lambda b, h: (b, 2 * n_heads + h)),
            ],
            out_specs=pl.BlockSpec((n_ctx, hw), lambda b, h: (b, h)),
            out_shape=jax.ShapeDtypeStruct((batch * n_ctx, n_heads * hw), BF16),
            scratch_shapes=[vt_ctx, pltpu.VMEM((2, hw, n_ctx), F32)]
            + [pltpu.VMEM((2, ck_ctx, n_ctx), F32)] * 2,
            compiler_params=_params("parallel", "parallel"),
            name="df_attention_context",
        )(lam, g2, qkv_ctx, qkv_ctx, qkv_ctx)
    return lat, ctx, casts


def _rt_kernel(lg_ref, gn_ref, *refs, n_heads, chunk, ctx_steps):
    ctx_in, lat_in = refs[0:8], refs[8:16]
    ctx_out, lat_out = refs[16:18], refs[18:20]
    st_scr, dec_scr = refs[20], refs[21]
    head0 = pl.program_id(1) * n_heads
    s_idx = pl.program_id(2)
    dk = ctx_in[0].shape[1] // n_heads
    dv = ctx_in[2].shape[1] // n_heads

    @pl.when(s_idx == 0)
    def _():
        st_scr[...] = jnp.zeros_like(st_scr)
        row = lax.broadcasted_iota(jnp.int32, (chunk, chunk), 0).astype(F32)
        col = lax.broadcasted_iota(jnp.int32, (chunk, chunk), 1).astype(F32)
        for d, rel in enumerate((row - col, col - row)):
            for h in range(n_heads):
                lg = lg_ref[d, head0 + h]
                dec_scr[d, h] = jnp.where(rel >= 0, jnp.exp(lg * jnp.maximum(rel, 0.0)), 0.0)

    def step(ins, outs):
        qf_ref, kf_ref, vf_ref, sf_ref, qb_ref, kb_ref, vb_ref, sb_ref = ins
        of_ref, ob_ref = outs
        pos = lax.broadcasted_iota(jnp.int32, (chunk, 1), 0).astype(F32)
        dirs = (
            (qf_ref, kf_ref, vf_ref, sf_ref, of_ref, pos + 1.0, chunk - 1.0 - pos),
            (qb_ref, kb_ref, vb_ref, sb_ref, ob_ref, chunk - pos, pos),
        )
        for d, (q_ref, k_ref, v_ref, s_ref, o_ref, q_pow, k_pow) in enumerate(dirs):
            for h in range(n_heads):
                lg = lg_ref[d, head0 + h]
                decay = dec_scr[d, h]
                q_decay = jnp.exp(lg * q_pow)
                k_decay = jnp.exp(lg * k_pow)
                chunk_decay = jnp.exp(lg * chunk)
                q = q_ref[:, h * dk:(h + 1) * dk]
                k = k_ref[:, h * dk:(h + 1) * dk]
                v = v_ref[:, h * dv:(h + 1) * dv]
                state = st_scr[d, h]
                inner = _dot_nt(q, k) * decay
                o = _dot(inner.astype(BF16), v) + _dot(q, state.astype(BF16)) * q_decay
                kd = (k.astype(F32) * k_decay).astype(BF16)
                st_scr[d, h] = state * chunk_decay + _dot_tn(kd, v)
                mu = jnp.mean(o, axis=-1, keepdims=True)
                dev = o - mu
                var = jnp.mean(dev * dev, axis=-1, keepdims=True)
                y = dev * lax.rsqrt(var + EPS) * gn_ref[d:d + 1, h * dv:(h + 1) * dv]
                y = y * s_ref[:, h * dv:(h + 1) * dv].astype(F32)
                o_ref[:, h * dv:(h + 1) * dv] = y.astype(o_ref.dtype)

    @pl.when(s_idx < ctx_steps)
    def _():
        step(ctx_in, ctx_out)

    @pl.when(s_idx >= ctx_steps)
    def _():
        step(lat_in, lat_out)


def _retention(proj_lat, proj_ctx, log_gamma, gn_g, batch, seq, n_ctx):
    n_heads = log_gamma.shape[1]
    qk = proj_lat.shape[1] // 8
    v = 2 * qk
    chunk = _pick(math.gcd(seq, n_ctx), (256, 128))
    ncc, nlc = n_ctx // chunk, seq // chunk
    dk, dv = qk // n_heads, v // n_heads
    groups = 2
    hpg = n_heads // groups
    qkw, vw = hpg * dk, hpg * dv

    def ctx_f(b, s):
        return b * ncc + jnp.minimum(s, ncc - 1)

    def ctx_b(b, s):
        return b * ncc + (ncc - 1 - jnp.minimum(s, ncc - 1))

    def lat_f(b, s):
        return b * nlc + jnp.maximum(s - ncc, 0)

    def lat_b(b, s):
        return b * nlc + (nlc - 1 - jnp.maximum(s - ncc, 0))

    def in_specs(row, direction):
        return [
            pl.BlockSpec((chunk, qkw), lambda b, g, s: (row(b, s), g)),
            pl.BlockSpec((chunk, qkw), lambda b, g, s: (row(b, s), groups + g)),
            pl.BlockSpec((chunk, vw), lambda b, g, s: (row(b, s), groups + g)),
            pl.BlockSpec((chunk, vw), lambda b, g, s: (row(b, s), (2 + direction) * groups + g)),
        ]

    def out_spec(row):
        return pl.BlockSpec((chunk, vw), lambda b, g, s: (row(b, s), g))

    out_ctx = jax.ShapeDtypeStruct((batch * n_ctx, v), BF16)
    out_lat = jax.ShapeDtypeStruct((batch * seq, v), BF16)
    yf_ctx, yb_ctx, yf_lat, yb_lat = pl.pallas_call(
        functools.partial(_rt_kernel, n_heads=hpg, chunk=chunk, ctx_steps=ncc),
        grid=(batch, groups, ncc + nlc),
        in_specs=[pl.BlockSpec(memory_space=pltpu.SMEM), pl.BlockSpec((2, vw), lambda b, g, s: (0, g))]
        + in_specs(ctx_f, 0) + in_specs(ctx_b, 1) + in_specs(lat_f, 0) + in_specs(lat_b, 1),
        out_specs=[out_spec(ctx_f), out_spec(ctx_b), out_spec(lat_f), out_spec(lat_b)],
        out_shape=[out_ctx, out_ctx, out_lat, out_lat],
        scratch_shapes=[pltpu.VMEM((2, hpg, dk, dv), F32), pltpu.VMEM((2, hpg, chunk, chunk), F32)],
        compiler_params=_params("parallel", "parallel", "arbitrary"),
        name="retention",
    )(log_gamma, gn_g, *([proj_ctx] * 8), *([proj_lat] * 8))
    return (yf_lat, yb_lat), (yf_ctx, yb_ctx)


def _axial_tables(rows_count, head_dim):
    rows = jnp.repeat(jnp.arange(rows_count), GRID_W).astype(F32)
    cols = jnp.tile(jnp.arange(GRID_W), rows_count).astype(F32)
    n_freq = head_dim // 4
    inv = ROPE_THETA ** (-jnp.arange(n_freq, dtype=F32) / n_freq)
    ang = jnp.concatenate([rows[:, None] * inv, cols[:, None] * inv], -1)
    cos, sin = jnp.cos(ang), jnp.sin(ang)
    return jnp.concatenate([cos, cos], -1), jnp.concatenate([-sin, sin], -1)


def _linear_tables(n_tokens, head_dim):
    half = head_dim // 2
    inv = ROPE_THETA ** (-jnp.arange(half, dtype=F32) / half)
    ang = jnp.arange(n_tokens, dtype=F32)[:, None] * inv
    return jnp.cos(ang), jnp.sin(ang)


def kernel(x, c, ctx, c_ctx, mod_w, mod_b, norm_g, ffn_w_in, ffn_conv_w, ffn_conv_b, ffn_w_out, ga_wqkv, ga_sink, ga_qk_norm, ga_wo, rt_w_in, rt_decay, rt_gn, rt_wo, df_wqkv, df_lambda, df_qk_norm, df_subln, df_wo):
    batch, seq, d = x.shape
    n_ctx = ctx.shape[1]
    depth = mod_w.shape[0]
    assert batch + 1 <= MOD_ROWS and seq % GRID_W == 0

    cond = jnp.zeros((MOD_ROWS, d), F32).at[0].set(c_ctx).at[1:batch + 1].set(c)
    mods = _adaln_table(cond, mod_w, mod_b).reshape(depth, 6, MOD_ROWS, 1, d)

    ax_cos, ax_sin = _axial_tables(seq // GRID_W, GA_HD)
    assert DF_HD == GA_HD
    rt_dk = d // RT_HEADS
    ln_cos, ln_sin = _linear_tables(seq, rt_dk)
    ones2 = jnp.ones((2, LANES), F32)

    h_lat = x.reshape(batch * seq, d)
    h_ctx = ctx.reshape(batch * n_ctx, d)

    lat = _Stream(seq, lambda i, bm: 1 + i // (seq // bm), lambda i, bm: i % (seq // bm), True)
    cst = _Stream(batch * n_ctx, lambda i, bm: 0, lambda i, bm: 0, False)

    ga_w, ga_o = ga_wqkv.astype(BF16), ga_wo.astype(BF16)
    late = [w for w in (rt_w_in, rt_wo, df_wqkv, df_wo) if w.shape[0] > 0]
    rt_w = rt_o = df_w = df_o = None
    ffn_bf16 = {}
    n_cw = ffn_conv_w.shape[2] // MXU_COLS
    f_cw = ffn_conv_w.reshape(depth, CONV_W, n_cw, MXU_COLS).transpose(0, 2, 1, 3)
    f_cb = ffn_conv_b.reshape(depth, n_cw, 1, MXU_COLS)

    for i in range(depth):
        need_ctx = i < depth - 1
        in_place = i > 0
        kind, j = i % N_MIXERS, i // N_MIXERS
        g1 = norm_g[i, 0].reshape(1, d)
        g2 = norm_g[i, 1].reshape(1, d)

        if kind == 0:
            w_all = ga_w
            qcols = ga_sink.shape[1] * GA_HD
            kcols = (w_all.shape[2] - qcols) // 2
            segments = ((qcols, "norm_rope_128", 0, GA_HD ** -0.5 * LOG2E), (kcols, "norm_rope_128", 1, 1.0),
                        (kcols, "plain", 0, 1.0))
            cos_t, sin_t, gains = ax_cos, ax_sin, ga_qk_norm[j]
        elif kind == 1:
            w_all = rt_w
            vcols = (w_all.shape[2] - 2 * d) // 3
            segments = ((d, "rope_256", 0, rt_dk ** -0.5), (d, "rope_256", 0, 1.0),
                        (vcols, "plain", 0, 1.0), (2 * vcols, "silu", 0, 1.0))
            cos_t, sin_t, gains = ln_cos, ln_sin, ones2
        else:
            w_all = df_w
            qcols = w_all.shape[2] // 3
            segments = ((qcols, "norm_rope_128", 0, DF_HD ** -0.5 * LOG2E), (qcols, "norm_rope_128", 1, 1.0),
                        (qcols, "plain", 0, 1.0))
            cos_t, sin_t, gains = ax_cos, ax_sin, df_qk_norm[j]

        proj = {"lat": _proj(h_lat, g1, mods, i, lat, w_all, j, segments, cos_t, sin_t, gains),
                "ctx": _proj(h_ctx, g1, mods, i, cst, w_all, j, segments, cos_t, sin_t, gains)}

        todo = [l for l in range(i, min(i + 2, depth)) if l not in ffn_bf16]
        cast = [(ffn_w_in, todo[0], len(todo)), (ffn_w_out, todo[0], len(todo))] if todo and kind != 1 else None
        if kind == 0:
            y_lat, y_ctx, casts = _ga_attention(proj["lat"], proj["ctx"], ga_sink[j], batch, seq, n_ctx, need_ctx,
                                                cast)
            h_lat = _oproj(y_lat, ga_o, j, h_lat, mods, i, lat, in_place)
            if need_ctx:
                h_ctx = _oproj(y_ctx, ga_o, j, h_ctx, mods, i, cst, in_place)
        elif kind == 1:
            log_gamma = jax.nn.log_sigmoid(rt_decay[j].astype(F32))
            y_lat, y_ctx = _retention(proj["lat"], proj["ctx"], log_gamma, rt_gn[j], batch, seq, n_ctx)
            h_lat = _oproj_rt(y_lat[0], y_lat[1], rt_o, j, h_lat, mods, i, lat, in_place)
            if need_ctx:
                h_ctx = _oproj_rt(y_ctx[0], y_ctx[1], rt_o, j, h_ctx, mods, i, cst, in_place)
        else:
            lambda_init = 0.8 - 0.6 * math.exp(-0.3 * i)
            y_lat, y_ctx, casts = _df_attention(proj["lat"], proj["ctx"], df_lambda[j], df_subln[j], batch, seq,
                                                n_ctx, lambda_init, need_ctx, cast)
            h_lat = _oproj(y_lat, df_o, j, h_lat, mods, i, lat, in_place)
            if need_ctx:
                h_ctx = _oproj(y_ctx, df_o, j, h_ctx, mods, i, cst, in_place)

        if cast is not None:
            w_in_c = casts[0].reshape(len(todo), d, -1)
            w_out_c = casts[1].reshape(len(todo), -1, d)
            for n, l in enumerate(todo):
                ffn_bf16[l] = (w_in_c, w_out_c, n)
        if i not in ffn_bf16:
            ffn_bf16[i] = (ffn_w_in[i:i + 1].astype(BF16), ffn_w_out[i:i + 1].astype(BF16), 0)
        f_in, f_out, f_idx = ffn_bf16[i]
        side = [(w, 0, w.shape[0]) for w in late] if i == 0 else None
        h_lat, side_out = _ffn(h_lat, g2, mods, i, f_idx, lat, f_in, f_cw, f_cb, f_out, seq, side)
        if side:
            done = {id(w): o.reshape(w.shape) for w, o in zip(late, side_out)}
            rt_w, rt_o, df_w, df_o = (done.get(id(w)) for w in (rt_w_in, rt_wo, df_wqkv, df_wo))
        if need_ctx:
            h_ctx, _ = _ffn(h_ctx, g2, mods, i, f_idx, cst, f_in, f_cw, f_cb, f_out, n_ctx)

    return h_lat.reshape(batch, seq, d)
```

```python
import functools
import math
from typing import Callable, NamedTuple

import jax
import jax.numpy as jnp
from jax import lax
from jax.experimental import pallas as pl
from jax.experimental.pallas import tpu as pltpu

F32 = jnp.float32
BF16 = jnp.bfloat16

EPS = 1e-6
NEG_INF = -1e30
LOG2E = math.log2(math.e)
ROPE_THETA = 10000.0
GRID_W = 64
WINDOW = 128
N_MIXERS = 3

GA_HD = 128
GA_GROUP = 4
RT_HEADS = 8
DF_HD = 128
CONV_W = 3

LANES = 128
MXU_COLS = 256
BF16_ROWS = 16
ONES_ROWS = BF16_ROWS
MOD_ROWS = 8
VMEM_LIMIT = 52 * 1024 * 1024


class _Stream(NamedTuple):
    group: int
    mod_row: Callable
    tab_block: Callable
    rope: bool


def _pick(n, candidates):
    for c in candidates:
        if n % c == 0:
            return c
    raise ValueError(f"no tile size in {candidates} divides {n}")


def _params(*sem):
    return pltpu.CompilerParams(dimension_semantics=sem, vmem_limit_bytes=VMEM_LIMIT)


def _silu(x):
    return x * (1.0 / (1.0 + jnp.exp(-x)))


def _dot(a, b):
    return jnp.dot(a, b, preferred_element_type=F32)


def _dot_nt(a, b):
    return lax.dot_general(a, b, (((1,), (1,)), ((), ())), preferred_element_type=F32)


def _dot_tn(a, b):
    return lax.dot_general(a, b, (((0,), (0,)), ((), ())), preferred_element_type=F32)


def _adaln_kernel(c_ref, w_ref, b_ref, o_ref):
    s = _silu(c_ref[...]).astype(BF16)
    o_ref[...] = _dot(s, w_ref[...].astype(BF16)) + b_ref[...]


def _adaln_table(cond, mod_w, mod_b):
    depth, d, _ = mod_w.shape
    bn = _pick(d, (1024, 512, 256, 128))
    nj = d // bn
    return pl.pallas_call(
        _adaln_kernel,
        grid=(depth, 6, nj),
        in_specs=[
            pl.BlockSpec((MOD_ROWS, d), lambda l, k, j: (0, 0)),
            pl.BlockSpec((None, d, bn), lambda l, k, j: (l, 0, k * nj + j)),
            pl.BlockSpec((None, 1, bn), lambda l, k, j: (l, 0, k * nj + j)),
        ],
        out_specs=pl.BlockSpec((None, None, MOD_ROWS, bn), lambda l, k, j: (l, k, 0, j)),
        out_shape=jax.ShapeDtypeStruct((depth, 6, MOD_ROWS, d), F32),
        compiler_params=_params("parallel", "parallel", "parallel"),
        name="adaln_table",
    )(cond, mod_w, mod_b.reshape(depth, 1, 6 * d))


def _mod_spec(layer, slot, row_fn, d):
    return pl.BlockSpec((None, None, None, 1, d), lambda i, j: (layer, slot, row_fn(i), 0, 0))


def _mod_spec_cols(layer, slot, row_fn, bn):
    return pl.BlockSpec((None, None, None, 1, bn), lambda i, j: (layer, slot, row_fn(i), 0, j))


def _norm_mod(x, g, shift, scale, mean_mat=None):
    if mean_mat is None:
        r = lax.rsqrt(jnp.mean(x * x, axis=-1, keepdims=True) + EPS)
    else:
        ms = _dot((x * x).astype(BF16), mean_mat)
        r = jnp.concatenate([lax.rsqrt(ms + EPS)] * (x.shape[1] // LANES), axis=1)
    return x * r * (g * (1.0 + scale)) + shift


def _mean_mat(d):
    return jnp.full((d, LANES), 1.0 / d, BF16)


def _proj_kernel(h_ref, g_ref, sh_ref, sc_ref, mm_ref, w_ref, cos_ref, sin_ref, gain_ref, seg_ref, o_ref, a_scr, acc_scr,
                 *, plan, rope):
    j = pl.program_id(1)
    sub = acc_scr.shape[2]
    n_sub = o_ref.shape[1] // sub

    @pl.when(j == 0)
    def _():
        a_scr[...] = _norm_mod(h_ref[...], g_ref[...], sh_ref[...], sc_ref[...], mm_ref[...]).astype(BF16)

    def epilogue(s, kind, arg, scale):
        base = s * sub
        if kind == "plain":
            o_ref[:, base:base + sub] = acc_scr[s].astype(o_ref.dtype)
        elif kind == "silu":
            o_ref[:, base:base + sub] = _silu(acc_scr[s]).astype(o_ref.dtype)
        elif kind == "norm_rope_128":
            x = acc_scr[s]
            ms = _dot((x * x).astype(BF16), seg_ref[...])
            y = x * lax.rsqrt(ms + EPS) * (gain_ref[arg:arg + 1, :] * scale)
            for t in range(0, sub, LANES):
                yt = y[:, t:t + LANES]
                if rope:
                    yt = yt * cos_ref[...] + pltpu.roll(yt, LANES // 2, 1) * sin_ref[...]
                o_ref[:, base + t:base + t + LANES] = yt.astype(o_ref.dtype)
        else:
            assert kind == "rope_256" and sub == 2 * LANES
            x1 = acc_scr[s, :, 0:LANES] * scale
            x2 = acc_scr[s, :, LANES:sub] * scale
            if rope:
                c, sn = cos_ref[...], sin_ref[...]
                x1, x2 = x1 * c - x2 * sn, x1 * sn + x2 * c
            o_ref[:, base:base + LANES] = x1.astype(o_ref.dtype)
            o_ref[:, base + LANES:base + sub] = x2.astype(o_ref.dtype)

    for lo, hi, kinds in plan:
        @pl.when((j >= lo) & (j < hi))
        def _(kinds=kinds):
            a = a_scr[...]
            for s in range(n_sub):
                acc_scr[s] = _dot(a, w_ref[:, s * sub:(s + 1) * sub])
            for s, (kind, arg, scale) in enumerate(kinds):
                epilogue(s, kind, arg, scale)


def _proj(h, norm_g, mods, layer, st, w_all, w_idx, segments, cos_t, sin_t, gains):
    m, d = h.shape
    n = w_all.shape[2]
    bm = _pick(st.group, (1024, 512, 256))
    bn = _pick(n, (1024, 512, 256))
    sub = MXU_COLS
    n_sub = bn // sub
    per_sub = []
    for width, kind, arg, scale in segments:
        assert width % sub == 0
        per_sub += [(kind, arg, scale)] * (width // sub)
    assert len(per_sub) * sub == n
    tiles = [tuple(per_sub[t * n_sub:(t + 1) * n_sub]) for t in range(n // bn)]
    plan, lo = [], 0
    for t in range(1, len(tiles) + 1):
        if t == len(tiles) or tiles[t] != tiles[lo]:
            plan.append((lo, t, tiles[lo]))
            lo = t
    row_fn, tab_fn = st.mod_row, st.tab_block
    kernel = functools.partial(_proj_kernel, plan=tuple(plan), rope=st.rope)
    head = jnp.arange(sub) // LANES
    seg_mean = jnp.where(head[:, None] == head[None, :], 1.0 / LANES, 0.0).astype(BF16)
    gains = jnp.tile(gains, (1, sub // LANES))
    return pl.pallas_call(
        kernel,
        grid=(m // bm, n // bn),
        in_specs=[
            pl.BlockSpec((bm, d), lambda i, j: (i, 0)),
            pl.BlockSpec((1, d), lambda i, j: (0, 0)),
            _mod_spec(layer, 0, lambda i: row_fn(i, bm), d),
            _mod_spec(layer, 1, lambda i: row_fn(i, bm), d),
            pl.BlockSpec((d, LANES), lambda i, j: (0, 0)),
            pl.BlockSpec((None, d, bn), lambda i, j: (w_idx, 0, j)),
            pl.BlockSpec((bm, LANES), lambda i, j: (tab_fn(i, bm), 0)),
            pl.BlockSpec((bm, LANES), lambda i, j: (tab_fn(i, bm), 0)),
            pl.BlockSpec(gains.shape, lambda i, j: (0, 0)),
            pl.BlockSpec((sub, sub), lambda i, j: (0, 0)),
        ],
        out_specs=pl.BlockSpec((bm, bn), lambda i, j: (i, j)),
        out_shape=jax.ShapeDtypeStruct((m, n), BF16),
        scratch_shapes=[pltpu.VMEM((bm, d), BF16), pltpu.VMEM((n_sub, bm, sub), F32)],
        compiler_params=_params("parallel", "arbitrary"),
        name="proj",
    )(h, norm_g, mods, mods, _mean_mat(d), w_all, cos_t, sin_t, gains, seg_mean)


def _oproj_kernel(y_ref, w_ref, h_ref, gate_ref, o_ref):
    o_ref[...] = h_ref[...] + gate_ref[...] * _dot(y_ref[...], w_ref[...])


def _oproj_rt_kernel(yf_ref, yb_ref, w_ref, h_ref, gate_ref, o_ref):
    c = pl.program_id(1)

    @pl.when(c == 0)
    def _():
        o_ref[...] = jnp.zeros_like(o_ref)

    o_ref[...] += _dot(yf_ref[...] + yb_ref[...], w_ref[...])

    @pl.when(c == pl.num_programs(1) - 1)
    def _():
        o_ref[...] = h_ref[...] + gate_ref[...] * o_ref[...]


def _oproj(y, w_all, w_idx, h, mods, layer, st, in_place):
    m, k = y.shape
    d = w_all.shape[2]
    bm = _pick(st.group, (1024, 512, 256))
    row_fn = st.mod_row
    bn = _pick(d, (1024, 512, 256, 128))
    return pl.pallas_call(
        _oproj_kernel,
        grid=(m // bm, d // bn),
        in_specs=[
            pl.BlockSpec((bm, k), lambda i, j: (i, 0)),
            pl.BlockSpec((None, k, bn), lambda i, j: (w_idx, 0, j)),
            pl.BlockSpec((bm, bn), lambda i, j: (i, j)),
            _mod_spec_cols(layer, 2, lambda i: row_fn(i, bm), bn),
        ],
        out_specs=pl.BlockSpec((bm, bn), lambda i, j: (i, j)),
        out_shape=jax.ShapeDtypeStruct((m, d), F32),
        input_output_aliases={2: 0} if in_place else {},
        compiler_params=_params("parallel", "arbitrary"),
        name="oproj",
    )(y, w_all, h, mods)


def _oproj_rt(yf, yb, w_all, w_idx, h, mods, layer, st, in_place):
    m, v = yf.shape
    d = w_all.shape[2]
    bm = _pick(st.group, (1024, 512, 256))
    row_fn = st.mod_row
    ck = _pick(v, (1024, 512, 256, 128))
    return pl.pallas_call(
        _oproj_rt_kernel,
        grid=(m // bm, v // ck),
        in_specs=[
            pl.BlockSpec((bm, ck), lambda i, c: (i, c)),
            pl.BlockSpec((bm, ck), lambda i, c: (i, c)),
            pl.BlockSpec((None, ck, d), lambda i, c: (w_idx, c, 0)),
            pl.BlockSpec((bm, d), lambda i, c: (i, 0)),
            _mod_spec(layer, 2, lambda i: row_fn(i, bm), d),
        ],
        out_specs=pl.BlockSpec((bm, d), lambda i, c: (i, 0)),
        out_shape=jax.ShapeDtypeStruct((m, d), F32),
        input_output_aliases={3: 0} if in_place else {},
        compiler_params=_params("parallel", "arbitrary"),
        name="oproj_rt",
    )(yf, yb, w_all, h, mods)


def _ffn_kernel(h_ref, hp_ref, hn_ref, g_ref, sh_ref, sc_ref, gate_ref, wa_ref, wb_ref, cw_ref, cb_ref,
                wo_lo, wo_hi, wo_top, *rest, bm, seq_len, steps, n_cast):
    o_ref = rest[n_cast]
    a_scr, u_even, u_odd_a, u_odd_b = rest[-4:]
    _cast_carried(rest[:n_cast], rest[n_cast + 1:2 * n_cast + 1])
    i = pl.program_id(0)
    j = pl.program_id(1)
    halo = BF16_ROWS
    sub = u_even.shape[2]
    nsc = 2 * steps

    def up(half, u):
        cols = slice(half * sub, (half + 1) * sub)
        a = a_scr[...]
        u[0] = _dot(a, wa_ref[:, cols])
        u[1] = _dot(a, wb_ref[:, cols])

    def gate(u, k):
        if bm > seq_len:
            pos = lax.broadcasted_iota(jnp.int32, (bm, sub), 0) % seq_len
            first, final = pos == 0, pos == seq_len - 1

        def conv(idx, kk):
            cw = cw_ref[kk]
            prev = u[idx, halo - 1:halo - 1 + bm, :]
            cur = u[idx, halo:halo + bm, :]
            nxt = u[idx, halo + 1:halo + 1 + bm, :]
            if bm > seq_len:
                prev, nxt = jnp.where(first, 0.0, prev), jnp.where(final, 0.0, nxt)
            return cb_ref[kk] + prev * cw[0:1, :] + cur * cw[1:2, :] + nxt * cw[2:3, :]

        return (_silu(conv(0, k)) * conv(1, nsc + k)).astype(BF16)

    @pl.when(j == 0)
    def _():
        def nm(x):
            return _norm_mod(x, g_ref[...], sh_ref[...], sc_ref[...])

        keep_prev = jnp.where((i * bm) % seq_len != 0, 1.0, 0.0)
        keep_next = jnp.where(((i + 1) * bm) % seq_len != 0, 1.0, 0.0)
        a_scr[0:halo, :] = (nm(hp_ref[...]) * keep_prev).astype(BF16)
        a_scr[halo:halo + bm, :] = nm(h_ref[...]).astype(BF16)
        a_scr[halo + bm:, :] = (nm(hn_ref[...]) * keep_next).astype(BF16)
        o_ref[...] = jnp.zeros_like(o_ref)
        up(0, u_even)
        up(1, u_odd_a)
        o_ref[...] += _dot(gate(u_even, 0), wo_hi[...])

    def middle(u_new, u_old):
        up(0, u_even)
        up(1, u_new)
        g = jnp.concatenate([gate(u_old, 2 * j - 1), gate(u_even, 2 * j)], axis=1)
        w = jnp.concatenate([wo_lo[...], wo_hi[...]], axis=0)
        o_ref[...] += _dot(g, w)

    inner = (j > 0) & (j < steps - 1)
    pl.when(inner & (j % 2 == 1))(lambda: middle(u_odd_b, u_odd_a))
    pl.when(inner & (j % 2 == 0))(lambda: middle(u_odd_a, u_odd_b))

    @pl.when(j == steps - 1)
    def _():
        u_new, u_old = (u_odd_a, u_odd_b) if (steps - 1) % 2 == 0 else (u_odd_b, u_odd_a)
        middle(u_new, u_old)
        o_ref[...] += _dot(gate(u_new, nsc - 1), wo_top[...])
        o_ref[...] = h_ref[...] + gate_ref[...] * o_ref[...]


def _ffn(h, norm_g, mods, layer, w_idx, st, w_in, conv_w, conv_b, w_out, seq_len, cast=None):
    m, d = h.shape
    dff = w_out.shape[1]
    bm = _pick(m, (512, 256))
    assert bm % seq_len == 0 or seq_len % bm == 0
    sub = MXU_COLS
    nsc = dff // sub
    assert dff % (2 * sub) == 0 and conv_w.shape[1:] == (2 * nsc, CONV_W, sub)
    steps = nsc // 2
    assert steps >= 2
    halo = BF16_ROWS
    hb = bm // halo
    last = m // halo - 1
    c_ins, c_in_specs, c_out_specs, c_out_shapes = _cast_plan(
        cast or [], (m // bm) * steps, lambda i, j: i * steps + j)
    kernel = functools.partial(_ffn_kernel, bm=bm, seq_len=seq_len, steps=steps, n_cast=len(c_ins))
    mrow = lambda i: st.mod_row(i, bm)
    wo_rows = (lambda j: jnp.maximum(2 * j - 1, 0), lambda j: 2 * j, lambda j: nsc - 1)
    out = pl.pallas_call(
        kernel,
        grid=(m // bm, steps),
        in_specs=[
            pl.BlockSpec((bm, d), lambda i, j: (i, 0)),
            pl.BlockSpec((halo, d), lambda i, j: (jnp.maximum(i * hb - 1, 0), 0)),
            pl.BlockSpec((halo, d), lambda i, j: (jnp.minimum((i + 1) * hb, last), 0)),
            pl.BlockSpec((1, d), lambda i, j: (0, 0)),
            _mod_spec(layer, 3, mrow, d),
            _mod_spec(layer, 4, mrow, d),
            _mod_spec(layer, 5, mrow, d),
            pl.BlockSpec((None, d, 2 * sub), lambda i, j: (w_idx, 0, j)),
            pl.BlockSpec((None, d, 2 * sub), lambda i, j: (w_idx, 0, steps + j)),
            pl.BlockSpec((None, 2 * nsc, CONV_W, sub), lambda i, j: (layer, 0, 0, 0)),
            pl.BlockSpec((None, 2 * nsc, 1, sub), lambda i, j: (layer, 0, 0, 0)),
        ] + [pl.BlockSpec((None, sub, d), lambda i, j, k=k: (w_idx, k(j), 0)) for k in wo_rows] + c_in_specs,
        out_specs=[pl.BlockSpec((bm, d), lambda i, j: (i, 0))] + c_out_specs,
        out_shape=[jax.ShapeDtypeStruct((m, d), F32)] + c_out_shapes,
        scratch_shapes=[pltpu.VMEM((bm + 2 * halo, d), BF16),
                        pltpu.VMEM((2, bm + 2 * halo, sub), F32),
                        pltpu.VMEM((2, bm + 2 * halo, sub), F32),
                        pltpu.VMEM((2, bm + 2 * halo, sub), F32)],
        compiler_params=_params("arbitrary", "arbitrary"),
        name="conv_ffn",
    )(h, h, h, norm_g, mods, mods, mods, w_in, w_in, conv_w, conv_b, w_out, w_out, w_out, *c_ins)
    return out[0], out[1:]


def _cast_plan(items, n_steps, step_fn):
    ins, in_specs, out_specs, out_shapes = [], [], [], []
    for w, layer0, n_layers in items:
        depth, r, c = w.shape
        rows = n_layers * r

        def fits(n):
            b = rows // n
            return rows % n == 0 and b % BF16_ROWS == 0 and (r % b == 0 or (b % r == 0 and layer0 * r % b == 0))

        nblk = max(n for n in range(1, n_steps + 1) if fits(n))
        br = rows // nblk
        first = layer0 * r // br

        def blk(*g, nblk=nblk):
            return jnp.minimum(step_fn(*g), nblk - 1)

        ins.append(w.reshape(depth * r, c))
        in_specs.append(pl.BlockSpec((br, c), lambda *g, blk=blk, first=first: (first + blk(*g), 0)))
        out_specs.append(pl.BlockSpec((br, c), lambda *g, blk=blk: (blk(*g), 0)))
        out_shapes.append(jax.ShapeDtypeStruct((rows, c), BF16))
    return ins, in_specs, out_specs, out_shapes


def _cast_carried(in_refs, out_refs):
    for src, dst in zip(in_refs, out_refs):
        dst[...] = src[...].astype(dst.dtype)


def _ga_kernel(sink_ref, q_ref, kvc_ref, *rest, seq, n_kv, local, n_cast=0):
    if local:
        kvo_ref, kvp_ref, kvn_ref = rest[:3]
        o_ref, s_a, s_b = rest[3 + n_cast], rest[-2], rest[-1]
        _cast_carried(rest[3:3 + n_cast], rest[4 + n_cast:4 + 2 * n_cast])
    else:
        o_ref, s_a, s_b = rest
    hd = GA_HD
    kcols = n_kv * hd
    bq = q_ref.shape[0]
    sq = s_a.shape[1]
    n_ctx = kvc_ref.shape[0]
    bufs = (s_a, s_b)

    def band(u):
        if not local:
            return [], 0
        lo, hi = u * sq - WINDOW, (u + 1) * sq + WINDOW
        pieces = []
        if lo < 0:
            pieces.append((kvp_ref, WINDOW + lo, -lo))
        pieces.append((kvo_ref, max(lo, 0), min(hi, bq) - max(lo, 0)))
        if hi > bq:
            pieces.append((kvn_ref, 0, hi - bq))
        return pieces, lo

    def bias(u):
        pieces, lo = band(u)
        n_loc = sum(p[2] for p in pieces)
        kl = lax.broadcasted_iota(jnp.int32, (n_loc, sq), 0) + lo
        ql = lax.broadcasted_iota(jnp.int32, (n_loc, sq), 1) + u * sq
        pos = pl.program_id(1) * bq + kl
        ok = (jnp.abs(ql - kl) <= WINDOW) & (pos >= 0) & (pos < seq)
        return jnp.concatenate([jnp.where(ok, 0.0, NEG_INF).astype(F32), jnp.zeros((n_ctx, sq), F32)], axis=0)

    def keys_values(u, n):
        pieces, _ = band(u)
        ks, vs = slice(n * hd, (n + 1) * hd), slice(kcols + n * hd, kcols + (n + 1) * hd)
        k_all = jnp.concatenate([r[a:a + c, ks] for r, a, c in pieces] + [kvc_ref[:, ks]], axis=0)
        v_all = jnp.concatenate([r[a:a + c, vs] for r, a, c in pieces] + [kvc_ref[:, vs]], axis=0)
        v_t = jnp.concatenate([v_all.astype(F32).T.astype(BF16), jnp.ones((ONES_ROWS, v_all.shape[0]), BF16)], axis=0)
        return k_all, v_t

    chains = [(u, n, g) for u in range(bq // sq) for n in range(n_kv) for g in range(GA_GROUP)]
    cache = {}

    def operands(u, n):
        if (u, n) not in cache:
            cache.clear()
            cache[(u, n)] = keys_values(u, n)
        return cache[(u, n)]

    biases = {}

    def scores(idx):
        u, n, g = chains[idx]
        head = n * GA_GROUP + g
        k_all, _ = operands(u, n)
        s = _dot_nt(k_all, q_ref[u * sq:(u + 1) * sq, head * hd:(head + 1) * hd])
        if local:
            if u not in biases:
                biases[u] = bias(u)
            s = s + biases[u]
        bufs[idx % 2][...] = s

    scores(0)
    for idx, (u, n, g) in enumerate(chains):
        _, v_t = operands(u, n)
        if idx + 1 < len(chains):
            scores(idx + 1)
        buf = bufs[idx % 2]
        head = n * GA_GROUP + g
        sink = sink_ref[head] * LOG2E
        m = jnp.maximum(jnp.max(buf[...], axis=0, keepdims=True), sink)
        p = jnp.exp2((buf[...] - m).astype(BF16))
        ov = _dot(v_t, p)
        o_t = ov[0:hd, :] / (ov[hd:hd + 1, :] + jnp.exp2(sink - m))
        o_ref[u * sq:(u + 1) * sq, head * hd:(head + 1) * hd] = o_t.T.astype(o_ref.dtype)


def _ga_attention(qkv_lat, qkv_ctx, sink, batch, seq, n_ctx, need_ctx, cast=None):
    n_heads = sink.shape[0]
    qcols = n_heads * GA_HD
    n_kv = n_heads // GA_GROUP
    kvw = 2 * n_kv * GA_HD
    kvblk = qcols // kvw
    assert qcols % kvw == 0
    bq = _pick(seq, (512, 256, 128))
    sq = min(bq, MXU_COLS)
    nq = seq // bq
    wb = bq // WINDOW
    last_w = batch * seq // WINDOW - 1
    smem = pl.BlockSpec(memory_space=pltpu.SMEM)
    c_ins, c_in_specs, c_out_specs, c_out_shapes = _cast_plan(cast or [], batch * nq, lambda b, i: b * nq + i)
    lat = pl.pallas_call(
        functools.partial(_ga_kernel, seq=seq, n_kv=n_kv, local=True, n_cast=len(c_ins)),
        grid=(batch, nq),
        in_specs=[
            smem,
            pl.BlockSpec((bq, qcols), lambda b, i: (b * nq + i, 0)),
            pl.BlockSpec((n_ctx, kvw), lambda b, i: (b, kvblk)),
            pl.BlockSpec((bq, kvw), lambda b, i: (b * nq + i, kvblk)),
            pl.BlockSpec((WINDOW, kvw), lambda b, i: (jnp.maximum((b * nq + i) * wb - 1, 0), kvblk)),
            pl.BlockSpec((WINDOW, kvw), lambda b, i: (jnp.minimum((b * nq + i + 1) * wb, last_w), kvblk)),
        ] + c_in_specs,
        out_specs=[pl.BlockSpec((bq, qcols), lambda b, i: (b * nq + i, 0))] + c_out_specs,
        out_shape=[jax.ShapeDtypeStruct((batch * seq, qcols), BF16)] + c_out_shapes,
        scratch_shapes=[pltpu.VMEM((sq + 2 * WINDOW + n_ctx, sq), F32)] * 2,
        compiler_params=_params("arbitrary", "arbitrary"),
        name="ga_attention_latent",
    )(sink, qkv_lat, qkv_ctx, qkv_lat, qkv_lat, qkv_lat, *c_ins)
    lat, casts = lat[0], lat[1:]
    ctx = None
    if need_ctx:
        cq = min(n_ctx, MXU_COLS)
        ctx = pl.pallas_call(
            functools.partial(_ga_kernel, seq=n_ctx, n_kv=n_kv, local=False),
            grid=(batch,),
            in_specs=[
                smem,
                pl.BlockSpec((n_ctx, qcols), lambda b: (b, 0)),
                pl.BlockSpec((n_ctx, kvw), lambda b: (b, kvblk)),
            ],
            out_specs=pl.BlockSpec((n_ctx, qcols), lambda b: (b, 0)),
            out_shape=jax.ShapeDtypeStruct((batch * n_ctx, qcols), BF16),
            scratch_shapes=[pltpu.VMEM((n_ctx, cq), F32)] * 2,
            compiler_params=_params("parallel"),
            name="ga_attention_context",
        )(sink, qkv_ctx, qkv_ctx)
    return lat, ctx, casts


def _df_kernel(lam_ref, g_ref, q_ref, *rest, lambda_init, chunks, first_q_axis, n_cast=0):
    n_seg = len(chunks)
    kv_refs = rest[:2 * n_seg]
    o_ref = rest[2 * n_seg + n_cast]
    _cast_carried(rest[2 * n_seg:2 * n_seg + n_cast], rest[2 * n_seg + n_cast + 1:2 * n_seg + 2 * n_cast + 1])
    vt_scrs = rest[-3 - n_seg:-3]
    acc_scr, s_a, s_b = rest[-3:]
    hd = DF_HD
    bq = q_ref.shape[0]
    lam = lam_ref[...]
    lmbda = (jnp.exp(jnp.sum(lam[0:1] * lam[1:2], axis=-1, keepdims=True))
             - jnp.exp(jnp.sum(lam[2:3] * lam[3:4], axis=-1, keepdims=True)) + lambda_init)

    def transpose_values():
        for seg, ck in enumerate(chunks):
            v_ref, vt = kv_refs[2 * seg + 1], vt_scrs[seg]
            for c in range(v_ref.shape[0] // ck):
                vt[c] = v_ref[c * ck:(c + 1) * ck, :].astype(F32).T.astype(BF16)

    if first_q_axis is None:
        transpose_values()
    else:
        pl.when(pl.program_id(first_q_axis) == 0)(transpose_values)

    qs = (q_ref[:, 0:hd], q_ref[:, hd:2 * hd])
    acc_scr[...] = jnp.zeros_like(acc_scr)
    stat0 = (jnp.full((1, bq), NEG_INF, F32), jnp.zeros((1, bq), F32))
    carry = (stat0, stat0)

    def scores(k_ref, c, ck, dst):
        kblk = k_ref[c * ck:(c + 1) * ck, :]
        for r in range(2):
            dst[r] = _dot_nt(kblk[:, r * hd:(r + 1) * hd], qs[r])

    def update(src, v_t, carry):
        new = []
        for r in range(2):
            m_old, l_old = carry[r]
            m_new = jnp.maximum(m_old, jnp.max(src[r], axis=0, keepdims=True))
            alpha = jnp.exp2(m_old - m_new)
            p = jnp.exp2(src[r] - m_new)
            l_new = alpha * l_old + jnp.sum(p, axis=0, keepdims=True)
            acc_scr[r] = alpha * acc_scr[r] + _dot(v_t, p.astype(BF16))
            new.append((m_new, l_new))
        return tuple(new)

    items = [(kv_refs[2 * seg], c, chunks[seg], vt_scrs[seg])
             for seg in range(n_seg) for c in range(kv_refs[2 * seg].shape[0] // chunks[seg])]
    bufs = (s_a, s_b)

    def view(buf, ck):
        return buf if ck == buf.shape[1] else buf.at[:, 0:ck, :]

    scores(items[0][0], items[0][1], items[0][2], view(bufs[0], items[0][2]))
    for idx, (_, c, ck, vt) in enumerate(items):
        if idx + 1 < len(items):
            k_next, c_next, ck_next, _ = items[idx + 1]
            scores(k_next, c_next, ck_next, view(bufs[(idx + 1) % 2], ck_next))
        carry = update(view(bufs[idx % 2], ck), vt[c], carry)
    (_, l0), (_, l1) = carry
    o = (acc_scr[0] / l0 - lmbda * (acc_scr[1] / l1)).T
    ms = jnp.mean(o * o, axis=-1, keepdims=True)
    y = o * lax.rsqrt(ms + EPS) * g_ref[...] * (1.0 - lambda_init)
    o_ref[...] = y.astype(o_ref.dtype)


def _df_attention(qkv_lat, qkv_ctx, lam, subln_g, batch, seq, n_ctx, lambda_init, need_ctx, cast=None):
    hw = 2 * DF_HD
    n_heads = qkv_lat.shape[1] // (3 * hw)
    bq = _pick(seq, (512, 256, 128))
    nq = seq // bq
    ck_lat = _pick(seq, (4096, 2048, 1024, 512, 256, 128))
    ck_ctx = _pick(n_ctx, (512, 256, 128))
    g2 = subln_g.reshape(1, hw)
    const = lambda shape: pl.BlockSpec(shape, lambda *_: (0,) * len(shape))
    vt_ctx = pltpu.VMEM((n_ctx // ck_ctx, hw, ck_ctx), BF16)
    vt_lat = pltpu.VMEM((seq // ck_lat, hw, ck_lat), BF16)
    c_ins, c_in_specs, c_out_specs, c_out_shapes = _cast_plan(
        cast or [], batch * n_heads * nq, lambda b, h, i: (b * n_heads + h) * nq + i)
    lat = pl.pallas_call(
        functools.partial(_df_kernel, lambda_init=lambda_init, chunks=(ck_ctx, ck_lat), first_q_axis=2,
                          n_cast=len(c_ins)),
        grid=(batch, n_heads, nq),
        in_specs=[
            const(lam.shape),
            const((1, hw)),
            pl.BlockSpec((bq, hw), lambda b, h, i: (b * nq + i, h)),
            pl.BlockSpec((n_ctx, hw), lambda b, h, i: (b, n_heads + h)),
            pl.BlockSpec((n_ctx, hw), lambda b, h, i: (b, 2 * n_heads + h)),
            pl.BlockSpec((seq, hw), lambda b, h, i: (b, n_heads + h)),
            pl.BlockSpec((seq, hw), lambda b, h, i: (b, 2 * n_heads + h)),
        ] + c_in_specs,
        out_specs=[pl.BlockSpec((bq, hw), lambda b, h, i: (b * nq + i, h))] + c_out_specs,
        out_shape=[jax.ShapeDtypeStruct((batch * seq, n_heads * hw), BF16)] + c_out_shapes,
        scratch_shapes=[vt_ctx, vt_lat, pltpu.VMEM((2, hw, bq), F32)]
        + [pltpu.VMEM((2, max([ck_ctx] + ([ck_lat] if seq // ck_lat > 1 else [])), bq), F32),
           pltpu.VMEM((2, ck_lat, bq), F32)],
        compiler_params=_params("arbitrary", "arbitrary", "arbitrary"),
        name="df_attention_latent",
    )(lam, g2, qkv_lat, qkv_ctx, qkv_ctx, qkv_lat, qkv_lat, *c_ins)
    lat, casts = lat[0], lat[1:]
    ctx = None
    if need_ctx:
        ctx = pl.pallas_call(
            functools.partial(_df_kernel, lambda_init=lambda_init, chunks=(ck_ctx,), first_q_axis=None),
            grid=(batch, n_heads),
            in_specs=[
                const(lam.shape),
                const((1, hw)),
                pl.BlockSpec((n_ctx, hw), lambda b, h: (b, h)),
                pl.BlockSpec((n_ctx, hw), lambda b, h: (b, n_heads + h)),
                pl.BlockSpec((n_ctx, hw), lambda b, h: (b, 2 * n_heads + h)),
            ],
            out_specs=pl.BlockSpec((n_ctx, hw), lambda b, h: (b, h)),
            out_shape=jax.ShapeDtypeStruct((batch * n_ctx, n_heads * hw), BF16),
            scratch_shapes=[vt_ctx, pltpu.VMEM((2, hw, n_ctx), F32)]
            + [pltpu.VMEM((2, ck_ctx, n_ctx), F32)] * 2,
            compiler_params=_params("parallel", "parallel"),
            name="df_attention_context",
        )(lam, g2, qkv_ctx, qkv_ctx, qkv_ctx)
    return lat, ctx, casts


def _rt_kernel(lg_ref, gn_ref, *refs, n_heads, chunk, ctx_steps):
    ctx_in, lat_in = refs[0:8], refs[8:16]
    ctx_out, lat_out = refs[16:18], refs[18:20]
    st_scr, dec_scr = refs[20], refs[21]
    head0 = pl.program_id(1) * n_heads
    s_idx = pl.program_id(2)
    dk = ctx_in[0].shape[1] // n_heads
    dv = ctx_in[2].shape[1] // n_heads

    @pl.when(s_idx == 0)
    def _():
        st_scr[...] = jnp.zeros_like(st_scr)
        row = lax.broadcasted_iota(jnp.int32, (chunk, chunk), 0).astype(F32)
        col = lax.broadcasted_iota(jnp.int32, (chunk, chunk), 1).astype(F32)
        for d, rel in enumerate((row - col, col - row)):
            for h in range(n_heads):
                lg = lg_ref[d, head0 + h]
                dec_scr[d, h] = jnp.where(rel >= 0, jnp.exp(lg * jnp.maximum(rel, 0.0)), 0.0)

    def step(ins, outs):
        qf_ref, kf_ref, vf_ref, sf_ref, qb_ref, kb_ref, vb_ref, sb_ref = ins
        of_ref, ob_ref = outs
        pos = lax.broadcasted_iota(jnp.int32, (chunk, 1), 0).astype(F32)
        dirs = (
            (qf_ref, kf_ref, vf_ref, sf_ref, of_ref, pos + 1.0, chunk - 1.0 - pos),
            (qb_ref, kb_ref, vb_ref, sb_ref, ob_ref, chunk - pos, pos),
        )
        for d, (q_ref, k_ref, v_ref, s_ref, o_ref, q_pow, k_pow) in enumerate(dirs):
            for h in range(n_heads):
                lg = lg_ref[d, head0 + h]
                decay = dec_scr[d, h]
                q_decay = jnp.exp(lg * q_pow)
                k_decay = jnp.exp(lg * k_pow)
                chunk_decay = jnp.exp(lg * chunk)
                q = q_ref[:, h * dk:(h + 1) * dk]
                k = k_ref[:, h * dk:(h + 1) * dk]
                v = v_ref[:, h * dv:(h + 1) * dv]
                state = st_scr[d, h]
                inner = _dot_nt(q, k) * decay
                o = _dot(inner.astype(BF16), v) + _dot(q, state.astype(BF16)) * q_decay
                kd = (k.astype(F32) * k_decay).astype(BF16)
                st_scr[d, h] = state * chunk_decay + _dot_tn(kd, v)
                mu = jnp.mean(o, axis=-1, keepdims=True)
                dev = o - mu
                var = jnp.mean(dev * dev, axis=-1, keepdims=True)
                y = dev * lax.rsqrt(var + EPS) * gn_ref[d:d + 1, h * dv:(h + 1) * dv]
                y = y * s_ref[:, h * dv:(h + 1) * dv].astype(F32)
                o_ref[:, h * dv:(h + 1) * dv] = y.astype(o_ref.dtype)

    @pl.when(s_idx < ctx_steps)
    def _():
        step(ctx_in, ctx_out)

    @pl.when(s_idx >= ctx_steps)
    def _():
        step(lat_in, lat_out)


def _retention(proj_lat, proj_ctx, log_gamma, gn_g, batch, seq, n_ctx):
    n_heads = log_gamma.shape[1]
    qk = proj_lat.shape[1] // 8
    v = 2 * qk
    chunk = _pick(math.gcd(seq, n_ctx), (256, 128))
    ncc, nlc = n_ctx // chunk, seq // chunk
    dk, dv = qk // n_heads, v // n_heads
    groups = 2
    hpg = n_heads // groups
    qkw, vw = hpg * dk, hpg * dv

    def ctx_f(b, s):
        return b * ncc + jnp.minimum(s, ncc - 1)

    def ctx_b(b, s):
        return b * ncc + (ncc - 1 - jnp.minimum(s, ncc - 1))

    def lat_f(b, s):
        return b * nlc + jnp.maximum(s - ncc, 0)

    def lat_b(b, s):
        return b * nlc + (nlc - 1 - jnp.maximum(s - ncc, 0))

    def in_specs(row, direction):
        return [
            pl.BlockSpec((chunk, qkw), lambda b, g, s: (row(b, s), g)),
            pl.BlockSpec((chunk, qkw), lambda b, g, s: (row(b, s), groups + g)),
            pl.BlockSpec((chunk, vw), lambda b, g, s: (row(b, s), groups + g)),
            pl.BlockSpec((chunk, vw), lambda b, g, s: (row(b, s), (2 + direction) * groups + g)),
        ]

    def out_spec(row):
        return pl.BlockSpec((chunk, vw), lambda b, g, s: (row(b, s), g))

    out_ctx = jax.ShapeDtypeStruct((batch * n_ctx, v), BF16)
    out_lat = jax.ShapeDtypeStruct((batch * seq, v), BF16)
    yf_ctx, yb_ctx, yf_lat, yb_lat = pl.pallas_call(
        functools.partial(_rt_kernel, n_heads=hpg, chunk=chunk, ctx_steps=ncc),
        grid=(batch, groups, ncc + nlc),
        in_specs=[pl.BlockSpec(memory_space=pltpu.SMEM), pl.BlockSpec((2, vw), lambda b, g, s: (0, g))]
        + in_specs(ctx_f, 0) + in_specs(ctx_b, 1) + in_specs(lat_f, 0) + in_specs(lat_b, 1),
        out_specs=[out_spec(ctx_f), out_spec(ctx_b), out_spec(lat_f), out_spec(lat_b)],
        out_shape=[out_ctx, out_ctx, out_lat, out_lat],
        scratch_shapes=[pltpu.VMEM((2, hpg, dk, dv), F32), pltpu.VMEM((2, hpg, chunk, chunk), F32)],
        compiler_params=_params("parallel", "parallel", "arbitrary"),
        name="retention",
    )(log_gamma, gn_g, *([proj_ctx] * 8), *([proj_lat] * 8))
    return (yf_lat, yb_lat), (yf_ctx, yb_ctx)


def _axial_tables(rows_count, head_dim):
    rows = jnp.repeat(jnp.arange(rows_count), GRID_W).astype(F32)
    cols = jnp.tile(jnp.arange(GRID_W), rows_count).astype(F32)
    n_freq = head_dim // 4
    inv = ROPE_THETA ** (-jnp.arange(n_freq, dtype=F32) / n_freq)
    ang = jnp.concatenate([rows[:, None] * inv, cols[:, None] * inv], -1)
    cos, sin = jnp.cos(ang), jnp.sin(ang)
    return jnp.concatenate([cos, cos], -1), jnp.concatenate([-sin, sin], -1)


def _linear_tables(n_tokens, head_dim):
    half = head_dim // 2
    inv = ROPE_THETA ** (-jnp.arange(half, dtype=F32) / half)
    ang = jnp.arange(n_tokens, dtype=F32)[:, None] * inv
    return jnp.cos(ang), jnp.sin(ang)


def kernel(x, c, ctx, c_ctx, mod_w, mod_b, norm_g, ffn_w_in, ffn_conv_w, ffn_conv_b, ffn_w_out, ga_wqkv, ga_sink, ga_qk_norm, ga_wo, rt_w_in, rt_decay, rt_gn, rt_wo, df_wqkv, df_lambda, df_qk_norm, df_subln, df_wo):
    batch, seq, d = x.shape
    n_ctx = ctx.shape[1]
    depth = mod_w.shape[0]
    assert batch + 1 <= MOD_ROWS and seq % GRID_W == 0

    cond = jnp.zeros((MOD_ROWS, d), F32).at[0].set(c_ctx).at[1:batch + 1].set(c)
    mods = _adaln_table(cond, mod_w, mod_b).reshape(depth, 6, MOD_ROWS, 1, d)

    ax_cos, ax_sin = _axial_tables(seq // GRID_W, GA_HD)
    assert DF_HD == GA_HD
    rt_dk = d // RT_HEADS
    ln_cos, ln_sin = _linear_tables(seq, rt_dk)
    ones2 = jnp.ones((2, LANES), F32)

    h_lat = x.reshape(batch * seq, d)
    h_ctx = ctx.reshape(batch * n_ctx, d)

    lat = _Stream(seq, lambda i, bm: 1 + i // (seq // bm), lambda i, bm: i % (seq // bm), True)
    cst = _Stream(batch * n_ctx, lambda i, bm: 0, lambda i, bm: 0, False)

    ga_w, ga_o = ga_wqkv.astype(BF16), ga_wo.astype(BF16)
    late = [w for w in (rt_w_in, rt_wo, df_wqkv, df_wo) if w.shape[0] > 0]
    rt_w = rt_o = df_w = df_o = None
    ffn_bf16 = {}
    n_cw = ffn_conv_w.shape[2] // MXU_COLS
    f_cw = ffn_conv_w.reshape(depth, CONV_W, n_cw, MXU_COLS).transpose(0, 2, 1, 3)
    f_cb = ffn_conv_b.reshape(depth, n_cw, 1, MXU_COLS)

    for i in range(depth):
        need_ctx = i < depth - 1
        in_place = i > 0
        kind, j = i % N_MIXERS, i // N_MIXERS
        g1 = norm_g[i, 0].reshape(1, d)
        g2 = norm_g[i, 1].reshape(1, d)

        if kind == 0:
            w_all = ga_w
            qcols = ga_sink.shape[1] * GA_HD
            kcols = (w_all.shape[2] - qcols) // 2
            segments = ((qcols, "norm_rope_128", 0, GA_HD ** -0.5 * LOG2E), (kcols, "norm_rope_128", 1, 1.0),
                        (kcols, "plain", 0, 1.0))
            cos_t, sin_t, gains = ax_cos, ax_sin, ga_qk_norm[j]
        elif kind == 1:
            w_all = rt_w
            vcols = (w_all.shape[2] - 2 * d) // 3
            segments = ((d, "rope_256", 0, rt_dk ** -0.5), (d, "rope_256", 0, 1.0),
                        (vcols, "plain", 0, 1.0), (2 * vcols, "silu", 0, 1.0))
            cos_t, sin_t, gains = ln_cos, ln_sin, ones2
        else:
            w_all = df_w
            qcols = w_all.shape[2] // 3
            segments = ((qcols, "norm_rope_128", 0, DF_HD ** -0.5 * LOG2E), (qcols, "norm_rope_128", 1, 1.0),
                        (qcols, "plain", 0, 1.0))
            cos_t, sin_t, gains = ax_cos, ax_sin, df_qk_norm[j]

        proj = {"lat": _proj(h_lat, g1, mods, i, lat, w_all, j, segments, cos_t, sin_t, gains),
                "ctx": _proj(h_ctx, g1, mods, i, cst, w_all, j, segments, cos_t, sin_t, gains)}

        todo = [l for l in range(i, min(i + 2, depth)) if l not in ffn_bf16]
        cast = [(ffn_w_in, todo[0], len(todo)), (ffn_w_out, todo[0], len(todo))] if todo and kind != 1 else None
        if kind == 0:
            y_lat, y_ctx, casts = _ga_attention(proj["lat"], proj["ctx"], ga_sink[j], batch, seq, n_ctx, need_ctx,
                                                cast)
            h_lat = _oproj(y_lat, ga_o, j, h_lat, mods, i, lat, in_place)
            if need_ctx:
                h_ctx = _oproj(y_ctx, ga_o, j, h_ctx, mods, i, cst, in_place)
        elif kind == 1:
            log_gamma = jax.nn.log_sigmoid(rt_decay[j].astype(F32))
            y_lat, y_ctx = _retention(proj["lat"], proj["ctx"], log_gamma, rt_gn[j], batch, seq, n_ctx)
            h_lat = _oproj_rt(y_lat[0], y_lat[1], rt_o, j, h_lat, mods, i, lat, in_place)
            if need_ctx:
                h_ctx = _oproj_rt(y_ctx[0], y_ctx[1], rt_o, j, h_ctx, mods, i, cst, in_place)
        else:
            lambda_init = 0.8 - 0.6 * math.exp(-0.3 * i)
            y_lat, y_ctx, casts = _df_attention(proj["lat"], proj["ctx"], df_lambda[j], df_subln[j], batch, seq,
                                                n_ctx, lambda_init, need_ctx, cast)
            h_lat = _oproj(y_lat, df_o, j, h_lat, mods, i, lat, in_place)
            if need_ctx:
                h_ctx = _oproj(y_ctx, df_o, j, h_ctx, mods, i, cst, in_place)

        if cast is not None:
            w_in_c = casts[0].reshape(len(todo), d, -1)
            w_out_c = casts[1].reshape(len(todo), -1, d)
            for n, l in enumerate(todo):
                ffn_bf16[l] = (w_in_c, w_out_c, n)
        if i not in ffn_bf16:
            ffn_bf16[i] = (ffn_w_in[i:i + 1].astype(BF16), ffn_w_out[i:i + 1].astype(BF16), 0)
        f_in, f_out, f_idx = ffn_bf16[i]
        side = [(w, 0, w.shape[0]) for w in late] if i == 0 else None
        h_lat, side_out = _ffn(h_lat, g2, mods, i, f_idx, lat, f_in, f_cw, f_cb, f_out, seq, side)
        if side:
            done = {id(w): o.reshape(w.shape) for w, o in zip(late, side_out)}
            rt_w, rt_o, df_w, df_o = (done.get(id(w)) for w in (rt_w_in, rt_wo, df_wqkv, df_wo))
        if need_ctx:
            h_ctx, _ = _ffn(h_ctx, g2, mods, i, f_idx, cst, f_in, f_cw, f_cb, f_out, n_ctx)

    return h_lat.reshape(batch, seq, d)
```
